```python
import math
import jax, jax.numpy as jnp
from jax import lax
import numpy as np

D_MODEL = 1024
BATCH = 8
SEQ = 2048
DEPTH = 2

CHUNK = 64
RET_HEADS = 4
RET_DK = 128
RET_DV = 256
RET_QK = RET_HEADS * RET_DK
RET_V = RET_HEADS * RET_DV
ROPE_BASE = 10000.0
HG_HEADS = 8
HG_DK = 128
HG_DV = 128
HG_K = HG_HEADS * HG_DK
HG_V = HG_HEADS * HG_DV
HG_BLOCK = 16
SWA_HQ = 16
SWA_HKV = 4
SWA_HD = 64
SWA_Q = SWA_HQ * SWA_HD
SWA_KV = SWA_HKV * SWA_HD
SWA_WINDOW = 128
SWA_WIN_CHUNKS = SWA_WINDOW // CHUNK
D_FF = 2816
N_EXPERTS = 8
TOP_K = 2
D_EXP = 3584
N_DENSE = (DEPTH + 1) // 2
N_MOE = DEPTH // 2
LN_EPS = 1e-5
RMS_EPS = 1e-6
DN_ALPHA = (2.0 * DEPTH) ** 0.25
DN_BETA = (8.0 * DEPTH) ** -0.25
IN_SPLITS = (RET_QK, RET_QK, RET_V, RET_V,
             HG_K, HG_K, HG_V, HG_V,
             SWA_Q, SWA_KV, SWA_KV,
             D_MODEL, D_MODEL, D_MODEL)
N_IN = sum(IN_SPLITS)

kernel_name = "hybrid_retention_hgrn2_swa_moe_deepnorm"


def layer_norm(x, g, b):
    xf = x.astype(jnp.float32)
    mu = jnp.mean(xf, -1, keepdims=True)
    var = jnp.mean(jnp.square(xf - mu), -1, keepdims=True)
    return ((xf - mu) * lax.rsqrt(var + LN_EPS) * g + b).astype(x.dtype)


def rms_norm(x):
    xf = x.astype(jnp.float32)
    return xf * lax.rsqrt(jnp.mean(xf * xf, -1, keepdims=True) + RMS_EPS)


def rotary(x, pos):
    half = x.shape[-1] // 2
    inv = 1.0 / (ROPE_BASE ** jnp.linspace(0.0, 1.0, half, dtype=jnp.float32))
    ang = pos[:, None] * inv[None, :]
    cos = jnp.cos(ang)[None, :, None, :]
    sin = jnp.sin(ang)[None, :, None, :]
    x1, x2 = x[..., :half], x[..., half:]
    return jnp.concatenate([x1 * cos - x2 * sin, x1 * sin + x2 * cos], axis=-1)


def retention(q, k, v):
    B, S, H, DK = q.shape
    DV = v.shape[-1]
    n = S // CHUNK
    log_gamma = jnp.log(1.0 - 2.0 ** (-5.0 - jnp.arange(H, dtype=jnp.float32)))
    k = k * DK ** -0.5
    qc = q.reshape(B, n, CHUNK, H, DK)
    kc = k.reshape(B, n, CHUNK, H, DK)
    vc = v.reshape(B, n, CHUNK, H, DV)
    idx = jnp.arange(CHUNK, dtype=jnp.float32)
    diff = idx[:, None] - idx[None, :]
    decay = jnp.where(diff[None] >= 0,
                      jnp.exp(jnp.maximum(diff, 0.0)[None] * log_gamma[:, None, None]), 0.0)
    scores = jnp.einsum('bnthd,bnshd->bnhts', qc, kc) * decay
    o_intra = jnp.einsum('bnhts,bnshv->bnthv', scores, vc)
    k_decay = jnp.exp((CHUNK - 1.0 - idx)[:, None] * log_gamma[None, :])
    q_decay = jnp.exp((idx + 1.0)[:, None] * log_gamma[None, :])
    chunk_decay = jnp.exp(CHUNK * log_gamma)[None, :, None, None]

    def step(state, inp):
        q_n, k_n, v_n = inp
        o = jnp.einsum('bthd,th,bhdv->bthv', q_n, q_decay, state)
        state = chunk_decay * state + jnp.einsum('bshd,sh,bshv->bhdv', k_n, k_decay, v_n)
        return state, o

    init = jnp.zeros((B, H, DK, DV), jnp.float32)
    _, o_inter = lax.scan(step, init, (jnp.moveaxis(qc, 1, 0), jnp.moveaxis(kc, 1, 0),
                                       jnp.moveaxis(vc, 1, 0)))
    return (o_intra + jnp.moveaxis(o_inter, 0, 1)).reshape(B, S, H, DV)


def hgrn2_scan(q, log_f, k, v):
    B, S, H, DK = q.shape
    DV = v.shape[-1]
    n = S // HG_BLOCK
    L = HG_BLOCK
    qc = q.reshape(B, n, L, H, DK)
    kc = k.reshape(B, n, L, H, DK)
    vc = v.reshape(B, n, L, H, DV)
    bcum = jnp.cumsum(log_f.reshape(B, n, L, H, DK), axis=2)
    q_t = qc * jnp.exp(bcum)
    k_t = kc * jnp.exp(-bcum)
    causal = jnp.tril(jnp.ones((L, L), dtype=bool))
    a = jnp.where(causal, jnp.einsum('bnthd,bnshd->bnhts', q_t, k_t), 0.0)
    o_intra = jnp.einsum('bnhts,bnshv->bnthv', a, vc)
    b_last = bcum[:, :, -1]
    k_state = kc * jnp.exp(b_last[:, :, None] - bcum)
    blk_decay = jnp.exp(b_last)

    def step(state, inp):
        q_n, ks_n, v_n, d_n = inp
        o = jnp.einsum('bthd,bhdv->bthv', q_n, state)
        state = d_n[..., None] * state + jnp.einsum('bshd,bshv->bhdv', ks_n, v_n)
        return state, o

    init = jnp.zeros((B, H, DK, DV), jnp.float32)
    _, o_inter = lax.scan(step, init, (jnp.moveaxis(q_t, 1, 0), jnp.moveaxis(k_state, 1, 0),
                                       jnp.moveaxis(vc, 1, 0), jnp.moveaxis(blk_decay, 1, 0)))
    return (o_intra + jnp.moveaxis(o_inter, 0, 1)).reshape(B, S, H, DV)


def swa_with_sinks(q, k, v, sinks):
    B, S, HQ, HD = q.shape
    HKV = k.shape[2]
    G = HQ // HKV
    W = SWA_WIN_CHUNKS
    n = S // CHUNK
    qc = q.astype(jnp.float32).reshape(B, n, CHUNK, HKV, G, HD)
    kc = k.astype(jnp.float32).reshape(B, n, CHUNK, HKV, HD)
    vc = v.astype(jnp.float32).reshape(B, n, CHUNK, HKV, HD)
    padw = ((0, 0), (W, 0), (0, 0), (0, 0), (0, 0))
    kp = jnp.pad(kc, padw)
    vp = jnp.pad(vc, padw)
    kb = jnp.concatenate([kp[:, j:j + n] for j in range(W + 1)], axis=2)
    vb = jnp.concatenate([vp[:, j:j + n] for j in range(W + 1)], axis=2)
    chunk_ids = jnp.arange(n)[:, None] - W + jnp.arange(W + 1)[None, :]
    valid = jnp.repeat(chunk_ids >= 0, CHUNK, axis=1)
    s = jnp.einsum('bntkgd,bnskd->bnkgts', qc, kb) * HD ** -0.5
    s = jnp.where(valid[None, :, None, None, None, :], s, -jnp.inf)
    sink = sinks.astype(jnp.float32).reshape(HKV, G)[None, None, :, :, None, None]
    m = jnp.maximum(jnp.max(s, -1, keepdims=True), sink)
    p = jnp.exp(s - m)
    p = p / (jnp.sum(p, -1, keepdims=True) + jnp.exp(sink - m))
    o = jnp.einsum('bnkgts,bnskd->bntkgd', p, vb)
    return o.reshape(B, S, HQ * HD)


def swiglu(h, w_gate, w_up, w_down):
    return (jax.nn.silu(h @ w_gate) * (h @ w_up)) @ w_down


def moe_swiglu(h2, w_router, w_gate, w_up, w_down):
    logits = (h2 @ w_router).astype(jnp.float32)
    top_logits, top_idx = lax.top_k(logits, TOP_K)
    top_w = jax.nn.softmax(top_logits, axis=-1)
    flat_e = top_idx.reshape(-1)
    order = jnp.argsort(flat_e)
    tok = order // TOP_K
    xs = h2[tok]
    sizes = jnp.bincount(flat_e, length=N_EXPERTS).astype(jnp.int32)
    hid = jax.nn.silu(lax.ragged_dot(xs, w_gate, sizes)) * lax.ragged_dot(xs, w_up, sizes)
    y = lax.ragged_dot(hid, w_down, sizes)
    y = y * top_w.reshape(-1)[order][:, None].astype(y.dtype)
    return jnp.zeros_like(h2).at[tok].add(y.astype(h2.dtype))


def setup_inputs(seed: int = 0) -> dict:
    key = jax.random.key(seed)
    ks = jax.random.split(key, 24)
    f32 = jnp.float32
    D = D_MODEL

    def nrm(k, shape, scale):
        return jax.random.normal(k, shape, f32) * scale

    return {
        'x': nrm(ks[0], (BATCH, SEQ, D), 1.0),
        'ln_in_g': 1.0 + nrm(ks[1], (D,), 0.02),
        'ln_in_b': nrm(ks[2], (D,), 0.02),
        'w_in': nrm(ks[3], (DEPTH, D, N_IN), D ** -0.5),
        'ret_w_out': nrm(ks[4], (DEPTH, RET_V, D), DN_BETA * RET_V ** -0.5),
        'hgrn_lower_bounds': 1.0 + nrm(ks[5], (DEPTH, HG_K), 0.5),
        'hgrn_norm_g': 1.0 + nrm(ks[6], (DEPTH, HG_DV), 0.02),
        'hgrn_w_out': nrm(ks[7], (DEPTH, HG_V, D), DN_BETA * HG_V ** -0.5),
        'swa_sinks': nrm(ks[8], (DEPTH, SWA_HQ), 0.5),
        'swa_w_out': nrm(ks[9], (DEPTH, SWA_Q, D), DN_BETA * SWA_Q ** -0.5),
        'w_o': nrm(ks[10], (DEPTH, D, D), DN_BETA * D ** -0.5),
        'ln_mix_g': 1.0 + nrm(ks[11], (DEPTH, D), 0.02),
        'ln_mix_b': nrm(ks[12], (DEPTH, D), 0.02),
        'ffn_w_gate': nrm(ks[13], (N_DENSE, D, D_FF), D ** -0.5),
        'ffn_w_up': nrm(ks[14], (N_DENSE, D, D_FF), D ** -0.5),
        'ffn_w_down': nrm(ks[15], (N_DENSE, D_FF, D), DN_BETA * D_FF ** -0.5),
        'moe_router': nrm(ks[16], (N_MOE, D, N_EXPERTS), D ** -0.5),
        'moe_w_gate': nrm(ks[17], (N_MOE, N_EXPERTS, D, D_EXP), D ** -0.5),
        'moe_w_up': nrm(ks[18], (N_MOE, N_EXPERTS, D, D_EXP), D ** -0.5),
        'moe_w_down': nrm(ks[19], (N_MOE, N_EXPERTS, D_EXP, D), DN_BETA * D_EXP ** -0.5),
        'ln_ffn_g': 1.0 + nrm(ks[20], (DEPTH, D), 0.02),
        'ln_ffn_b': nrm(ks[21], (DEPTH, D), 0.02),
    }


def reference(x, ln_in_g, ln_in_b, w_in, ret_w_out, hgrn_lower_bounds, hgrn_norm_g, hgrn_w_out,
              swa_sinks, swa_w_out, w_o, ln_mix_g, ln_mix_b, ffn_w_gate, ffn_w_up, ffn_w_down,
              moe_router, moe_w_gate, moe_w_up, moe_w_down, ln_ffn_g, ln_ffn_b):
    B, S, D = x.shape
    f32 = jnp.float32
    pos = jnp.arange(S, dtype=f32)
    split_points = np.cumsum(np.array(IN_SPLITS))[:-1].tolist()
    lb_all = jnp.cumsum(jax.nn.softmax(hgrn_lower_bounds.astype(f32), axis=0), axis=0)
    lb_all = lb_all - lb_all[0]

    h = layer_norm(x, ln_in_g, ln_in_b)
    for layer in range(DEPTH):
        proj = jnp.einsum('bsd,dc->bsc', h, w_in[layer])
        (rq, rk, rv, rg, hq, hf, hi, hg, sq, sk, sv, ga, gb, gc) = jnp.split(proj, split_points, axis=-1)

        r_q = rotary(rq.astype(f32).reshape(B, S, RET_HEADS, RET_DK), pos)
        r_k = rotary(rk.astype(f32).reshape(B, S, RET_HEADS, RET_DK), pos)
        r_v = rv.astype(f32).reshape(B, S, RET_HEADS, RET_DV)
        r_o = rms_norm(retention(r_q, r_k, r_v)).reshape(B, S, RET_V)
        y_a = (jax.nn.silu(rg.astype(f32)) * r_o) @ ret_w_out[layer]

        lb = lb_all[layer].reshape(HG_HEADS, HG_DK)
        z = hf.astype(f32).reshape(B, S, HG_HEADS, HG_DK)
        log_f = jnp.logaddexp(jnp.log(lb), jnp.log1p(-lb) + jax.nn.log_sigmoid(z))
        h_k = -jnp.expm1(log_f)
        h_q = hq.astype(f32).reshape(B, S, HG_HEADS, HG_DK)
        h_v = hi.astype(f32).reshape(B, S, HG_HEADS, HG_DV)
        h_o = rms_norm(hgrn2_scan(h_q, log_f, h_k, h_v)) * hgrn_norm_g[layer]
        y_b = (jax.nn.silu(hg.astype(f32)) * h_o.reshape(B, S, HG_V)) @ hgrn_w_out[layer]

        s_o = swa_with_sinks(sq.reshape(B, S, SWA_HQ, SWA_HD), sk.reshape(B, S, SWA_HKV, SWA_HD),
                             sv.reshape(B, S, SWA_HKV, SWA_HD), swa_sinks[layer])
        y_c = s_o @ swa_w_out[layer]

        merged = (jax.nn.sigmoid(ga.astype(f32)) * y_a + jax.nn.sigmoid(gb.astype(f32)) * y_b
                  + jax.nn.sigmoid(gc.astype(f32)) * y_c)
        mix = merged @ w_o[layer]
        h = layer_norm((DN_ALPHA * h + mix).astype(h.dtype), ln_mix_g[layer], ln_mix_b[layer])

        j = layer // 2
        if layer % 2 == 0:
            ff = swiglu(h, ffn_w_gate[j], ffn_w_up[j], ffn_w_down[j])
        else:
            ff = moe_swiglu(h.reshape(B * S, D), moe_router[j], moe_w_gate[j], moe_w_up[j],
                            moe_w_down[j]).reshape(B, S, D)
        h = layer_norm((DN_ALPHA * h + ff).astype(h.dtype), ln_ffn_g[layer], ln_ffn_b[layer])
    return h
```

```python
import functools

import numpy as np
import jax
import jax.numpy as jnp
from jax import lax
from jax.experimental import pallas as pl
from jax.experimental.pallas import tpu as pltpu

F32 = jnp.float32
BF16 = jnp.bfloat16

D_MODEL = 1024
RET_HEADS, RET_DK, RET_DV = 4, 128, 256
RET_QK, RET_V = RET_HEADS * RET_DK, RET_HEADS * RET_DV
ROPE_BASE = 10000.0
HG_HEADS, HG_DK, HG_DV = 8, 128, 128
HG_K, HG_V = HG_HEADS * HG_DK, HG_HEADS * HG_DV
SWA_HQ, SWA_HKV, SWA_HD = 16, 4, 64
SWA_G = SWA_HQ // SWA_HKV
SWA_Q, SWA_KV = SWA_HQ * SWA_HD, SWA_HKV * SWA_HD
SWA_CHUNK = 64
SWA_WIN_CHUNKS = 2
N_EXPERTS, TOP_K = 8, 2
DEPTH = 2
LN_EPS, RMS_EPS = 1e-5, 1e-6
DN_ALPHA = (2.0 * DEPTH) ** 0.25

VMEM_LIMIT_BYTES = 56 * 1024 * 1024
LANES = 128

RET_CHUNK = 128
HG_CHUNK = 32

OFF_RQ, OFF_RK, OFF_RV, OFF_RG = 0, 512, 1024, 2048
OFF_HQ, OFF_HI, OFF_HG = 3072, 4096, 5120
OFF_GA, OFF_GB, OFF_GC = 6144, 7168, 8192
OFF_SWA = 9216
SWA_GROUP_W = SWA_G * SWA_HD + 2 * SWA_HD
N_PROJ = OFF_SWA + SWA_HKV * SWA_GROUP_W


def _cparams(sem, vmem=VMEM_LIMIT_BYTES):
    return pltpu.CompilerParams(dimension_semantics=sem, vmem_limit_bytes=vmem)


def _layer_norm(x, g, b):
    mu = jnp.mean(x, -1, keepdims=True)
    xc = x - mu
    var = jnp.mean(xc * xc, -1, keepdims=True)
    return xc * lax.rsqrt(var + LN_EPS) * g + b


def _silu(x):
    return x * jax.nn.sigmoid(x)


def _ln_in_kernel(x_ref, g_ref, b_ref, h_ref, hb_ref):
    h = _layer_norm(x_ref[...], g_ref[...], b_ref[...])
    h_ref[...] = h
    hb_ref[...] = h.astype(BF16)


def ln_in(x2, g, b, tm=512):
    T, D = x2.shape
    tm = min(tm, T)
    return pl.pallas_call(
        _ln_in_kernel,
        grid=(T // tm,),
        in_specs=[pl.BlockSpec((tm, D), lambda i: (i, 0)),
                  pl.BlockSpec((1, D), lambda i: (0, 0)),
                  pl.BlockSpec((1, D), lambda i: (0, 0))],
        out_specs=[pl.BlockSpec((tm, D), lambda i: (i, 0)),
                   pl.BlockSpec((tm, D), lambda i: (i, 0))],
        out_shape=[jax.ShapeDtypeStruct((T, D), F32), jax.ShapeDtypeStruct((T, D), BF16)],
        compiler_params=_cparams(("parallel",)),
        name="ln_in",
    )(x2, g.reshape(1, D), b.reshape(1, D))


def _matmul_kernel(x_ref, w_ref, o_ref):
    o_ref[...] = jnp.dot(x_ref[...], w_ref[...], preferred_element_type=F32).astype(o_ref.dtype)


def matmul(x, w, out_dtype, tm=1024, tn=512, name="matmul"):
    T, K = x.shape
    N = w.shape[1]
    tm = min(tm, T)
    return pl.pallas_call(
        _matmul_kernel,
        grid=(T // tm, N // tn),
        in_specs=[pl.BlockSpec((tm, K), lambda i, j: (i, 0)),
                  pl.BlockSpec((K, tn), lambda i, j: (0, j))],
        out_specs=pl.BlockSpec((tm, tn), lambda i, j: (i, j)),
        out_shape=jax.ShapeDtypeStruct((T, N), out_dtype),
        compiler_params=_cparams(("parallel", "parallel")),
        name=name,
    )(x, w)


def _retention_kernel(q_ref, k_ref, v_ref, g_ref, cos_ref, sin_ref, dm_ref, qd_ref, kd_ref, cd_ref,
                      o_ref, st_ref, *, chunk, nchunks):
    st_ref[...] = jnp.zeros_like(st_ref)
    dm = dm_ref[0]
    qd = qd_ref[0]
    kd = kd_ref[0]
    cd = cd_ref[0]

    def body(c, carry):
        r = pl.ds(pl.multiple_of(c * chunk, chunk), chunk)
        cos = cos_ref[r, :]
        sin = sin_ref[r, :]
        q = q_ref[r, :].astype(F32)
        k = k_ref[r, :].astype(F32)
        qr = q * cos + pltpu.roll(q, RET_DK // 2, 1) * sin
        kr = k * cos + pltpu.roll(k, RET_DK // 2, 1) * sin
        v = v_ref[r, :]
        s = lax.dot_general(qr.astype(BF16), kr.astype(BF16), (((1,), (1,)), ((), ())),
                            preferred_element_type=F32) * dm
        o = jnp.dot(s.astype(BF16), v, preferred_element_type=F32)
        st = st_ref[...]
        o = o + jnp.dot((qr * qd).astype(BF16), st.astype(BF16), preferred_element_type=F32)
        upd = lax.dot_general((kr * kd).astype(BF16), v, (((0,), (0,)), ((), ())),
                              preferred_element_type=F32)
        st_ref[...] = st * cd + upd
        on = o * lax.rsqrt(jnp.mean(o * o, -1, keepdims=True) + RMS_EPS)
        g = g_ref[r, :].astype(F32)
        o_ref[r, :] = (_silu(g) * on).astype(o_ref.dtype)
        return carry

    lax.fori_loop(0, nchunks, body, 0)


def retention_mixer(proj, tabs, B, S):
    T = B * S
    C = RET_CHUNK
    nq, nv = OFF_RQ // RET_DK, OFF_RV // RET_DV
    nk, ng = OFF_RK // RET_DK, OFF_RG // RET_DV
    kern = functools.partial(_retention_kernel, chunk=C, nchunks=S // C)
    return pl.pallas_call(
        kern,
        grid=(B, RET_HEADS),
        in_specs=[pl.BlockSpec((S, RET_DK), lambda b, h: (b, nq + h)),
                  pl.BlockSpec((S, RET_DK), lambda b, h: (b, nk + h)),
                  pl.BlockSpec((S, RET_DV), lambda b, h: (b, nv + h)),
                  pl.BlockSpec((S, RET_DV), lambda b, h: (b, ng + h)),
                  pl.BlockSpec((S, RET_DK), lambda b, h: (0, 0)),
                  pl.BlockSpec((S, RET_DK), lambda b, h: (0, 0)),
                  pl.BlockSpec((1, C, C), lambda b, h: (h, 0, 0)),
                  pl.BlockSpec((1, C, RET_DK), lambda b, h: (h, 0, 0)),
                  pl.BlockSpec((1, C, RET_DK), lambda b, h: (h, 0, 0)),
                  pl.BlockSpec((1, 1, RET_DV), lambda b, h: (h, 0, 0))],
        out_specs=pl.BlockSpec((S, RET_DV), lambda b, h: (b, h)),
        out_shape=jax.ShapeDtypeStruct((T, RET_V), BF16),
        scratch_shapes=[pltpu.VMEM((RET_DK, RET_DV), F32)],
        compiler_params=_cparams(("parallel", "parallel")),
        name="retention",
    )(proj, proj, proj, proj, tabs["cos"], tabs["sin"], tabs["dm"], tabs["qd"], tabs["kd"], tabs["cd"])


def retention_tables(S):
    C = RET_CHUNK
    half = RET_DK // 2
    pos = jnp.arange(S, dtype=F32)
    inv = 1.0 / (ROPE_BASE ** jnp.linspace(0.0, 1.0, half, dtype=F32))
    ang = pos[:, None] * inv[None, :]
    cos, sin = jnp.cos(ang), jnp.sin(ang)
    log_gamma = jnp.log(1.0 - 2.0 ** (-5.0 - jnp.arange(RET_HEADS, dtype=F32)))
    idx = jnp.arange(C, dtype=F32)
    diff = idx[:, None] - idx[None, :]
    decay = jnp.where(diff[None] >= 0,
                      jnp.exp(jnp.maximum(diff, 0.0)[None] * log_gamma[:, None, None]), 0.0)
    scale = RET_DK ** -0.5
    qd = jnp.exp((idx + 1.0)[None, :] * log_gamma[:, None])
    kd = jnp.exp((C - 1.0 - idx)[None, :] * log_gamma[:, None]) * scale
    cd = jnp.exp(C * log_gamma)
    return {
        "cos": jnp.concatenate([cos, cos], axis=1),
        "sin": jnp.concatenate([-sin, sin], axis=1),
        "dm": decay * scale,
        "qd": jnp.broadcast_to(qd[:, :, None], (RET_HEADS, C, RET_DK)),
        "kd": jnp.broadcast_to(kd[:, :, None], (RET_HEADS, C, RET_DK)),
        "cd": jnp.broadcast_to(cd[:, None, None], (RET_HEADS, 1, RET_DV)),
    }


def _hgrn_kernel(q_ref, z_ref, i_ref, g_ref, lb_ref, ng_ref, o_ref, st_ref, *, chunk, nchunks):
    st_ref[...] = jnp.zeros_like(st_ref)
    lb = lb_ref[0]
    ng = ng_ref[...]
    half = chunk // 2
    row = lax.broadcasted_iota(jnp.int32, (chunk, HG_DK), 0)
    causal = (lax.broadcasted_iota(jnp.int32, (chunk, chunk), 0)
              >= lax.broadcasted_iota(jnp.int32, (chunk, chunk), 1))

    def body(c, carry):
        r = pl.ds(pl.multiple_of(c * chunk, chunk), chunk)
        f = lb + (1.0 - lb) * jax.nn.sigmoid(z_ref[r, :])
        cum = jnp.log(f)
        shift = 1
        while shift < chunk:
            cum = cum + jnp.where(row >= shift, pltpu.roll(cum, shift, 0), 0.0)
            shift *= 2
        mid = cum[half - 1:half, :]
        last = cum[chunk - 1:chunk, :]
        qh = q_ref[r, :].astype(F32) * jnp.exp(cum - mid)
        kh = (1.0 - f) * jnp.exp(mid - cum)
        v = i_ref[r, :]
        a = lax.dot_general(qh.astype(BF16), kh.astype(BF16), (((1,), (1,)), ((), ())),
                            preferred_element_type=F32)
        a = jnp.where(causal, a, 0.0)
        o = jnp.dot(a.astype(BF16), v, preferred_element_type=F32)
        st = st_ref[...]
        q_in = qh * jnp.exp(mid)
        o = o + lax.dot_general(q_in.astype(BF16), st.astype(BF16), (((1,), (1,)), ((), ())),
                                preferred_element_type=F32)
        k_st = kh * jnp.exp(last - mid)
        upd = lax.dot_general(v, k_st.astype(BF16), (((0,), (0,)), ((), ())),
                              preferred_element_type=F32)
        st_ref[...] = st * jnp.exp(last) + upd
        on = o * lax.rsqrt(jnp.mean(o * o, -1, keepdims=True) + RMS_EPS) * ng
        g = g_ref[r, :].astype(F32)
        o_ref[r, :] = (_silu(g) * on).astype(o_ref.dtype)
        return carry

    lax.fori_loop(0, nchunks, body, 0)


def hgrn_mixer(proj, hf, lb, norm_g, B, S):
    T = B * S
    C = HG_CHUNK
    nq, ni, ng = OFF_HQ // HG_DK, OFF_HI // HG_DV, OFF_HG // HG_DV
    kern = functools.partial(_hgrn_kernel, chunk=C, nchunks=S // C)
    return pl.pallas_call(
        kern,
        grid=(B, HG_HEADS),
        in_specs=[pl.BlockSpec((S, HG_DK), lambda b, h: (b, nq + h)),
                  pl.BlockSpec((S, HG_DK), lambda b, h: (b, h)),
                  pl.BlockSpec((S, HG_DV), lambda b, h: (b, ni + h)),
                  pl.BlockSpec((S, HG_DV), lambda b, h: (b, ng + h)),
                  pl.BlockSpec((1, 1, HG_DK), lambda b, h: (h, 0, 0)),
                  pl.BlockSpec((1, HG_DV), lambda b, h: (0, 0))],
        out_specs=pl.BlockSpec((S, HG_DV), lambda b, h: (b, h)),
        out_shape=jax.ShapeDtypeStruct((T, HG_V), BF16),
        scratch_shapes=[pltpu.VMEM((HG_DV, HG_DK), F32)],
        compiler_params=_cparams(("parallel", "parallel")),
        name="hgrn2",
    )(proj, hf, proj, proj, lb.reshape(HG_HEADS, 1, HG_DK), norm_g.reshape(1, HG_DV))


def _swa_kernel(sink_ref, x_ref, o_ref, *, nchunks):
    kvh = pl.program_id(1)
    CH = SWA_CHUNK
    WIN = (SWA_WIN_CHUNKS + 1) * CH
    QW = SWA_G * SWA_HD
    col = lax.broadcasted_iota(jnp.int32, (CH, WIN), 1)
    scale = SWA_HD ** -0.5

    def body(c, carry):
        first = jnp.maximum(c - SWA_WIN_CHUNKS, 0)
        rq = pl.ds(pl.multiple_of(c * CH, CH), CH)
        rk = pl.ds(pl.multiple_of(first * CH, CH), WIN)
        q = x_ref[rq, 0:QW]
        kv = x_ref[rk, QW:QW + 2 * SWA_HD]
        k = kv[:, 0:SWA_HD]
        v = kv[:, SWA_HD:2 * SWA_HD]
        valid = col < (c + 1 - first) * CH
        outs = []
        for g in range(SWA_G):
            sink = sink_ref[kvh * SWA_G + g]
            s = lax.dot_general(q[:, g * SWA_HD:(g + 1) * SWA_HD], k, (((1,), (1,)), ((), ())),
                                preferred_element_type=F32) * scale
            s = jnp.where(valid, s, -jnp.inf)
            m = jnp.maximum(jnp.max(s, -1, keepdims=True), sink)
            p = jnp.exp(s - m)
            den = jnp.sum(p, -1, keepdims=True) + jnp.exp(sink - m)
            og = jnp.dot(p.astype(BF16), v, preferred_element_type=F32)
            outs.append(og / den)
        o_ref[rq, :] = jnp.concatenate(outs, axis=1).astype(o_ref.dtype)
        return carry

    lax.fori_loop(0, nchunks, body, 0)


def swa_mixer(proj, sinks, B, S):
    T = B * S
    nb = OFF_SWA // SWA_GROUP_W
    kern = functools.partial(_swa_kernel, nchunks=S // SWA_CHUNK)
    return pl.pallas_call(
        kern,
        grid=(B, SWA_HKV),
        in_specs=[pl.BlockSpec(memory_space=pltpu.SMEM),
                  pl.BlockSpec((S, SWA_GROUP_W), lambda b, h: (b, nb + h))],
        out_specs=pl.BlockSpec((S, SWA_G * SWA_HD), lambda b, h: (b, h)),
        out_shape=jax.ShapeDtypeStruct((T, SWA_Q), BF16),
        compiler_params=_cparams(("parallel", "parallel")),
        name="swa",
    )(sinks.astype(F32), proj)


def _merge_kernel(a_ref, b_ref, c_ref, ga_ref, gb_ref, gc_ref, h_ref, wa_ref, wb_ref, wc_ref, wo_ref,
                  g_ref, beta_ref, ho_ref, hob_ref):
    def branch(x_ref, gate_ref, w_ref):
        y = jnp.dot(x_ref[...], w_ref[...], preferred_element_type=F32)
        return jax.nn.sigmoid(gate_ref[...].astype(F32)) * y

    merged = branch(a_ref, ga_ref, wa_ref) + branch(b_ref, gb_ref, wb_ref) + branch(c_ref, gc_ref, wc_ref)
    mix = jnp.dot(merged.astype(BF16), wo_ref[...], preferred_element_type=F32)
    hn = _layer_norm(DN_ALPHA * h_ref[...] + mix, g_ref[...], beta_ref[...])
    ho_ref[...] = hn
    hob_ref[...] = hn.astype(BF16)


def merge_project(ya, yb, yc, proj, h, wa, wb, wc, wo, g, beta, tm=256):
    T, D = h.shape
    tm = min(tm, T)
    row = lambda i: (i, 0)
    const = lambda i: (0, 0)
    wspec = pl.BlockSpec((D, D), const)
    return pl.pallas_call(
        _merge_kernel,
        grid=(T // tm,),
        in_specs=[pl.BlockSpec((tm, D), row), pl.BlockSpec((tm, D), row), pl.BlockSpec((tm, D), row),
                  pl.BlockSpec((tm, D), lambda i: (i, OFF_GA // D_MODEL)),
                  pl.BlockSpec((tm, D), lambda i: (i, OFF_GB // D_MODEL)),
                  pl.BlockSpec((tm, D), lambda i: (i, OFF_GC // D_MODEL)),
                  pl.BlockSpec((tm, D), row),
                  wspec, wspec, wspec, wspec,
                  pl.BlockSpec((1, D), const), pl.BlockSpec((1, D), const)],
        out_specs=[pl.BlockSpec((tm, D), row), pl.BlockSpec((tm, D), row)],
        out_shape=[jax.ShapeDtypeStruct((T, D), F32), jax.ShapeDtypeStruct((T, D), BF16)],
        compiler_params=_cparams(("parallel",)),
        name="merge_project",
    )(ya, yb, yc, proj, proj, proj, h, wa, wb, wc, wo, g.reshape(1, D), beta.reshape(1, D))


def _swiglu_step(x, wg_ref, wu_ref, wd_ref, acc_ref):
    g = jnp.dot(x, wg_ref[...], preferred_element_type=F32)
    u = jnp.dot(x, wu_ref[...], preferred_element_type=F32)
    a = (_silu(g) * u).astype(BF16)
    acc_ref[...] += jnp.dot(a, wd_ref[...], preferred_element_type=F32)


def _ffn_kernel(xb_ref, h_ref, wg_ref, wu_ref, wd_ref, g_ref, beta_ref, ho_ref, hob_ref, acc_ref):
    f = pl.program_id(1)

    @pl.when(f == 0)
    def _():
        acc_ref[...] = jnp.zeros_like(acc_ref)

    _swiglu_step(xb_ref[...], wg_ref, wu_ref, wd_ref, acc_ref)

    @pl.when(f == pl.num_programs(1) - 1)
    def _():
        hn = _layer_norm(DN_ALPHA * h_ref[...] + acc_ref[...], g_ref[...], beta_ref[...])
        ho_ref[...] = hn
        hob_ref[...] = hn.astype(BF16)


def dense_ffn(hb, h, wg, wu, wd, g, beta, tm=512, tf=256):
    T, D = h.shape
    F = wg.shape[1]
    tm = min(tm, T)
    row = lambda i, f: (i, 0)
    const = lambda i, f: (0, 0)
    return pl.pallas_call(
        _ffn_kernel,
        grid=(T // tm, F // tf),
        in_specs=[pl.BlockSpec((tm, D), row), pl.BlockSpec((tm, D), row),
                  pl.BlockSpec((D, tf), lambda i, f: (0, f)),
                  pl.BlockSpec((D, tf), lambda i, f: (0, f)),
                  pl.BlockSpec((tf, D), lambda i, f: (f, 0)),
                  pl.BlockSpec((1, D), const), pl.BlockSpec((1, D), const)],
        out_specs=[pl.BlockSpec((tm, D), row), pl.BlockSpec((tm, D), row)],
        out_shape=[jax.ShapeDtypeStruct((T, D), F32), jax.ShapeDtypeStruct((T, D), BF16)],
        scratch_shapes=[pltpu.VMEM((tm, D), F32)],
        compiler_params=_cparams(("parallel", "arbitrary")),
        name="dense_ffn",
    )(hb, h, wg, wu, wd, g.reshape(1, D), beta.reshape(1, D))


def _router_kernel(h_ref, w_ref, o_ref):
    logits = jnp.dot(h_ref[...], w_ref[...], preferred_element_type=F32, precision=lax.Precision.HIGHEST)
    lane = lax.broadcasted_iota(jnp.int32, logits.shape, 1)
    l1 = jnp.where(lane < N_EXPERTS, logits, -jnp.inf)
    m1 = jnp.max(l1, -1, keepdims=True)
    i1 = jnp.min(jnp.where(l1 == m1, lane, LANES), -1, keepdims=True)
    l2 = jnp.where(lane == i1, -jnp.inf, l1)
    m2 = jnp.max(l2, -1, keepdims=True)
    i2 = jnp.min(jnp.where(l2 == m2, lane, LANES), -1, keepdims=True)
    e = jnp.exp(m2 - m1)
    w1 = 1.0 / (1.0 + e)
    w2 = e / (1.0 + e)
    out = jnp.where(lane == 0, i1.astype(F32),
                    jnp.where(lane == 1, i2.astype(F32),
                              jnp.where(lane == 2, w1, jnp.where(lane == 3, w2, 0.0))))
    o_ref[...] = out


def router_top2(h, w_router, tm=512):
    T, D = h.shape
    tm = min(tm, T)
    wpad = jnp.zeros((D, LANES), F32).at[:, :N_EXPERTS].set(w_router.astype(F32))
    return pl.pallas_call(
        _router_kernel,
        grid=(T // tm,),
        in_specs=[pl.BlockSpec((tm, D), lambda i: (i, 0)), pl.BlockSpec((D, LANES), lambda i: (0, 0))],
        out_specs=pl.BlockSpec((tm, LANES), lambda i: (i, 0)),
        out_shape=jax.ShapeDtypeStruct((T, LANES), F32),
        compiler_params=_cparams(("parallel",)),
        name="router_top2",
    )(h, wpad)


def _gather_rows_kernel(idx_ref, src_ref, dst_ref, sem, *, rows):
    base = pl.program_id(0) * rows

    def row_copy(r, src_row):
        return pltpu.make_async_copy(src_ref.at[pl.ds(src_row, 1), :],
                                     dst_ref.at[pl.ds(base + r, 1), :], sem)

    def issue(r, carry):
        row_copy(r, idx_ref[base + r]).start()
        return carry

    def drain(r, carry):
        row_copy(r, 0).wait()
        return carry

    lax.fori_loop(0, rows, issue, 0)
    lax.fori_loop(0, rows, drain, 0)


def gather_rows(src, idx, rows=256):
    n = idx.shape[0]
    D = src.shape[1]
    kern = functools.partial(_gather_rows_kernel, rows=rows)
    return pl.pallas_call(
        kern,
        grid_spec=pltpu.PrefetchScalarGridSpec(
            num_scalar_prefetch=1,
            grid=(n // rows,),
            in_specs=[pl.BlockSpec(memory_space=pl.ANY)],
            out_specs=pl.BlockSpec(memory_space=pl.ANY),
            scratch_shapes=[pltpu.SemaphoreType.DMA(())],
        ),
        out_shape=jax.ShapeDtypeStruct((n, D), src.dtype),
        compiler_params=_cparams(("arbitrary",)),
        name="gather_rows",
    )(idx, src)


def _expert_ffn_kernel(te_ref, nt_ref, x_ref, wg_ref, wu_ref, wd_ref, y_ref, acc_ref):
    i = pl.program_id(0)
    f = pl.program_id(1)
    last = pl.num_programs(1) - 1
    active = i < nt_ref[0]

    @pl.when(f == 0)
    def _():
        acc_ref[...] = jnp.zeros_like(acc_ref)

    @pl.when(active)
    def _():
        _swiglu_step(x_ref[...].astype(BF16), wg_ref, wu_ref, wd_ref, acc_ref)

    @pl.when(f == last)
    def _():
        y_ref[...] = acc_ref[...]


def expert_ffn(xs, te, nt, wg, wu, wd, tm, tf=512):
    P, D = xs.shape
    F = wg.shape[2]
    nf = F // tf

    def wcol(i, f, te_ref, nt_ref):
        return (te_ref[i], 0, jnp.where(i < nt_ref[0], f, nf - 1))

    def wrow(i, f, te_ref, nt_ref):
        return (te_ref[i], jnp.where(i < nt_ref[0], f, nf - 1), 0)

    return pl.pallas_call(
        _expert_ffn_kernel,
        grid_spec=pltpu.PrefetchScalarGridSpec(
            num_scalar_prefetch=2,
            grid=(P // tm, nf),
            in_specs=[pl.BlockSpec((tm, D), lambda i, f, te_ref, nt_ref: (i, 0)),
                      pl.BlockSpec((None, D, tf), wcol),
                      pl.BlockSpec((None, D, tf), wcol),
                      pl.BlockSpec((None, tf, D), wrow)],
            out_specs=pl.BlockSpec((tm, D), lambda i, f, te_ref, nt_ref: (i, 0)),
            scratch_shapes=[pltpu.VMEM((tm, D), F32)],
        ),
        out_shape=jax.ShapeDtypeStruct((P, D), F32),
        compiler_params=_cparams(("arbitrary", "arbitrary")),
        name="expert_ffn",
    )(te, nt, xs, wg, wu, wd)


def _moe_combine_kernel(y0_ref, y1_ref, r_ref, h_ref, g_ref, beta_ref, ho_ref, hob_ref):
    r = r_ref[...]
    ff = y0_ref[0] * r[:, 2:3] + y1_ref[0] * r[:, 3:4]
    hn = _layer_norm(DN_ALPHA * h_ref[...] + ff, g_ref[...], beta_ref[...])
    ho_ref[...] = hn
    hob_ref[...] = hn.astype(BF16)


def moe_combine(y2, route, h, g, beta, tm=512):
    T, D = h.shape
    tm = min(tm, T)
    row = lambda i: (i, 0)
    const = lambda i: (0, 0)
    return pl.pallas_call(
        _moe_combine_kernel,
        grid=(T // tm,),
        in_specs=[pl.BlockSpec((1, tm, D), lambda i: (0, i, 0)),
                  pl.BlockSpec((1, tm, D), lambda i: (1, i, 0)),
                  pl.BlockSpec((tm, LANES), row),
                  pl.BlockSpec((tm, D), row),
                  pl.BlockSpec((1, D), const), pl.BlockSpec((1, D), const)],
        out_specs=[pl.BlockSpec((tm, D), row), pl.BlockSpec((tm, D), row)],
        out_shape=[jax.ShapeDtypeStruct((T, D), F32), jax.ShapeDtypeStruct((T, D), BF16)],
        compiler_params=_cparams(("parallel",)),
        name="moe_combine",
    )(y2, y2, route, h, g.reshape(1, D), beta.reshape(1, D))


def moe_plan(route, T, tm):
    i32 = jnp.int32
    e_flat = route[:, 0:TOP_K].astype(i32).T.reshape(-1)
    onehot = (e_flat[:, None] == jnp.arange(N_EXPERTS, dtype=i32)[None, :]).astype(i32)
    csum = jnp.cumsum(onehot, axis=0)
    rank = jnp.sum(onehot * (csum - 1), axis=1)
    sizes = csum[-1]
    padded = ((sizes + tm - 1) // tm) * tm
    gend = jnp.cumsum(padded)
    gstart = gend - padded
    pos = gstart[e_flat] + rank
    n_rows = TOP_K * T + N_EXPERTS * tm
    tok = jnp.arange(TOP_K * T, dtype=i32) % T
    src_tok = jnp.zeros((n_rows,), i32).at[pos].set(tok)
    n_tiles = n_rows // tm
    nt = (gend[-1] // tm).astype(i32)
    tile_start = jnp.arange(n_tiles, dtype=i32) * tm
    te = jnp.minimum(jnp.sum((tile_start[:, None] >= gend[None, :]).astype(i32), axis=1), N_EXPERTS - 1)
    te = jnp.where(jnp.arange(n_tiles) < nt, te, te[jnp.maximum(nt - 1, 0)])
    return src_tok, pos.astype(i32), te.astype(i32), nt.reshape(1)


def moe_ffn(h, w_router, wg, wu, wd, g, beta, tm=512):
    T, D = h.shape
    route = router_top2(h, w_router)
    src_tok, pos, te, nt = moe_plan(route, T, tm)
    xs = gather_rows(h, src_tok)
    ys = expert_ffn(xs, te, nt, wg, wu, wd, tm)
    y2 = gather_rows(ys, pos).reshape(TOP_K, T, D)
    return moe_combine(y2, route, h, g, beta)


def permute_w_in(w):
    D = w.shape[0]
    cuts = np.cumsum([RET_QK, RET_QK, RET_V, RET_V, HG_K, HG_K, HG_V, HG_V, SWA_Q, SWA_KV, SWA_KV,
                      D_MODEL, D_MODEL])
    rq, rk, rv, rg, hq, hf, hi, hg, sq, sk, sv, ga, gb, gc = jnp.split(w, cuts.tolist(), axis=1)
    swa = jnp.concatenate([sq.reshape(D, SWA_HKV, SWA_G * SWA_HD),
                           sk.reshape(D, SWA_HKV, SWA_HD),
                           sv.reshape(D, SWA_HKV, SWA_HD)], axis=2).reshape(D, SWA_HKV * SWA_GROUP_W)
    main = jnp.concatenate([rq, rk, rv, rg, hq, hi, hg, ga, gb, gc, swa], axis=1)
    return main.astype(BF16), hf.astype(BF16)


def kernel(x, ln_in_g, ln_in_b, w_in, ret_w_out, hgrn_lower_bounds, hgrn_norm_g, hgrn_w_out, swa_sinks,
           swa_w_out, w_o, ln_mix_g, ln_mix_b, ffn_w_gate, ffn_w_up, ffn_w_down, moe_router, moe_w_gate,
           moe_w_up, moe_w_down, ln_ffn_g, ln_ffn_b):
    B, S, D = x.shape
    T = B * S
    assert D == D_MODEL and S % RET_CHUNK == 0 and S % SWA_CHUNK == 0 and S >= 3 * SWA_CHUNK

    lb_all = jnp.cumsum(jax.nn.softmax(hgrn_lower_bounds.astype(F32), axis=0), axis=0)
    lb_all = lb_all - lb_all[0]
    tabs = retention_tables(S)

    h, hb = ln_in(x.reshape(T, D), ln_in_g, ln_in_b)
    for layer in range(DEPTH):
        w_main, w_hf = permute_w_in(w_in[layer])
        proj = matmul(hb, w_main, BF16, name="in_proj")
        hf = matmul(hb, w_hf, F32, name="in_proj_forget")

        ya = retention_mixer(proj, tabs, B, S)
        yb = hgrn_mixer(proj, hf, lb_all[layer], hgrn_norm_g[layer], B, S)
        yc = swa_mixer(proj, swa_sinks[layer], B, S)
        h, hb = merge_project(ya, yb, yc, proj, h,
                              ret_w_out[layer].astype(BF16), hgrn_w_out[layer].astype(BF16),
                              swa_w_out[layer].astype(BF16), w_o[layer].astype(BF16),
                              ln_mix_g[layer], ln_mix_b[layer])

        j = layer // 2
        if layer % 2 == 0:
            h, hb = dense_ffn(hb, h, ffn_w_gate[j].astype(BF16), ffn_w_up[j].astype(BF16),
                              ffn_w_down[j].astype(BF16), ln_ffn_g[layer], ln_ffn_b[layer])
        else:
            h, hb = moe_ffn(h, moe_router[j], moe_w_gate[j].astype(BF16), moe_w_up[j].astype(BF16),
                            moe_w_down[j].astype(BF16), ln_ffn_g[layer], ln_ffn_b[layer])
    return h.reshape(B, S, D)
```

```python
import functools

import numpy as np
import jax
import jax.numpy as jnp
from jax import lax
from jax.experimental import pallas as pl
from jax.experimental.pallas import tpu as pltpu

F32 = jnp.float32
BF16 = jnp.bfloat16

D_MODEL = 1024
RET_HEADS, RET_DK, RET_DV = 4, 128, 256
RET_QK, RET_V = RET_HEADS * RET_DK, RET_HEADS * RET_DV
ROPE_BASE = 10000.0
HG_HEADS, HG_DK, HG_DV = 8, 128, 128
HG_K, HG_V = HG_HEADS * HG_DK, HG_HEADS * HG_DV
SWA_HQ, SWA_HKV, SWA_HD = 16, 4, 64
SWA_G = SWA_HQ // SWA_HKV
SWA_Q, SWA_KV = SWA_HQ * SWA_HD, SWA_HKV * SWA_HD
SWA_CHUNK = 64
SWA_WIN_CHUNKS = 2
N_EXPERTS, TOP_K = 8, 2
DEPTH = 2
LN_EPS, RMS_EPS = 1e-5, 1e-6
DN_ALPHA = (2.0 * DEPTH) ** 0.25

VMEM_LIMIT_BYTES = 56 * 1024 * 1024
LANES = 128

RET_CHUNK = 128
HG_CHUNK = 32

OFF_RQ, OFF_RK, OFF_RV, OFF_RG = 0, 512, 1024, 2048
OFF_HQ, OFF_HI, OFF_HG = 3072, 4096, 5120
OFF_GA, OFF_GB, OFF_GC = 6144, 7168, 8192
OFF_SWA = 9216
SWA_GROUP_W = SWA_G * SWA_HD + 2 * SWA_HD
N_PROJ = OFF_SWA + SWA_HKV * SWA_GROUP_W


def _cparams(sem, vmem=VMEM_LIMIT_BYTES):
    return pltpu.CompilerParams(dimension_semantics=sem, vmem_limit_bytes=vmem)


def _layer_norm(x, g, b):
    mu = jnp.mean(x, -1, keepdims=True)
    xc = x - mu
    var = jnp.mean(xc * xc, -1, keepdims=True)
    return xc * lax.rsqrt(var + LN_EPS) * g + b


def _silu(x):
    return x * jax.nn.sigmoid(x)


def _ln_in_kernel(x_ref, g_ref, b_ref, h_ref, hb_ref):
    h = _layer_norm(x_ref[...], g_ref[...], b_ref[...])
    h_ref[...] = h
    hb_ref[...] = h.astype(BF16)


def ln_in(x2, g, b, tm=512):
    T, D = x2.shape
    tm = min(tm, T)
    return pl.pallas_call(
        _ln_in_kernel,
        grid=(T // tm,),
        in_specs=[pl.BlockSpec((tm, D), lambda i: (i, 0)),
                  pl.BlockSpec((1, D), lambda i: (0, 0)),
                  pl.BlockSpec((1, D), lambda i: (0, 0))],
        out_specs=[pl.BlockSpec((tm, D), lambda i: (i, 0)),
                   pl.BlockSpec((tm, D), lambda i: (i, 0))],
        out_shape=[jax.ShapeDtypeStruct((T, D), F32), jax.ShapeDtypeStruct((T, D), BF16)],
        compiler_params=_cparams(("parallel",)),
        name="ln_in",
    )(x2, g.reshape(1, D), b.reshape(1, D))


def _matmul_kernel(x_ref, w_ref, o_ref):
    o_ref[...] = jnp.dot(x_ref[...], w_ref[...], preferred_element_type=F32).astype(o_ref.dtype)


def matmul(x, w, out_dtype, tm=1024, tn=512, name="matmul"):
    T, K = x.shape
    N = w.shape[1]
    tm = min(tm, T)
    return pl.pallas_call(
        _matmul_kernel,
        grid=(T // tm, N // tn),
        in_specs=[pl.BlockSpec((tm, K), lambda i, j: (i, 0)),
                  pl.BlockSpec((K, tn), lambda i, j: (0, j))],
        out_specs=pl.BlockSpec((tm, tn), lambda i, j: (i, j)),
        out_shape=jax.ShapeDtypeStruct((T, N), out_dtype),
        compiler_params=_cparams(("parallel", "parallel")),
        name=name,
    )(x, w)


def _retention_kernel(q_ref, k_ref, v_ref, g_ref, cos_ref, sin_ref, dm_ref, qd_ref, kd_ref, cd_ref,
                      o_ref, st_ref, *, chunk, nchunks):
    st_ref[...] = jnp.zeros_like(st_ref)
    dm = dm_ref[0]
    qd = qd_ref[0]
    kd = kd_ref[0]
    cd = cd_ref[0]

    def body(c, carry):
        r = pl.ds(pl.multiple_of(c * chunk, chunk), chunk)
        cos = cos_ref[r, :]
        sin = sin_ref[r, :]
        q = q_ref[r, :].astype(F32)
        k = k_ref[r, :].astype(F32)
        qr = q * cos + pltpu.roll(q, RET_DK // 2, 1) * sin
        kr = k * cos + pltpu.roll(k, RET_DK // 2, 1) * sin
        v = v_ref[r, :]
        s = lax.dot_general(qr.astype(BF16), kr.astype(BF16), (((1,), (1,)), ((), ())),
                            preferred_element_type=F32) * dm
        o = jnp.dot(s.astype(BF16), v, preferred_element_type=F32)
        st = st_ref[...]
        o = o + jnp.dot((qr * qd).astype(BF16), st.astype(BF16), preferred_element_type=F32)
        upd = lax.dot_general((kr * kd).astype(BF16), v, (((0,), (0,)), ((), ())),
                              preferred_element_type=F32)
        st_ref[...] = st * cd + upd
        on = o * lax.rsqrt(jnp.mean(o * o, -1, keepdims=True) + RMS_EPS)
        g = g_ref[r, :].astype(F32)
        o_ref[r, :] = (_silu(g) * on).astype(o_ref.dtype)
        return carry

    lax.fori_loop(0, nchunks, body, 0)


def retention_mixer(proj, tabs, B, S):
    T = B * S
    C = RET_CHUNK
    nq, nv = OFF_RQ // RET_DK, OFF_RV // RET_DV
    nk, ng = OFF_RK // RET_DK, OFF_RG // RET_DV
    kern = functools.partial(_retention_kernel, chunk=C, nchunks=S // C)
    return pl.pallas_call(
        kern,
        grid=(B, RET_HEADS),
        in_specs=[pl.BlockSpec((S, RET_DK), lambda b, h: (b, nq + h)),
                  pl.BlockSpec((S, RET_DK), lambda b, h: (b, nk + h)),
                  pl.BlockSpec((S, RET_DV), lambda b, h: (b, nv + h)),
                  pl.BlockSpec((S, RET_DV), lambda b, h: (b, ng + h)),
                  pl.BlockSpec((S, RET_DK), lambda b, h: (0, 0)),
                  pl.BlockSpec((S, RET_DK), lambda b, h: (0, 0)),
                  pl.BlockSpec((1, C, C), lambda b, h: (h, 0, 0)),
                  pl.BlockSpec((1, C, RET_DK), lambda b, h: (h, 0, 0)),
                  pl.BlockSpec((1, C, RET_DK), lambda b, h: (h, 0, 0)),
                  pl.BlockSpec((1, 1, RET_DV), lambda b, h: (h, 0, 0))],
        out_specs=pl.BlockSpec((S, RET_DV), lambda b, h: (b, h)),
        out_shape=jax.ShapeDtypeStruct((T, RET_V), BF16),
        scratch_shapes=[pltpu.VMEM((RET_DK, RET_DV), F32)],
        compiler_params=_cparams(("parallel", "parallel")),
        name="retention",
    )(proj, proj, proj, proj, tabs["cos"], tabs["sin"], tabs["dm"], tabs["qd"], tabs["kd"], tabs["cd"])


def retention_tables(S):
    C = RET_CHUNK
    half = RET_DK // 2
    pos = jnp.arange(S, dtype=F32)
    inv = 1.0 / (ROPE_BASE ** jnp.linspace(0.0, 1.0, half, dtype=F32))
    ang = pos[:, None] * inv[None, :]
    cos, sin = jnp.cos(ang), jnp.sin(ang)
    log_gamma = jnp.log(1.0 - 2.0 ** (-5.0 - jnp.arange(RET_HEADS, dtype=F32)))
    idx = jnp.arange(C, dtype=F32)
    diff = idx[:, None] - idx[None, :]
    decay = jnp.where(diff[None] >= 0,
                      jnp.exp(jnp.maximum(diff, 0.0)[None] * log_gamma[:, None, None]), 0.0)
    scale = RET_DK ** -0.5
    qd = jnp.exp((idx + 1.0)[None, :] * log_gamma[:, None])
    kd = jnp.exp((C - 1.0 - idx)[None, :] * log_gamma[:, None]) * scale
    cd = jnp.exp(C * log_gamma)
    return {
        "cos": jnp.concatenate([cos, cos], axis=1),
        "sin": jnp.concatenate([-sin, sin], axis=1),
        "dm": decay * scale,
        "qd": jnp.broadcast_to(qd[:, :, None], (RET_HEADS, C, RET_DK)),
        "kd": jnp.broadcast_to(kd[:, :, None], (RET_HEADS, C, RET_DK)),
        "cd": jnp.broadcast_to(cd[:, None, None], (RET_HEADS, 1, RET_DV)),
    }


def _hgrn_kernel(q_ref, z_ref, i_ref, g_ref, lb_ref, ng_ref, o_ref, st_ref, *, chunk, nchunks):
    st_ref[...] = jnp.zeros_like(st_ref)
    lb = lb_ref[0]
    ng = ng_ref[...]
    half = chunk // 2
    row = lax.broadcasted_iota(jnp.int32, (chunk, HG_DK), 0)
    causal = (lax.broadcasted_iota(jnp.int32, (chunk, chunk), 0)
              >= lax.broadcasted_iota(jnp.int32, (chunk, chunk), 1))

    def body(c, carry):
        r = pl.ds(pl.multiple_of(c * chunk, chunk), chunk)
        f = lb + (1.0 - lb) * jax.nn.sigmoid(z_ref[r, :])
        cum = jnp.log(f)
        shift = 1
        while shift < chunk:
            cum = cum + jnp.where(row >= shift, pltpu.roll(cum, shift, 0), 0.0)
            shift *= 2
        mid = cum[half - 1:half, :]
        last = cum[chunk - 1:chunk, :]
        qh = q_ref[r, :].astype(F32) * jnp.exp(cum - mid)
        kh = (1.0 - f) * jnp.exp(mid - cum)
        v = i_ref[r, :]
        a = lax.dot_general(qh.astype(BF16), kh.astype(BF16), (((1,), (1,)), ((), ())),
                            preferred_element_type=F32)
        a = jnp.where(causal, a, 0.0)
        o = jnp.dot(a.astype(BF16), v, preferred_element_type=F32)
        st = st_ref[...]
        q_in = qh * jnp.exp(mid)
        o = o + lax.dot_general(q_in.astype(BF16), st.astype(BF16), (((1,), (1,)), ((), ())),
                                preferred_element_type=F32)
        k_st = kh * jnp.exp(last - mid)
        upd = lax.dot_general(v, k_st.astype(BF16), (((0,), (0,)), ((), ())),
                              preferred_element_type=F32)
        st_ref[...] = st * jnp.exp(last) + upd
        on = o * lax.rsqrt(jnp.mean(o * o, -1, keepdims=True) + RMS_EPS) * ng
        g = g_ref[r, :].astype(F32)
        o_ref[r, :] = (_silu(g) * on).astype(o_ref.dtype)
        return carry

    lax.fori_loop(0, nchunks, body, 0)


def hgrn_mixer(proj, hf, lb, norm_g, B, S):
    T = B * S
    C = HG_CHUNK
    nq, ni, ng = OFF_HQ // HG_DK, OFF_HI // HG_DV, OFF_HG // HG_DV
    kern = functools.partial(_hgrn_kernel, chunk=C, nchunks=S // C)
    return pl.pallas_call(
        kern,
        grid=(B, HG_HEADS),
        in_specs=[pl.BlockSpec((S, HG_DK), lambda b, h: (b, nq + h)),
                  pl.BlockSpec((S, HG_DK), lambda b, h: (b, h)),
                  pl.BlockSpec((S, HG_DV), lambda b, h: (b, ni + h)),
                  pl.BlockSpec((S, HG_DV), lambda b, h: (b, ng + h)),
                  pl.BlockSpec((1, 1, HG_DK), lambda b, h: (h, 0, 0)),
                  pl.BlockSpec((1, HG_DV), lambda b, h: (0, 0))],
        out_specs=pl.BlockSpec((S, HG_DV), lambda b, h: (b, h)),
        out_shape=jax.ShapeDtypeStruct((T, HG_V), BF16),
        scratch_shapes=[pltpu.VMEM((HG_DV, HG_DK), F32)],
        compiler_params=_cparams(("parallel", "parallel")),
        name="hgrn2",
    )(proj, hf, proj, proj, lb.reshape(HG_HEADS, 1, HG_DK), norm_g.reshape(1, HG_DV))


def _swa_kernel(sink_ref, x_ref, o_ref, *, nchunks):
    kvh = pl.program_id(1)
    CH = SWA_CHUNK
    WIN = (SWA_WIN_CHUNKS + 1) * CH
    QW = SWA_G * SWA_HD
    col = lax.broadcasted_iota(jnp.int32, (CH, WIN), 1)
    scale = SWA_HD ** -0.5

    def body(c, carry):
        first = jnp.maximum(c - SWA_WIN_CHUNKS, 0)
        rq = pl.ds(pl.multiple_of(c * CH, CH), CH)
        rk = pl.ds(pl.multiple_of(first * CH, CH), WIN)
        q = x_ref[rq, 0:QW]
        kv = x_ref[rk, QW:QW + 2 * SWA_HD]
        k = kv[:, 0:SWA_HD]
        v = kv[:, SWA_HD:2 * SWA_HD]
        valid = col < (c + 1 - first) * CH
        outs = []
        for g in range(SWA_G):
            sink = sink_ref[kvh * SWA_G + g]
            s = lax.dot_general(q[:, g * SWA_HD:(g + 1) * SWA_HD], k, (((1,), (1,)), ((), ())),
                                preferred_element_type=F32) * scale
            s = jnp.where(valid, s, -jnp.inf)
            m = jnp.maximum(jnp.max(s, -1, keepdims=True), sink)
            p = jnp.exp(s - m)
            den = jnp.sum(p, -1, keepdims=True) + jnp.exp(sink - m)
            og = jnp.dot(p.astype(BF16), v, preferred_element_type=F32)
            outs.append(og / den)
        o_ref[rq, :] = jnp.concatenate(outs, axis=1).astype(o_ref.dtype)
        return carry

    lax.fori_loop(0, nchunks, body, 0)


def swa_mixer(proj, sinks, B, S):
    T = B * S
    nb = OFF_SWA // SWA_GROUP_W
    kern = functools.partial(_swa_kernel, nchunks=S // SWA_CHUNK)
    return pl.pallas_call(
        kern,
        grid=(B, SWA_HKV),
        in_specs=[pl.BlockSpec(memory_space=pltpu.SMEM),
                  pl.BlockSpec((S, SWA_GROUP_W), lambda b, h: (b, nb + h))],
        out_specs=pl.BlockSpec((S, SWA_G * SWA_HD), lambda b, h: (b, h)),
        out_shape=jax.ShapeDtypeStruct((T, SWA_Q), BF16),
        compiler_params=_cparams(("parallel", "parallel")),
        name="swa",
    )(sinks.astype(F32), proj)


def _merge_kernel(a_ref, b_ref, c_ref, ga_ref, gb_ref, gc_ref, h_ref, wa_ref, wb_ref, wc_ref, wo_ref,
                  g_ref, beta_ref, ho_ref, hob_ref):
    def branch(x_ref, gate_ref, w_ref):
        y = jnp.dot(x_ref[...], w_ref[...], preferred_element_type=F32)
        return jax.nn.sigmoid(gate_ref[...].astype(F32)) * y

    merged = branch(a_ref, ga_ref, wa_ref) + branch(b_ref, gb_ref, wb_ref) + branch(c_ref, gc_ref, wc_ref)
    mix = jnp.dot(merged.astype(BF16), wo_ref[...], preferred_element_type=F32)
    hn = _layer_norm(DN_ALPHA * h_ref[...] + mix, g_ref[...], beta_ref[...])
    ho_ref[...] = hn
    hob_ref[...] = hn.astype(BF16)


def merge_project(ya, yb, yc, proj, h, wa, wb, wc, wo, g, beta, tm=256):
    T, D = h.shape
    tm = min(tm, T)
    row = lambda i: (i, 0)
    const = lambda i: (0, 0)
    wspec = pl.BlockSpec((D, D), const)
    return pl.pallas_call(
        _merge_kernel,
        grid=(T // tm,),
        in_specs=[pl.BlockSpec((tm, D), row), pl.BlockSpec((tm, D), row), pl.BlockSpec((tm, D), row),
                  pl.BlockSpec((tm, D), lambda i: (i, OFF_GA // D_MODEL)),
                  pl.BlockSpec((tm, D), lambda i: (i, OFF_GB // D_MODEL)),
                  pl.BlockSpec((tm, D), lambda i: (i, OFF_GC // D_MODEL)),
                  pl.BlockSpec((tm, D), row),
                  wspec, wspec, wspec, wspec,
                  pl.BlockSpec((1, D), const), pl.BlockSpec((1, D), const)],
        out_specs=[pl.BlockSpec((tm, D), row), pl.BlockSpec((tm, D), row)],
        out_shape=[jax.ShapeDtypeStruct((T, D), F32), jax.ShapeDtypeStruct((T, D), BF16)],
        compiler_params=_cparams(("parallel",)),
        name="merge_project",
    )(ya, yb, yc, proj, proj, proj, h, wa, wb, wc, wo, g.reshape(1, D), beta.reshape(1, D))


def _swiglu_step(x, wg_ref, wu_ref, wd_ref, acc_ref):
    g = jnp.dot(x, wg_ref[...], preferred_element_type=F32)
    u = jnp.dot(x, wu_ref[...], preferred_element_type=F32)
    a = (_silu(g) * u).astype(BF16)
    acc_ref[...] += jnp.dot(a, wd_ref[...], preferred_element_type=F32)


def _ffn_kernel(xb_ref, h_ref, wg_ref, wu_ref, wd_ref, g_ref, beta_ref, ho_ref, hob_ref, acc_ref):
    f = pl.program_id(1)

    @pl.when(f == 0)
    def _():
        acc_ref[...] = jnp.zeros_like(acc_ref)

    _swiglu_step(xb_ref[...], wg_ref, wu_ref, wd_ref, acc_ref)

    @pl.when(f == pl.num_programs(1) - 1)
    def _():
        hn = _layer_norm(DN_ALPHA * h_ref[...] + acc_ref[...], g_ref[...], beta_ref[...])
        ho_ref[...] = hn
        hob_ref[...] = hn.astype(BF16)


def dense_ffn(hb, h, wg, wu, wd, g, beta, tm=512, tf=256):
    T, D = h.shape
    F = wg.shape[1]
    tm = min(tm, T)
    row = lambda i, f: (i, 0)
    const = lambda i, f: (0, 0)
    return pl.pallas_call(
        _ffn_kernel,
        grid=(T // tm, F // tf),
        in_specs=[pl.BlockSpec((tm, D), row), pl.BlockSpec((tm, D), row),
                  pl.BlockSpec((D, tf), lambda i, f: (0, f)),
                  pl.BlockSpec((D, tf), lambda i, f: (0, f)),
                  pl.BlockSpec((tf, D), lambda i, f: (f, 0)),
                  pl.BlockSpec((1, D), const), pl.BlockSpec((1, D), const)],
        out_specs=[pl.BlockSpec((tm, D), row), pl.BlockSpec((tm, D), row)],
        out_shape=[jax.ShapeDtypeStruct((T, D), F32), jax.ShapeDtypeStruct((T, D), BF16)],
        scratch_shapes=[pltpu.VMEM((tm, D), F32)],
        compiler_params=_cparams(("parallel", "arbitrary")),
        name="dense_ffn",
    )(hb, h, wg, wu, wd, g.reshape(1, D), beta.reshape(1, D))


def _router_kernel(h_ref, w_ref, o_ref):
    logits = jnp.dot(h_ref[...], w_ref[...], preferred_element_type=F32, precision=lax.Precision.HIGHEST)
    lane = lax.broadcasted_iota(jnp.int32, logits.shape, 1)
    l1 = jnp.where(lane < N_EXPERTS, logits, -jnp.inf)
    m1 = jnp.max(l1, -1, keepdims=True)
    i1 = jnp.min(jnp.where(l1 == m1, lane, LANES), -1, keepdims=True)
    l2 = jnp.where(lane == i1, -jnp.inf, l1)
    m2 = jnp.max(l2, -1, keepdims=True)
    i2 = jnp.min(jnp.where(l2 == m2, lane, LANES), -1, keepdims=True)
    e = jnp.exp(m2 - m1)
    w1 = 1.0 / (1.0 + e)
    w2 = e / (1.0 + e)
    out = jnp.where(lane == 0, i1.astype(F32),
                    jnp.where(lane == 1, i2.astype(F32),
                              jnp.where(lane == 2, w1, jnp.where(lane == 3, w2, 0.0))))
    o_ref[...] = out


def router_top2(h, w_router, tm=512):
    T, D = h.shape
    tm = min(tm, T)
    wpad = jnp.zeros((D, LANES), F32).at[:, :N_EXPERTS].set(w_router.astype(F32))
    return pl.pallas_call(
        _router_kernel,
        grid=(T // tm,),
        in_specs=[pl.BlockSpec((tm, D), lambda i: (i, 0)), pl.BlockSpec((D, LANES), lambda i: (0, 0))],
        out_specs=pl.BlockSpec((tm, LANES), lambda i: (i, 0)),
        out_shape=jax.ShapeDtypeStruct((T, LANES), F32),
        compiler_params=_cparams(("parallel",)),
        name="router_top2",
    )(h, wpad)


SUB = 8


def _expert_ffn_kernel(te_ref, nt_ref, src_ref, dst_ref, h_hbm, wg_ref, wu_ref, wd_ref, y_hbm,
                       xbuf, xb_ref, ybuf, acc_ref, gsem, ssem, *, tm):
    i = pl.program_id(0)
    f = pl.program_id(1)
    n_i = pl.num_programs(0)
    last = pl.num_programs(1) - 1
    active = i < nt_ref[0]

    def start_gather(tile, buf):
        def body(r, carry):
            tok = src_ref[tile * tm + r]
            pltpu.make_async_copy(h_hbm.at[pl.ds(pl.multiple_of(tok * SUB, SUB), SUB), :],
                                  xbuf.at[buf, pl.ds(pl.multiple_of(r * SUB, SUB), SUB), :],
                                  gsem.at[buf]).start()
            return carry
        lax.fori_loop(0, tm, body, 0, unroll=8)

    def wait_gather(buf):
        pltpu.make_async_copy(h_hbm.at[pl.ds(0, tm * SUB), :], xbuf.at[buf], gsem.at[buf]).wait()

    def start_scatter():
        def body(r, carry):
            slot = dst_ref[i * tm + r]
            pltpu.make_async_copy(ybuf.at[pl.ds(pl.multiple_of(r * SUB, SUB), SUB), :],
                                  y_hbm.at[pl.ds(pl.multiple_of(slot * SUB, SUB), SUB), :], ssem).start()
            return carry
        lax.fori_loop(0, tm, body, 0, unroll=8)

    def wait_scatter():
        pltpu.make_async_copy(ybuf, y_hbm.at[pl.ds(0, tm * SUB), :], ssem).wait()

    @pl.when(f == 0)
    def _():
        buf = i % 2

        @pl.when(i == 0)
        def _():
            start_gather(0, 0)

        wait_gather(buf)

        @pl.when(i + 1 < n_i)
        def _():
            start_gather(i + 1, 1 - buf)

        for j in range(SUB):
            xb_ref[:, j * LANES:(j + 1) * LANES] = xbuf[buf, pl.ds(j, tm, stride=SUB), :].astype(BF16)

        @pl.when(i > 0)
        def _():
            wait_scatter()

        acc_ref[...] = jnp.zeros_like(acc_ref)

    @pl.when(active)
    def _():
        _swiglu_step(xb_ref[...], wg_ref, wu_ref, wd_ref, acc_ref)

    @pl.when(f == last)
    def _():
        for j in range(SUB):
            ybuf[pl.ds(j, tm, stride=SUB), :] = acc_ref[:, j * LANES:(j + 1) * LANES]
        start_scatter()

        @pl.when(i == n_i - 1)
        def _():
            wait_scatter()


def expert_ffn(h_tiles, te, nt, src_tok, dst_slot, wg, wu, wd, tm, tf=512):
    P = src_tok.shape[0]
    D = wg.shape[1]
    F = wg.shape[2]
    nf = F // tf

    def wcol(i, f, te_ref, nt_ref, src_ref, dst_ref):
        return (te_ref[i], 0, jnp.where(i < nt_ref[0], f, nf - 1))

    def wrow(i, f, te_ref, nt_ref, src_ref, dst_ref):
        return (te_ref[i], jnp.where(i < nt_ref[0], f, nf - 1), 0)

    kern = functools.partial(_expert_ffn_kernel, tm=tm)
    return pl.pallas_call(
        kern,
        grid_spec=pltpu.PrefetchScalarGridSpec(
            num_scalar_prefetch=4,
            grid=(P // tm, nf),
            in_specs=[pl.BlockSpec(memory_space=pl.ANY),
                      pl.BlockSpec((None, D, tf), wcol),
                      pl.BlockSpec((None, D, tf), wcol),
                      pl.BlockSpec((None, tf, D), wrow)],
            out_specs=pl.BlockSpec(memory_space=pl.ANY),
            scratch_shapes=[pltpu.VMEM((2, tm * SUB, LANES), F32),
                            pltpu.VMEM((tm, D), BF16),
                            pltpu.VMEM((tm * SUB, LANES), F32),
                            pltpu.VMEM((tm, D), F32),
                            pltpu.SemaphoreType.DMA((2,)),
                            pltpu.SemaphoreType.DMA(())],
        ),
        out_shape=jax.ShapeDtypeStruct((P * SUB, LANES), F32),
        compiler_params=_cparams(("arbitrary", "arbitrary")),
        name="expert_ffn",
    )(te, nt, src_tok, dst_slot, h_tiles, wg, wu, wd)


def _moe_combine_kernel(y0_ref, y1_ref, r_ref, h_ref, g_ref, beta_ref, ho_ref, hob_ref, *, tm):
    def rows(y_ref):
        return jnp.concatenate([y_ref[pl.ds(j, tm, stride=SUB), :] for j in range(SUB)], axis=1)

    r = r_ref[...]
    ff = rows(y0_ref) * r[:, 2:3] + rows(y1_ref) * r[:, 3:4]
    hn = _layer_norm(DN_ALPHA * h_ref[...] + ff, g_ref[...], beta_ref[...])
    ho_ref[...] = hn
    hob_ref[...] = hn.astype(BF16)


def moe_combine(y_tiles, route, h, g, beta, tm=512):
    T, D = h.shape
    tm = min(tm, T)
    nb = T // tm
    row = lambda i: (i, 0)
    const = lambda i: (0, 0)
    kern = functools.partial(_moe_combine_kernel, tm=tm)
    return pl.pallas_call(
        kern,
        grid=(nb,),
        in_specs=[pl.BlockSpec((tm * SUB, LANES), lambda i: (i, 0)),
                  pl.BlockSpec((tm * SUB, LANES), lambda i: (nb + i, 0)),
                  pl.BlockSpec((tm, LANES), row),
                  pl.BlockSpec((tm, D), row),
                  pl.BlockSpec((1, D), const), pl.BlockSpec((1, D), const)],
        out_specs=[pl.BlockSpec((tm, D), row), pl.BlockSpec((tm, D), row)],
        out_shape=[jax.ShapeDtypeStruct((T, D), F32), jax.ShapeDtypeStruct((T, D), BF16)],
        compiler_params=_cparams(("parallel",)),
        name="moe_combine",
    )(y_tiles, y_tiles, route, h, g.reshape(1, D), beta.reshape(1, D))


def moe_plan(route, T, tm):
    i32 = jnp.int32
    n_slots = TOP_K * T
    e_flat = route[:, 0:TOP_K].astype(i32).T.reshape(-1)
    onehot = (e_flat[:, None] == jnp.arange(N_EXPERTS, dtype=i32)[None, :]).astype(i32)
    csum = jnp.cumsum(onehot, axis=0)
    rank = jnp.sum(onehot * (csum - 1), axis=1)
    sizes = csum[-1]
    padded = ((sizes + tm - 1) // tm) * tm
    gend = jnp.cumsum(padded)
    gstart = gend - padded
    pos = gstart[e_flat] + rank
    n_rows = n_slots + N_EXPERTS * tm
    slot_plus1 = jnp.zeros((n_rows,), i32).at[pos].set(jnp.arange(1, n_slots + 1, dtype=i32))
    is_pad = slot_plus1 == 0
    pad_rank = jnp.cumsum(is_pad.astype(i32)) - 1
    dst_slot = jnp.where(is_pad, n_slots + pad_rank, slot_plus1 - 1)
    src_tok = jnp.where(is_pad, 0, (slot_plus1 - 1) % T)
    n_tiles = n_rows // tm
    nt = (gend[-1] // tm).astype(i32)
    tile_start = jnp.arange(n_tiles, dtype=i32) * tm
    te = jnp.minimum(jnp.sum((tile_start[:, None] >= gend[None, :]).astype(i32), axis=1), N_EXPERTS - 1)
    te = jnp.where(jnp.arange(n_tiles) < nt, te, te[jnp.maximum(nt - 1, 0)])
    return src_tok.astype(i32), dst_slot.astype(i32), te.astype(i32), nt.reshape(1)


def moe_ffn(h, w_router, wg, wu, wd, g, beta, tm=512):
    T, D = h.shape
    route = router_top2(h, w_router)
    src_tok, dst_slot, te, nt = moe_plan(route, T, tm)
    y_tiles = expert_ffn(h.reshape(T * SUB, LANES), te, nt, src_tok, dst_slot, wg, wu, wd, tm)
    return moe_combine(y_tiles, route, h, g, beta)


def permute_w_in(w):
    D = w.shape[0]
    cuts = np.cumsum([RET_QK, RET_QK, RET_V, RET_V, HG_K, HG_K, HG_V, HG_V, SWA_Q, SWA_KV, SWA_KV,
                      D_MODEL, D_MODEL])
    rq, rk, rv, rg, hq, hf, hi, hg, sq, sk, sv, ga, gb, gc = jnp.split(w, cuts.tolist(), axis=1)
    swa = jnp.concatenate([sq.reshape(D, SWA_HKV, SWA_G * SWA_HD),
                           sk.reshape(D, SWA_HKV, SWA_HD),
                           sv.reshape(D, SWA_HKV, SWA_HD)], axis=2).reshape(D, SWA_HKV * SWA_GROUP_W)
    main = jnp.concatenate([rq, rk, rv, rg, hq, hi, hg, ga, gb, gc, swa], axis=1)
    return main.astype(BF16), hf.astype(BF16)


def kernel(x, ln_in_g, ln_in_b, w_in, ret_w_out, hgrn_lower_bounds, hgrn_norm_g, hgrn_w_out, swa_sinks,
           swa_w_out, w_o, ln_mix_g, ln_mix_b, ffn_w_gate, ffn_w_up, ffn_w_down, moe_router, moe_w_gate,
           moe_w_up, moe_w_down, ln_ffn_g, ln_ffn_b):
    B, S, D = x.shape
    T = B * S
    assert D == D_MODEL and S % RET_CHUNK == 0 and S % SWA_CHUNK == 0 and S >= 3 * SWA_CHUNK

    lb_all = jnp.cumsum(jax.nn.softmax(hgrn_lower_bounds.astype(F32), axis=0), axis=0)
    lb_all = lb_all - lb_all[0]
    tabs = retention_tables(S)

    h, hb = ln_in(x.reshape(T, D), ln_in_g, ln_in_b)
    for layer in range(DEPTH):
        w_main, w_hf = permute_w_in(w_in[layer])
        proj = matmul(hb, w_main, BF16, name="in_proj")
        hf = matmul(hb, w_hf, F32, name="in_proj_forget")

        ya = retention_mixer(proj, tabs, B, S)
        yb = hgrn_mixer(proj, hf, lb_all[layer], hgrn_norm_g[layer], B, S)
        yc = swa_mixer(proj, swa_sinks[layer], B, S)
        h, hb = merge_project(ya, yb, yc, proj, h,
                              ret_w_out[layer].astype(BF16), hgrn_w_out[layer].astype(BF16),
                              swa_w_out[layer].astype(BF16), w_o[layer].astype(BF16),
                              ln_mix_g[layer], ln_mix_b[layer])

        j = layer // 2
        if layer % 2 == 0:
            h, hb = dense_ffn(hb, h, ffn_w_gate[j].astype(BF16), ffn_w_up[j].astype(BF16),
                              ffn_w_down[j].astype(BF16), ln_ffn_g[layer], ln_ffn_b[layer])
        else:
            h, hb = moe_ffn(h, moe_router[j], moe_w_gate[j].astype(BF16), moe_w_up[j].astype(BF16),
                            moe_w_down[j].astype(BF16), ln_ffn_g[layer], ln_ffn_b[layer])
    return h.reshape(B, S, D)
```

```python
import functools

import numpy as np
import jax
import jax.numpy as jnp
from jax import lax
from jax.experimental import pallas as pl
from jax.experimental.pallas import tpu as pltpu

F32 = jnp.float32
BF16 = jnp.bfloat16

D_MODEL = 1024
RET_HEADS, RET_DK, RET_DV = 4, 128, 256
RET_QK, RET_V = RET_HEADS * RET_DK, RET_HEADS * RET_DV
ROPE_BASE = 10000.0
HG_HEADS, HG_DK, HG_DV = 8, 128, 128
HG_K, HG_V = HG_HEADS * HG_DK, HG_HEADS * HG_DV
SWA_HQ, SWA_HKV, SWA_HD = 16, 4, 64
SWA_G = SWA_HQ // SWA_HKV
SWA_Q, SWA_KV = SWA_HQ * SWA_HD, SWA_HKV * SWA_HD
SWA_CHUNK = 64
SWA_WIN_CHUNKS = 2
N_EXPERTS, TOP_K = 8, 2
DEPTH = 2
LN_EPS, RMS_EPS = 1e-5, 1e-6
DN_ALPHA = (2.0 * DEPTH) ** 0.25

VMEM_LIMIT_BYTES = 56 * 1024 * 1024
LANES = 128

RET_CHUNK = 128
HG_CHUNK = 32
HG_BLOCK = 256

OFF_RQ, OFF_RK, OFF_RV, OFF_RG = 0, 512, 1024, 2048
OFF_HQ, OFF_HI, OFF_HG = 3072, 4096, 5120
OFF_GA, OFF_GB, OFF_GC = 6144, 7168, 8192
OFF_SWA = 9216
SWA_GROUP_W = SWA_G * SWA_HD + 4 * SWA_HD
N_PROJ = OFF_SWA + SWA_HKV * SWA_GROUP_W


def _cparams(sem, vmem=VMEM_LIMIT_BYTES):
    return pltpu.CompilerParams(dimension_semantics=sem, vmem_limit_bytes=vmem)


def _layer_norm(x, g, b):
    mu = jnp.mean(x, -1, keepdims=True)
    xc = x - mu
    var = jnp.mean(xc * xc, -1, keepdims=True)
    return xc * lax.rsqrt(var + LN_EPS) * g + b


def _silu(x):
    return x * jax.nn.sigmoid(x)


def _ln_in_kernel(x_ref, g_ref, b_ref, h_ref, hb_ref):
    h = _layer_norm(x_ref[...], g_ref[...], b_ref[...])
    h_ref[...] = h
    hb_ref[...] = h.astype(BF16)


def ln_in(x2, g, b, tm=512):
    T, D = x2.shape
    tm = min(tm, T)
    return pl.pallas_call(
        _ln_in_kernel,
        grid=(T // tm,),
        in_specs=[pl.BlockSpec((tm, D), lambda i: (i, 0)),
                  pl.BlockSpec((1, D), lambda i: (0, 0)),
                  pl.BlockSpec((1, D), lambda i: (0, 0))],
        out_specs=[pl.BlockSpec((tm, D), lambda i: (i, 0)),
                   pl.BlockSpec((tm, D), lambda i: (i, 0))],
        out_shape=[jax.ShapeDtypeStruct((T, D), F32), jax.ShapeDtypeStruct((T, D), BF16)],
        compiler_params=_cparams(("parallel",)),
        name="ln_in",
    )(x2, g.reshape(1, D), b.reshape(1, D))


def _matmul_kernel(x_ref, w_ref, o_ref):
    o_ref[...] = jnp.dot(x_ref[...], w_ref[...], preferred_element_type=F32).astype(o_ref.dtype)


def matmul(x, w, out_dtype, tm=1024, tn=512, name="matmul"):
    T, K = x.shape
    N = w.shape[1]
    tm = min(tm, T)
    return pl.pallas_call(
        _matmul_kernel,
        grid=(T // tm, N // tn),
        in_specs=[pl.BlockSpec((tm, K), lambda i, j: (i, 0)),
                  pl.BlockSpec((K, tn), lambda i, j: (0, j))],
        out_specs=pl.BlockSpec((tm, tn), lambda i, j: (i, j)),
        out_shape=jax.ShapeDtypeStruct((T, N), out_dtype),
        compiler_params=_cparams(("parallel", "parallel")),
        name=name,
    )(x, w)


def _retention_kernel(q_ref, k_ref, v_ref, g_ref, cos_ref, sin_ref, dm_ref, qd_ref, kd_ref, cd_ref,
                      o_ref, st_ref, *, chunk, nchunks):
    st_ref[...] = jnp.zeros_like(st_ref)
    dm = dm_ref[0]
    qd = qd_ref[0]
    kd = kd_ref[0]
    cd = cd_ref[0]

    def body(c, carry):
        r = pl.ds(pl.multiple_of(c * chunk, chunk), chunk)
        cos = cos_ref[r, :]
        sin = sin_ref[r, :]
        q = q_ref[r, :].astype(F32)
        k = k_ref[r, :].astype(F32)
        qr = q * cos + pltpu.roll(q, RET_DK // 2, 1) * sin
        kr = k * cos + pltpu.roll(k, RET_DK // 2, 1) * sin
        v = v_ref[r, :]
        s = lax.dot_general(qr.astype(BF16), kr.astype(BF16), (((1,), (1,)), ((), ())),
                            preferred_element_type=F32) * dm
        o = jnp.dot(s.astype(BF16), v, preferred_element_type=F32)
        st = st_ref[...]
        o = o + jnp.dot((qr * qd).astype(BF16), st.astype(BF16), preferred_element_type=F32)
        upd = lax.dot_general((kr * kd).astype(BF16), v, (((0,), (0,)), ((), ())),
                              preferred_element_type=F32)
        st_ref[...] = st * cd + upd
        on = o * lax.rsqrt(jnp.mean(o * o, -1, keepdims=True) + RMS_EPS)
        g = g_ref[r, :].astype(F32)
        o_ref[r, :] = (_silu(g) * on).astype(o_ref.dtype)
        return carry

    lax.fori_loop(0, nchunks, body, 0)


def retention_mixer(proj, tabs, B, S):
    T = B * S
    C = RET_CHUNK
    nq, nv = OFF_RQ // RET_DK, OFF_RV // RET_DV
    nk, ng = OFF_RK // RET_DK, OFF_RG // RET_DV
    kern = functools.partial(_retention_kernel, chunk=C, nchunks=S // C)
    return pl.pallas_call(
        kern,
        grid=(B, RET_HEADS),
        in_specs=[pl.BlockSpec((S, RET_DK), lambda b, h: (b, nq + h)),
                  pl.BlockSpec((S, RET_DK), lambda b, h: (b, nk + h)),
                  pl.BlockSpec((S, RET_DV), lambda b, h: (b, nv + h)),
                  pl.BlockSpec((S, RET_DV), lambda b, h: (b, ng + h)),
                  pl.BlockSpec((S, RET_DK), lambda b, h: (0, 0)),
                  pl.BlockSpec((S, RET_DK), lambda b, h: (0, 0)),
                  pl.BlockSpec((1, C, C), lambda b, h: (h, 0, 0)),
                  pl.BlockSpec((1, C, RET_DK), lambda b, h: (h, 0, 0)),
                  pl.BlockSpec((1, C, RET_DK), lambda b, h: (h, 0, 0)),
                  pl.BlockSpec((1, 1, RET_DV), lambda b, h: (h, 0, 0))],
        out_specs=pl.BlockSpec((S, RET_DV), lambda b, h: (b, h)),
        out_shape=jax.ShapeDtypeStruct((T, RET_V), BF16),
        scratch_shapes=[pltpu.VMEM((RET_DK, RET_DV), F32)],
        compiler_params=_cparams(("parallel", "parallel")),
        name="retention",
    )(proj, proj, proj, proj, tabs["cos"], tabs["sin"], tabs["dm"], tabs["qd"], tabs["kd"], tabs["cd"])


def retention_tables(S):
    C = RET_CHUNK
    half = RET_DK // 2
    pos = jnp.arange(S, dtype=F32)
    inv = 1.0 / (ROPE_BASE ** jnp.linspace(0.0, 1.0, half, dtype=F32))
    ang = pos[:, None] * inv[None, :]
    cos, sin = jnp.cos(ang), jnp.sin(ang)
    log_gamma = jnp.log(1.0 - 2.0 ** (-5.0 - jnp.arange(RET_HEADS, dtype=F32)))
    idx = jnp.arange(C, dtype=F32)
    diff = idx[:, None] - idx[None, :]
    decay = jnp.where(diff[None] >= 0,
                      jnp.exp(jnp.maximum(diff, 0.0)[None] * log_gamma[:, None, None]), 0.0)
    scale = RET_DK ** -0.5
    qd = jnp.exp((idx + 1.0)[None, :] * log_gamma[:, None])
    kd = jnp.exp((C - 1.0 - idx)[None, :] * log_gamma[:, None]) * scale
    cd = jnp.exp(C * log_gamma)
    return {
        "cos": jnp.concatenate([cos, cos], axis=1),
        "sin": jnp.concatenate([-sin, sin], axis=1),
        "dm": decay * scale,
        "qd": jnp.broadcast_to(qd[:, :, None], (RET_HEADS, C, RET_DK)),
        "kd": jnp.broadcast_to(kd[:, :, None], (RET_HEADS, C, RET_DK)),
        "cd": jnp.broadcast_to(cd[:, None, None], (RET_HEADS, 1, RET_DV)),
    }


def _hgrn_kernel(q_ref, z_ref, i_ref, g_ref, lb_ref, ng_ref, o_ref, st_ref, *, chunk, block, nblocks):
    st_ref[...] = jnp.zeros_like(st_ref)
    lb = lb_ref[0]
    ng = ng_ref[...]
    half = chunk // 2
    nc = block // chunk
    row = lax.broadcasted_iota(jnp.int32, (block, HG_DK), 0) % chunk
    causal = (lax.broadcasted_iota(jnp.int32, (chunk, chunk), 0)
              >= lax.broadcasted_iota(jnp.int32, (chunk, chunk), 1))
    nt_dims = (((1,), (1,)), ((), ()))
    tn_dims = (((0,), (0,)), ((), ()))

    def per_chunk(x, lo):
        return jnp.concatenate(
            [jnp.broadcast_to(x[c * chunk + lo:c * chunk + lo + 1, :], (chunk, x.shape[1])) for c in range(nc)],
            axis=0)

    def body(bi, carry):
        r = pl.ds(pl.multiple_of(bi * block, block), block)
        f = lb + (1.0 - lb) * jax.nn.sigmoid(z_ref[r, :])
        cum = jnp.log(f)
        shift = 1
        while shift < chunk:
            cum = cum + jnp.where(row >= shift, pltpu.roll(cum, shift, 0), 0.0)
            shift *= 2
        mid = per_chunk(cum, half - 1)
        last = per_chunk(cum, chunk - 1)
        qh = q_ref[r, :].astype(F32) * jnp.exp(cum - mid)
        kh = (1.0 - f) * jnp.exp(mid - cum)
        q_in = (qh * jnp.exp(mid)).astype(BF16)
        k_st = (kh * jnp.exp(last - mid)).astype(BF16)
        dec = jnp.exp(last)
        qh = qh.astype(BF16)
        kh = kh.astype(BF16)
        v = i_ref[r, :]
        st = st_ref[...]
        outs = []
        for c in range(nc):
            rc = slice(c * chunk, (c + 1) * chunk)
            a = lax.dot_general(qh[rc], kh[rc], nt_dims, preferred_element_type=F32)
            a = jnp.where(causal, a, 0.0).astype(BF16)
            o = jnp.dot(a, v[rc], preferred_element_type=F32)
            o = o + lax.dot_general(q_in[rc], st.astype(BF16), nt_dims, preferred_element_type=F32)
            upd = lax.dot_general(v[rc], k_st[rc], tn_dims, preferred_element_type=F32)
            st = st * dec[c * chunk:c * chunk + 1, :] + upd
            outs.append(o)
        st_ref[...] = st
        o = jnp.concatenate(outs, axis=0)
        on = o * lax.rsqrt(jnp.mean(o * o, -1, keepdims=True) + RMS_EPS) * ng
        g = g_ref[r, :].astype(F32)
        o_ref[r, :] = (_silu(g) * on).astype(o_ref.dtype)
        return carry

    lax.fori_loop(0, nblocks, body, 0)


def hgrn_mixer(proj, hf, lb, norm_g, B, S):
    T = B * S
    C = HG_CHUNK
    blk = min(HG_BLOCK, S)
    nq, ni, ng = OFF_HQ // HG_DK, OFF_HI // HG_DV, OFF_HG // HG_DV
    kern = functools.partial(_hgrn_kernel, chunk=C, block=blk, nblocks=S // blk)
    return pl.pallas_call(
        kern,
        grid=(B, HG_HEADS),
        in_specs=[pl.BlockSpec((S, HG_DK), lambda b, h: (b, nq + h)),
                  pl.BlockSpec((S, HG_DK), lambda b, h: (b, h)),
                  pl.BlockSpec((S, HG_DV), lambda b, h: (b, ni + h)),
                  pl.BlockSpec((S, HG_DV), lambda b, h: (b, ng + h)),
                  pl.BlockSpec((1, 1, HG_DK), lambda b, h: (h, 0, 0)),
                  pl.BlockSpec((1, HG_DV), lambda b, h: (0, 0))],
        out_specs=pl.BlockSpec((S, HG_DV), lambda b, h: (b, h)),
        out_shape=jax.ShapeDtypeStruct((T, HG_V), BF16),
        scratch_shapes=[pltpu.VMEM((HG_DV, HG_DK), F32)],
        compiler_params=_cparams(("parallel", "parallel")),
        name="hgrn2",
    )(proj, hf, proj, proj, lb.reshape(HG_HEADS, 1, HG_DK), norm_g.reshape(1, HG_DV))


def _swa_kernel(sink_ref, x_ref, o_ref, *, nsteps):
    kvh = pl.program_id(1)
    CH = SWA_CHUNK
    HD = SWA_HD
    QR = 2 * CH
    KW = 4 * CH
    QW = SWA_G * HD
    scale = HD ** -0.5
    lo_kv = lax.broadcasted_iota(jnp.int32, (KW, 2 * HD), 1) < HD
    lo_o = lax.broadcasted_iota(jnp.int32, (QR, 2 * HD), 1) < HD
    kcol = lax.broadcasted_iota(jnp.int32, (QR, 2 * KW), 1)
    qrow = lax.broadcasted_iota(jnp.int32, (QR, 2 * KW), 0)
    rel = (kcol % KW) // CH - qrow // CH
    nt_dims = (((1,), (1,)), ((), ()))
    zero = jnp.zeros((KW, 2 * HD), BF16)

    def body(j, carry):
        first = jnp.maximum(2 * j - SWA_WIN_CHUNKS, 0)
        rq = pl.ds(pl.multiple_of(j * QR, QR), QR)
        rk = pl.ds(pl.multiple_of(first * CH, CH), KW)
        kv = x_ref[rk, QW:QW + 2 * HD]
        vk = x_ref[rk, QW + 2 * HD:QW + 4 * HD]
        kbd = jnp.concatenate([jnp.where(lo_kv, kv, zero), jnp.where(lo_kv, zero, vk)], axis=0)
        vbd = jnp.concatenate([jnp.where(lo_kv, vk, zero), jnp.where(lo_kv, zero, kv)], axis=0)
        d = rel + (first - 2 * j)
        valid = d * (d + SWA_WIN_CHUNKS) <= 0
        for p in range(SWA_G // 2):
            q = x_ref[rq, p * 2 * HD:(p + 1) * 2 * HD]
            s = lax.dot_general(q, kbd, nt_dims, preferred_element_type=F32) * scale
            s = jnp.where(valid, s, -jnp.inf)
            probs, inv = [], []
            for hh in range(2):
                sink = sink_ref[kvh * SWA_G + 2 * p + hh]
                sh = s[:, hh * KW:(hh + 1) * KW]
                m = jnp.maximum(jnp.max(sh, -1, keepdims=True), sink)
                e = jnp.exp(sh - m)
                den = jnp.sum(e, -1, keepdims=True) + jnp.exp(sink - m)
                probs.append(e.astype(BF16))
                inv.append(1.0 / den)
            o = jnp.dot(jnp.concatenate(probs, axis=1), vbd, preferred_element_type=F32)
            o = o * jnp.where(lo_o, inv[0], inv[1])
            o_ref[rq, p * 2 * HD:(p + 1) * 2 * HD] = o.astype(o_ref.dtype)
        return carry

    lax.fori_loop(0, nsteps, body, 0)


def swa_mixer(proj, sinks, B, S):
    T = B * S
    nb = OFF_SWA // SWA_GROUP_W
    kern = functools.partial(_swa_kernel, nsteps=S // (2 * SWA_CHUNK))
    return pl.pallas_call(
        kern,
        grid=(B, SWA_HKV),
        in_specs=[pl.BlockSpec(memory_space=pltpu.SMEM),
                  pl.BlockSpec((S, SWA_GROUP_W), lambda b, h: (b, nb + h))],
        out_specs=pl.BlockSpec((S, SWA_G * SWA_HD), lambda b, h: (b, h)),
        out_shape=jax.ShapeDtypeStruct((T, SWA_Q), BF16),
        compiler_params=_cparams(("parallel", "parallel")),
        name="swa",
    )(sinks.astype(F32), proj)


def _merge_kernel(a_ref, b_ref, c_ref, ga_ref, gb_ref, gc_ref, h_ref, wa_ref, wb_ref, wc_ref, wo_ref,
                  g_ref, beta_ref, ho_ref, hob_ref):
    def branch(x_ref, gate_ref, w_ref):
        y = jnp.dot(x_ref[...], w_ref[...], preferred_element_type=F32)
        return jax.nn.sigmoid(gate_ref[...].astype(F32)) * y

    merged = branch(a_ref, ga_ref, wa_ref) + branch(b_ref, gb_ref, wb_ref) + branch(c_ref, gc_ref, wc_ref)
    mix = jnp.dot(merged.astype(BF16), wo_ref[...], preferred_element_type=F32)
    hn = _layer_norm(DN_ALPHA * h_ref[...] + mix, g_ref[...], beta_ref[...])
    ho_ref[...] = hn
    hob_ref[...] = hn.astype(BF16)


def merge_project(ya, yb, yc, proj, h, wa, wb, wc, wo, g, beta, tm=256):
    T, D = h.shape
    tm = min(tm, T)
    row = lambda i: (i, 0)
    const = lambda i: (0, 0)
    wspec = pl.BlockSpec((D, D), const)
    return pl.pallas_call(
        _merge_kernel,
        grid=(T // tm,),
        in_specs=[pl.BlockSpec((tm, D), row), pl.BlockSpec((tm, D), row), pl.BlockSpec((tm, D), row),
                  pl.BlockSpec((tm, D), lambda i: (i, OFF_GA // D_MODEL)),
                  pl.BlockSpec((tm, D), lambda i: (i, OFF_GB // D_MODEL)),
                  pl.BlockSpec((tm, D), lambda i: (i, OFF_GC // D_MODEL)),
                  pl.BlockSpec((tm, D), row),
                  wspec, wspec, wspec, wspec,
                  pl.BlockSpec((1, D), const), pl.BlockSpec((1, D), const)],
        out_specs=[pl.BlockSpec((tm, D), row), pl.BlockSpec((tm, D), row)],
        out_shape=[jax.ShapeDtypeStruct((T, D), F32), jax.ShapeDtypeStruct((T, D), BF16)],
        compiler_params=_cparams(("parallel",)),
        name="merge_project",
    )(ya, yb, yc, proj, proj, proj, h, wa, wb, wc, wo, g.reshape(1, D), beta.reshape(1, D))


def _swiglu_step(x, wg_ref, wu_ref, wd_ref, acc_ref):
    g = jnp.dot(x, wg_ref[...], preferred_element_type=F32)
    u = jnp.dot(x, wu_ref[...], preferred_element_type=F32)
    a = (_silu(g) * u).astype(BF16)
    acc_ref[...] += jnp.dot(a, wd_ref[...], preferred_element_type=F32)


def _ffn_kernel(xb_ref, h_ref, wg_ref, wu_ref, wd_ref, g_ref, beta_ref, ho_ref, hob_ref, acc_ref):
    f = pl.program_id(1)

    @pl.when(f == 0)
    def _():
        acc_ref[...] = jnp.zeros_like(acc_ref)

    _swiglu_step(xb_ref[...], wg_ref, wu_ref, wd_ref, acc_ref)

    @pl.when(f == pl.num_programs(1) - 1)
    def _():
        hn = _layer_norm(DN_ALPHA * h_ref[...] + acc_ref[...], g_ref[...], beta_ref[...])
        ho_ref[...] = hn
        hob_ref[...] = hn.astype(BF16)


def dense_ffn(hb, h, wg, wu, wd, g, beta, tm=512, tf=256):
    T, D = h.shape
    F = wg.shape[1]
    tm = min(tm, T)
    row = lambda i, f: (i, 0)
    const = lambda i, f: (0, 0)
    return pl.pallas_call(
        _ffn_kernel,
        grid=(T // tm, F // tf),
        in_specs=[pl.BlockSpec((tm, D), row), pl.BlockSpec((tm, D), row),
                  pl.BlockSpec((D, tf), lambda i, f: (0, f)),
                  pl.BlockSpec((D, tf), lambda i, f: (0, f)),
                  pl.BlockSpec((tf, D), lambda i, f: (f, 0)),
                  pl.BlockSpec((1, D), const), pl.BlockSpec((1, D), const)],
        out_specs=[pl.BlockSpec((tm, D), row), pl.BlockSpec((tm, D), row)],
        out_shape=[jax.ShapeDtypeStruct((T, D), F32), jax.ShapeDtypeStruct((T, D), BF16)],
        scratch_shapes=[pltpu.VMEM((tm, D), F32)],
        compiler_params=_cparams(("parallel", "arbitrary")),
        name="dense_ffn",
    )(hb, h, wg, wu, wd, g.reshape(1, D), beta.reshape(1, D))


def _router_kernel(h_ref, w_ref, o_ref):
    logits = jnp.dot(h_ref[...], w_ref[...], preferred_element_type=F32, precision=lax.Precision.HIGHEST)
    lane = lax.broadcasted_iota(jnp.int32, logits.shape, 1)
    l1 = jnp.where(lane < N_EXPERTS, logits, -jnp.inf)
    m1 = jnp.max(l1, -1, keepdims=True)
    i1 = jnp.min(jnp.where(l1 == m1, lane, LANES), -1, keepdims=True)
    l2 = jnp.where(lane == i1, -jnp.inf, l1)
    m2 = jnp.max(l2, -1, keepdims=True)
    i2 = jnp.min(jnp.where(l2 == m2, lane, LANES), -1, keepdims=True)
    e = jnp.exp(m2 - m1)
    w1 = 1.0 / (1.0 + e)
    w2 = e / (1.0 + e)
    out = jnp.where(lane == 0, i1.astype(F32),
                    jnp.where(lane == 1, i2.astype(F32),
                              jnp.where(lane == 2, w1, jnp.where(lane == 3, w2, 0.0))))
    o_ref[...] = out


def router_top2(h, w_router, tm=512):
    T, D = h.shape
    tm = min(tm, T)
    wpad = jnp.zeros((D, LANES), F32).at[:, :N_EXPERTS].set(w_router.astype(F32))
    return pl.pallas_call(
        _router_kernel,
        grid=(T // tm,),
        in_specs=[pl.BlockSpec((tm, D), lambda i: (i, 0)), pl.BlockSpec((D, LANES), lambda i: (0, 0))],
        out_specs=pl.BlockSpec((tm, LANES), lambda i: (i, 0)),
        out_shape=jax.ShapeDtypeStruct((T, LANES), F32),
        compiler_params=_cparams(("parallel",)),
        name="router_top2",
    )(h, wpad)


SUB = 8


def _expert_ffn_kernel(te_ref, nt_ref, src_ref, dst_ref, h_hbm, wg_ref, wu_ref, wd_ref, y_hbm,
                       xbuf, xb_ref, ybuf, acc_ref, gsem, ssem, *, tm):
    i = pl.program_id(0)
    f = pl.program_id(1)
    n_i = pl.num_programs(0)
    last = pl.num_programs(1) - 1
    active = i < nt_ref[0]

    def start_gather(tile, buf):
        def body(r, carry):
            tok = src_ref[tile * tm + r]
            pltpu.make_async_copy(h_hbm.at[pl.ds(pl.multiple_of(tok * SUB, SUB), SUB), :],
                                  xbuf.at[buf, pl.ds(pl.multiple_of(r * SUB, SUB), SUB), :],
                                  gsem.at[buf]).start()
            return carry
        lax.fori_loop(0, tm, body, 0, unroll=8)

    def wait_gather(buf):
        pltpu.make_async_copy(h_hbm.at[pl.ds(0, tm * SUB), :], xbuf.at[buf], gsem.at[buf]).wait()

    def start_scatter():
        def body(r, carry):
            slot = dst_ref[i * tm + r]
            pltpu.make_async_copy(ybuf.at[pl.ds(pl.multiple_of(r * SUB, SUB), SUB), :],
                                  y_hbm.at[pl.ds(pl.multiple_of(slot * SUB, SUB), SUB), :], ssem).start()
            return carry
        lax.fori_loop(0, tm, body, 0, unroll=8)

    def wait_scatter():
        pltpu.make_async_copy(ybuf, y_hbm.at[pl.ds(0, tm * SUB), :], ssem).wait()

    @pl.when(f == 0)
    def _():
        buf = i % 2

        @pl.when(i == 0)
        def _():
            start_gather(0, 0)

        wait_gather(buf)

        @pl.when(i + 1 < n_i)
        def _():
            start_gather(i + 1, 1 - buf)

        for j in range(SUB):
            xb_ref[:, j * LANES:(j + 1) * LANES] = xbuf[buf, pl.ds(j, tm, stride=SUB), :].astype(BF16)

        @pl.when(i > 0)
        def _():
            wait_scatter()

        acc_ref[...] = jnp.zeros_like(acc_ref)

    @pl.when(active)
    def _():
        _swiglu_step(xb_ref[...], wg_ref, wu_ref, wd_ref, acc_ref)

    @pl.when(f == last)
    def _():
        for j in range(SUB):
            ybuf[pl.ds(j, tm, stride=SUB), :] = acc_ref[:, j * LANES:(j + 1) * LANES]
        start_scatter()

        @pl.when(i == n_i - 1)
        def _():
            wait_scatter()


def expert_ffn(h_tiles, te, nt, src_tok, dst_slot, wg, wu, wd, tm, tf=512):
    P = src_tok.shape[0]
    D = wg.shape[1]
    F = wg.shape[2]
    nf = F // tf

    def wcol(i, f, te_ref, nt_ref, src_ref, dst_ref):
        return (te_ref[i], 0, jnp.where(i < nt_ref[0], f, nf - 1))

    def wrow(i, f, te_ref, nt_ref, src_ref, dst_ref):
        return (te_ref[i], jnp.where(i < nt_ref[0], f, nf - 1), 0)

    kern = functools.partial(_expert_ffn_kernel, tm=tm)
    return pl.pallas_call(
        kern,
        grid_spec=pltpu.PrefetchScalarGridSpec(
            num_scalar_prefetch=4,
            grid=(P // tm, nf),
            in_specs=[pl.BlockSpec(memory_space=pl.ANY),
                      pl.BlockSpec((None, D, tf), wcol),
                      pl.BlockSpec((None, D, tf), wcol),
                      pl.BlockSpec((None, tf, D), wrow)],
            out_specs=pl.BlockSpec(memory_space=pl.ANY),
            scratch_shapes=[pltpu.VMEM((2, tm * SUB, LANES), F32),
                            pltpu.VMEM((tm, D), BF16),
                            pltpu.VMEM((tm * SUB, LANES), F32),
                            pltpu.VMEM((tm, D), F32),
                            pltpu.SemaphoreType.DMA((2,)),
                            pltpu.SemaphoreType.DMA(())],
        ),
        out_shape=jax.ShapeDtypeStruct((P * SUB, LANES), F32),
        compiler_params=_cparams(("arbitrary", "arbitrary")),
        name="expert_ffn",
    )(te, nt, src_tok, dst_slot, h_tiles, wg, wu, wd)


def _moe_combine_kernel(y0_ref, y1_ref, r_ref, h_ref, g_ref, beta_ref, ho_ref, hob_ref, *, tm):
    def rows(y_ref):
        return jnp.concatenate([y_ref[pl.ds(j, tm, stride=SUB), :] for j in range(SUB)], axis=1)

    r = r_ref[...]
    ff = rows(y0_ref) * r[:, 2:3] + rows(y1_ref) * r[:, 3:4]
    hn = _layer_norm(DN_ALPHA * h_ref[...] + ff, g_ref[...], beta_ref[...])
    ho_ref[...] = hn
    hob_ref[...] = hn.astype(BF16)


def moe_combine(y_tiles, route, h, g, beta, tm=512):
    T, D = h.shape
    tm = min(tm, T)
    nb = T // tm
    row = lambda i: (i, 0)
    const = lambda i: (0, 0)
    kern = functools.partial(_moe_combine_kernel, tm=tm)
    return pl.pallas_call(
        kern,
        grid=(nb,),
        in_specs=[pl.BlockSpec((tm * SUB, LANES), lambda i: (i, 0)),
                  pl.BlockSpec((tm * SUB, LANES), lambda i: (nb + i, 0)),
                  pl.BlockSpec((tm, LANES), row),
                  pl.BlockSpec((tm, D), row),
                  pl.BlockSpec((1, D), const), pl.BlockSpec((1, D), const)],
        out_specs=[pl.BlockSpec((tm, D), row), pl.BlockSpec((tm, D), row)],
        out_shape=[jax.ShapeDtypeStruct((T, D), F32), jax.ShapeDtypeStruct((T, D), BF16)],
        compiler_params=_cparams(("parallel",)),
        name="moe_combine",
    )(y_tiles, y_tiles, route, h, g.reshape(1, D), beta.reshape(1, D))


def moe_plan(route, T, tm):
    i32 = jnp.int32
    n_slots = TOP_K * T
    e_flat = route[:, 0:TOP_K].astype(i32).T.reshape(-1)
    onehot = (e_flat[:, None] == jnp.arange(N_EXPERTS, dtype=i32)[None, :]).astype(i32)
    csum = jnp.cumsum(onehot, axis=0)
    rank = jnp.sum(onehot * (csum - 1), axis=1)
    sizes = csum[-1]
    padded = ((sizes + tm - 1) // tm) * tm
    gend = jnp.cumsum(padded)
    gstart = gend - padded
    pos = gstart[e_flat] + rank
    n_rows = n_slots + N_EXPERTS * tm
    slot_plus1 = jnp.zeros((n_rows,), i32).at[pos].set(jnp.arange(1, n_slots + 1, dtype=i32))
    is_pad = slot_plus1 == 0
    pad_rank = jnp.cumsum(is_pad.astype(i32)) - 1
    dst_slot = jnp.where(is_pad, n_slots + pad_rank, slot_plus1 - 1)
    src_tok = jnp.where(is_pad, 0, (slot_plus1 - 1) % T)
    n_tiles = n_rows // tm
    nt = (gend[-1] // tm).astype(i32)
    tile_start = jnp.arange(n_tiles, dtype=i32) * tm
    te = jnp.minimum(jnp.sum((tile_start[:, None] >= gend[None, :]).astype(i32), axis=1), N_EXPERTS - 1)
    te = jnp.where(jnp.arange(n_tiles) < nt, te, te[jnp.maximum(nt - 1, 0)])
    return src_tok.astype(i32), dst_slot.astype(i32), te.astype(i32), nt.reshape(1)


def moe_ffn(h, w_router, wg, wu, wd, g, beta, tm=512):
    T, D = h.shape
    route = router_top2(h, w_router)
    src_tok, dst_slot, te, nt = moe_plan(route, T, tm)
    y_tiles = expert_ffn(h.reshape(T * SUB, LANES), te, nt, src_tok, dst_slot, wg, wu, wd, tm)
    return moe_combine(y_tiles, route, h, g, beta)


def permute_w_in(w):
    D = w.shape[0]
    cuts = np.cumsum([RET_QK, RET_QK, RET_V, RET_V, HG_K, HG_K, HG_V, HG_V, SWA_Q, SWA_KV, SWA_KV,
                      D_MODEL, D_MODEL])
    rq, rk, rv, rg, hq, hf, hi, hg, sq, sk, sv, ga, gb, gc = jnp.split(w, cuts.tolist(), axis=1)
    sk = sk.reshape(D, SWA_HKV, SWA_HD)
    sv = sv.reshape(D, SWA_HKV, SWA_HD)
    swa = jnp.concatenate([sq.reshape(D, SWA_HKV, SWA_G * SWA_HD), sk, sv, sv, sk],
                          axis=2).reshape(D, SWA_HKV * SWA_GROUP_W)
    main = jnp.concatenate([rq, rk, rv, rg, hq, hi, hg, ga, gb, gc, swa], axis=1)
    return main.astype(BF16), hf.astype(BF16)


def kernel(x, ln_in_g, ln_in_b, w_in, ret_w_out, hgrn_lower_bounds, hgrn_norm_g, hgrn_w_out, swa_sinks,
           swa_w_out, w_o, ln_mix_g, ln_mix_b, ffn_w_gate, ffn_w_up, ffn_w_down, moe_router, moe_w_gate,
           moe_w_up, moe_w_down, ln_ffn_g, ln_ffn_b):
    B, S, D = x.shape
    T = B * S
    assert D == D_MODEL and S % RET_CHUNK == 0 and S % (2 * SWA_CHUNK) == 0 and S >= 4 * SWA_CHUNK

    lb_all = jnp.cumsum(jax.nn.softmax(hgrn_lower_bounds.astype(F32), axis=0), axis=0)
    lb_all = lb_all - lb_all[0]
    tabs = retention_tables(S)

    h, hb = ln_in(x.reshape(T, D), ln_in_g, ln_in_b)
    for layer in range(DEPTH):
        w_main, w_hf = permute_w_in(w_in[layer])
        proj = matmul(hb, w_main, BF16, name="in_proj")
        hf = matmul(hb, w_hf, F32, name="in_proj_forget")

        ya = retention_mixer(proj, tabs, B, S)
        yb = hgrn_mixer(proj, hf, lb_all[layer], hgrn_norm_g[layer], B, S)
        yc = swa_mixer(proj, swa_sinks[layer], B, S)
        h, hb = merge_project(ya, yb, yc, proj, h,
                              ret_w_out[layer].astype(BF16), hgrn_w_out[layer].astype(BF16),
                              swa_w_out[layer].astype(BF16), w_o[layer].astype(BF16),
                              ln_mix_g[layer], ln_mix_b[layer])

        j = layer // 2
        if layer % 2 == 0:
            h, hb = dense_ffn(hb, h, ffn_w_gate[j].astype(BF16), ffn_w_up[j].astype(BF16),
                              ffn_w_down[j].astype(BF16), ln_ffn_g[layer], ln_ffn_b[layer])
        else:
            h, hb = moe_ffn(h, moe_router[j], moe_w_gate[j].astype(BF16), moe_w_up[j].astype(BF16),
                            moe_w_down[j].astype(BF16), ln_ffn_g[layer], ln_ffn_b[layer])
    return h.reshape(B, S, D)
```

```python
import functools

import numpy as np
import jax
import jax.numpy as jnp
from jax import lax
from jax.experimental import pallas as pl
from jax.experimental.pallas import tpu as pltpu

F32 = jnp.float32
BF16 = jnp.bfloat16

D_MODEL = 1024
RET_HEADS, RET_DK, RET_DV = 4, 128, 256
RET_QK, RET_V = RET_HEADS * RET_DK, RET_HEADS * RET_DV
ROPE_BASE = 10000.0
HG_HEADS, HG_DK, HG_DV = 8, 128, 128
HG_K, HG_V = HG_HEADS * HG_DK, HG_HEADS * HG_DV
SWA_HQ, SWA_HKV, SWA_HD = 16, 4, 64
SWA_G = SWA_HQ // SWA_HKV
SWA_Q, SWA_KV = SWA_HQ * SWA_HD, SWA_HKV * SWA_HD
SWA_CHUNK = 64
SWA_WIN_CHUNKS = 2
N_EXPERTS, TOP_K = 8, 2
DEPTH = 2
LN_EPS, RMS_EPS = 1e-5, 1e-6
DN_ALPHA = (2.0 * DEPTH) ** 0.25

VMEM_LIMIT_BYTES = 56 * 1024 * 1024
LANES = 128

RET_CHUNK = 128
RET_BLOCK = 512
HG_CHUNK = 32
HG_BLOCK = 256

OFF_RQ, OFF_RK, OFF_RV, OFF_RG = 0, 512, 1024, 2048
OFF_HQ, OFF_HI, OFF_HG = 3072, 4096, 5120
OFF_GA, OFF_GB, OFF_GC = 6144, 7168, 8192
OFF_SWA = 9216
SWA_GROUP_W = SWA_G * SWA_HD + 4 * SWA_HD
N_PROJ = OFF_SWA + SWA_HKV * SWA_GROUP_W


def _cparams(sem, vmem=VMEM_LIMIT_BYTES):
    return pltpu.CompilerParams(dimension_semantics=sem, vmem_limit_bytes=vmem)


def _layer_norm(x, g, b):
    mu = jnp.mean(x, -1, keepdims=True)
    xc = x - mu
    var = jnp.mean(xc * xc, -1, keepdims=True)
    return xc * lax.rsqrt(var + LN_EPS) * g + b


def _silu(x):
    return x * jax.nn.sigmoid(x)


def _ln_in_kernel(x_ref, g_ref, b_ref, h_ref, hb_ref):
    h = _layer_norm(x_ref[...], g_ref[...], b_ref[...])
    h_ref[...] = h
    hb_ref[...] = h.astype(BF16)


def ln_in(x2, g, b, tm=512):
    T, D = x2.shape
    tm = min(tm, T)
    return pl.pallas_call(
        _ln_in_kernel,
        grid=(T // tm,),
        in_specs=[pl.BlockSpec((tm, D), lambda i: (i, 0)),
                  pl.BlockSpec((1, D), lambda i: (0, 0)),
                  pl.BlockSpec((1, D), lambda i: (0, 0))],
        out_specs=[pl.BlockSpec((tm, D), lambda i: (i, 0)),
                   pl.BlockSpec((tm, D), lambda i: (i, 0))],
        out_shape=[jax.ShapeDtypeStruct((T, D), F32), jax.ShapeDtypeStruct((T, D), BF16)],
        compiler_params=_cparams(("parallel",)),
        name="ln_in",
    )(x2, g.reshape(1, D), b.reshape(1, D))


def _matmul_kernel(x_ref, w_ref, o_ref):
    o_ref[...] = jnp.dot(x_ref[...], w_ref[...], preferred_element_type=F32).astype(o_ref.dtype)


def matmul(x, w, out_dtype, tm=1024, tn=512, name="matmul"):
    T, K = x.shape
    N = w.shape[1]
    tm = min(tm, T)
    return pl.pallas_call(
        _matmul_kernel,
        grid=(T // tm, N // tn),
        in_specs=[pl.BlockSpec((tm, K), lambda i, j: (i, 0)),
                  pl.BlockSpec((K, tn), lambda i, j: (0, j))],
        out_specs=pl.BlockSpec((tm, tn), lambda i, j: (i, j)),
        out_shape=jax.ShapeDtypeStruct((T, N), out_dtype),
        compiler_params=_cparams(("parallel", "parallel")),
        name=name,
    )(x, w)


def _retention_kernel(q_ref, k_ref, v_ref, g_ref, cos_ref, sin_ref, dm_ref, qd_ref, kd_ref, cd_ref,
                      o_ref, st_ref, *, chunk, block, nblocks):
    st_ref[...] = jnp.zeros_like(st_ref)
    dm = dm_ref[0]
    qd = qd_ref[0]
    kd = kd_ref[0]
    cd = cd_ref[0]
    nc = block // chunk
    nt_dims = (((1,), (1,)), ((), ()))
    tn_dims = (((0,), (0,)), ((), ()))

    def body(bi, carry):
        r = pl.ds(pl.multiple_of(bi * block, block), block)
        cos = cos_ref[r, :]
        sin = sin_ref[r, :]
        q = q_ref[r, :].astype(F32)
        k = k_ref[r, :].astype(F32)
        qr = q * cos + pltpu.roll(q, RET_DK // 2, 1) * sin
        kr = k * cos + pltpu.roll(k, RET_DK // 2, 1) * sin
        v = v_ref[r, :]
        rows = [slice(c * chunk, (c + 1) * chunk) for c in range(nc)]
        upd = [lax.dot_general((kr[rc] * kd).astype(BF16), v[rc], tn_dims, preferred_element_type=F32)
               for rc in rows]
        att = [lax.dot_general(qr[rc].astype(BF16), kr[rc].astype(BF16), nt_dims,
                               preferred_element_type=F32) * dm for rc in rows]
        intra = [jnp.dot(a.astype(BF16), v[rc], preferred_element_type=F32) for a, rc in zip(att, rows)]
        q_in = [(qr[rc] * qd).astype(BF16) for rc in rows]
        st = st_ref[...]
        outs = []
        for c in range(nc):
            outs.append(intra[c] + jnp.dot(q_in[c], st.astype(BF16), preferred_element_type=F32))
            st = st * cd + upd[c]
        st_ref[...] = st
        o = jnp.concatenate(outs, axis=0)
        on = o * lax.rsqrt(jnp.mean(o * o, -1, keepdims=True) + RMS_EPS)
        g = g_ref[r, :].astype(F32)
        o_ref[r, :] = (_silu(g) * on).astype(o_ref.dtype)
        return carry

    lax.fori_loop(0, nblocks, body, 0, unroll=2)


def retention_mixer(proj, tabs, B, S):
    T = B * S
    C = RET_CHUNK
    blk = min(RET_BLOCK, S)
    nq, nv = OFF_RQ // RET_DK, OFF_RV // RET_DV
    nk, ng = OFF_RK // RET_DK, OFF_RG // RET_DV
    kern = functools.partial(_retention_kernel, chunk=C, block=blk, nblocks=S // blk)
    return pl.pallas_call(
        kern,
        grid=(B, RET_HEADS),
        in_specs=[pl.BlockSpec((S, RET_DK), lambda b, h: (b, nq + h)),
                  pl.BlockSpec((S, RET_DK), lambda b, h: (b, nk + h)),
                  pl.BlockSpec((S, RET_DV), lambda b, h: (b, nv + h)),
                  pl.BlockSpec((S, RET_DV), lambda b, h: (b, ng + h)),
                  pl.BlockSpec((S, RET_DK), lambda b, h: (0, 0)),
                  pl.BlockSpec((S, RET_DK), lambda b, h: (0, 0)),
                  pl.BlockSpec((1, C, C), lambda b, h: (h, 0, 0)),
                  pl.BlockSpec((1, C, RET_DK), lambda b, h: (h, 0, 0)),
                  pl.BlockSpec((1, C, RET_DK), lambda b, h: (h, 0, 0)),
                  pl.BlockSpec((1, 1, RET_DV), lambda b, h: (h, 0, 0))],
        out_specs=pl.BlockSpec((S, RET_DV), lambda b, h: (b, h)),
        out_shape=jax.ShapeDtypeStruct((T, RET_V), BF16),
        scratch_shapes=[pltpu.VMEM((RET_DK, RET_DV), F32)],
        compiler_params=_cparams(("parallel", "parallel")),
        name="retention",
    )(proj, proj, proj, proj, tabs["cos"], tabs["sin"], tabs["dm"], tabs["qd"], tabs["kd"], tabs["cd"])


def retention_tables(S):
    C = RET_CHUNK
    half = RET_DK // 2
    pos = jnp.arange(S, dtype=F32)
    inv = 1.0 / (ROPE_BASE ** jnp.linspace(0.0, 1.0, half, dtype=F32))
    ang = pos[:, None] * inv[None, :]
    cos, sin = jnp.cos(ang), jnp.sin(ang)
    log_gamma = jnp.log(1.0 - 2.0 ** (-5.0 - jnp.arange(RET_HEADS, dtype=F32)))
    idx = jnp.arange(C, dtype=F32)
    diff = idx[:, None] - idx[None, :]
    decay = jnp.where(diff[None] >= 0,
                      jnp.exp(jnp.maximum(diff, 0.0)[None] * log_gamma[:, None, None]), 0.0)
    scale = RET_DK ** -0.5
    qd = jnp.exp((idx + 1.0)[None, :] * log_gamma[:, None])
    kd = jnp.exp((C - 1.0 - idx)[None, :] * log_gamma[:, None]) * scale
    cd = jnp.exp(C * log_gamma)
    return {
        "cos": jnp.concatenate([cos, cos], axis=1),
        "sin": jnp.concatenate([-sin, sin], axis=1),
        "dm": decay * scale,
        "qd": jnp.broadcast_to(qd[:, :, None], (RET_HEADS, C, RET_DK)),
        "kd": jnp.broadcast_to(kd[:, :, None], (RET_HEADS, C, RET_DK)),
        "cd": jnp.broadcast_to(cd[:, None, None], (RET_HEADS, 1, RET_DV)),
    }


def _hgrn_kernel(q_ref, z_ref, i_ref, g_ref, lb_ref, ng_ref, o_ref, st_ref, *, chunk, block, nblocks):
    st_ref[...] = jnp.zeros_like(st_ref)
    lb = lb_ref[0]
    ng = ng_ref[...]
    half = chunk // 2
    nc = block // chunk
    row = lax.broadcasted_iota(jnp.int32, (block, HG_DK), 0) % chunk
    causal = (lax.broadcasted_iota(jnp.int32, (chunk, chunk), 0)
              >= lax.broadcasted_iota(jnp.int32, (chunk, chunk), 1))
    nt_dims = (((1,), (1,)), ((), ()))
    tn_dims = (((0,), (0,)), ((), ()))

    def per_chunk(x, lo):
        return jnp.concatenate(
            [jnp.broadcast_to(x[c * chunk + lo:c * chunk + lo + 1, :], (chunk, x.shape[1])) for c in range(nc)],
            axis=0)

    def body(bi, carry):
        r = pl.ds(pl.multiple_of(bi * block, block), block)
        f = lb + (1.0 - lb) * jax.nn.sigmoid(z_ref[r, :])
        cum = jnp.log(f)
        shift = 1
        while shift < chunk:
            cum = cum + jnp.where(row >= shift, pltpu.roll(cum, shift, 0), 0.0)
            shift *= 2
        mid = per_chunk(cum, half - 1)
        last = per_chunk(cum, chunk - 1)
        qh = q_ref[r, :].astype(F32) * jnp.exp(cum - mid)
        kh = (1.0 - f) * jnp.exp(mid - cum)
        q_in = (qh * jnp.exp(mid)).astype(BF16)
        k_st = (kh * jnp.exp(last - mid)).astype(BF16)
        dec = jnp.exp(last)
        qh = qh.astype(BF16)
        kh = kh.astype(BF16)
        v = i_ref[r, :]
        rows = [slice(c * chunk, (c + 1) * chunk) for c in range(nc)]
        upd = [lax.dot_general(v[rc], k_st[rc], tn_dims, preferred_element_type=F32) for rc in rows]
        att = [lax.dot_general(qh[rc], kh[rc], nt_dims, preferred_element_type=F32) for rc in rows]
        intra = [jnp.dot(jnp.where(causal, a, 0.0).astype(BF16), v[rc], preferred_element_type=F32)
                 for a, rc in zip(att, rows)]
        st = st_ref[...]
        outs = []
        for c, rc in enumerate(rows):
            inter = lax.dot_general(q_in[rc], st.astype(BF16), nt_dims, preferred_element_type=F32)
            outs.append(intra[c] + inter)
            st = st * dec[c * chunk:c * chunk + 1, :] + upd[c]
        st_ref[...] = st
        o = jnp.concatenate(outs, axis=0)
        on = o * lax.rsqrt(jnp.mean(o * o, -1, keepdims=True) + RMS_EPS) * ng
        g = g_ref[r, :].astype(F32)
        o_ref[r, :] = (_silu(g) * on).astype(o_ref.dtype)
        return carry

    lax.fori_loop(0, nblocks, body, 0, unroll=4)


def hgrn_mixer(proj, hf, lb, norm_g, B, S):
    T = B * S
    C = HG_CHUNK
    blk = min(HG_BLOCK, S)
    nq, ni, ng = OFF_HQ // HG_DK, OFF_HI // HG_DV, OFF_HG // HG_DV
    kern = functools.partial(_hgrn_kernel, chunk=C, block=blk, nblocks=S // blk)
    return pl.pallas_call(
        kern,
        grid=(B, HG_HEADS),
        in_specs=[pl.BlockSpec((S, HG_DK), lambda b, h: (b, nq + h)),
                  pl.BlockSpec((S, HG_DK), lambda b, h: (b, h)),
                  pl.BlockSpec((S, HG_DV), lambda b, h: (b, ni + h)),
                  pl.BlockSpec((S, HG_DV), lambda b, h: (b, ng + h)),
                  pl.BlockSpec((1, 1, HG_DK), lambda b, h: (h, 0, 0)),
                  pl.BlockSpec((1, HG_DV), lambda b, h: (0, 0))],
        out_specs=pl.BlockSpec((S, HG_DV), lambda b, h: (b, h)),
        out_shape=jax.ShapeDtypeStruct((T, HG_V), BF16),
        scratch_shapes=[pltpu.VMEM((HG_DV, HG_DK), F32)],
        compiler_params=_cparams(("parallel", "parallel")),
        name="hgrn2",
    )(proj, hf, proj, proj, lb.reshape(HG_HEADS, 1, HG_DK), norm_g.reshape(1, HG_DV))


def _swa_kernel(sink_ref, x_ref, o_ref, *, nsteps):
    kvh = pl.program_id(1)
    CH = SWA_CHUNK
    HD = SWA_HD
    QR = 2 * CH
    KW = 4 * CH
    QW = SWA_G * HD
    lo_kv = lax.broadcasted_iota(jnp.int32, (KW, 2 * HD), 1) < HD
    lo_o = lax.broadcasted_iota(jnp.int32, (QR, 2 * HD), 1) < HD
    nt_dims = (((1,), (1,)), ((), ()))
    zero = jnp.zeros((KW, 2 * HD), BF16)

    kcol = lax.broadcasted_iota(jnp.int32, (QR, 2 * KW), 1)
    qrow = lax.broadcasted_iota(jnp.int32, (QR, 2 * KW), 0)
    rel = (kcol % KW) // CH - qrow // CH

    def body(j, carry):
        first = jnp.maximum(2 * j - SWA_WIN_CHUNKS, 0)
        rq = pl.ds(pl.multiple_of(j * QR, QR), QR)
        rk = pl.ds(pl.multiple_of(first * CH, CH), KW)
        kv = x_ref[rk, QW:QW + 2 * HD]
        vk = x_ref[rk, QW + 2 * HD:QW + 4 * HD]
        kbd = jnp.concatenate([jnp.where(lo_kv, kv, zero), jnp.where(lo_kv, zero, vk)], axis=0)
        vbd = jnp.concatenate([jnp.where(lo_kv, vk, zero), jnp.where(lo_kv, zero, kv)], axis=0)
        d = rel + (first - 2 * j)
        valid = d * (d + SWA_WIN_CHUNKS) <= 0
        scores = [lax.dot_general(x_ref[rq, p * 2 * HD:(p + 1) * 2 * HD], kbd, nt_dims,
                                  preferred_element_type=F32) for p in range(SWA_G // 2)]
        for p, s in enumerate(scores):
            s = jnp.where(valid, s, -jnp.inf)
            probs, inv = [], []
            for hh in range(2):
                sink = sink_ref[kvh * SWA_G + 2 * p + hh]
                sh = s[:, hh * KW:(hh + 1) * KW]
                m = jnp.maximum(jnp.max(sh, -1, keepdims=True), sink)
                e = jnp.exp(sh - m)
                den = jnp.sum(e, -1, keepdims=True) + jnp.exp(sink - m)
                probs.append(e.astype(BF16))
                inv.append(1.0 / den)
            o = jnp.dot(jnp.concatenate(probs, axis=1), vbd, preferred_element_type=F32)
            o = o * jnp.where(lo_o, inv[0], inv[1])
            o_ref[rq, p * 2 * HD:(p + 1) * 2 * HD] = o.astype(o_ref.dtype)
        return carry

    lax.fori_loop(0, nsteps, body, 0, unroll=2)


def swa_mixer(proj, sinks, B, S):
    T = B * S
    nb = OFF_SWA // SWA_GROUP_W
    kern = functools.partial(_swa_kernel, nsteps=S // (2 * SWA_CHUNK))
    return pl.pallas_call(
        kern,
        grid=(B, SWA_HKV),
        in_specs=[pl.BlockSpec(memory_space=pltpu.SMEM),
                  pl.BlockSpec((S, SWA_GROUP_W), lambda b, h: (b, nb + h))],
        out_specs=pl.BlockSpec((S, SWA_G * SWA_HD), lambda b, h: (b, h)),
        out_shape=jax.ShapeDtypeStruct((T, SWA_Q), BF16),
        compiler_params=_cparams(("parallel", "parallel")),
        name="swa",
    )(sinks.astype(F32), proj)


def _merge_kernel(a_ref, b_ref, c_ref, ga_ref, gb_ref, gc_ref, h_ref, wa_ref, wb_ref, wc_ref, wo_ref,
                  g_ref, beta_ref, ho_ref, hob_ref):
    def branch(x_ref, gate_ref, w_ref):
        y = jnp.dot(x_ref[...], w_ref[...], preferred_element_type=F32)
        return jax.nn.sigmoid(gate_ref[...].astype(F32)) * y

    merged = branch(a_ref, ga_ref, wa_ref) + branch(b_ref, gb_ref, wb_ref) + branch(c_ref, gc_ref, wc_ref)
    mix = jnp.dot(merged.astype(BF16), wo_ref[...], preferred_element_type=F32)
    hn = _layer_norm(DN_ALPHA * h_ref[...] + mix, g_ref[...], beta_ref[...])
    ho_ref[...] = hn
    hob_ref[...] = hn.astype(BF16)


def merge_project(ya, yb, yc, proj, h, wa, wb, wc, wo, g, beta, tm=256):
    T, D = h.shape
    tm = min(tm, T)
    row = lambda i: (i, 0)
    const = lambda i: (0, 0)
    wspec = pl.BlockSpec((D, D), const)
    return pl.pallas_call(
        _merge_kernel,
        grid=(T // tm,),
        in_specs=[pl.BlockSpec((tm, D), row), pl.BlockSpec((tm, D), row), pl.BlockSpec((tm, D), row),
                  pl.BlockSpec((tm, D), lambda i: (i, OFF_GA // D_MODEL)),
                  pl.BlockSpec((tm, D), lambda i: (i, OFF_GB // D_MODEL)),
                  pl.BlockSpec((tm, D), lambda i: (i, OFF_GC // D_MODEL)),
                  pl.BlockSpec((tm, D), row),
                  wspec, wspec, wspec, wspec,
                  pl.BlockSpec((1, D), const), pl.BlockSpec((1, D), const)],
        out_specs=[pl.BlockSpec((tm, D), row), pl.BlockSpec((tm, D), row)],
        out_shape=[jax.ShapeDtypeStruct((T, D), F32), jax.ShapeDtypeStruct((T, D), BF16)],
        compiler_params=_cparams(("parallel",)),
        name="merge_project",
    )(ya, yb, yc, proj, proj, proj, h, wa, wb, wc, wo, g.reshape(1, D), beta.reshape(1, D))


def _swiglu_step(x, wg_ref, wu_ref, wd_ref, acc_ref):
    g = jnp.dot(x, wg_ref[...], preferred_element_type=F32)
    u = jnp.dot(x, wu_ref[...], preferred_element_type=F32)
    a = (_silu(g) * u).astype(BF16)
    acc_ref[...] += jnp.dot(a, wd_ref[...], preferred_element_type=F32)


def _ffn_kernel(xb_ref, h_ref, wg_ref, wu_ref, wd_ref, g_ref, beta_ref, ho_ref, hob_ref):
    x = xb_ref[...]
    gate = jnp.dot(x, wg_ref[...], preferred_element_type=F32)
    up = jnp.dot(x, wu_ref[...], preferred_element_type=F32)
    a = (_silu(gate) * up).astype(BF16)
    ff = jnp.dot(a, wd_ref[...], preferred_element_type=F32)
    hn = _layer_norm(DN_ALPHA * h_ref[...] + ff, g_ref[...], beta_ref[...])
    ho_ref[...] = hn
    hob_ref[...] = hn.astype(BF16)


def dense_ffn(hb, h, wg, wu, wd, g, beta, tm=512):
    T, D = h.shape
    F = wg.shape[1]
    tm = min(tm, T)
    row = lambda i: (i, 0)
    const = lambda i: (0, 0)
    resident = pl.Buffered(1)
    return pl.pallas_call(
        _ffn_kernel,
        grid=(T // tm,),
        in_specs=[pl.BlockSpec((tm, D), row), pl.BlockSpec((tm, D), row),
                  pl.BlockSpec((D, F), const, pipeline_mode=resident),
                  pl.BlockSpec((D, F), const, pipeline_mode=resident),
                  pl.BlockSpec((F, D), const, pipeline_mode=resident),
                  pl.BlockSpec((1, D), const), pl.BlockSpec((1, D), const)],
        out_specs=[pl.BlockSpec((tm, D), row), pl.BlockSpec((tm, D), row)],
        out_shape=[jax.ShapeDtypeStruct((T, D), F32), jax.ShapeDtypeStruct((T, D), BF16)],
        compiler_params=_cparams(("parallel",)),
        name="dense_ffn",
    )(hb, h, wg, wu, wd, g.reshape(1, D), beta.reshape(1, D))


def _router_kernel(h_ref, w_ref, o_ref):
    logits = jnp.dot(h_ref[...], w_ref[...], preferred_element_type=F32, precision=lax.Precision.HIGHEST)
    lane = lax.broadcasted_iota(jnp.int32, logits.shape, 1)
    l1 = jnp.where(lane < N_EXPERTS, logits, -jnp.inf)
    m1 = jnp.max(l1, -1, keepdims=True)
    i1 = jnp.min(jnp.where(l1 == m1, lane, LANES), -1, keepdims=True)
    l2 = jnp.where(lane == i1, -jnp.inf, l1)
    m2 = jnp.max(l2, -1, keepdims=True)
    i2 = jnp.min(jnp.where(l2 == m2, lane, LANES), -1, keepdims=True)
    e = jnp.exp(m2 - m1)
    w1 = 1.0 / (1.0 + e)
    w2 = e / (1.0 + e)
    out = jnp.where(lane == 0, i1.astype(F32),
                    jnp.where(lane == 1, i2.astype(F32),
                              jnp.where(lane == 2, w1, jnp.where(lane == 3, w2, 0.0))))
    o_ref[...] = out


def router_top2(h, w_router, tm=512):
    T, D = h.shape
    tm = min(tm, T)
    wpad = jnp.zeros((D, LANES), F32).at[:, :N_EXPERTS].set(w_router.astype(F32))
    return pl.pallas_call(
        _router_kernel,
        grid=(T // tm,),
        in_specs=[pl.BlockSpec((tm, D), lambda i: (i, 0)), pl.BlockSpec((D, LANES), lambda i: (0, 0))],
        out_specs=pl.BlockSpec((tm, LANES), lambda i: (i, 0)),
        out_shape=jax.ShapeDtypeStruct((T, LANES), F32),
        compiler_params=_cparams(("parallel",)),
        name="router_top2",
    )(h, wpad)


SUB = 8


def _expert_ffn_kernel(te_ref, nt_ref, src_ref, dst_ref, h_hbm, wg_ref, wu_ref, wd_ref, y_hbm,
                       xbuf, xb_ref, ybuf, acc_ref, gsem, ssem, *, tm):
    i = pl.program_id(0)
    f = pl.program_id(1)
    n_i = pl.num_programs(0)
    last = pl.num_programs(1) - 1
    active = i < nt_ref[0]

    def start_gather(tile, buf):
        def body(r, carry):
            tok = src_ref[tile * tm + r]
            pltpu.make_async_copy(h_hbm.at[pl.ds(pl.multiple_of(tok * SUB, SUB), SUB), :],
                                  xbuf.at[buf, pl.ds(pl.multiple_of(r * SUB, SUB), SUB), :],
                                  gsem.at[buf]).start()
            return carry
        lax.fori_loop(0, tm, body, 0, unroll=8)

    def wait_gather(buf):
        pltpu.make_async_copy(h_hbm.at[pl.ds(0, tm * SUB), :], xbuf.at[buf], gsem.at[buf]).wait()

    def start_scatter():
        def body(r, carry):
            slot = dst_ref[i * tm + r]
            pltpu.make_async_copy(ybuf.at[pl.ds(pl.multiple_of(r * SUB, SUB), SUB), :],
                                  y_hbm.at[pl.ds(pl.multiple_of(slot * SUB, SUB), SUB), :], ssem).start()
            return carry
        lax.fori_loop(0, tm, body, 0, unroll=8)

    def wait_scatter():
        pltpu.make_async_copy(ybuf, y_hbm.at[pl.ds(0, tm * SUB), :], ssem).wait()

    @pl.when(f == 0)
    def _():
        buf = i % 2

        @pl.when(i == 0)
        def _():
            start_gather(0, 0)

        wait_gather(buf)

        @pl.when(i + 1 < n_i)
        def _():
            start_gather(i + 1, 1 - buf)

        for j in range(SUB):
            xb_ref[:, j * LANES:(j + 1) * LANES] = xbuf[buf, pl.ds(j, tm, stride=SUB), :].astype(BF16)

        @pl.when(i > 0)
        def _():
            wait_scatter()

        acc_ref[...] = jnp.zeros_like(acc_ref)

    @pl.when(active)
    def _():
        _swiglu_step(xb_ref[...], wg_ref, wu_ref, wd_ref, acc_ref)

    @pl.when(f == last)
    def _():
        for j in range(SUB):
            ybuf[pl.ds(j, tm, stride=SUB), :] = acc_ref[:, j * LANES:(j + 1) * LANES]
        start_scatter()

        @pl.when(i == n_i - 1)
        def _():
            wait_scatter()


def expert_ffn(h_tiles, te, nt, src_tok, dst_slot, wg, wu, wd, tm, tf=1792):
    P = src_tok.shape[0]
    D = wg.shape[1]
    F = wg.shape[2]
    nf = F // tf

    def wcol(i, f, te_ref, nt_ref, src_ref, dst_ref):
        return (te_ref[i], 0, jnp.where(i < nt_ref[0], f, nf - 1))

    def wrow(i, f, te_ref, nt_ref, src_ref, dst_ref):
        return (te_ref[i], jnp.where(i < nt_ref[0], f, nf - 1), 0)

    kern = functools.partial(_expert_ffn_kernel, tm=tm)
    return pl.pallas_call(
        kern,
        grid_spec=pltpu.PrefetchScalarGridSpec(
            num_scalar_prefetch=4,
            grid=(P // tm, nf),
            in_specs=[pl.BlockSpec(memory_space=pl.ANY),
                      pl.BlockSpec((None, D, tf), wcol),
                      pl.BlockSpec((None, D, tf), wcol),
                      pl.BlockSpec((None, tf, D), wrow)],
            out_specs=pl.BlockSpec(memory_space=pl.ANY),
            scratch_shapes=[pltpu.VMEM((2, tm * SUB, LANES), F32),
                            pltpu.VMEM((tm, D), BF16),
                            pltpu.VMEM((tm * SUB, LANES), F32),
                            pltpu.VMEM((tm, D), F32),
                            pltpu.SemaphoreType.DMA((2,)),
                            pltpu.SemaphoreType.DMA(())],
        ),
        out_shape=jax.ShapeDtypeStruct((P * SUB, LANES), F32),
        compiler_params=_cparams(("arbitrary", "arbitrary")),
        name="expert_ffn",
    )(te, nt, src_tok, dst_slot, h_tiles, wg, wu, wd)


def _moe_combine_kernel(y0_ref, y1_ref, r_ref, h_ref, g_ref, beta_ref, ho_ref, hob_ref, *, tm):
    def rows(y_ref):
        return jnp.concatenate([y_ref[pl.ds(j, tm, stride=SUB), :] for j in range(SUB)], axis=1)

    r = r_ref[...]
    ff = rows(y0_ref) * r[:, 2:3] + rows(y1_ref) * r[:, 3:4]
    hn = _layer_norm(DN_ALPHA * h_ref[...] + ff, g_ref[...], beta_ref[...])
    ho_ref[...] = hn
    hob_ref[...] = hn.astype(BF16)


def moe_combine(y_tiles, route, h, g, beta, tm=512):
    T, D = h.shape
    tm = min(tm, T)
    nb = T // tm
    row = lambda i: (i, 0)
    const = lambda i: (0, 0)
    kern = functools.partial(_moe_combine_kernel, tm=tm)
    return pl.pallas_call(
        kern,
        grid=(nb,),
        in_specs=[pl.BlockSpec((tm * SUB, LANES), lambda i: (i, 0)),
                  pl.BlockSpec((tm * SUB, LANES), lambda i: (nb + i, 0)),
                  pl.BlockSpec((tm, LANES), row),
                  pl.BlockSpec((tm, D), row),
                  pl.BlockSpec((1, D), const), pl.BlockSpec((1, D), const)],
        out_specs=[pl.BlockSpec((tm, D), row), pl.BlockSpec((tm, D), row)],
        out_shape=[jax.ShapeDtypeStruct((T, D), F32), jax.ShapeDtypeStruct((T, D), BF16)],
        compiler_params=_cparams(("parallel",)),
        name="moe_combine",
    )(y_tiles, y_tiles, route, h, g.reshape(1, D), beta.reshape(1, D))


def moe_plan(route, T, tm):
    i32 = jnp.int32
    n_slots = TOP_K * T
    e_flat = route[:, 0:TOP_K].astype(i32).T.reshape(-1)
    onehot = (e_flat[:, None] == jnp.arange(N_EXPERTS, dtype=i32)[None, :]).astype(i32)
    csum = jnp.cumsum(onehot, axis=0)
    rank = jnp.sum(onehot * (csum - 1), axis=1)
    sizes = csum[-1]
    padded = ((sizes + tm - 1) // tm) * tm
    gend = jnp.cumsum(padded)
    gstart = gend - padded
    pos = gstart[e_flat] + rank
    n_rows = n_slots + N_EXPERTS * tm
    slot_plus1 = jnp.zeros((n_rows,), i32).at[pos].set(jnp.arange(1, n_slots + 1, dtype=i32))
    is_pad = slot_plus1 == 0
    pad_rank = jnp.cumsum(is_pad.astype(i32)) - 1
    dst_slot = jnp.where(is_pad, n_slots + pad_rank, slot_plus1 - 1)
    src_tok = jnp.where(is_pad, 0, (slot_plus1 - 1) % T)
    n_tiles = n_rows // tm
    nt = (gend[-1] // tm).astype(i32)
    tile_start = jnp.arange(n_tiles, dtype=i32) * tm
    te = jnp.minimum(jnp.sum((tile_start[:, None] >= gend[None, :]).astype(i32), axis=1), N_EXPERTS - 1)
    te = jnp.where(jnp.arange(n_tiles) < nt, te, te[jnp.maximum(nt - 1, 0)])
    return src_tok.astype(i32), dst_slot.astype(i32), te.astype(i32), nt.reshape(1)


def moe_ffn(h, w_router, wg, wu, wd, g, beta, tm=512):
    T, D = h.shape
    route = router_top2(h, w_router)
    src_tok, dst_slot, te, nt = moe_plan(route, T, tm)
    y_tiles = expert_ffn(h.reshape(T * SUB, LANES), te, nt, src_tok, dst_slot, wg, wu, wd, tm)
    return moe_combine(y_tiles, route, h, g, beta)


def permute_w_in(w):
    D = w.shape[0]
    cuts = np.cumsum([RET_QK, RET_QK, RET_V, RET_V, HG_K, HG_K, HG_V, HG_V, SWA_Q, SWA_KV, SWA_KV,
                      D_MODEL, D_MODEL])
    rq, rk, rv, rg, hq, hf, hi, hg, sq, sk, sv, ga, gb, gc = jnp.split(w, cuts.tolist(), axis=1)
    sk = sk.reshape(D, SWA_HKV, SWA_HD)
    sv = sv.reshape(D, SWA_HKV, SWA_HD)
    sq = sq * SWA_HD ** -0.5
    swa = jnp.concatenate([sq.reshape(D, SWA_HKV, SWA_G * SWA_HD), sk, sv, sv, sk],
                          axis=2).reshape(D, SWA_HKV * SWA_GROUP_W)
    main = jnp.concatenate([rq, rk, rv, rg, hq, hi, hg, ga, gb, gc, swa], axis=1)
    return main.astype(BF16), hf.astype(BF16)


def kernel(x, ln_in_g, ln_in_b, w_in, ret_w_out, hgrn_lower_bounds, hgrn_norm_g, hgrn_w_out, swa_sinks,
           swa_w_out, w_o, ln_mix_g, ln_mix_b, ffn_w_gate, ffn_w_up, ffn_w_down, moe_router, moe_w_gate,
           moe_w_up, moe_w_down, ln_ffn_g, ln_ffn_b):
    B, S, D = x.shape
    T = B * S
    assert D == D_MODEL and S % RET_CHUNK == 0 and S % (2 * SWA_CHUNK) == 0 and S >= 4 * SWA_CHUNK

    lb_all = jnp.cumsum(jax.nn.softmax(hgrn_lower_bounds.astype(F32), axis=0), axis=0)
    lb_all = lb_all - lb_all[0]
    tabs = retention_tables(S)

    h, hb = ln_in(x.reshape(T, D), ln_in_g, ln_in_b)
    for layer in range(DEPTH):
        w_main, w_hf = permute_w_in(w_in[layer])
        proj = matmul(hb, w_main, BF16, tn=N_PROJ // 4, name="in_proj")
        hf = matmul(hb, w_hf, F32, tn=HG_K, name="in_proj_forget")

        ya = retention_mixer(proj, tabs, B, S)
        yb = hgrn_mixer(proj, hf, lb_all[layer], hgrn_norm_g[layer], B, S)
        yc = swa_mixer(proj, swa_sinks[layer], B, S)
        h, hb = merge_project(ya, yb, yc, proj, h,
                              ret_w_out[layer].astype(BF16), hgrn_w_out[layer].astype(BF16),
                              swa_w_out[layer].astype(BF16), w_o[layer].astype(BF16),
                              ln_mix_g[layer], ln_mix_b[layer])

        j = layer // 2
        if layer % 2 == 0:
            h, hb = dense_ffn(hb, h, ffn_w_gate[j].astype(BF16), ffn_w_up[j].astype(BF16),
                              ffn_w_down[j].astype(BF16), ln_ffn_g[layer], ln_ffn_b[layer])
        else:
            h, hb = moe_ffn(h, moe_router[j], moe_w_gate[j].astype(BF16), moe_w_up[j].astype(BF16),
                            moe_w_down[j].astype(BF16), ln_ffn_g[layer], ln_ffn_b[layer])
    return h.reshape(B, S, D)
```

```python
import functools

import numpy as np
import jax
import jax.numpy as jnp
from jax import lax
from jax.experimental import pallas as pl
from jax.experimental.pallas import tpu as pltpu

F32 = jnp.float32
BF16 = jnp.bfloat16

D_MODEL = 1024
RET_HEADS, RET_DK, RET_DV = 4, 128, 256
RET_QK, RET_V = RET_HEADS * RET_DK, RET_HEADS * RET_DV
ROPE_BASE = 10000.0
HG_HEADS, HG_DK, HG_DV = 8, 128, 128
HG_K, HG_V = HG_HEADS * HG_DK, HG_HEADS * HG_DV
SWA_HQ, SWA_HKV, SWA_HD = 16, 4, 64
SWA_G = SWA_HQ // SWA_HKV
SWA_Q, SWA_KV = SWA_HQ * SWA_HD, SWA_HKV * SWA_HD
SWA_CHUNK = 64
SWA_WIN_CHUNKS = 2
N_EXPERTS, TOP_K = 8, 2
DEPTH = 2
LN_EPS, RMS_EPS = 1e-5, 1e-6
DN_ALPHA = (2.0 * DEPTH) ** 0.25

VMEM_LIMIT_BYTES = 56 * 1024 * 1024
LANES = 128
SUB = 8

RET_CHUNK = 128
RET_BLOCK = 512
HG_CHUNK = 32
HG_BLOCK = 256

OFF_RQ, OFF_RK, OFF_RV, OFF_RG = 0, 512, 1024, 2048
OFF_HQ, OFF_HI, OFF_HG = 3072, 4096, 5120
OFF_GA, OFF_GB, OFF_GC = 6144, 7168, 8192
OFF_SWA = 9216
SWA_GROUP_W = SWA_G * SWA_HD + 4 * SWA_HD
N_PROJ = OFF_SWA + SWA_HKV * SWA_GROUP_W


def _cparams(sem, vmem=VMEM_LIMIT_BYTES):
    return pltpu.CompilerParams(dimension_semantics=sem, vmem_limit_bytes=vmem)


def _layer_norm(x, g, b):
    mu = jnp.mean(x, -1, keepdims=True)
    xc = x - mu
    var = jnp.mean(xc * xc, -1, keepdims=True)
    return xc * lax.rsqrt(var + LN_EPS) * g + b


def _silu(x):
    return x * jax.nn.sigmoid(x)


def _ln_in_kernel(x_ref, g_ref, b_ref, h_ref, hb_ref):
    h = _layer_norm(x_ref[...], g_ref[...], b_ref[...])
    h_ref[...] = h
    hb_ref[...] = h.astype(BF16)


def ln_in(x2, g, b, tm=512):
    T, D = x2.shape
    tm = min(tm, T)
    return pl.pallas_call(
        _ln_in_kernel,
        grid=(T // tm,),
        in_specs=[pl.BlockSpec((tm, D), lambda i: (i, 0)),
                  pl.BlockSpec((1, D), lambda i: (0, 0)),
                  pl.BlockSpec((1, D), lambda i: (0, 0))],
        out_specs=[pl.BlockSpec((tm, D), lambda i: (i, 0)),
                   pl.BlockSpec((tm, D), lambda i: (i, 0))],
        out_shape=[jax.ShapeDtypeStruct((T, D), F32), jax.ShapeDtypeStruct((T, D), BF16)],
        compiler_params=_cparams(("parallel",)),
        name="ln_in",
    )(x2, g.reshape(1, D), b.reshape(1, D))


def _matmul_kernel(x_ref, w_ref, o_ref):
    o_ref[...] = jnp.dot(x_ref[...], w_ref[...], preferred_element_type=F32).astype(o_ref.dtype)


def matmul(x, w, out_dtype, tm=1024, tn=512, name="matmul"):
    T, K = x.shape
    N = w.shape[1]
    tm = min(tm, T)
    return pl.pallas_call(
        _matmul_kernel,
        grid=(T // tm, N // tn),
        in_specs=[pl.BlockSpec((tm, K), lambda i, j: (i, 0)),
                  pl.BlockSpec((K, tn), lambda i, j: (0, j))],
        out_specs=pl.BlockSpec((tm, tn), lambda i, j: (i, j)),
        out_shape=jax.ShapeDtypeStruct((T, N), out_dtype),
        compiler_params=_cparams(("parallel", "parallel")),
        name=name,
    )(x, w)


def _retention_kernel(q_ref, k_ref, v_ref, g_ref, cos_ref, sin_ref, dm_ref, qd_ref, kd_ref, cd_ref,
                      o_ref, st_ref, *, chunk, block, nblocks):
    st_ref[...] = jnp.zeros_like(st_ref)
    dm = dm_ref[0]
    qd = qd_ref[0]
    kd = kd_ref[0]
    cd = cd_ref[0]
    nc = block // chunk
    nt_dims = (((1,), (1,)), ((), ()))
    tn_dims = (((0,), (0,)), ((), ()))

    def body(bi, carry):
        r = pl.ds(pl.multiple_of(bi * block, block), block)
        cos = cos_ref[r, :]
        sin = sin_ref[r, :]
        q = q_ref[r, :].astype(F32)
        k = k_ref[r, :].astype(F32)
        qr = q * cos + pltpu.roll(q, RET_DK // 2, 1) * sin
        kr = k * cos + pltpu.roll(k, RET_DK // 2, 1) * sin
        v = v_ref[r, :]
        rows = [slice(c * chunk, (c + 1) * chunk) for c in range(nc)]
        upd = [lax.dot_general((kr[rc] * kd).astype(BF16), v[rc], tn_dims, preferred_element_type=F32)
               for rc in rows]
        att = [lax.dot_general(qr[rc].astype(BF16), kr[rc].astype(BF16), nt_dims,
                               preferred_element_type=F32) * dm for rc in rows]
        intra = [jnp.dot(a.astype(BF16), v[rc], preferred_element_type=F32) for a, rc in zip(att, rows)]
        q_in = [(qr[rc] * qd).astype(BF16) for rc in rows]
        st = st_ref[...]
        outs = []
        for c in range(nc):
            outs.append(intra[c] + jnp.dot(q_in[c], st.astype(BF16), preferred_element_type=F32))
            st = st * cd + upd[c]
        st_ref[...] = st
        o = jnp.concatenate(outs, axis=0)
        on = o * lax.rsqrt(jnp.mean(o * o, -1, keepdims=True) + RMS_EPS)
        g = g_ref[r, :].astype(F32)
        o_ref[r, :] = (_silu(g) * on).astype(o_ref.dtype)
        return carry

    lax.fori_loop(0, nblocks, body, 0, unroll=2)


def retention_mixer(proj, tabs, B, S):
    T = B * S
    C = RET_CHUNK
    blk = min(RET_BLOCK, S)
    nq, nv = OFF_RQ // RET_DK, OFF_RV // RET_DV
    nk, ng = OFF_RK // RET_DK, OFF_RG // RET_DV
    kern = functools.partial(_retention_kernel, chunk=C, block=blk, nblocks=S // blk)
    return pl.pallas_call(
        kern,
        grid=(B, RET_HEADS),
        in_specs=[pl.BlockSpec((S, RET_DK), lambda b, h: (b, nq + h)),
                  pl.BlockSpec((S, RET_DK), lambda b, h: (b, nk + h)),
                  pl.BlockSpec((S, RET_DV), lambda b, h: (b, nv + h)),
                  pl.BlockSpec((S, RET_DV), lambda b, h: (b, ng + h)),
                  pl.BlockSpec((S, RET_DK), lambda b, h: (0, 0)),
                  pl.BlockSpec((S, RET_DK), lambda b, h: (0, 0)),
                  pl.BlockSpec((1, C, C), lambda b, h: (h, 0, 0)),
                  pl.BlockSpec((1, C, RET_DK), lambda b, h: (h, 0, 0)),
                  pl.BlockSpec((1, C, RET_DK), lambda b, h: (h, 0, 0)),
                  pl.BlockSpec((1, 1, RET_DV), lambda b, h: (h, 0, 0))],
        out_specs=pl.BlockSpec((S, RET_DV), lambda b, h: (b, h)),
        out_shape=jax.ShapeDtypeStruct((T, RET_V), BF16),
        scratch_shapes=[pltpu.VMEM((RET_DK, RET_DV), F32)],
        compiler_params=_cparams(("parallel", "parallel")),
        name="retention",
    )(proj, proj, proj, proj, tabs["cos"], tabs["sin"], tabs["dm"], tabs["qd"], tabs["kd"], tabs["cd"])


def retention_tables(S):
    C = RET_CHUNK
    half = RET_DK // 2
    pos = jnp.arange(S, dtype=F32)
    inv = 1.0 / (ROPE_BASE ** jnp.linspace(0.0, 1.0, half, dtype=F32))
    ang = pos[:, None] * inv[None, :]
    cos, sin = jnp.cos(ang), jnp.sin(ang)
    log_gamma = jnp.log(1.0 - 2.0 ** (-5.0 - jnp.arange(RET_HEADS, dtype=F32)))
    idx = jnp.arange(C, dtype=F32)
    diff = idx[:, None] - idx[None, :]
    decay = jnp.where(diff[None] >= 0,
                      jnp.exp(jnp.maximum(diff, 0.0)[None] * log_gamma[:, None, None]), 0.0)
    scale = RET_DK ** -0.5
    qd = jnp.exp((idx + 1.0)[None, :] * log_gamma[:, None])
    kd = jnp.exp((C - 1.0 - idx)[None, :] * log_gamma[:, None]) * scale
    cd = jnp.exp(C * log_gamma)
    return {
        "cos": jnp.concatenate([cos, cos], axis=1),
        "sin": jnp.concatenate([-sin, sin], axis=1),
        "dm": decay * scale,
        "qd": jnp.broadcast_to(qd[:, :, None], (RET_HEADS, C, RET_DK)),
        "kd": jnp.broadcast_to(kd[:, :, None], (RET_HEADS, C, RET_DK)),
        "cd": jnp.broadcast_to(cd[:, None, None], (RET_HEADS, 1, RET_DV)),
    }


def _hgrn_kernel(q_ref, z_ref, i_ref, g_ref, lb_ref, ng_ref, o_ref, st_ref, *, chunk, block, nblocks):
    st_ref[...] = jnp.zeros_like(st_ref)
    lb = lb_ref[0]
    ng = ng_ref[...]
    half = chunk // 2
    nc = block // chunk
    row = lax.broadcasted_iota(jnp.int32, (block, HG_DK), 0) % chunk
    causal = (lax.broadcasted_iota(jnp.int32, (chunk, chunk), 0)
              >= lax.broadcasted_iota(jnp.int32, (chunk, chunk), 1))
    nt_dims = (((1,), (1,)), ((), ()))
    tn_dims = (((0,), (0,)), ((), ()))

    def per_chunk(x, lo):
        return jnp.concatenate(
            [jnp.broadcast_to(x[c * chunk + lo:c * chunk + lo + 1, :], (chunk, x.shape[1])) for c in range(nc)],
            axis=0)

    def body(bi, carry):
        r = pl.ds(pl.multiple_of(bi * block, block), block)
        f = lb + (1.0 - lb) * jax.nn.sigmoid(z_ref[r, :])
        cum = jnp.log(f)
        shift = 1
        while shift < chunk:
            cum = cum + jnp.where(row >= shift, pltpu.roll(cum, shift, 0), 0.0)
            shift *= 2
        mid = per_chunk(cum, half - 1)
        last = per_chunk(cum, chunk - 1)
        qh = q_ref[r, :].astype(F32) * jnp.exp(cum - mid)
        kh = (1.0 - f) * jnp.exp(mid - cum)
        q_in = (qh * jnp.exp(mid)).astype(BF16)
        k_st = (kh * jnp.exp(last - mid)).astype(BF16)
        dec = jnp.exp(last)
        qh = qh.astype(BF16)
        kh = kh.astype(BF16)
        v = i_ref[r, :]
        rows = [slice(c * chunk, (c + 1) * chunk) for c in range(nc)]
        upd = [lax.dot_general(v[rc], k_st[rc], tn_dims, preferred_element_type=F32) for rc in rows]
        att = [lax.dot_general(qh[rc], kh[rc], nt_dims, preferred_element_type=F32) for rc in rows]
        intra = [jnp.dot(jnp.where(causal, a, 0.0).astype(BF16), v[rc], preferred_element_type=F32)
                 for a, rc in zip(att, rows)]
        st = st_ref[...]
        outs = []
        for c, rc in enumerate(rows):
            inter = lax.dot_general(q_in[rc], st.astype(BF16), nt_dims, preferred_element_type=F32)
            outs.append(intra[c] + inter)
            st = st * dec[c * chunk:c * chunk + 1, :] + upd[c]
        st_ref[...] = st
        o = jnp.concatenate(outs, axis=0)
        on = o * lax.rsqrt(jnp.mean(o * o, -1, keepdims=True) + RMS_EPS) * ng
        g = g_ref[r, :].astype(F32)
        o_ref[r, :] = (_silu(g) * on).astype(o_ref.dtype)
        return carry

    lax.fori_loop(0, nblocks, body, 0, unroll=4)


def hgrn_mixer(proj, hf, lb, norm_g, B, S):
    T = B * S
    C = HG_CHUNK
    blk = min(HG_BLOCK, S)
    nq, ni, ng = OFF_HQ // HG_DK, OFF_HI // HG_DV, OFF_HG // HG_DV
    kern = functools.partial(_hgrn_kernel, chunk=C, block=blk, nblocks=S // blk)
    return pl.pallas_call(
        kern,
        grid=(B, HG_HEADS),
        in_specs=[pl.BlockSpec((S, HG_DK), lambda b, h: (b, nq + h)),
                  pl.BlockSpec((S, HG_DK), lambda b, h: (b, h)),
                  pl.BlockSpec((S, HG_DV), lambda b, h: (b, ni + h)),
                  pl.BlockSpec((S, HG_DV), lambda b, h: (b, ng + h)),
                  pl.BlockSpec((1, 1, HG_DK), lambda b, h: (h, 0, 0)),
                  pl.BlockSpec((1, HG_DV), lambda b, h: (0, 0))],
        out_specs=pl.BlockSpec((S, HG_DV), lambda b, h: (b, h)),
        out_shape=jax.ShapeDtypeStruct((T, HG_V), BF16),
        scratch_shapes=[pltpu.VMEM((HG_DV, HG_DK), F32)],
        compiler_params=_cparams(("parallel", "parallel")),
        name="hgrn2",
    )(proj, hf, proj, proj, lb.reshape(HG_HEADS, 1, HG_DK), norm_g.reshape(1, HG_DV))


def _swa_kernel(sink_ref, x_ref, o_ref, *, nsteps):
    kvh = pl.program_id(1)
    CH = SWA_CHUNK
    HD = SWA_HD
    QR = 2 * CH
    KW = 4 * CH
    QW = SWA_G * HD
    lo_kv = lax.broadcasted_iota(jnp.int32, (KW, 2 * HD), 1) < HD
    lo_o = lax.broadcasted_iota(jnp.int32, (QR, 2 * HD), 1) < HD
    nt_dims = (((1,), (1,)), ((), ()))
    zero = jnp.zeros((KW, 2 * HD), BF16)

    kcol = lax.broadcasted_iota(jnp.int32, (QR, 2 * KW), 1)
    qrow = lax.broadcasted_iota(jnp.int32, (QR, 2 * KW), 0)
    rel = (kcol % KW) // CH - qrow // CH

    def body(j, carry):
        first = jnp.maximum(2 * j - SWA_WIN_CHUNKS, 0)
        rq = pl.ds(pl.multiple_of(j * QR, QR), QR)
        rk = pl.ds(pl.multiple_of(first * CH, CH), KW)
        kv = x_ref[rk, QW:QW + 2 * HD]
        vk = x_ref[rk, QW + 2 * HD:QW + 4 * HD]
        kbd = jnp.concatenate([jnp.where(lo_kv, kv, zero), jnp.where(lo_kv, zero, vk)], axis=0)
        vbd = jnp.concatenate([jnp.where(lo_kv, vk, zero), jnp.where(lo_kv, zero, kv)], axis=0)
        d = rel + (first - 2 * j)
        valid = d * (d + SWA_WIN_CHUNKS) <= 0
        scores = [lax.dot_general(x_ref[rq, p * 2 * HD:(p + 1) * 2 * HD], kbd, nt_dims,
                                  preferred_element_type=F32) for p in range(SWA_G // 2)]
        for p, s in enumerate(scores):
            s = jnp.where(valid, s, -jnp.inf)
            probs, inv = [], []
            for hh in range(2):
                sink = sink_ref[kvh * SWA_G + 2 * p + hh]
                sh = s[:, hh * KW:(hh + 1) * KW]
                m = jnp.maximum(jnp.max(sh, -1, keepdims=True), sink)
                e = jnp.exp(sh - m)
                den = jnp.sum(e, -1, keepdims=True) + jnp.exp(sink - m)
                probs.append(e.astype(BF16))
                inv.append(1.0 / den)
            o = jnp.dot(jnp.concatenate(probs, axis=1), vbd, preferred_element_type=F32)
            o = o * jnp.where(lo_o, inv[0], inv[1])
            o_ref[rq, p * 2 * HD:(p + 1) * 2 * HD] = o.astype(o_ref.dtype)
        return carry

    lax.fori_loop(0, nsteps, body, 0, unroll=2)


def swa_mixer(proj, sinks, B, S):
    T = B * S
    nb = OFF_SWA // SWA_GROUP_W
    kern = functools.partial(_swa_kernel, nsteps=S // (2 * SWA_CHUNK))
    return pl.pallas_call(
        kern,
        grid=(B, SWA_HKV),
        in_specs=[pl.BlockSpec(memory_space=pltpu.SMEM),
                  pl.BlockSpec((S, SWA_GROUP_W), lambda b, h: (b, nb + h))],
        out_specs=pl.BlockSpec((S, SWA_G * SWA_HD), lambda b, h: (b, h)),
        out_shape=jax.ShapeDtypeStruct((T, SWA_Q), BF16),
        compiler_params=_cparams(("parallel", "parallel")),
        name="swa",
    )(sinks.astype(F32), proj)


def _merge_kernel(a_ref, b_ref, c_ref, ga_ref, gb_ref, gc_ref, h_ref, wa_ref, wb_ref, wc_ref, wo_ref,
                  g_ref, beta_ref, ho_ref, aux_ref, *, tm, token_tiles):
    def branch(x_ref, gate_ref, w_ref):
        y = jnp.dot(x_ref[...], w_ref[...], preferred_element_type=F32)
        return jax.nn.sigmoid(gate_ref[...].astype(F32)) * y

    merged = branch(a_ref, ga_ref, wa_ref) + branch(b_ref, gb_ref, wb_ref) + branch(c_ref, gc_ref, wc_ref)
    mix = jnp.dot(merged.astype(BF16), wo_ref[...], preferred_element_type=F32)
    hn = _layer_norm(DN_ALPHA * h_ref[...] + mix, g_ref[...], beta_ref[...])
    ho_ref[...] = hn
    if token_tiles:
        for j in range(SUB):
            aux_ref[pl.ds(j, tm, stride=SUB), :] = hn[:, j * LANES:(j + 1) * LANES]
    else:
        aux_ref[...] = hn.astype(BF16)


def merge_project(ya, yb, yc, proj, h, wa, wb, wc, wo, g, beta, token_tiles, tm=512):
    T, D = h.shape
    tm = min(tm, T)
    row = lambda i: (i, 0)
    const = lambda i: (0, 0)
    wspec = pl.BlockSpec((D, D), const, pipeline_mode=pl.Buffered(1))
    if token_tiles:
        aux_spec = pl.BlockSpec((tm * SUB, LANES), row)
        aux_shape = jax.ShapeDtypeStruct((T * SUB, LANES), F32)
    else:
        aux_spec = pl.BlockSpec((tm, D), row)
        aux_shape = jax.ShapeDtypeStruct((T, D), BF16)
    kern = functools.partial(_merge_kernel, tm=tm, token_tiles=token_tiles)
    return pl.pallas_call(
        kern,
        grid=(T // tm,),
        in_specs=[pl.BlockSpec((tm, D), row), pl.BlockSpec((tm, D), row), pl.BlockSpec((tm, D), row),
                  pl.BlockSpec((tm, D), lambda i: (i, OFF_GA // D_MODEL)),
                  pl.BlockSpec((tm, D), lambda i: (i, OFF_GB // D_MODEL)),
                  pl.BlockSpec((tm, D), lambda i: (i, OFF_GC // D_MODEL)),
                  pl.BlockSpec((tm, D), row),
                  wspec, wspec, wspec, wspec,
                  pl.BlockSpec((1, D), const), pl.BlockSpec((1, D), const)],
        out_specs=[pl.BlockSpec((tm, D), row), aux_spec],
        out_shape=[jax.ShapeDtypeStruct((T, D), F32), aux_shape],
        compiler_params=_cparams(("parallel",)),
        name="merge_project",
    )(ya, yb, yc, proj, proj, proj, h, wa, wb, wc, wo, g.reshape(1, D), beta.reshape(1, D))


def _swiglu_step(x, wg_ref, wu_ref, wd_ref, acc_ref):
    g = jnp.dot(x, wg_ref[...], preferred_element_type=F32)
    u = jnp.dot(x, wu_ref[...], preferred_element_type=F32)
    a = (_silu(g) * u).astype(BF16)
    acc_ref[...] += jnp.dot(a, wd_ref[...], preferred_element_type=F32)


def _ffn_kernel(xb_ref, h_ref, wg_ref, wu_ref, wd_ref, g_ref, beta_ref, ho_ref, hob_ref):
    x = xb_ref[...]
    gate = jnp.dot(x, wg_ref[...], preferred_element_type=F32)
    up = jnp.dot(x, wu_ref[...], preferred_element_type=F32)
    a = (_silu(gate) * up).astype(BF16)
    ff = jnp.dot(a, wd_ref[...], preferred_element_type=F32)
    hn = _layer_norm(DN_ALPHA * h_ref[...] + ff, g_ref[...], beta_ref[...])
    ho_ref[...] = hn
    hob_ref[...] = hn.astype(BF16)


def dense_ffn(hb, h, wg, wu, wd, g, beta, tm=512):
    T, D = h.shape
    F = wg.shape[1]
    tm = min(tm, T)
    row = lambda i: (i, 0)
    const = lambda i: (0, 0)
    resident = pl.Buffered(1)
    return pl.pallas_call(
        _ffn_kernel,
        grid=(T // tm,),
        in_specs=[pl.BlockSpec((tm, D), row), pl.BlockSpec((tm, D), row),
                  pl.BlockSpec((D, F), const, pipeline_mode=resident),
                  pl.BlockSpec((D, F), const, pipeline_mode=resident),
                  pl.BlockSpec((F, D), const, pipeline_mode=resident),
                  pl.BlockSpec((1, D), const), pl.BlockSpec((1, D), const)],
        out_specs=[pl.BlockSpec((tm, D), row), pl.BlockSpec((tm, D), row)],
        out_shape=[jax.ShapeDtypeStruct((T, D), F32), jax.ShapeDtypeStruct((T, D), BF16)],
        compiler_params=_cparams(("parallel",)),
        name="dense_ffn",
    )(hb, h, wg, wu, wd, g.reshape(1, D), beta.reshape(1, D))


def _router_kernel(h_ref, w_ref, o_ref):
    logits = jnp.dot(h_ref[...], w_ref[...], preferred_element_type=F32, precision=lax.Precision.HIGHEST)
    lane = lax.broadcasted_iota(jnp.int32, logits.shape, 1)
    l1 = jnp.where(lane < N_EXPERTS, logits, -jnp.inf)
    m1 = jnp.max(l1, -1, keepdims=True)
    i1 = jnp.min(jnp.where(l1 == m1, lane, LANES), -1, keepdims=True)
    l2 = jnp.where(lane == i1, -jnp.inf, l1)
    m2 = jnp.max(l2, -1, keepdims=True)
    i2 = jnp.min(jnp.where(l2 == m2, lane, LANES), -1, keepdims=True)
    e = jnp.exp(m2 - m1)
    w1 = 1.0 / (1.0 + e)
    w2 = e / (1.0 + e)
    out = jnp.where(lane == 0, i1.astype(F32),
                    jnp.where(lane == 1, i2.astype(F32),
                              jnp.where(lane == 2, w1, jnp.where(lane == 3, w2, 0.0))))
    o_ref[...] = out


def router_top2(h, w_router, tm=512):
    T, D = h.shape
    tm = min(tm, T)
    wpad = jnp.zeros((D, LANES), F32).at[:, :N_EXPERTS].set(w_router.astype(F32))
    return pl.pallas_call(
        _router_kernel,
        grid=(T // tm,),
        in_specs=[pl.BlockSpec((tm, D), lambda i: (i, 0)), pl.BlockSpec((D, LANES), lambda i: (0, 0))],
        out_specs=pl.BlockSpec((tm, LANES), lambda i: (i, 0)),
        out_shape=jax.ShapeDtypeStruct((T, LANES), F32),
        compiler_params=_cparams(("parallel",)),
        name="router_top2",
    )(h, wpad)


def _expert_ffn_kernel(te_ref, nt_ref, src_ref, dst_ref, h_hbm, wg_ref, wu_ref, wd_ref, y_hbm,
                       xbuf, xb_ref, ybuf, acc_ref, gsem, ssem, *, tm, nf):
    i = pl.program_id(0)
    f = pl.program_id(1)
    n_i = pl.num_programs(0)
    active = i < nt_ref[0]
    buf = i % 2
    share = tm // nf
    nxt = jnp.minimum(i + 1, n_i - 1)
    prv = jnp.maximum(i - 1, 0)

    def gather_row(tile, b, r):
        tok = src_ref[tile * tm + r]
        pltpu.make_async_copy(h_hbm.at[pl.ds(pl.multiple_of(tok * SUB, SUB), SUB), :],
                              xbuf.at[b, pl.ds(pl.multiple_of(r * SUB, SUB), SUB), :], gsem.at[b]).start()

    def scatter_row(tile, r):
        slot = dst_ref[tile * tm + r]
        pltpu.make_async_copy(ybuf.at[pl.ds(pl.multiple_of(r * SUB, SUB), SUB), :],
                              y_hbm.at[pl.ds(pl.multiple_of(slot * SUB, SUB), SUB), :], ssem).start()

    def wait_gather(b):
        pltpu.make_async_copy(h_hbm.at[pl.ds(0, tm * SUB), :], xbuf.at[b], gsem.at[b]).wait()

    def wait_scatter():
        pltpu.make_async_copy(ybuf, y_hbm.at[pl.ds(0, tm * SUB), :], ssem).wait()

    def looped(n, fn):
        def body(r, carry):
            fn(r)
            return carry
        lax.fori_loop(0, n, body, 0, unroll=8)

    def share_row(r):
        gather_row(nxt, 1 - buf, f * share + r)
        scatter_row(prv, f * share + r)

    @pl.when(f == 0)
    def _():
        @pl.when(i == 0)
        def _():
            looped(tm, lambda r: gather_row(0, 0, r))
            ybuf[...] = jnp.zeros_like(ybuf)

        wait_gather(buf)
        for j in range(SUB):
            xb_ref[:, j * LANES:(j + 1) * LANES] = xbuf[buf, pl.ds(j, tm, stride=SUB), :].astype(BF16)
        acc_ref[...] = jnp.zeros_like(acc_ref)

    @pl.when(active)
    def _():
        for r in range(share):
            share_row(r)
        _swiglu_step(xb_ref[...], wg_ref, wu_ref, wd_ref, acc_ref)

    @pl.when(jnp.logical_not(active))
    def _():
        looped(share, share_row)

    @pl.when(f == nf - 1)
    def _():
        wait_scatter()
        for j in range(SUB):
            ybuf[pl.ds(j, tm, stride=SUB), :] = acc_ref[:, j * LANES:(j + 1) * LANES]

        @pl.when(i == n_i - 1)
        def _():
            looped(tm, lambda r: scatter_row(i, r))
            wait_scatter()
            wait_gather(1 - buf)


def expert_ffn(h_tiles, te, nt, src_tok, dst_slot, wg, wu, wd, tm, tf=1792):
    P = src_tok.shape[0]
    D = wg.shape[1]
    F = wg.shape[2]
    nf = F // tf

    def wcol(i, f, te_ref, nt_ref, src_ref, dst_ref):
        return (te_ref[i], 0, jnp.where(i < nt_ref[0], f, nf - 1))

    def wrow(i, f, te_ref, nt_ref, src_ref, dst_ref):
        return (te_ref[i], jnp.where(i < nt_ref[0], f, nf - 1), 0)

    kern = functools.partial(_expert_ffn_kernel, tm=tm, nf=nf)
    return pl.pallas_call(
        kern,
        grid_spec=pltpu.PrefetchScalarGridSpec(
            num_scalar_prefetch=4,
            grid=(P // tm, nf),
            in_specs=[pl.BlockSpec(memory_space=pl.ANY),
                      pl.BlockSpec((None, D, tf), wcol),
                      pl.BlockSpec((None, D, tf), wcol),
                      pl.BlockSpec((None, tf, D), wrow)],
            out_specs=pl.BlockSpec(memory_space=pl.ANY),
            scratch_shapes=[pltpu.VMEM((2, tm * SUB, LANES), F32),
                            pltpu.VMEM((tm, D), BF16),
                            pltpu.VMEM((tm * SUB, LANES), F32),
                            pltpu.VMEM((tm, D), F32),
                            pltpu.SemaphoreType.DMA((2,)),
                            pltpu.SemaphoreType.DMA(())],
        ),
        out_shape=jax.ShapeDtypeStruct((P * SUB, LANES), F32),
        compiler_params=_cparams(("arbitrary", "arbitrary")),
        name="expert_ffn",
    )(te, nt, src_tok, dst_slot, h_tiles, wg, wu, wd)


def _moe_combine_kernel(y0_ref, y1_ref, r_ref, h_ref, g_ref, beta_ref, ho_ref, hob_ref, *, tm):
    def rows(y_ref):
        return jnp.concatenate([y_ref[pl.ds(j, tm, stride=SUB), :] for j in range(SUB)], axis=1)

    r = r_ref[...]
    ff = rows(y0_ref) * r[:, 2:3] + rows(y1_ref) * r[:, 3:4]
    hn = _layer_norm(DN_ALPHA * h_ref[...] + ff, g_ref[...], beta_ref[...])
    ho_ref[...] = hn
    hob_ref[...] = hn.astype(BF16)


def moe_combine(y_tiles, route, h, g, beta, tm=512):
    T, D = h.shape
    tm = min(tm, T)
    nb = T // tm
    row = lambda i: (i, 0)
    const = lambda i: (0, 0)
    kern = functools.partial(_moe_combine_kernel, tm=tm)
    return pl.pallas_call(
        kern,
        grid=(nb,),
        in_specs=[pl.BlockSpec((tm * SUB, LANES), lambda i: (i, 0)),
                  pl.BlockSpec((tm * SUB, LANES), lambda i: (nb + i, 0)),
                  pl.BlockSpec((tm, LANES), row),
                  pl.BlockSpec((tm, D), row),
                  pl.BlockSpec((1, D), const), pl.BlockSpec((1, D), const)],
        out_specs=[pl.BlockSpec((tm, D), row), pl.BlockSpec((tm, D), row)],
        out_shape=[jax.ShapeDtypeStruct((T, D), F32), jax.ShapeDtypeStruct((T, D), BF16)],
        compiler_params=_cparams(("parallel",)),
        name="moe_combine",
    )(y_tiles, y_tiles, route, h, g.reshape(1, D), beta.reshape(1, D))


def moe_plan(route, T, tm):
    i32 = jnp.int32
    n_slots = TOP_K * T
    e_flat = route[:, 0:TOP_K].astype(i32).T.reshape(-1)
    onehot = (e_flat[:, None] == jnp.arange(N_EXPERTS, dtype=i32)[None, :]).astype(i32)
    csum = jnp.cumsum(onehot, axis=0)
    rank = jnp.sum(onehot * (csum - 1), axis=1)
    sizes = csum[-1]
    padded = ((sizes + tm - 1) // tm) * tm
    gend = jnp.cumsum(padded)
    gstart = gend - padded
    pos = gstart[e_flat] + rank
    n_rows = n_slots + N_EXPERTS * tm
    slot_plus1 = jnp.zeros((n_rows,), i32).at[pos].set(jnp.arange(1, n_slots + 1, dtype=i32))
    is_pad = slot_plus1 == 0
    pad_rank = jnp.cumsum(is_pad.astype(i32)) - 1
    dst_slot = jnp.where(is_pad, n_slots + pad_rank, slot_plus1 - 1)
    src_tok = jnp.where(is_pad, 0, (slot_plus1 - 1) % T)
    n_tiles = n_rows // tm
    nt = (gend[-1] // tm).astype(i32)
    tile_start = jnp.arange(n_tiles, dtype=i32) * tm
    te = jnp.minimum(jnp.sum((tile_start[:, None] >= gend[None, :]).astype(i32), axis=1), N_EXPERTS - 1)
    te = jnp.where(jnp.arange(n_tiles) < nt, te, te[jnp.maximum(nt - 1, 0)])
    return src_tok.astype(i32), dst_slot.astype(i32), te.astype(i32), nt.reshape(1)


def moe_ffn(h, h_tiles, w_router, wg, wu, wd, g, beta, tm=512):
    T, D = h.shape
    route = router_top2(h, w_router)
    src_tok, dst_slot, te, nt = moe_plan(route, T, tm)
    y_tiles = expert_ffn(h_tiles, te, nt, src_tok, dst_slot, wg, wu, wd, tm)
    return moe_combine(y_tiles, route, h, g, beta)


def permute_w_in(w):
    D = w.shape[0]
    cuts = np.cumsum([RET_QK, RET_QK, RET_V, RET_V, HG_K, HG_K, HG_V, HG_V, SWA_Q, SWA_KV, SWA_KV,
                      D_MODEL, D_MODEL])
    wb = w.astype(BF16)
    rq, rk, rv, rg, hq, hf, hi, hg, sq, sk, sv, ga, gb, gc = jnp.split(wb, cuts.tolist(), axis=1)
    sk = sk.reshape(D, SWA_HKV, SWA_HD)
    sv = sv.reshape(D, SWA_HKV, SWA_HD)
    sq = sq * SWA_HD ** -0.5
    swa = jnp.concatenate([sq.reshape(D, SWA_HKV, SWA_G * SWA_HD), sk, sv, sv, sk],
                          axis=2).reshape(D, SWA_HKV * SWA_GROUP_W)
    main = jnp.concatenate([rq, rk, rv, rg, hq, hi, hg, ga, gb, gc, swa], axis=1)
    return main, hf


def kernel(x, ln_in_g, ln_in_b, w_in, ret_w_out, hgrn_lower_bounds, hgrn_norm_g, hgrn_w_out, swa_sinks,
           swa_w_out, w_o, ln_mix_g, ln_mix_b, ffn_w_gate, ffn_w_up, ffn_w_down, moe_router, moe_w_gate,
           moe_w_up, moe_w_down, ln_ffn_g, ln_ffn_b):
    B, S, D = x.shape
    T = B * S
    assert D == D_MODEL and S % RET_CHUNK == 0 and S % (2 * SWA_CHUNK) == 0 and S >= 4 * SWA_CHUNK

    lb_all = jnp.cumsum(jax.nn.softmax(hgrn_lower_bounds.astype(F32), axis=0), axis=0)
    lb_all = lb_all - lb_all[0]
    tabs = retention_tables(S)

    h, hb = ln_in(x.reshape(T, D), ln_in_g, ln_in_b)
    for layer in range(DEPTH):
        w_main, w_hf = permute_w_in(w_in[layer])
        proj = matmul(hb, w_main, BF16, tn=N_PROJ // 4, name="in_proj")
        hf = matmul(hb, w_hf, F32, tn=HG_K, name="in_proj_forget")

        ya = retention_mixer(proj, tabs, B, S)
        yb = hgrn_mixer(proj, hf, lb_all[layer], hgrn_norm_g[layer], B, S)
        yc = swa_mixer(proj, swa_sinks[layer], B, S)
        dense = layer % 2 == 0
        h, aux = merge_project(ya, yb, yc, proj, h,
                               ret_w_out[layer].astype(BF16), hgrn_w_out[layer].astype(BF16),
                               swa_w_out[layer].astype(BF16), w_o[layer].astype(BF16),
                               ln_mix_g[layer], ln_mix_b[layer], token_tiles=not dense)

        j = layer // 2
        if dense:
            h, hb = dense_ffn(aux, h, ffn_w_gate[j].astype(BF16), ffn_w_up[j].astype(BF16),
                              ffn_w_down[j].astype(BF16), ln_ffn_g[layer], ln_ffn_b[layer])
        else:
            h, hb = moe_ffn(h, aux, moe_router[j], moe_w_gate[j].astype(BF16), moe_w_up[j].astype(BF16),
                            moe_w_down[j].astype(BF16), ln_ffn_g[layer], ln_ffn_b[layer])
    return h.reshape(B, S, D)
```

```python
import functools

import numpy as np
import jax
import jax.numpy as jnp
from jax import lax
from jax.experimental import pallas as pl
from jax.experimental.pallas import tpu as pltpu

F32 = jnp.float32
BF16 = jnp.bfloat16

D_MODEL = 1024
RET_HEADS, RET_DK, RET_DV = 4, 128, 256
RET_QK, RET_V = RET_HEADS * RET_DK, RET_HEADS * RET_DV
ROPE_BASE = 10000.0
HG_HEADS, HG_DK, HG_DV = 8, 128, 128
HG_K, HG_V = HG_HEADS * HG_DK, HG_HEADS * HG_DV
SWA_HQ, SWA_HKV, SWA_HD = 16, 4, 64
SWA_G = SWA_HQ // SWA_HKV
SWA_Q, SWA_KV = SWA_HQ * SWA_HD, SWA_HKV * SWA_HD
SWA_CHUNK = 64
SWA_WIN_CHUNKS = 2
N_EXPERTS, TOP_K = 8, 2
DEPTH = 2
LN_EPS, RMS_EPS = 1e-5, 1e-6
DN_ALPHA = (2.0 * DEPTH) ** 0.25

VMEM_LIMIT_BYTES = 56 * 1024 * 1024
LANES = 128
SUB = 8

RET_CHUNK = 128
RET_BLOCK = 512
HG_CHUNK = 32
HG_BLOCK = 256

OFF_RQ, OFF_RK, OFF_RV, OFF_RG = 0, 512, 1024, 2048
OFF_HQ, OFF_HI, OFF_HG = 3072, 4096, 5120
OFF_GA, OFF_GB, OFF_GC = 6144, 7168, 8192
OFF_SWA = 9216
SWA_GROUP_W = SWA_G * SWA_HD + 4 * SWA_HD
N_PROJ = OFF_SWA + SWA_HKV * SWA_GROUP_W


def _cparams(sem, vmem=VMEM_LIMIT_BYTES):
    return pltpu.CompilerParams(dimension_semantics=sem, vmem_limit_bytes=vmem)


def _layer_norm(x, g, b):
    mu = jnp.mean(x, -1, keepdims=True)
    xc = x - mu
    var = jnp.mean(xc * xc, -1, keepdims=True)
    return xc * lax.rsqrt(var + LN_EPS) * g + b


def _silu(x):
    return x * jax.nn.sigmoid(x)


def _ln_in_kernel(x_ref, g_ref, b_ref, h_ref, hb_ref):
    h = _layer_norm(x_ref[...], g_ref[...], b_ref[...])
    h_ref[...] = h
    hb_ref[...] = h.astype(BF16)


def ln_in(x2, g, b, tm=512):
    T, D = x2.shape
    tm = min(tm, T)
    return pl.pallas_call(
        _ln_in_kernel,
        grid=(T // tm,),
        in_specs=[pl.BlockSpec((tm, D), lambda i: (i, 0)),
                  pl.BlockSpec((1, D), lambda i: (0, 0)),
                  pl.BlockSpec((1, D), lambda i: (0, 0))],
        out_specs=[pl.BlockSpec((tm, D), lambda i: (i, 0)),
                   pl.BlockSpec((tm, D), lambda i: (i, 0))],
        out_shape=[jax.ShapeDtypeStruct((T, D), F32), jax.ShapeDtypeStruct((T, D), BF16)],
        compiler_params=_cparams(("parallel",)),
        name="ln_in",
    )(x2, g.reshape(1, D), b.reshape(1, D))


def _matmul_kernel(x_ref, w_ref, o_ref):
    o_ref[...] = jnp.dot(x_ref[...], w_ref[...], preferred_element_type=F32).astype(o_ref.dtype)


def matmul(x, w, out_dtype, tm=1024, tn=512, name="matmul"):
    T, K = x.shape
    N = w.shape[1]
    tm = min(tm, T)
    return pl.pallas_call(
        _matmul_kernel,
        grid=(T // tm, N // tn),
        in_specs=[pl.BlockSpec((tm, K), lambda i, j: (i, 0)),
                  pl.BlockSpec((K, tn), lambda i, j: (0, j))],
        out_specs=pl.BlockSpec((tm, tn), lambda i, j: (i, j)),
        out_shape=jax.ShapeDtypeStruct((T, N), out_dtype),
        compiler_params=_cparams(("parallel", "parallel")),
        name=name,
    )(x, w)


def _retention_kernel(q_ref, k_ref, v_ref, g_ref, cos_ref, sin_ref, dm_ref, qd_ref, kd_ref, cd_ref,
                      o_ref, st_ref, *, chunk, block, nblocks):
    st_ref[...] = jnp.zeros_like(st_ref)
    dm = dm_ref[0]
    qd = qd_ref[0]
    kd = kd_ref[0]
    cd = cd_ref[0]
    nc = block // chunk
    nt_dims = (((1,), (1,)), ((), ()))
    tn_dims = (((0,), (0,)), ((), ()))

    def body(bi, carry):
        r = pl.ds(pl.multiple_of(bi * block, block), block)
        cos = cos_ref[r, :]
        sin = sin_ref[r, :]
        q = q_ref[r, :].astype(F32)
        k = k_ref[r, :].astype(F32)
        qr = q * cos + pltpu.roll(q, RET_DK // 2, 1) * sin
        kr = k * cos + pltpu.roll(k, RET_DK // 2, 1) * sin
        v = v_ref[r, :]
        rows = [slice(c * chunk, (c + 1) * chunk) for c in range(nc)]
        upd = [lax.dot_general((kr[rc] * kd).astype(BF16), v[rc], tn_dims, preferred_element_type=F32)
               for rc in rows]
        att = [lax.dot_general(qr[rc].astype(BF16), kr[rc].astype(BF16), nt_dims,
                               preferred_element_type=F32) * dm for rc in rows]
        intra = [jnp.dot(a.astype(BF16), v[rc], preferred_element_type=F32) for a, rc in zip(att, rows)]
        q_in = [(qr[rc] * qd).astype(BF16) for rc in rows]
        st = st_ref[...]
        outs = []
        for c in range(nc):
            outs.append(intra[c] + jnp.dot(q_in[c], st.astype(BF16), preferred_element_type=F32))
            st = st * cd + upd[c]
        st_ref[...] = st
        o = jnp.concatenate(outs, axis=0)
        on = o * lax.rsqrt(jnp.mean(o * o, -1, keepdims=True) + RMS_EPS)
        g = g_ref[r, :].astype(F32)
        o_ref[r, :] = (_silu(g) * on).astype(o_ref.dtype)
        return carry

    lax.fori_loop(0, nblocks, body, 0, unroll=2)


def retention_mixer(proj, tabs, B, S):
    T = B * S
    C = RET_CHUNK
    blk = min(RET_BLOCK, S)
    nq, nv = OFF_RQ // RET_DK, OFF_RV // RET_DV
    nk, ng = OFF_RK // RET_DK, OFF_RG // RET_DV
    kern = functools.partial(_retention_kernel, chunk=C, block=blk, nblocks=S // blk)
    return pl.pallas_call(
        kern,
        grid=(B, RET_HEADS),
        in_specs=[pl.BlockSpec((S, RET_DK), lambda b, h: (b, nq + h)),
                  pl.BlockSpec((S, RET_DK), lambda b, h: (b, nk + h)),
                  pl.BlockSpec((S, RET_DV), lambda b, h: (b, nv + h)),
                  pl.BlockSpec((S, RET_DV), lambda b, h: (b, ng + h)),
                  pl.BlockSpec((S, RET_DK), lambda b, h: (0, 0)),
                  pl.BlockSpec((S, RET_DK), lambda b, h: (0, 0)),
                  pl.BlockSpec((1, C, C), lambda b, h: (h, 0, 0)),
                  pl.BlockSpec((1, C, RET_DK), lambda b, h: (h, 0, 0)),
                  pl.BlockSpec((1, C, RET_DK), lambda b, h: (h, 0, 0)),
                  pl.BlockSpec((1, 1, RET_DV), lambda b, h: (h, 0, 0))],
        out_specs=pl.BlockSpec((S, RET_DV), lambda b, h: (b, h)),
        out_shape=jax.ShapeDtypeStruct((T, RET_V), BF16),
        scratch_shapes=[pltpu.VMEM((RET_DK, RET_DV), F32)],
        compiler_params=_cparams(("parallel", "parallel")),
        name="retention",
    )(proj, proj, proj, proj, tabs["cos"], tabs["sin"], tabs["dm"], tabs["qd"], tabs["kd"], tabs["cd"])


def retention_tables(S):
    C = RET_CHUNK
    half = RET_DK // 2
    pos = jnp.arange(S, dtype=F32)
    inv = 1.0 / (ROPE_BASE ** jnp.linspace(0.0, 1.0, half, dtype=F32))
    ang = pos[:, None] * inv[None, :]
    cos, sin = jnp.cos(ang), jnp.sin(ang)
    log_gamma = jnp.log(1.0 - 2.0 ** (-5.0 - jnp.arange(RET_HEADS, dtype=F32)))
    idx = jnp.arange(C, dtype=F32)
    diff = idx[:, None] - idx[None, :]
    decay = jnp.where(diff[None] >= 0,
                      jnp.exp(jnp.maximum(diff, 0.0)[None] * log_gamma[:, None, None]), 0.0)
    scale = RET_DK ** -0.5
    qd = jnp.exp((idx + 1.0)[None, :] * log_gamma[:, None])
    kd = jnp.exp((C - 1.0 - idx)[None, :] * log_gamma[:, None]) * scale
    cd = jnp.exp(C * log_gamma)
    return {
        "cos": jnp.concatenate([cos, cos], axis=1),
        "sin": jnp.concatenate([-sin, sin], axis=1),
        "dm": decay * scale,
        "qd": jnp.broadcast_to(qd[:, :, None], (RET_HEADS, C, RET_DK)),
        "kd": jnp.broadcast_to(kd[:, :, None], (RET_HEADS, C, RET_DK)),
        "cd": jnp.broadcast_to(cd[:, None, None], (RET_HEADS, 1, RET_DV)),
    }


def _hgrn_kernel(q_ref, z_ref, i_ref, g_ref, lb_ref, ng_ref, o_ref, st_ref, *, chunk, block, nblocks):
    st_ref[...] = jnp.zeros_like(st_ref)
    lb = lb_ref[0]
    ng = ng_ref[...]
    half = chunk // 2
    nc = block // chunk
    row = lax.broadcasted_iota(jnp.int32, (block, HG_DK), 0) % chunk
    causal = (lax.broadcasted_iota(jnp.int32, (chunk, chunk), 0)
              >= lax.broadcasted_iota(jnp.int32, (chunk, chunk), 1))
    nt_dims = (((1,), (1,)), ((), ()))
    tn_dims = (((0,), (0,)), ((), ()))

    def per_chunk(x, lo):
        return jnp.concatenate(
            [jnp.broadcast_to(x[c * chunk + lo:c * chunk + lo + 1, :], (chunk, x.shape[1])) for c in range(nc)],
            axis=0)

    def body(bi, carry):
        r = pl.ds(pl.multiple_of(bi * block, block), block)
        f = lb + (1.0 - lb) * jax.nn.sigmoid(z_ref[r, :])
        cum = jnp.log(f)
        shift = 1
        while shift < chunk:
            cum = cum + jnp.where(row >= shift, pltpu.roll(cum, shift, 0), 0.0)
            shift *= 2
        mid = per_chunk(cum, half - 1)
        last = per_chunk(cum, chunk - 1)
        qh = q_ref[r, :].astype(F32) * jnp.exp(cum - mid)
        kh = (1.0 - f) * jnp.exp(mid - cum)
        q_in = (qh * jnp.exp(mid)).astype(BF16)
        k_st = (kh * jnp.exp(last - mid)).astype(BF16)
        dec = jnp.exp(last)
        qh = qh.astype(BF16)
        kh = kh.astype(BF16)
        v = i_ref[r, :]
        rows = [slice(c * chunk, (c + 1) * chunk) for c in range(nc)]
        upd = [lax.dot_general(v[rc], k_st[rc], tn_dims, preferred_element_type=F32) for rc in rows]
        att = [lax.dot_general(qh[rc], kh[rc], nt_dims, preferred_element_type=F32) for rc in rows]
        intra = [jnp.dot(jnp.where(causal, a, 0.0).astype(BF16), v[rc], preferred_element_type=F32)
                 for a, rc in zip(att, rows)]
        st = st_ref[...]
        outs = []
        for c, rc in enumerate(rows):
            inter = lax.dot_general(q_in[rc], st.astype(BF16), nt_dims, preferred_element_type=F32)
            outs.append(intra[c] + inter)
            st = st * dec[c * chunk:c * chunk + 1, :] + upd[c]
        st_ref[...] = st
        o = jnp.concatenate(outs, axis=0)
        on = o * lax.rsqrt(jnp.mean(o * o, -1, keepdims=True) + RMS_EPS) * ng
        g = g_ref[r, :].astype(F32)
        o_ref[r, :] = (_silu(g) * on).astype(o_ref.dtype)
        return carry

    lax.fori_loop(0, nblocks, body, 0, unroll=4)


def hgrn_mixer(proj, hf, lb, norm_g, B, S):
    T = B * S
    C = HG_CHUNK
    blk = min(HG_BLOCK, S)
    nq, ni, ng = OFF_HQ // HG_DK, OFF_HI // HG_DV, OFF_HG // HG_DV
    kern = functools.partial(_hgrn_kernel, chunk=C, block=blk, nblocks=S // blk)
    return pl.pallas_call(
        kern,
        grid=(B, HG_HEADS),
        in_specs=[pl.BlockSpec((S, HG_DK), lambda b, h: (b, nq + h)),
                  pl.BlockSpec((S, HG_DK), lambda b, h: (b, h)),
                  pl.BlockSpec((S, HG_DV), lambda b, h: (b, ni + h)),
                  pl.BlockSpec((S, HG_DV), lambda b, h: (b, ng + h)),
                  pl.BlockSpec((1, 1, HG_DK), lambda b, h: (h, 0, 0)),
                  pl.BlockSpec((1, HG_DV), lambda b, h: (0, 0))],
        out_specs=pl.BlockSpec((S, HG_DV), lambda b, h: (b, h)),
        out_shape=jax.ShapeDtypeStruct((T, HG_V), BF16),
        scratch_shapes=[pltpu.VMEM((HG_DV, HG_DK), F32)],
        compiler_params=_cparams(("parallel", "parallel")),
        name="hgrn2",
    )(proj, hf, proj, proj, lb.reshape(HG_HEADS, 1, HG_DK), norm_g.reshape(1, HG_DV))


def _swa_kernel(sink_ref, x_ref, o_ref, *, nsteps):
    kvh = pl.program_id(1)
    CH = SWA_CHUNK
    HD = SWA_HD
    QR = 2 * CH
    KW = 4 * CH
    QW = SWA_G * HD
    lo_kv = lax.broadcasted_iota(jnp.int32, (KW, 2 * HD), 1) < HD
    lo_o = lax.broadcasted_iota(jnp.int32, (QR, 2 * HD), 1) < HD
    nt_dims = (((1,), (1,)), ((), ()))
    zero = jnp.zeros((KW, 2 * HD), BF16)

    kcol = lax.broadcasted_iota(jnp.int32, (QR, 2 * KW), 1)
    qrow = lax.broadcasted_iota(jnp.int32, (QR, 2 * KW), 0)
    rel = (kcol % KW) // CH - qrow // CH

    def body(j, carry):
        first = jnp.maximum(2 * j - SWA_WIN_CHUNKS, 0)
        rq = pl.ds(pl.multiple_of(j * QR, QR), QR)
        rk = pl.ds(pl.multiple_of(first * CH, CH), KW)
        kv = x_ref[rk, QW:QW + 2 * HD]
        vk = x_ref[rk, QW + 2 * HD:QW + 4 * HD]
        kbd = jnp.concatenate([jnp.where(lo_kv, kv, zero), jnp.where(lo_kv, zero, vk)], axis=0)
        vbd = jnp.concatenate([jnp.where(lo_kv, vk, zero), jnp.where(lo_kv, zero, kv)], axis=0)
        d = rel + (first - 2 * j)
        valid = d * (d + SWA_WIN_CHUNKS) <= 0
        scores = [lax.dot_general(x_ref[rq, p * 2 * HD:(p + 1) * 2 * HD], kbd, nt_dims,
                                  preferred_element_type=F32) for p in range(SWA_G // 2)]
        for p, s in enumerate(scores):
            s = jnp.where(valid, s, -jnp.inf)
            probs, inv = [], []
            for hh in range(2):
                sink = sink_ref[kvh * SWA_G + 2 * p + hh]
                sh = s[:, hh * KW:(hh + 1) * KW]
                m = jnp.maximum(jnp.max(sh, -1, keepdims=True), sink)
                e = jnp.exp(sh - m)
                den = jnp.sum(e, -1, keepdims=True) + jnp.exp(sink - m)
                probs.append(e.astype(BF16))
                inv.append(1.0 / den)
            o = jnp.dot(jnp.concatenate(probs, axis=1), vbd, preferred_element_type=F32)
            o = o * jnp.where(lo_o, inv[0], inv[1])
            o_ref[rq, p * 2 * HD:(p + 1) * 2 * HD] = o.astype(o_ref.dtype)
        return carry

    lax.fori_loop(0, nsteps, body, 0, unroll=2)


def swa_mixer(proj, sinks, B, S):
    T = B * S
    nb = OFF_SWA // SWA_GROUP_W
    kern = functools.partial(_swa_kernel, nsteps=S // (2 * SWA_CHUNK))
    return pl.pallas_call(
        kern,
        grid=(B, SWA_HKV),
        in_specs=[pl.BlockSpec(memory_space=pltpu.SMEM),
                  pl.BlockSpec((S, SWA_GROUP_W), lambda b, h: (b, nb + h))],
        out_specs=pl.BlockSpec((S, SWA_G * SWA_HD), lambda b, h: (b, h)),
        out_shape=jax.ShapeDtypeStruct((T, SWA_Q), BF16),
        compiler_params=_cparams(("parallel", "parallel")),
        name="swa",
    )(sinks.astype(F32), proj)


def _merge_kernel(a_ref, b_ref, c_ref, ga_ref, gb_ref, gc_ref, h_ref, wa_ref, wb_ref, wc_ref, wo_ref,
                  g_ref, beta_ref, ho_ref, aux_ref, *, tm, token_tiles):
    def branch(x_ref, gate_ref, w_ref):
        y = jnp.dot(x_ref[...], w_ref[...], preferred_element_type=F32)
        return jax.nn.sigmoid(gate_ref[...].astype(F32)) * y

    merged = branch(a_ref, ga_ref, wa_ref) + branch(b_ref, gb_ref, wb_ref) + branch(c_ref, gc_ref, wc_ref)
    mix = jnp.dot(merged.astype(BF16), wo_ref[...], preferred_element_type=F32)
    hn = _layer_norm(DN_ALPHA * h_ref[...] + mix, g_ref[...], beta_ref[...])
    ho_ref[...] = hn
    if token_tiles:
        for j in range(SUB):
            aux_ref[pl.ds(j, tm, stride=SUB), :] = hn[:, j * LANES:(j + 1) * LANES]
    else:
        aux_ref[...] = hn.astype(BF16)


def merge_project(ya, yb, yc, proj, h, wa, wb, wc, wo, g, beta, token_tiles, tm=512):
    T, D = h.shape
    tm = min(tm, T)
    row = lambda i: (i, 0)
    const = lambda i: (0, 0)
    wspec = pl.BlockSpec((D, D), const, pipeline_mode=pl.Buffered(1))
    if token_tiles:
        aux_spec = pl.BlockSpec((tm * SUB, LANES), row)
        aux_shape = jax.ShapeDtypeStruct((T * SUB, LANES), F32)
    else:
        aux_spec = pl.BlockSpec((tm, D), row)
        aux_shape = jax.ShapeDtypeStruct((T, D), BF16)
    kern = functools.partial(_merge_kernel, tm=tm, token_tiles=token_tiles)
    return pl.pallas_call(
        kern,
        grid=(T // tm,),
        in_specs=[pl.BlockSpec((tm, D), row), pl.BlockSpec((tm, D), row), pl.BlockSpec((tm, D), row),
                  pl.BlockSpec((tm, D), lambda i: (i, OFF_GA // D_MODEL)),
                  pl.BlockSpec((tm, D), lambda i: (i, OFF_GB // D_MODEL)),
                  pl.BlockSpec((tm, D), lambda i: (i, OFF_GC // D_MODEL)),
                  pl.BlockSpec((tm, D), row),
                  wspec, wspec, wspec, wspec,
                  pl.BlockSpec((1, D), const), pl.BlockSpec((1, D), const)],
        out_specs=[pl.BlockSpec((tm, D), row), aux_spec],
        out_shape=[jax.ShapeDtypeStruct((T, D), F32), aux_shape],
        compiler_params=_cparams(("parallel",)),
        name="merge_project",
    )(ya, yb, yc, proj, proj, proj, h, wa, wb, wc, wo, g.reshape(1, D), beta.reshape(1, D))


def _swiglu_step(x, wg_ref, wu_ref, wd_ref, acc_ref):
    g = jnp.dot(x, wg_ref[...], preferred_element_type=F32)
    u = jnp.dot(x, wu_ref[...], preferred_element_type=F32)
    a = (_silu(g) * u).astype(BF16)
    acc_ref[...] += jnp.dot(a, wd_ref[...], preferred_element_type=F32)


def _ffn_kernel(xb_ref, h_ref, wg_ref, wu_ref, wd_ref, g_ref, beta_ref, ho_ref, hob_ref):
    x = xb_ref[...]
    gate = jnp.dot(x, wg_ref[...], preferred_element_type=F32)
    up = jnp.dot(x, wu_ref[...], preferred_element_type=F32)
    a = (_silu(gate) * up).astype(BF16)
    ff = jnp.dot(a, wd_ref[...], preferred_element_type=F32)
    hn = _layer_norm(DN_ALPHA * h_ref[...] + ff, g_ref[...], beta_ref[...])
    ho_ref[...] = hn
    hob_ref[...] = hn.astype(BF16)


def dense_ffn(hb, h, wg, wu, wd, g, beta, tm=512):
    T, D = h.shape
    F = wg.shape[1]
    tm = min(tm, T)
    row = lambda i: (i, 0)
    const = lambda i: (0, 0)
    resident = pl.Buffered(1)
    return pl.pallas_call(
        _ffn_kernel,
        grid=(T // tm,),
        in_specs=[pl.BlockSpec((tm, D), row), pl.BlockSpec((tm, D), row),
                  pl.BlockSpec((D, F), const, pipeline_mode=resident),
                  pl.BlockSpec((D, F), const, pipeline_mode=resident),
                  pl.BlockSpec((F, D), const, pipeline_mode=resident),
                  pl.BlockSpec((1, D), const), pl.BlockSpec((1, D), const)],
        out_specs=[pl.BlockSpec((tm, D), row), pl.BlockSpec((tm, D), row)],
        out_shape=[jax.ShapeDtypeStruct((T, D), F32), jax.ShapeDtypeStruct((T, D), BF16)],
        compiler_params=_cparams(("parallel",)),
        name="dense_ffn",
    )(hb, h, wg, wu, wd, g.reshape(1, D), beta.reshape(1, D))


def _router_kernel(h_ref, w_ref, o_ref):
    logits = jnp.dot(h_ref[...], w_ref[...], preferred_element_type=F32, precision=lax.Precision.HIGHEST)
    lane = lax.broadcasted_iota(jnp.int32, logits.shape, 1)
    l1 = jnp.where(lane < N_EXPERTS, logits, -jnp.inf)
    m1 = jnp.max(l1, -1, keepdims=True)
    i1 = jnp.min(jnp.where(l1 == m1, lane, LANES), -1, keepdims=True)
    l2 = jnp.where(lane == i1, -jnp.inf, l1)
    m2 = jnp.max(l2, -1, keepdims=True)
    i2 = jnp.min(jnp.where(l2 == m2, lane, LANES), -1, keepdims=True)
    e = jnp.exp(m2 - m1)
    w1 = 1.0 / (1.0 + e)
    w2 = e / (1.0 + e)
    out = jnp.where(lane == 0, i1.astype(F32),
                    jnp.where(lane == 1, i2.astype(F32),
                              jnp.where(lane == 2, w1, jnp.where(lane == 3, w2, 0.0))))
    o_ref[...] = out


def router_top2(h, w_router, tm=512):
    T, D = h.shape
    tm = min(tm, T)
    wpad = jnp.zeros((D, LANES), F32).at[:, :N_EXPERTS].set(w_router.astype(F32))
    return pl.pallas_call(
        _router_kernel,
        grid=(T // tm,),
        in_specs=[pl.BlockSpec((tm, D), lambda i: (i, 0)), pl.BlockSpec((D, LANES), lambda i: (0, 0))],
        out_specs=pl.BlockSpec((tm, LANES), lambda i: (i, 0)),
        out_shape=jax.ShapeDtypeStruct((T, LANES), F32),
        compiler_params=_cparams(("parallel",)),
        name="router_top2",
    )(h, wpad)


def _expert_ffn_kernel(te_ref, nt_ref, src_ref, dst_ref, h_hbm, wg_ref, wu_ref, wd_ref, y_hbm,
                       xbuf, xb_ref, ybuf, acc_ref, gsem, ssem, *, tm, nf):
    i = pl.program_id(0)
    f = pl.program_id(1)
    n_i = pl.num_programs(0)
    active = i < nt_ref[0]
    buf = i % 2
    share = tm // nf
    nxt = jnp.minimum(i + 1, n_i - 1)
    prv = jnp.maximum(i - 1, 0)

    def gather_row(tile, b, r):
        tok = src_ref[tile * tm + r]
        pltpu.make_async_copy(h_hbm.at[pl.ds(pl.multiple_of(tok * SUB, SUB), SUB), :],
                              xbuf.at[b, pl.ds(pl.multiple_of(r * SUB, SUB), SUB), :], gsem.at[b]).start()

    def scatter_row(tile, r):
        slot = dst_ref[tile * tm + r]
        pltpu.make_async_copy(ybuf.at[pl.ds(pl.multiple_of(r * SUB, SUB), SUB), :],
                              y_hbm.at[pl.ds(pl.multiple_of(slot * SUB, SUB), SUB), :], ssem).start()

    def wait_gather(b):
        pltpu.make_async_copy(h_hbm.at[pl.ds(0, tm * SUB), :], xbuf.at[b], gsem.at[b]).wait()

    def wait_scatter():
        pltpu.make_async_copy(ybuf, y_hbm.at[pl.ds(0, tm * SUB), :], ssem).wait()

    def looped(n, fn):
        def body(r, carry):
            fn(r)
            return carry
        lax.fori_loop(0, n, body, 0, unroll=8)

    def share_row(r):
        gather_row(nxt, 1 - buf, f * share + r)
        scatter_row(prv, f * share + r)

    @pl.when(f == 0)
    def _():
        @pl.when(i == 0)
        def _():
            looped(tm, lambda r: gather_row(0, 0, r))
            ybuf[...] = jnp.zeros_like(ybuf)

        wait_gather(buf)
        for j in range(SUB):
            xb_ref[:, j * LANES:(j + 1) * LANES] = xbuf[buf, pl.ds(j, tm, stride=SUB), :].astype(BF16)
        acc_ref[...] = jnp.zeros_like(acc_ref)

    @pl.when(active)
    def _():
        for r in range(share):
            share_row(r)
        _swiglu_step(xb_ref[...], wg_ref, wu_ref, wd_ref, acc_ref)

    @pl.when(jnp.logical_not(active))
    def _():
        looped(share, share_row)

    @pl.when(f == nf - 1)
    def _():
        wait_scatter()
        for j in range(SUB):
            ybuf[pl.ds(j, tm, stride=SUB), :] = acc_ref[:, j * LANES:(j + 1) * LANES]

        @pl.when(i == n_i - 1)
        def _():
            looped(tm, lambda r: scatter_row(i, r))
            wait_scatter()
            wait_gather(1 - buf)


def expert_ffn(h_tiles, te, nt, src_tok, dst_slot, wg, wu, wd, tm, tf=None):
    P = src_tok.shape[0]
    D = wg.shape[1]
    F = wg.shape[2]
    tf = F if tf is None else tf
    nf = F // tf

    def wcol(i, f, te_ref, nt_ref, src_ref, dst_ref):
        return (te_ref[i], 0, jnp.where(i < nt_ref[0], f, nf - 1))

    def wrow(i, f, te_ref, nt_ref, src_ref, dst_ref):
        return (te_ref[i], jnp.where(i < nt_ref[0], f, nf - 1), 0)

    wmode = pl.Buffered(1) if nf == 1 else pl.Buffered(2)
    kern = functools.partial(_expert_ffn_kernel, tm=tm, nf=nf)
    return pl.pallas_call(
        kern,
        grid_spec=pltpu.PrefetchScalarGridSpec(
            num_scalar_prefetch=4,
            grid=(P // tm, nf),
            in_specs=[pl.BlockSpec(memory_space=pl.ANY),
                      pl.BlockSpec((None, D, tf), wcol, pipeline_mode=wmode),
                      pl.BlockSpec((None, D, tf), wcol, pipeline_mode=wmode),
                      pl.BlockSpec((None, tf, D), wrow, pipeline_mode=wmode)],
            out_specs=pl.BlockSpec(memory_space=pl.ANY),
            scratch_shapes=[pltpu.VMEM((2, tm * SUB, LANES), F32),
                            pltpu.VMEM((tm, D), BF16),
                            pltpu.VMEM((tm * SUB, LANES), F32),
                            pltpu.VMEM((tm, D), F32),
                            pltpu.SemaphoreType.DMA((2,)),
                            pltpu.SemaphoreType.DMA(())],
        ),
        out_shape=jax.ShapeDtypeStruct((P * SUB, LANES), F32),
        compiler_params=_cparams(("arbitrary", "arbitrary")),
        name="expert_ffn",
    )(te, nt, src_tok, dst_slot, h_tiles, wg, wu, wd)


def _moe_combine_kernel(y0_ref, y1_ref, r_ref, h_ref, g_ref, beta_ref, ho_ref, hob_ref, *, tm):
    def rows(y_ref):
        return jnp.concatenate([y_ref[pl.ds(j, tm, stride=SUB), :] for j in range(SUB)], axis=1)

    r = r_ref[...]
    ff = rows(y0_ref) * r[:, 2:3] + rows(y1_ref) * r[:, 3:4]
    hn = _layer_norm(DN_ALPHA * h_ref[...] + ff, g_ref[...], beta_ref[...])
    ho_ref[...] = hn
    hob_ref[...] = hn.astype(BF16)


def moe_combine(y_tiles, route, h, g, beta, tm=512):
    T, D = h.shape
    tm = min(tm, T)
    nb = T // tm
    row = lambda i: (i, 0)
    const = lambda i: (0, 0)
    kern = functools.partial(_moe_combine_kernel, tm=tm)
    return pl.pallas_call(
        kern,
        grid=(nb,),
        in_specs=[pl.BlockSpec((tm * SUB, LANES), lambda i: (i, 0)),
                  pl.BlockSpec((tm * SUB, LANES), lambda i: (nb + i, 0)),
                  pl.BlockSpec((tm, LANES), row),
                  pl.BlockSpec((tm, D), row),
                  pl.BlockSpec((1, D), const), pl.BlockSpec((1, D), const)],
        out_specs=[pl.BlockSpec((tm, D), row), pl.BlockSpec((tm, D), row)],
        out_shape=[jax.ShapeDtypeStruct((T, D), F32), jax.ShapeDtypeStruct((T, D), BF16)],
        compiler_params=_cparams(("parallel",)),
        name="moe_combine",
    )(y_tiles, y_tiles, route, h, g.reshape(1, D), beta.reshape(1, D))


def moe_plan(route, T, tm):
    i32 = jnp.int32
    n_slots = TOP_K * T
    e_flat = route[:, 0:TOP_K].astype(i32).T.reshape(-1)
    onehot = (e_flat[:, None] == jnp.arange(N_EXPERTS, dtype=i32)[None, :]).astype(i32)
    csum = jnp.cumsum(onehot, axis=0)
    rank = jnp.sum(onehot * (csum - 1), axis=1)
    sizes = csum[-1]
    padded = ((sizes + tm - 1) // tm) * tm
    gend = jnp.cumsum(padded)
    gstart = gend - padded
    pos = gstart[e_flat] + rank
    n_rows = n_slots + N_EXPERTS * tm
    slot_plus1 = jnp.zeros((n_rows,), i32).at[pos].set(jnp.arange(1, n_slots + 1, dtype=i32))
    is_pad = slot_plus1 == 0
    pad_rank = jnp.cumsum(is_pad.astype(i32)) - 1
    dst_slot = jnp.where(is_pad, n_slots + pad_rank, slot_plus1 - 1)
    src_tok = jnp.where(is_pad, 0, (slot_plus1 - 1) % T)
    n_tiles = n_rows // tm
    nt = (gend[-1] // tm).astype(i32)
    tile_start = jnp.arange(n_tiles, dtype=i32) * tm
    te = jnp.minimum(jnp.sum((tile_start[:, None] >= gend[None, :]).astype(i32), axis=1), N_EXPERTS - 1)
    te = jnp.where(jnp.arange(n_tiles) < nt, te, te[jnp.maximum(nt - 1, 0)])
    return src_tok.astype(i32), dst_slot.astype(i32), te.astype(i32), nt.reshape(1)


def moe_ffn(h, h_tiles, w_router, wg, wu, wd, g, beta, tm=512):
    T, D = h.shape
    route = router_top2(h, w_router)
    src_tok, dst_slot, te, nt = moe_plan(route, T, tm)
    y_tiles = expert_ffn(h_tiles, te, nt, src_tok, dst_slot, wg, wu, wd, tm)
    return moe_combine(y_tiles, route, h, g, beta)


def permute_w_in(w):
    D = w.shape[0]
    cuts = np.cumsum([RET_QK, RET_QK, RET_V, RET_V, HG_K, HG_K, HG_V, HG_V, SWA_Q, SWA_KV, SWA_KV,
                      D_MODEL, D_MODEL])
    wb = w.astype(BF16)
    rq, rk, rv, rg, hq, hf, hi, hg, sq, sk, sv, ga, gb, gc = jnp.split(wb, cuts.tolist(), axis=1)
    sk = sk.reshape(D, SWA_HKV, SWA_HD)
    sv = sv.reshape(D, SWA_HKV, SWA_HD)
    sq = sq * SWA_HD ** -0.5
    swa = jnp.concatenate([sq.reshape(D, SWA_HKV, SWA_G * SWA_HD), sk, sv, sv, sk],
                          axis=2).reshape(D, SWA_HKV * SWA_GROUP_W)
    main = jnp.concatenate([rq, rk, rv, rg, hq, hi, hg, ga, gb, gc, swa], axis=1)
    return main, hf


def kernel(x, ln_in_g, ln_in_b, w_in, ret_w_out, hgrn_lower_bounds, hgrn_norm_g, hgrn_w_out, swa_sinks,
           swa_w_out, w_o, ln_mix_g, ln_mix_b, ffn_w_gate, ffn_w_up, ffn_w_down, moe_router, moe_w_gate,
           moe_w_up, moe_w_down, ln_ffn_g, ln_ffn_b):
    B, S, D = x.shape
    T = B * S
    assert D == D_MODEL and S % RET_CHUNK == 0 and S % (2 * SWA_CHUNK) == 0 and S >= 4 * SWA_CHUNK

    lb_all = jnp.cumsum(jax.nn.softmax(hgrn_lower_bounds.astype(F32), axis=0), axis=0)
    lb_all = lb_all - lb_all[0]
    tabs = retention_tables(S)

    h, hb = ln_in(x.reshape(T, D), ln_in_g, ln_in_b)
    for layer in range(DEPTH):
        w_main, w_hf = permute_w_in(w_in[layer])
        proj = matmul(hb, w_main, BF16, tn=N_PROJ // 4, name="in_proj")
        hf = matmul(hb, w_hf, F32, tn=HG_K, name="in_proj_forget")

        ya = retention_mixer(proj, tabs, B, S)
        yb = hgrn_mixer(proj, hf, lb_all[layer], hgrn_norm_g[layer], B, S)
        yc = swa_mixer(proj, swa_sinks[layer], B, S)
        dense = layer % 2 == 0
        h, aux = merge_project(ya, yb, yc, proj, h,
                               ret_w_out[layer].astype(BF16), hgrn_w_out[layer].astype(BF16),
                               swa_w_out[layer].astype(BF16), w_o[layer].astype(BF16),
                               ln_mix_g[layer], ln_mix_b[layer], token_tiles=not dense)

        j = layer // 2
        if dense:
            h, hb = dense_ffn(aux, h, ffn_w_gate[j].astype(BF16), ffn_w_up[j].astype(BF16),
                              ffn_w_down[j].astype(BF16), ln_ffn_g[layer], ln_ffn_b[layer])
        else:
            h, hb = moe_ffn(h, aux, moe_router[j], moe_w_gate[j].astype(BF16), moe_w_up[j].astype(BF16),
                            moe_w_down[j].astype(BF16), ln_ffn_g[layer], ln_ffn_b[layer])
    return h.reshape(B, S, D)
```

```python
import functools

import numpy as np
import jax
import jax.numpy as jnp
from jax import lax
from jax.experimental import pallas as pl
from jax.experimental.pallas import tpu as pltpu

F32 = jnp.float32
BF16 = jnp.bfloat16

D_MODEL = 1024
RET_HEADS, RET_DK, RET_DV = 4, 128, 256
RET_QK, RET_V = RET_HEADS * RET_DK, RET_HEADS * RET_DV
ROPE_BASE = 10000.0
HG_HEADS, HG_DK, HG_DV = 8, 128, 128
HG_K, HG_V = HG_HEADS * HG_DK, HG_HEADS * HG_DV
SWA_HQ, SWA_HKV, SWA_HD = 16, 4, 64
SWA_G = SWA_HQ // SWA_HKV
SWA_Q, SWA_KV = SWA_HQ * SWA_HD, SWA_HKV * SWA_HD
SWA_CHUNK = 64
SWA_WIN_CHUNKS = 2
N_EXPERTS, TOP_K = 8, 2
DEPTH = 2
LN_EPS, RMS_EPS = 1e-5, 1e-6
DN_ALPHA = (2.0 * DEPTH) ** 0.25

VMEM_LIMIT_BYTES = 56 * 1024 * 1024
LANES = 128
SUB = 8

RET_CHUNK = 128
RET_BLOCK = 512
HG_CHUNK = 32
HG_BLOCK = 256

OFF_RQ, OFF_RK, OFF_RV, OFF_RG = 0, 512, 1024, 2048
OFF_HQ, OFF_HI, OFF_HG = 3072, 4096, 5120
OFF_GA, OFF_GB, OFF_GC = 6144, 7168, 8192
OFF_SWA = 9216
SWA_GROUP_W = SWA_G * SWA_HD + 4 * SWA_HD
N_PROJ = OFF_SWA + SWA_HKV * SWA_GROUP_W


def _cparams(sem, vmem=VMEM_LIMIT_BYTES):
    return pltpu.CompilerParams(dimension_semantics=sem, vmem_limit_bytes=vmem)


def _layer_norm(x, g, b):
    mu = jnp.mean(x, -1, keepdims=True)
    xc = x - mu
    var = jnp.mean(xc * xc, -1, keepdims=True)
    return xc * lax.rsqrt(var + LN_EPS) * g + b


def _silu(x):
    return x * jax.nn.sigmoid(x)


def _ln_in_kernel(x_ref, g_ref, b_ref, h_ref, hb_ref):
    h = _layer_norm(x_ref[...], g_ref[...], b_ref[...])
    h_ref[...] = h
    hb_ref[...] = h.astype(BF16)


def ln_in(x2, g, b, tm=512):
    T, D = x2.shape
    tm = min(tm, T)
    return pl.pallas_call(
        _ln_in_kernel,
        grid=(T // tm,),
        in_specs=[pl.BlockSpec((tm, D), lambda i: (i, 0)),
                  pl.BlockSpec((1, D), lambda i: (0, 0)),
                  pl.BlockSpec((1, D), lambda i: (0, 0))],
        out_specs=[pl.BlockSpec((tm, D), lambda i: (i, 0)),
                   pl.BlockSpec((tm, D), lambda i: (i, 0))],
        out_shape=[jax.ShapeDtypeStruct((T, D), F32), jax.ShapeDtypeStruct((T, D), BF16)],
        compiler_params=_cparams(("parallel",)),
        name="ln_in",
    )(x2, g.reshape(1, D), b.reshape(1, D))


def _matmul_kernel(x_ref, w_ref, o_ref):
    o_ref[...] = jnp.dot(x_ref[...], w_ref[...], preferred_element_type=F32).astype(o_ref.dtype)


def matmul(x, w, out_dtype, tm=1024, tn=512, name="matmul"):
    T, K = x.shape
    N = w.shape[1]
    tm = min(tm, T)
    return pl.pallas_call(
        _matmul_kernel,
        grid=(T // tm, N // tn),
        in_specs=[pl.BlockSpec((tm, K), lambda i, j: (i, 0)),
                  pl.BlockSpec((K, tn), lambda i, j: (0, j))],
        out_specs=pl.BlockSpec((tm, tn), lambda i, j: (i, j)),
        out_shape=jax.ShapeDtypeStruct((T, N), out_dtype),
        compiler_params=_cparams(("parallel", "parallel")),
        name=name,
    )(x, w)


def _retention_kernel(q_ref, k_ref, v_ref, g_ref, cos_ref, sin_ref, dm_ref, qd_ref, kd_ref, cd_ref,
                      o_ref, st_ref, *, chunk, block, nblocks):
    st_ref[...] = jnp.zeros_like(st_ref)
    dm = dm_ref[0]
    qd = qd_ref[0]
    kd = kd_ref[0]
    cd = cd_ref[0]
    nc = block // chunk
    nt_dims = (((1,), (1,)), ((), ()))
    tn_dims = (((0,), (0,)), ((), ()))

    def body(bi, carry):
        r = pl.ds(pl.multiple_of(bi * block, block), block)
        cos = cos_ref[r, :]
        sin = sin_ref[r, :]
        q = q_ref[r, :].astype(F32)
        k = k_ref[r, :].astype(F32)
        qr = q * cos + pltpu.roll(q, RET_DK // 2, 1) * sin
        kr = k * cos + pltpu.roll(k, RET_DK // 2, 1) * sin
        v = v_ref[r, :]
        rows = [slice(c * chunk, (c + 1) * chunk) for c in range(nc)]
        upd = [lax.dot_general((kr[rc] * kd).astype(BF16), v[rc], tn_dims, preferred_element_type=F32)
               for rc in rows]
        att = [lax.dot_general(qr[rc].astype(BF16), kr[rc].astype(BF16), nt_dims,
                               preferred_element_type=F32) * dm for rc in rows]
        intra = [jnp.dot(a.astype(BF16), v[rc], preferred_element_type=F32) for a, rc in zip(att, rows)]
        q_in = [(qr[rc] * qd).astype(BF16) for rc in rows]
        st = st_ref[...]
        outs = []
        for c in range(nc):
            outs.append(intra[c] + jnp.dot(q_in[c], st.astype(BF16), preferred_element_type=F32))
            st = st * cd + upd[c]
        st_ref[...] = st
        o = jnp.concatenate(outs, axis=0)
        on = o * lax.rsqrt(jnp.mean(o * o, -1, keepdims=True) + RMS_EPS)
        g = g_ref[r, :].astype(F32)
        o_ref[r, :] = (_silu(g) * on).astype(o_ref.dtype)
        return carry

    lax.fori_loop(0, nblocks, body, 0, unroll=2)


def retention_mixer(proj, tabs, B, S):
    T = B * S
    C = RET_CHUNK
    blk = min(RET_BLOCK, S)
    nq, nv = OFF_RQ // RET_DK, OFF_RV // RET_DV
    nk, ng = OFF_RK // RET_DK, OFF_RG // RET_DV
    kern = functools.partial(_retention_kernel, chunk=C, block=blk, nblocks=S // blk)
    return pl.pallas_call(
        kern,
        grid=(B, RET_HEADS),
        in_specs=[pl.BlockSpec((S, RET_DK), lambda b, h: (b, nq + h)),
                  pl.BlockSpec((S, RET_DK), lambda b, h: (b, nk + h)),
                  pl.BlockSpec((S, RET_DV), lambda b, h: (b, nv + h)),
                  pl.BlockSpec((S, RET_DV), lambda b, h: (b, ng + h)),
                  pl.BlockSpec((S, RET_DK), lambda b, h: (0, 0)),
                  pl.BlockSpec((S, RET_DK), lambda b, h: (0, 0)),
                  pl.BlockSpec((1, C, C), lambda b, h: (h, 0, 0)),
                  pl.BlockSpec((1, C, RET_DK), lambda b, h: (h, 0, 0)),
                  pl.BlockSpec((1, C, RET_DK), lambda b, h: (h, 0, 0)),
                  pl.BlockSpec((1, 1, RET_DV), lambda b, h: (h, 0, 0))],
        out_specs=pl.BlockSpec((S, RET_DV), lambda b, h: (b, h)),
        out_shape=jax.ShapeDtypeStruct((T, RET_V), BF16),
        scratch_shapes=[pltpu.VMEM((RET_DK, RET_DV), F32)],
        compiler_params=_cparams(("parallel", "parallel")),
        name="retention",
    )(proj, proj, proj, proj, tabs["cos"], tabs["sin"], tabs["dm"], tabs["qd"], tabs["kd"], tabs["cd"])


def retention_tables(S):
    C = RET_CHUNK
    half = RET_DK // 2
    pos = jnp.arange(S, dtype=F32)
    inv = 1.0 / (ROPE_BASE ** jnp.linspace(0.0, 1.0, half, dtype=F32))
    ang = pos[:, None] * inv[None, :]
    cos, sin = jnp.cos(ang), jnp.sin(ang)
    log_gamma = jnp.log(1.0 - 2.0 ** (-5.0 - jnp.arange(RET_HEADS, dtype=F32)))
    idx = jnp.arange(C, dtype=F32)
    diff = idx[:, None] - idx[None, :]
    decay = jnp.where(diff[None] >= 0,
                      jnp.exp(jnp.maximum(diff, 0.0)[None] * log_gamma[:, None, None]), 0.0)
    scale = RET_DK ** -0.5
    qd = jnp.exp((idx + 1.0)[None, :] * log_gamma[:, None])
    kd = jnp.exp((C - 1.0 - idx)[None, :] * log_gamma[:, None]) * scale
    cd = jnp.exp(C * log_gamma)
    return {
        "cos": jnp.concatenate([cos, cos], axis=1),
        "sin": jnp.concatenate([-sin, sin], axis=1),
        "dm": decay * scale,
        "qd": jnp.broadcast_to(qd[:, :, None], (RET_HEADS, C, RET_DK)),
        "kd": jnp.broadcast_to(kd[:, :, None], (RET_HEADS, C, RET_DK)),
        "cd": jnp.broadcast_to(cd[:, None, None], (RET_HEADS, 1, RET_DV)),
    }


def _hgrn_kernel(q_ref, z_ref, i_ref, g_ref, lb_ref, ng_ref, o_ref, st_ref, *, chunk, block, nblocks):
    st_ref[...] = jnp.zeros_like(st_ref)
    lb = lb_ref[0]
    ng = ng_ref[...]
    half = chunk // 2
    nc = block // chunk
    row = lax.broadcasted_iota(jnp.int32, (block, HG_DK), 0) % chunk
    causal = (lax.broadcasted_iota(jnp.int32, (chunk, chunk), 0)
              >= lax.broadcasted_iota(jnp.int32, (chunk, chunk), 1))
    nt_dims = (((1,), (1,)), ((), ()))
    tn_dims = (((0,), (0,)), ((), ()))

    def per_chunk(x, lo):
        return jnp.concatenate(
            [jnp.broadcast_to(x[c * chunk + lo:c * chunk + lo + 1, :], (chunk, x.shape[1])) for c in range(nc)],
            axis=0)

    def body(bi, carry):
        r = pl.ds(pl.multiple_of(bi * block, block), block)
        f = lb + (1.0 - lb) * jax.nn.sigmoid(z_ref[r, :])
        cum = jnp.log(f)
        shift = 1
        while shift < chunk:
            cum = cum + jnp.where(row >= shift, pltpu.roll(cum, shift, 0), 0.0)
            shift *= 2
        mid = per_chunk(cum, half - 1)
        last = per_chunk(cum, chunk - 1)
        qh = q_ref[r, :].astype(F32) * jnp.exp(cum - mid)
        kh = (1.0 - f) * jnp.exp(mid - cum)
        q_in = (qh * jnp.exp(mid)).astype(BF16)
        k_st = (kh * jnp.exp(last - mid)).astype(BF16)
        dec = jnp.exp(last)
        qh = qh.astype(BF16)
        kh = kh.astype(BF16)
        v = i_ref[r, :]
        rows = [slice(c * chunk, (c + 1) * chunk) for c in range(nc)]
        upd = [lax.dot_general(v[rc], k_st[rc], tn_dims, preferred_element_type=F32) for rc in rows]
        att = [lax.dot_general(qh[rc], kh[rc], nt_dims, preferred_element_type=F32) for rc in rows]
        intra = [jnp.dot(jnp.where(causal, a, 0.0).astype(BF16), v[rc], preferred_element_type=F32)
                 for a, rc in zip(att, rows)]
        st = st_ref[...]
        outs = []
        for c, rc in enumerate(rows):
            inter = lax.dot_general(q_in[rc], st.astype(BF16), nt_dims, preferred_element_type=F32)
            outs.append(intra[c] + inter)
            st = st * dec[c * chunk:c * chunk + 1, :] + upd[c]
        st_ref[...] = st
        o = jnp.concatenate(outs, axis=0)
        on = o * lax.rsqrt(jnp.mean(o * o, -1, keepdims=True) + RMS_EPS) * ng
        g = g_ref[r, :].astype(F32)
        o_ref[r, :] = (_silu(g) * on).astype(o_ref.dtype)
        return carry

    lax.fori_loop(0, nblocks, body, 0, unroll=4)


def hgrn_mixer(proj, hf, lb, norm_g, B, S):
    T = B * S
    C = HG_CHUNK
    blk = min(HG_BLOCK, S)
    nq, ni, ng = OFF_HQ // HG_DK, OFF_HI // HG_DV, OFF_HG // HG_DV
    kern = functools.partial(_hgrn_kernel, chunk=C, block=blk, nblocks=S // blk)
    return pl.pallas_call(
        kern,
        grid=(B, HG_HEADS),
        in_specs=[pl.BlockSpec((S, HG_DK), lambda b, h: (b, nq + h)),
                  pl.BlockSpec((S, HG_DK), lambda b, h: (b, h)),
                  pl.BlockSpec((S, HG_DV), lambda b, h: (b, ni + h)),
                  pl.BlockSpec((S, HG_DV), lambda b, h: (b, ng + h)),
                  pl.BlockSpec((1, 1, HG_DK), lambda b, h: (h, 0, 0)),
                  pl.BlockSpec((1, HG_DV), lambda b, h: (0, 0))],
        out_specs=pl.BlockSpec((S, HG_DV), lambda b, h: (b, h)),
        out_shape=jax.ShapeDtypeStruct((T, HG_V), BF16),
        scratch_shapes=[pltpu.VMEM((HG_DV, HG_DK), F32)],
        compiler_params=_cparams(("parallel", "parallel")),
        name="hgrn2",
    )(proj, hf, proj, proj, lb.reshape(HG_HEADS, 1, HG_DK), norm_g.reshape(1, HG_DV))


def _swa_kernel(sink_ref, x_ref, o_ref, *, nsteps):
    kvh = pl.program_id(1)
    CH = SWA_CHUNK
    HD = SWA_HD
    QR = 2 * CH
    KW = 4 * CH
    QW = SWA_G * HD
    lo_kv = lax.broadcasted_iota(jnp.int32, (KW, 2 * HD), 1) < HD
    lo_o = lax.broadcasted_iota(jnp.int32, (QR, 2 * HD), 1) < HD
    nt_dims = (((1,), (1,)), ((), ()))
    zero = jnp.zeros((KW, 2 * HD), BF16)

    kcol = lax.broadcasted_iota(jnp.int32, (QR, 2 * KW), 1)
    qrow = lax.broadcasted_iota(jnp.int32, (QR, 2 * KW), 0)
    rel = (kcol % KW) // CH - qrow // CH

    def body(j, carry):
        first = jnp.maximum(2 * j - SWA_WIN_CHUNKS, 0)
        rq = pl.ds(pl.multiple_of(j * QR, QR), QR)
        rk = pl.ds(pl.multiple_of(first * CH, CH), KW)
        kv = x_ref[rk, QW:QW + 2 * HD]
        vk = x_ref[rk, QW + 2 * HD:QW + 4 * HD]
        kbd = jnp.concatenate([jnp.where(lo_kv, kv, zero), jnp.where(lo_kv, zero, vk)], axis=0)
        vbd = jnp.concatenate([jnp.where(lo_kv, vk, zero), jnp.where(lo_kv, zero, kv)], axis=0)
        d = rel + (first - 2 * j)
        valid = d * (d + SWA_WIN_CHUNKS) <= 0
        scores = [lax.dot_general(x_ref[rq, p * 2 * HD:(p + 1) * 2 * HD], kbd, nt_dims,
                                  preferred_element_type=F32) for p in range(SWA_G // 2)]
        for p, s in enumerate(scores):
            s = jnp.where(valid, s, -jnp.inf)
            probs, inv = [], []
            for hh in range(2):
                sink = sink_ref[kvh * SWA_G + 2 * p + hh]
                sh = s[:, hh * KW:(hh + 1) * KW]
                m = jnp.maximum(jnp.max(sh, -1, keepdims=True), sink)
                e = jnp.exp(sh - m)
                den = jnp.sum(e, -1, keepdims=True) + jnp.exp(sink - m)
                probs.append(e.astype(BF16))
                inv.append(1.0 / den)
            o = jnp.dot(jnp.concatenate(probs, axis=1), vbd, preferred_element_type=F32)
            o = o * jnp.where(lo_o, inv[0], inv[1])
            o_ref[rq, p * 2 * HD:(p + 1) * 2 * HD] = o.astype(o_ref.dtype)
        return carry

    lax.fori_loop(0, nsteps, body, 0, unroll=2)


def swa_mixer(proj, sinks, B, S):
    T = B * S
    nb = OFF_SWA // SWA_GROUP_W
    kern = functools.partial(_swa_kernel, nsteps=S // (2 * SWA_CHUNK))
    return pl.pallas_call(
        kern,
        grid=(B, SWA_HKV),
        in_specs=[pl.BlockSpec(memory_space=pltpu.SMEM),
                  pl.BlockSpec((S, SWA_GROUP_W), lambda b, h: (b, nb + h))],
        out_specs=pl.BlockSpec((S, SWA_G * SWA_HD), lambda b, h: (b, h)),
        out_shape=jax.ShapeDtypeStruct((T, SWA_Q), BF16),
        compiler_params=_cparams(("parallel", "parallel")),
        name="swa",
    )(sinks.astype(F32), proj)


def _merge_kernel(a_ref, b_ref, c_ref, ga_ref, gb_ref, gc_ref, h_ref, wa_ref, wb_ref, wc_ref, wo_ref,
                  g_ref, beta_ref, ho_ref, aux_ref, *, tm, token_tiles):
    def branch(x_ref, gate_ref, w_ref):
        y = jnp.dot(x_ref[...], w_ref[...], preferred_element_type=F32)
        return jax.nn.sigmoid(gate_ref[...].astype(F32)) * y

    merged = branch(a_ref, ga_ref, wa_ref) + branch(b_ref, gb_ref, wb_ref) + branch(c_ref, gc_ref, wc_ref)
    mix = jnp.dot(merged.astype(BF16), wo_ref[...], preferred_element_type=F32)
    hn = _layer_norm(DN_ALPHA * h_ref[...] + mix, g_ref[...], beta_ref[...])
    ho_ref[...] = hn
    if token_tiles:
        for j in range(SUB):
            aux_ref[pl.ds(j, tm, stride=SUB), :] = hn[:, j * LANES:(j + 1) * LANES]
    else:
        aux_ref[...] = hn.astype(BF16)


def merge_project(ya, yb, yc, proj, h, wa, wb, wc, wo, g, beta, token_tiles, tm=512):
    T, D = h.shape
    tm = min(tm, T)
    row = lambda i: (i, 0)
    const = lambda i: (0, 0)
    wspec = pl.BlockSpec((D, D), const, pipeline_mode=pl.Buffered(1))
    if token_tiles:
        aux_spec = pl.BlockSpec((tm * SUB, LANES), row)
        aux_shape = jax.ShapeDtypeStruct((T * SUB, LANES), F32)
    else:
        aux_spec = pl.BlockSpec((tm, D), row)
        aux_shape = jax.ShapeDtypeStruct((T, D), BF16)
    kern = functools.partial(_merge_kernel, tm=tm, token_tiles=token_tiles)
    return pl.pallas_call(
        kern,
        grid=(T // tm,),
        in_specs=[pl.BlockSpec((tm, D), row), pl.BlockSpec((tm, D), row), pl.BlockSpec((tm, D), row),
                  pl.BlockSpec((tm, D), lambda i: (i, OFF_GA // D_MODEL)),
                  pl.BlockSpec((tm, D), lambda i: (i, OFF_GB // D_MODEL)),
                  pl.BlockSpec((tm, D), lambda i: (i, OFF_GC // D_MODEL)),
                  pl.BlockSpec((tm, D), row),
                  wspec, wspec, wspec, wspec,
                  pl.BlockSpec((1, D), const), pl.BlockSpec((1, D), const)],
        out_specs=[pl.BlockSpec((tm, D), row), aux_spec],
        out_shape=[jax.ShapeDtypeStruct((T, D), F32), aux_shape],
        compiler_params=_cparams(("parallel",)),
        name="merge_project",
    )(ya, yb, yc, proj, proj, proj, h, wa, wb, wc, wo, g.reshape(1, D), beta.reshape(1, D))


def _swiglu_step(x, wg_ref, wu_ref, wd_ref, acc_ref):
    g = jnp.dot(x, wg_ref[...], preferred_element_type=F32)
    u = jnp.dot(x, wu_ref[...], preferred_element_type=F32)
    a = (_silu(g) * u).astype(BF16)
    acc_ref[...] += jnp.dot(a, wd_ref[...], preferred_element_type=F32)


def _ffn_kernel(xb_ref, h_ref, wg_ref, wu_ref, wd_ref, g_ref, beta_ref, ho_ref, hob_ref):
    x = xb_ref[...]
    gate = jnp.dot(x, wg_ref[...], preferred_element_type=F32)
    up = jnp.dot(x, wu_ref[...], preferred_element_type=F32)
    a = (_silu(gate) * up).astype(BF16)
    ff = jnp.dot(a, wd_ref[...], preferred_element_type=F32)
    hn = _layer_norm(DN_ALPHA * h_ref[...] + ff, g_ref[...], beta_ref[...])
    ho_ref[...] = hn
    hob_ref[...] = hn.astype(BF16)


def dense_ffn(hb, h, wg, wu, wd, g, beta, tm=512):
    T, D = h.shape
    F = wg.shape[1]
    tm = min(tm, T)
    row = lambda i: (i, 0)
    const = lambda i: (0, 0)
    resident = pl.Buffered(1)
    return pl.pallas_call(
        _ffn_kernel,
        grid=(T // tm,),
        in_specs=[pl.BlockSpec((tm, D), row), pl.BlockSpec((tm, D), row),
                  pl.BlockSpec((D, F), const, pipeline_mode=resident),
                  pl.BlockSpec((D, F), const, pipeline_mode=resident),
                  pl.BlockSpec((F, D), const, pipeline_mode=resident),
                  pl.BlockSpec((1, D), const), pl.BlockSpec((1, D), const)],
        out_specs=[pl.BlockSpec((tm, D), row), pl.BlockSpec((tm, D), row)],
        out_shape=[jax.ShapeDtypeStruct((T, D), F32), jax.ShapeDtypeStruct((T, D), BF16)],
        compiler_params=_cparams(("parallel",)),
        name="dense_ffn",
    )(hb, h, wg, wu, wd, g.reshape(1, D), beta.reshape(1, D))


def _router_kernel(h_ref, w_ref, o_ref):
    logits = jnp.dot(h_ref[...], w_ref[...], preferred_element_type=F32, precision=lax.Precision.HIGHEST)
    lane = lax.broadcasted_iota(jnp.int32, logits.shape, 1)
    l1 = jnp.where(lane < N_EXPERTS, logits, -jnp.inf)
    m1 = jnp.max(l1, -1, keepdims=True)
    i1 = jnp.min(jnp.where(l1 == m1, lane, LANES), -1, keepdims=True)
    l2 = jnp.where(lane == i1, -jnp.inf, l1)
    m2 = jnp.max(l2, -1, keepdims=True)
    i2 = jnp.min(jnp.where(l2 == m2, lane, LANES), -1, keepdims=True)
    e = jnp.exp(m2 - m1)
    w1 = 1.0 / (1.0 + e)
    w2 = e / (1.0 + e)
    out = jnp.where(lane == 0, i1.astype(F32),
                    jnp.where(lane == 1, i2.astype(F32),
                              jnp.where(lane == 2, w1, jnp.where(lane == 3, w2, 0.0))))
    o_ref[...] = out


def router_top2(h, w_router, tm=512):
    T, D = h.shape
    tm = min(tm, T)
    wpad = jnp.zeros((D, LANES), F32).at[:, :N_EXPERTS].set(w_router.astype(F32))
    return pl.pallas_call(
        _router_kernel,
        grid=(T // tm,),
        in_specs=[pl.BlockSpec((tm, D), lambda i: (i, 0)), pl.BlockSpec((D, LANES), lambda i: (0, 0))],
        out_specs=pl.BlockSpec((tm, LANES), lambda i: (i, 0)),
        out_shape=jax.ShapeDtypeStruct((T, LANES), F32),
        compiler_params=_cparams(("parallel",)),
        name="router_top2",
    )(h, wpad)


def _expert_ffn_kernel(te_ref, nt_ref, src_ref, dst_ref, h_hbm, wg_ref, wu_ref, wd_ref, y_hbm,
                       xbuf, xb_ref, ybuf, acc_ref, gsem, ssem, *, tm, nf):
    i = pl.program_id(0)
    f = pl.program_id(1)
    n_i = pl.num_programs(0)
    active = i < nt_ref[0]
    buf = i % 2
    share = tm // nf
    nxt = jnp.minimum(i + 1, n_i - 1)
    prv = jnp.maximum(i - 1, 0)

    def gather_row(tile, b, r):
        tok = src_ref[tile * tm + r]
        pltpu.make_async_copy(h_hbm.at[pl.ds(pl.multiple_of(tok * SUB, SUB), SUB), :],
                              xbuf.at[b, pl.ds(pl.multiple_of(r * SUB, SUB), SUB), :], gsem.at[b]).start()

    def scatter_row(tile, r):
        slot = dst_ref[tile * tm + r]
        pltpu.make_async_copy(ybuf.at[pl.ds(pl.multiple_of(r * SUB, SUB), SUB), :],
                              y_hbm.at[pl.ds(pl.multiple_of(slot * SUB, SUB), SUB), :], ssem).start()

    def wait_gather(b):
        pltpu.make_async_copy(h_hbm.at[pl.ds(0, tm * SUB), :], xbuf.at[b], gsem.at[b]).wait()

    def wait_scatter():
        pltpu.make_async_copy(ybuf, y_hbm.at[pl.ds(0, tm * SUB), :], ssem).wait()

    def looped(n, fn):
        def body(r, carry):
            fn(r)
            return carry
        lax.fori_loop(0, n, body, 0, unroll=8)

    def share_row(r):
        gather_row(nxt, 1 - buf, f * share + r)
        scatter_row(prv, f * share + r)

    @pl.when(f == 0)
    def _():
        @pl.when(i == 0)
        def _():
            looped(tm, lambda r: gather_row(0, 0, r))
            ybuf[...] = jnp.zeros_like(ybuf)

        wait_gather(buf)
        for j in range(SUB):
            xb_ref[:, j * LANES:(j + 1) * LANES] = xbuf[buf, pl.ds(j, tm, stride=SUB), :].astype(BF16)
        acc_ref[...] = jnp.zeros_like(acc_ref)

    @pl.when(active)
    def _():
        for r in range(share):
            share_row(r)
        _swiglu_step(xb_ref[...], wg_ref, wu_ref, wd_ref, acc_ref)

    @pl.when(jnp.logical_not(active))
    def _():
        looped(share, share_row)

    @pl.when(f == nf - 1)
    def _():
        wait_scatter()
        for j in range(SUB):
            ybuf[pl.ds(j, tm, stride=SUB), :] = acc_ref[:, j * LANES:(j + 1) * LANES]

        @pl.when(i == n_i - 1)
        def _():
            looped(tm, lambda r: scatter_row(i, r))
            wait_scatter()
            wait_gather(1 - buf)


def expert_ffn(h_tiles, te, nt, src_tok, dst_slot, wg, wu, wd, tm, tf=None):
    P = src_tok.shape[0]
    D = wg.shape[1]
    F = wg.shape[2]
    tf = F if tf is None else tf
    nf = F // tf

    def wcol(i, f, te_ref, nt_ref, src_ref, dst_ref):
        return (te_ref[i], 0, jnp.where(i < nt_ref[0], f, nf - 1))

    def wrow(i, f, te_ref, nt_ref, src_ref, dst_ref):
        return (te_ref[i], jnp.where(i < nt_ref[0], f, nf - 1), 0)

    wmode = pl.Buffered(1) if nf == 1 else pl.Buffered(2)
    kern = functools.partial(_expert_ffn_kernel, tm=tm, nf=nf)
    return pl.pallas_call(
        kern,
        grid_spec=pltpu.PrefetchScalarGridSpec(
            num_scalar_prefetch=4,
            grid=(P // tm, nf),
            in_specs=[pl.BlockSpec(memory_space=pl.ANY),
                      pl.BlockSpec((None, D, tf), wcol, pipeline_mode=wmode),
                      pl.BlockSpec((None, D, tf), wcol, pipeline_mode=wmode),
                      pl.BlockSpec((None, tf, D), wrow, pipeline_mode=wmode)],
            out_specs=pl.BlockSpec(memory_space=pl.ANY),
            scratch_shapes=[pltpu.VMEM((2, tm * SUB, LANES), F32),
                            pltpu.VMEM((tm, D), BF16),
                            pltpu.VMEM((tm * SUB, LANES), F32),
                            pltpu.VMEM((tm, D), F32),
                            pltpu.SemaphoreType.DMA((2,)),
                            pltpu.SemaphoreType.DMA(())],
        ),
        out_shape=jax.ShapeDtypeStruct((P * SUB, LANES), F32),
        compiler_params=_cparams(("arbitrary", "arbitrary")),
        name="expert_ffn",
    )(te, nt, src_tok, dst_slot, h_tiles, wg, wu, wd)


def _moe_combine_kernel(y0_ref, y1_ref, r_ref, h_ref, g_ref, beta_ref, ho_ref, hob_ref, *, tm):
    def rows(y_ref):
        return jnp.concatenate([y_ref[pl.ds(j, tm, stride=SUB), :] for j in range(SUB)], axis=1)

    r = r_ref[...]
    ff = rows(y0_ref) * r[:, 2:3] + rows(y1_ref) * r[:, 3:4]
    hn = _layer_norm(DN_ALPHA * h_ref[...] + ff, g_ref[...], beta_ref[...])
    ho_ref[...] = hn
    hob_ref[...] = hn.astype(BF16)


def moe_combine(y_tiles, route, h, g, beta, tm=512):
    T, D = h.shape
    tm = min(tm, T)
    nb = T // tm
    row = lambda i: (i, 0)
    const = lambda i: (0, 0)
    kern = functools.partial(_moe_combine_kernel, tm=tm)
    return pl.pallas_call(
        kern,
        grid=(nb,),
        in_specs=[pl.BlockSpec((tm * SUB, LANES), lambda i: (i, 0)),
                  pl.BlockSpec((tm * SUB, LANES), lambda i: (nb + i, 0)),
                  pl.BlockSpec((tm, LANES), row),
                  pl.BlockSpec((tm, D), row),
                  pl.BlockSpec((1, D), const), pl.BlockSpec((1, D), const)],
        out_specs=[pl.BlockSpec((tm, D), row), pl.BlockSpec((tm, D), row)],
        out_shape=[jax.ShapeDtypeStruct((T, D), F32), jax.ShapeDtypeStruct((T, D), BF16)],
        compiler_params=_cparams(("parallel",)),
        name="moe_combine",
    )(y_tiles, y_tiles, route, h, g.reshape(1, D), beta.reshape(1, D))


def moe_plan(route, T, tm):
    i32 = jnp.int32
    n_slots = TOP_K * T
    e_flat = route[:, 0:TOP_K].astype(i32).T.reshape(-1)
    onehot = (e_flat[:, None] == jnp.arange(N_EXPERTS, dtype=i32)[None, :]).astype(i32)
    csum = jnp.cumsum(onehot, axis=0)
    rank = jnp.sum(onehot * (csum - 1), axis=1)
    sizes = csum[-1]
    padded = ((sizes + tm - 1) // tm) * tm
    gend = jnp.cumsum(padded)
    gstart = gend - padded
    pos = gstart[e_flat] + rank
    n_rows = n_slots + N_EXPERTS * tm
    slot_plus1 = jnp.zeros((n_rows,), i32).at[pos].set(jnp.arange(1, n_slots + 1, dtype=i32))
    is_pad = slot_plus1 == 0
    pad_rank = jnp.cumsum(is_pad.astype(i32)) - 1
    dst_slot = jnp.where(is_pad, n_slots + pad_rank, slot_plus1 - 1)
    src_tok = jnp.where(is_pad, 0, (slot_plus1 - 1) % T)
    n_tiles = n_rows // tm
    nt = (gend[-1] // tm).astype(i32)
    tile_start = jnp.arange(n_tiles, dtype=i32) * tm
    te = jnp.minimum(jnp.sum((tile_start[:, None] >= gend[None, :]).astype(i32), axis=1), N_EXPERTS - 1)
    te = jnp.where(jnp.arange(n_tiles) < nt, te, te[jnp.maximum(nt - 1, 0)])
    return src_tok.astype(i32), dst_slot.astype(i32), te.astype(i32), nt.reshape(1)


def moe_ffn(h, h_tiles, w_router, wg, wu, wd, g, beta, tm=512):
    T, D = h.shape
    route = router_top2(h, w_router)
    src_tok, dst_slot, te, nt = moe_plan(route, T, tm)
    y_tiles = expert_ffn(h_tiles, te, nt, src_tok, dst_slot, wg, wu, wd, tm)
    return moe_combine(y_tiles, route, h, g, beta)


def permute_w_in(w_in, layer):
    widths = [RET_QK, RET_QK, RET_V, RET_V, HG_K, HG_K, HG_V, HG_V, SWA_Q, SWA_KV, SWA_KV,
              D_MODEL, D_MODEL, D_MODEL]
    starts = np.concatenate([[0], np.cumsum(widths)])
    names = ["rq", "rk", "rv", "rg", "hq", "hf", "hi", "hg", "sq", "sk", "sv", "ga", "gb", "gc"]
    off = dict(zip(names, starts[:-1].tolist()))
    wid = dict(zip(names, widths))

    def cols(name, lo=0, n=None):
        a = off[name] + lo
        return w_in[layer, :, a:a + (wid[name] if n is None else n)]

    pieces = [cols(n) for n in ("rq", "rk", "rv", "rg", "hq", "hi", "hg", "ga", "gb", "gc")]
    qw = SWA_G * SWA_HD
    for kvh in range(SWA_HKV):
        k = cols("sk", kvh * SWA_HD, SWA_HD)
        v = cols("sv", kvh * SWA_HD, SWA_HD)
        pieces += [cols("sq", kvh * qw, qw) * SWA_HD ** -0.5, k, v, v, k]
    main = jnp.concatenate(pieces, axis=1).astype(BF16)
    return main, cols("hf").astype(BF16)


def kernel(x, ln_in_g, ln_in_b, w_in, ret_w_out, hgrn_lower_bounds, hgrn_norm_g, hgrn_w_out, swa_sinks,
           swa_w_out, w_o, ln_mix_g, ln_mix_b, ffn_w_gate, ffn_w_up, ffn_w_down, moe_router, moe_w_gate,
           moe_w_up, moe_w_down, ln_ffn_g, ln_ffn_b):
    B, S, D = x.shape
    T = B * S
    assert D == D_MODEL and S % RET_CHUNK == 0 and S % (2 * SWA_CHUNK) == 0 and S >= 4 * SWA_CHUNK

    lb_all = jnp.cumsum(jax.nn.softmax(hgrn_lower_bounds.astype(F32), axis=0), axis=0)
    lb_all = lb_all - lb_all[0]
    tabs = retention_tables(S)

    h, hb = ln_in(x.reshape(T, D), ln_in_g, ln_in_b)
    for layer in range(DEPTH):
        w_main, w_hf = permute_w_in(w_in, layer)
        proj = matmul(hb, w_main, BF16, tn=N_PROJ // 4, name="in_proj")
        hf = matmul(hb, w_hf, F32, tn=HG_K, name="in_proj_forget")

        ya = retention_mixer(proj, tabs, B, S)
        yb = hgrn_mixer(proj, hf, lb_all[layer], hgrn_norm_g[layer], B, S)
        yc = swa_mixer(proj, swa_sinks[layer], B, S)
        dense = layer % 2 == 0
        h, aux = merge_project(ya, yb, yc, proj, h,
                               ret_w_out[layer].astype(BF16), hgrn_w_out[layer].astype(BF16),
                               swa_w_out[layer].astype(BF16), w_o[layer].astype(BF16),
                               ln_mix_g[layer], ln_mix_b[layer], token_tiles=not dense)

        j = layer // 2
        if dense:
            h, hb = dense_ffn(aux, h, ffn_w_gate[j].astype(BF16), ffn_w_up[j].astype(BF16),
                              ffn_w_down[j].astype(BF16), ln_ffn_g[layer], ln_ffn_b[layer])
        else:
            h, hb = moe_ffn(h, aux, moe_router[j], moe_w_gate[j].astype(BF16), moe_w_up[j].astype(BF16),
                            moe_w_down[j].astype(BF16), ln_ffn_g[layer], ln_ffn_b[layer])
    return h.reshape(B, S, D)
```

```python
import functools

import numpy as np
import jax
import jax.numpy as jnp
from jax import lax
from jax.experimental import pallas as pl
from jax.experimental.pallas import tpu as pltpu

F32 = jnp.float32
BF16 = jnp.bfloat16

D_MODEL = 1024
RET_HEADS, RET_DK, RET_DV = 4, 128, 256
RET_QK, RET_V = RET_HEADS * RET_DK, RET_HEADS * RET_DV
ROPE_BASE = 10000.0
HG_HEADS, HG_DK, HG_DV = 8, 128, 128
HG_K, HG_V = HG_HEADS * HG_DK, HG_HEADS * HG_DV
SWA_HQ, SWA_HKV, SWA_HD = 16, 4, 64
SWA_G = SWA_HQ // SWA_HKV
SWA_Q, SWA_KV = SWA_HQ * SWA_HD, SWA_HKV * SWA_HD
SWA_CHUNK = 64
SWA_WIN_CHUNKS = 2
N_EXPERTS, TOP_K = 8, 2
DEPTH = 2
LN_EPS, RMS_EPS = 1e-5, 1e-6
DN_ALPHA = (2.0 * DEPTH) ** 0.25

VMEM_LIMIT_BYTES = 56 * 1024 * 1024
LANES = 128
SUB = 8

RET_CHUNK = 128
RET_BLOCK = 512
HG_CHUNK = 32
HG_BLOCK = 256

OFF_RQ, OFF_RK, OFF_RV, OFF_RG = 0, 512, 1024, 2048
OFF_HQ, OFF_HI, OFF_HG = 3072, 4096, 5120
OFF_GA, OFF_GB, OFF_GC = 6144, 7168, 8192
OFF_SWA = 9216
SWA_GROUP_W = SWA_G * SWA_HD + 4 * SWA_HD
N_PROJ = OFF_SWA + SWA_HKV * SWA_GROUP_W


def _cparams(sem, vmem=VMEM_LIMIT_BYTES):
    return pltpu.CompilerParams(dimension_semantics=sem, vmem_limit_bytes=vmem)


def _layer_norm(x, g, b):
    mu = jnp.mean(x, -1, keepdims=True)
    xc = x - mu
    var = jnp.mean(xc * xc, -1, keepdims=True)
    return xc * lax.rsqrt(var + LN_EPS) * g + b


def _silu(x):
    return x * jax.nn.sigmoid(x)


def _ln_in_kernel(x_ref, g_ref, b_ref, h_ref, hb_ref):
    h = _layer_norm(x_ref[...], g_ref[...], b_ref[...])
    h_ref[...] = h
    hb_ref[...] = h.astype(BF16)


def ln_in(x2, g, b, tm=512):
    T, D = x2.shape
    tm = min(tm, T)
    return pl.pallas_call(
        _ln_in_kernel,
        grid=(T // tm,),
        in_specs=[pl.BlockSpec((tm, D), lambda i: (i, 0)),
                  pl.BlockSpec((1, D), lambda i: (0, 0)),
                  pl.BlockSpec((1, D), lambda i: (0, 0))],
        out_specs=[pl.BlockSpec((tm, D), lambda i: (i, 0)),
                   pl.BlockSpec((tm, D), lambda i: (i, 0))],
        out_shape=[jax.ShapeDtypeStruct((T, D), F32), jax.ShapeDtypeStruct((T, D), BF16)],
        compiler_params=_cparams(("parallel",)),
        name="ln_in",
    )(x2, g.reshape(1, D), b.reshape(1, D))


def _matmul_kernel(x_ref, w_ref, o_ref):
    o_ref[...] = jnp.dot(x_ref[...], w_ref[...], preferred_element_type=F32).astype(o_ref.dtype)


def matmul(x, w, out_dtype, tm=1024, tn=512, name="matmul"):
    T, K = x.shape
    N = w.shape[1]
    tm = min(tm, T)
    return pl.pallas_call(
        _matmul_kernel,
        grid=(T // tm, N // tn),
        in_specs=[pl.BlockSpec((tm, K), lambda i, j: (i, 0)),
                  pl.BlockSpec((K, tn), lambda i, j: (0, j))],
        out_specs=pl.BlockSpec((tm, tn), lambda i, j: (i, j)),
        out_shape=jax.ShapeDtypeStruct((T, N), out_dtype),
        compiler_params=_cparams(("parallel", "parallel")),
        name=name,
    )(x, w)


def _retention_kernel(q_ref, k_ref, v_ref, g_ref, cos_ref, sin_ref, dm_ref, qd_ref, kd_ref, cd_ref,
                      o_ref, st_ref, *, chunk, block, nblocks):
    st_ref[...] = jnp.zeros_like(st_ref)
    dm = dm_ref[0]
    qd = qd_ref[0]
    kd = kd_ref[0]
    cd = cd_ref[0]
    nc = block // chunk
    nt_dims = (((1,), (1,)), ((), ()))
    tn_dims = (((0,), (0,)), ((), ()))

    def body(bi, carry):
        r = pl.ds(pl.multiple_of(bi * block, block), block)
        cos = cos_ref[r, :]
        sin = sin_ref[r, :]
        q = q_ref[r, :].astype(F32)
        k = k_ref[r, :].astype(F32)
        qr = q * cos + pltpu.roll(q, RET_DK // 2, 1) * sin
        kr = k * cos + pltpu.roll(k, RET_DK // 2, 1) * sin
        v = v_ref[r, :]
        rows = [slice(c * chunk, (c + 1) * chunk) for c in range(nc)]
        upd = [lax.dot_general((kr[rc] * kd).astype(BF16), v[rc], tn_dims, preferred_element_type=F32)
               for rc in rows]
        att = [lax.dot_general(qr[rc].astype(BF16), kr[rc].astype(BF16), nt_dims,
                               preferred_element_type=F32) * dm for rc in rows]
        intra = [jnp.dot(a.astype(BF16), v[rc], preferred_element_type=F32) for a, rc in zip(att, rows)]
        q_in = [(qr[rc] * qd).astype(BF16) for rc in rows]
        st = st_ref[...]
        outs = []
        for c in range(nc):
            outs.append(intra[c] + jnp.dot(q_in[c], st.astype(BF16), preferred_element_type=F32))
            st = st * cd + upd[c]
        st_ref[...] = st
        o = jnp.concatenate(outs, axis=0)
        on = o * lax.rsqrt(jnp.mean(o * o, -1, keepdims=True) + RMS_EPS)
        g = g_ref[r, :].astype(F32)
        o_ref[r, :] = (_silu(g) * on).astype(o_ref.dtype)
        return carry

    lax.fori_loop(0, nblocks, body, 0, unroll=2)


def retention_mixer(proj, tabs, B, S):
    T = B * S
    C = RET_CHUNK
    blk = min(RET_BLOCK, S)
    nq, nv = OFF_RQ // RET_DK, OFF_RV // RET_DV
    nk, ng = OFF_RK // RET_DK, OFF_RG // RET_DV
    kern = functools.partial(_retention_kernel, chunk=C, block=blk, nblocks=S // blk)
    return pl.pallas_call(
        kern,
        grid=(B, RET_HEADS),
        in_specs=[pl.BlockSpec((S, RET_DK), lambda b, h: (b, nq + h)),
                  pl.BlockSpec((S, RET_DK), lambda b, h: (b, nk + h)),
                  pl.BlockSpec((S, RET_DV), lambda b, h: (b, nv + h)),
                  pl.BlockSpec((S, RET_DV), lambda b, h: (b, ng + h)),
                  pl.BlockSpec((S, RET_DK), lambda b, h: (0, 0)),
                  pl.BlockSpec((S, RET_DK), lambda b, h: (0, 0)),
                  pl.BlockSpec((1, C, C), lambda b, h: (h, 0, 0)),
                  pl.BlockSpec((1, C, RET_DK), lambda b, h: (h, 0, 0)),
                  pl.BlockSpec((1, C, RET_DK), lambda b, h: (h, 0, 0)),
                  pl.BlockSpec((1, 1, RET_DV), lambda b, h: (h, 0, 0))],
        out_specs=pl.BlockSpec((S, RET_DV), lambda b, h: (b, h)),
        out_shape=jax.ShapeDtypeStruct((T, RET_V), BF16),
        scratch_shapes=[pltpu.VMEM((RET_DK, RET_DV), F32)],
        compiler_params=_cparams(("parallel", "parallel")),
        name="retention",
    )(proj, proj, proj, proj, tabs["cos"], tabs["sin"], tabs["dm"], tabs["qd"], tabs["kd"], tabs["cd"])


def retention_tables(S):
    C = RET_CHUNK
    half = RET_DK // 2
    pos = jnp.arange(S, dtype=F32)
    inv = 1.0 / (ROPE_BASE ** jnp.linspace(0.0, 1.0, half, dtype=F32))
    ang = pos[:, None] * inv[None, :]
    cos, sin = jnp.cos(ang), jnp.sin(ang)
    log_gamma = jnp.log(1.0 - 2.0 ** (-5.0 - jnp.arange(RET_HEADS, dtype=F32)))
    idx = jnp.arange(C, dtype=F32)
    diff = idx[:, None] - idx[None, :]
    decay = jnp.where(diff[None] >= 0,
                      jnp.exp(jnp.maximum(diff, 0.0)[None] * log_gamma[:, None, None]), 0.0)
    scale = RET_DK ** -0.5
    qd = jnp.exp((idx + 1.0)[None, :] * log_gamma[:, None])
    kd = jnp.exp((C - 1.0 - idx)[None, :] * log_gamma[:, None]) * scale
    cd = jnp.exp(C * log_gamma)
    return {
        "cos": jnp.concatenate([cos, cos], axis=1),
        "sin": jnp.concatenate([-sin, sin], axis=1),
        "dm": decay * scale,
        "qd": jnp.broadcast_to(qd[:, :, None], (RET_HEADS, C, RET_DK)),
        "kd": jnp.broadcast_to(kd[:, :, None], (RET_HEADS, C, RET_DK)),
        "cd": jnp.broadcast_to(cd[:, None, None], (RET_HEADS, 1, RET_DV)),
    }


def _hgrn_kernel(q_ref, z_ref, i_ref, g_ref, lb_ref, ng_ref, o_ref, st_ref, *, chunk, block, nblocks):
    st_ref[...] = jnp.zeros_like(st_ref)
    lb = lb_ref[0]
    ng = ng_ref[...]
    half = chunk // 2
    nc = block // chunk
    row = lax.broadcasted_iota(jnp.int32, (block, HG_DK), 0) % chunk
    causal = (lax.broadcasted_iota(jnp.int32, (chunk, chunk), 0)
              >= lax.broadcasted_iota(jnp.int32, (chunk, chunk), 1))
    nt_dims = (((1,), (1,)), ((), ()))
    tn_dims = (((0,), (0,)), ((), ()))

    def per_chunk(x, lo):
        return jnp.concatenate(
            [jnp.broadcast_to(x[c * chunk + lo:c * chunk + lo + 1, :], (chunk, x.shape[1])) for c in range(nc)],
            axis=0)

    def body(bi, carry):
        r = pl.ds(pl.multiple_of(bi * block, block), block)
        f = lb + (1.0 - lb) * jax.nn.sigmoid(z_ref[r, :])
        cum = jnp.log(f)
        shift = 1
        while shift < chunk:
            cum = cum + jnp.where(row >= shift, pltpu.roll(cum, shift, 0), 0.0)
            shift *= 2
        mid = per_chunk(cum, half - 1)
        last = per_chunk(cum, chunk - 1)
        qh = q_ref[r, :].astype(F32) * jnp.exp(cum - mid)
        kh = (1.0 - f) * jnp.exp(mid - cum)
        q_in = (qh * jnp.exp(mid)).astype(BF16)
        k_st = (kh * jnp.exp(last - mid)).astype(BF16)
        dec = jnp.exp(last)
        qh = qh.astype(BF16)
        kh = kh.astype(BF16)
        v = i_ref[r, :]
        rows = [slice(c * chunk, (c + 1) * chunk) for c in range(nc)]
        upd = [lax.dot_general(v[rc], k_st[rc], tn_dims, preferred_element_type=F32) for rc in rows]
        att = [lax.dot_general(qh[rc], kh[rc], nt_dims, preferred_element_type=F32) for rc in rows]
        intra = [jnp.dot(jnp.where(causal, a, 0.0).astype(BF16), v[rc], preferred_element_type=F32)
                 for a, rc in zip(att, rows)]
        st = st_ref[...]
        outs = []
        for c, rc in enumerate(rows):
            inter = lax.dot_general(q_in[rc], st.astype(BF16), nt_dims, preferred_element_type=F32)
            outs.append(intra[c] + inter)
            st = st * dec[c * chunk:c * chunk + 1, :] + upd[c]
        st_ref[...] = st
        o = jnp.concatenate(outs, axis=0)
        on = o * lax.rsqrt(jnp.mean(o * o, -1, keepdims=True) + RMS_EPS) * ng
        g = g_ref[r, :].astype(F32)
        o_ref[r, :] = (_silu(g) * on).astype(o_ref.dtype)
        return carry

    lax.fori_loop(0, nblocks, body, 0, unroll=4)


def hgrn_mixer(proj, hf, lb, norm_g, B, S):
    T = B * S
    C = HG_CHUNK
    blk = min(HG_BLOCK, S)
    nq, ni, ng = OFF_HQ // HG_DK, OFF_HI // HG_DV, OFF_HG // HG_DV
    kern = functools.partial(_hgrn_kernel, chunk=C, block=blk, nblocks=S // blk)
    return pl.pallas_call(
        kern,
        grid=(B, HG_HEADS),
        in_specs=[pl.BlockSpec((S, HG_DK), lambda b, h: (b, nq + h)),
                  pl.BlockSpec((S, HG_DK), lambda b, h: (b, h)),
                  pl.BlockSpec((S, HG_DV), lambda b, h: (b, ni + h)),
                  pl.BlockSpec((S, HG_DV), lambda b, h: (b, ng + h)),
                  pl.BlockSpec((1, 1, HG_DK), lambda b, h: (h, 0, 0)),
                  pl.BlockSpec((1, HG_DV), lambda b, h: (0, 0))],
        out_specs=pl.BlockSpec((S, HG_DV), lambda b, h: (b, h)),
        out_shape=jax.ShapeDtypeStruct((T, HG_V), BF16),
        scratch_shapes=[pltpu.VMEM((HG_DV, HG_DK), F32)],
        compiler_params=_cparams(("parallel", "parallel")),
        name="hgrn2",
    )(proj, hf, proj, proj, lb.reshape(HG_HEADS, 1, HG_DK), norm_g.reshape(1, HG_DV))


def _swa_kernel(sink_ref, x_ref, o_ref, *, nsteps):
    kvh = pl.program_id(1)
    CH = SWA_CHUNK
    HD = SWA_HD
    QR = 2 * CH
    KW = 4 * CH
    QW = SWA_G * HD
    lo_kv = lax.broadcasted_iota(jnp.int32, (KW, 2 * HD), 1) < HD
    lo_o = lax.broadcasted_iota(jnp.int32, (QR, 2 * HD), 1) < HD
    nt_dims = (((1,), (1,)), ((), ()))
    zero = jnp.zeros((KW, 2 * HD), BF16)

    kcol = lax.broadcasted_iota(jnp.int32, (QR, 2 * KW), 1)
    qrow = lax.broadcasted_iota(jnp.int32, (QR, 2 * KW), 0)
    rel = (kcol % KW) // CH - qrow // CH

    def body(j, carry):
        first = jnp.maximum(2 * j - SWA_WIN_CHUNKS, 0)
        rq = pl.ds(pl.multiple_of(j * QR, QR), QR)
        rk = pl.ds(pl.multiple_of(first * CH, CH), KW)
        kv = x_ref[rk, QW:QW + 2 * HD]
        vk = x_ref[rk, QW + 2 * HD:QW + 4 * HD]
        kbd = jnp.concatenate([jnp.where(lo_kv, kv, zero), jnp.where(lo_kv, zero, vk)], axis=0)
        vbd = jnp.concatenate([jnp.where(lo_kv, vk, zero), jnp.where(lo_kv, zero, kv)], axis=0)
        d = rel + (first - 2 * j)
        valid = d * (d + SWA_WIN_CHUNKS) <= 0
        scores = [lax.dot_general(x_ref[rq, p * 2 * HD:(p + 1) * 2 * HD], kbd, nt_dims,
                                  preferred_element_type=F32) for p in range(SWA_G // 2)]
        for p, s in enumerate(scores):
            s = jnp.where(valid, s, -jnp.inf)
            probs, inv = [], []
            for hh in range(2):
                sink = sink_ref[kvh * SWA_G + 2 * p + hh]
                sh = s[:, hh * KW:(hh + 1) * KW]
                m = jnp.maximum(jnp.max(sh, -1, keepdims=True), sink)
                e = jnp.exp(sh - m)
                den = jnp.sum(e, -1, keepdims=True) + jnp.exp(sink - m)
                probs.append(e.astype(BF16))
                inv.append(1.0 / den)
            o = jnp.dot(jnp.concatenate(probs, axis=1), vbd, preferred_element_type=F32)
            o = o * jnp.where(lo_o, inv[0], inv[1])
            o_ref[rq, p * 2 * HD:(p + 1) * 2 * HD] = o.astype(o_ref.dtype)
        return carry

    lax.fori_loop(0, nsteps, body, 0, unroll=2)


def swa_mixer(proj, sinks, B, S):
    T = B * S
    nb = OFF_SWA // SWA_GROUP_W
    kern = functools.partial(_swa_kernel, nsteps=S // (2 * SWA_CHUNK))
    return pl.pallas_call(
        kern,
        grid=(B, SWA_HKV),
        in_specs=[pl.BlockSpec(memory_space=pltpu.SMEM),
                  pl.BlockSpec((S, SWA_GROUP_W), lambda b, h: (b, nb + h))],
        out_specs=pl.BlockSpec((S, SWA_G * SWA_HD), lambda b, h: (b, h)),
        out_shape=jax.ShapeDtypeStruct((T, SWA_Q), BF16),
        compiler_params=_cparams(("parallel", "parallel")),
        name="swa",
    )(sinks.astype(F32), proj)


def _merge_kernel(a_ref, b_ref, c_ref, ga_ref, gb_ref, gc_ref, h_ref, wa_ref, wb_ref, wc_ref, wo_ref,
                  g_ref, beta_ref, ho_ref, aux_ref, *, tm, token_tiles):
    def branch(x_ref, gate_ref, w_ref):
        y = jnp.dot(x_ref[...], w_ref[...], preferred_element_type=F32)
        return jax.nn.sigmoid(gate_ref[...].astype(F32)) * y

    merged = branch(a_ref, ga_ref, wa_ref) + branch(b_ref, gb_ref, wb_ref) + branch(c_ref, gc_ref, wc_ref)
    mix = jnp.dot(merged.astype(BF16), wo_ref[...], preferred_element_type=F32)
    hn = _layer_norm(DN_ALPHA * h_ref[...] + mix, g_ref[...], beta_ref[...])
    ho_ref[...] = hn
    if token_tiles:
        for j in range(SUB):
            aux_ref[pl.ds(j, tm, stride=SUB), :] = hn[:, j * LANES:(j + 1) * LANES]
    else:
        aux_ref[...] = hn.astype(BF16)


def merge_project(ya, yb, yc, proj, h, wa, wb, wc, wo, g, beta, token_tiles, tm=512):
    T, D = h.shape
    tm = min(tm, T)
    row = lambda i: (i, 0)
    const = lambda i: (0, 0)
    wspec = pl.BlockSpec((D, D), const, pipeline_mode=pl.Buffered(1))
    if token_tiles:
        aux_spec = pl.BlockSpec((tm * SUB, LANES), row)
        aux_shape = jax.ShapeDtypeStruct((T * SUB, LANES), F32)
    else:
        aux_spec = pl.BlockSpec((tm, D), row)
        aux_shape = jax.ShapeDtypeStruct((T, D), BF16)
    kern = functools.partial(_merge_kernel, tm=tm, token_tiles=token_tiles)
    return pl.pallas_call(
        kern,
        grid=(T // tm,),
        in_specs=[pl.BlockSpec((tm, D), row), pl.BlockSpec((tm, D), row), pl.BlockSpec((tm, D), row),
                  pl.BlockSpec((tm, D), lambda i: (i, OFF_GA // D_MODEL)),
                  pl.BlockSpec((tm, D), lambda i: (i, OFF_GB // D_MODEL)),
                  pl.BlockSpec((tm, D), lambda i: (i, OFF_GC // D_MODEL)),
                  pl.BlockSpec((tm, D), row),
                  wspec, wspec, wspec, wspec,
                  pl.BlockSpec((1, D), const), pl.BlockSpec((1, D), const)],
        out_specs=[pl.BlockSpec((tm, D), row), aux_spec],
        out_shape=[jax.ShapeDtypeStruct((T, D), F32), aux_shape],
        compiler_params=_cparams(("parallel",)),
        name="merge_project",
    )(ya, yb, yc, proj, proj, proj, h, wa, wb, wc, wo, g.reshape(1, D), beta.reshape(1, D))


def _ffn_kernel(xb_ref, h_ref, wg_ref, wu_ref, wd_ref, g_ref, beta_ref, ho_ref, hob_ref):
    x = xb_ref[...]
    gate = jnp.dot(x, wg_ref[...], preferred_element_type=F32)
    up = jnp.dot(x, wu_ref[...], preferred_element_type=F32)
    a = (_silu(gate) * up).astype(BF16)
    ff = jnp.dot(a, wd_ref[...], preferred_element_type=F32)
    hn = _layer_norm(DN_ALPHA * h_ref[...] + ff, g_ref[...], beta_ref[...])
    ho_ref[...] = hn
    hob_ref[...] = hn.astype(BF16)


def dense_ffn(hb, h, wg, wu, wd, g, beta, tm=512):
    T, D = h.shape
    F = wg.shape[1]
    tm = min(tm, T)
    row = lambda i: (i, 0)
    const = lambda i: (0, 0)
    resident = pl.Buffered(1)
    return pl.pallas_call(
        _ffn_kernel,
        grid=(T // tm,),
        in_specs=[pl.BlockSpec((tm, D), row), pl.BlockSpec((tm, D), row),
                  pl.BlockSpec((D, F), const, pipeline_mode=resident),
                  pl.BlockSpec((D, F), const, pipeline_mode=resident),
                  pl.BlockSpec((F, D), const, pipeline_mode=resident),
                  pl.BlockSpec((1, D), const), pl.BlockSpec((1, D), const)],
        out_specs=[pl.BlockSpec((tm, D), row), pl.BlockSpec((tm, D), row)],
        out_shape=[jax.ShapeDtypeStruct((T, D), F32), jax.ShapeDtypeStruct((T, D), BF16)],
        compiler_params=_cparams(("parallel",)),
        name="dense_ffn",
    )(hb, h, wg, wu, wd, g.reshape(1, D), beta.reshape(1, D))


def _router_kernel(h_ref, w_ref, o_ref):
    logits = jnp.dot(h_ref[...], w_ref[...], preferred_element_type=F32, precision=lax.Precision.HIGHEST)
    lane = lax.broadcasted_iota(jnp.int32, logits.shape, 1)
    l1 = jnp.where(lane < N_EXPERTS, logits, -jnp.inf)
    m1 = jnp.max(l1, -1, keepdims=True)
    i1 = jnp.min(jnp.where(l1 == m1, lane, LANES), -1, keepdims=True)
    l2 = jnp.where(lane == i1, -jnp.inf, l1)
    m2 = jnp.max(l2, -1, keepdims=True)
    i2 = jnp.min(jnp.where(l2 == m2, lane, LANES), -1, keepdims=True)
    e = jnp.exp(m2 - m1)
    w1 = 1.0 / (1.0 + e)
    w2 = e / (1.0 + e)
    out = jnp.where(lane == 0, i1.astype(F32),
                    jnp.where(lane == 1, i2.astype(F32),
                              jnp.where(lane == 2, w1, jnp.where(lane == 3, w2, 0.0))))
    o_ref[...] = out


def router_top2(h, w_router, tm=512):
    T, D = h.shape
    tm = min(tm, T)
    wpad = jnp.zeros((D, LANES), F32).at[:, :N_EXPERTS].set(w_router.astype(F32))
    return pl.pallas_call(
        _router_kernel,
        grid=(T // tm,),
        in_specs=[pl.BlockSpec((tm, D), lambda i: (i, 0)), pl.BlockSpec((D, LANES), lambda i: (0, 0))],
        out_specs=pl.BlockSpec((tm, LANES), lambda i: (i, 0)),
        out_shape=jax.ShapeDtypeStruct((T, LANES), F32),
        compiler_params=_cparams(("parallel",)),
        name="router_top2",
    )(h, wpad)


W_PIECES = 8


def _expert_ffn_kernel(te_ref, nt_ref, src_ref, dst_ref, h_hbm, wg_hbm, wu_hbm, wd_hbm, y_hbm,
                       xbuf, xb_ref, ybuf, y_ref, wg_s, wu_s, wd_s, stage_in, stage_out, gsem, ssem, wsem,
                       *, tm):
    i = pl.program_id(0)
    n_i = pl.num_programs(0)
    active = i < nt_ref[0]
    buf = i % 2
    nxt = jnp.minimum(i + 1, n_i - 1)
    prv = jnp.maximum(i - 1, 0)
    expert = te_ref[i]
    new_expert = jnp.logical_and(active, jnp.logical_or(i == 0, expert != te_ref[prv]))

    def gather_row(tile, b, r):
        tok = src_ref[tile * tm + r]
        pltpu.make_async_copy(h_hbm.at[pl.ds(pl.multiple_of(tok * SUB, SUB), SUB), :],
                              xbuf.at[b, pl.ds(pl.multiple_of(r * SUB, SUB), SUB), :], gsem.at[b]).start()

    def scatter_row(tile, r):
        slot = dst_ref[tile * tm + r]
        pltpu.make_async_copy(ybuf.at[pl.ds(pl.multiple_of(r * SUB, SUB), SUB), :],
                              y_hbm.at[pl.ds(pl.multiple_of(slot * SUB, SUB), SUB), :], ssem).start()

    def wait_gather(b):
        pltpu.make_async_copy(h_hbm.at[pl.ds(0, tm * SUB), :], xbuf.at[b], gsem.at[b]).wait()

    def wait_scatter():
        pltpu.make_async_copy(ybuf, y_hbm.at[pl.ds(0, tm * SUB), :], ssem).wait()

    def looped(n, fn):
        def body(r, carry):
            fn(r)
            return carry
        lax.fori_loop(0, n, body, 0, unroll=8)

    def share_row(r):
        gather_row(nxt, 1 - buf, r)
        scatter_row(prv, r)

    @pl.when(i == 0)
    def _():
        looped(tm, lambda r: gather_row(0, 0, r))
        ybuf[...] = jnp.zeros_like(ybuf)

    @pl.when(new_expert)
    def _():
        pieces = []
        for w_hbm, w_s, stage in ((wg_hbm, wg_s, stage_in), (wu_hbm, wu_s, stage_in), (wd_hbm, wd_s, stage_out)):
            rows = w_s.shape[0] // W_PIECES
            for c in range(W_PIECES):
                pieces.append((w_hbm.at[expert, pl.ds(c * rows, rows), :], w_s.at[pl.ds(c * rows, rows), :], stage))
        copies = [pltpu.make_async_copy(src, stage.at[k % 2], wsem.at[k % 2])
                  for k, (src, _, stage) in enumerate(pieces)]
        copies[0].start()
        for k, (_, dst, stage) in enumerate(pieces):
            if k + 1 < len(pieces):
                copies[k + 1].start()
            copies[k].wait()
            dst[...] = stage[k % 2].astype(BF16)

    wait_gather(buf)
    for j in range(SUB):
        xb_ref[:, j * LANES:(j + 1) * LANES] = xbuf[buf, pl.ds(j, tm, stride=SUB), :].astype(BF16)

    @pl.when(active)
    def _():
        for r in range(tm):
            share_row(r)
        x = xb_ref[...]
        gate = jnp.dot(x, wg_s[...], preferred_element_type=F32)
        up = jnp.dot(x, wu_s[...], preferred_element_type=F32)
        a = (_silu(gate) * up).astype(BF16)
        y_ref[...] = jnp.dot(a, wd_s[...], preferred_element_type=F32)

    @pl.when(jnp.logical_not(active))
    def _():
        looped(tm, share_row)
        y_ref[...] = jnp.zeros_like(y_ref)

    wait_scatter()
    for j in range(SUB):
        ybuf[pl.ds(j, tm, stride=SUB), :] = y_ref[:, j * LANES:(j + 1) * LANES]

    @pl.when(i == n_i - 1)
    def _():
        looped(tm, lambda r: scatter_row(i, r))
        wait_scatter()
        wait_gather(1 - buf)


def expert_ffn(h_tiles, te, nt, src_tok, dst_slot, wg, wu, wd, tm):
    P = src_tok.shape[0]
    D = wg.shape[1]
    F = wg.shape[2]
    kern = functools.partial(_expert_ffn_kernel, tm=tm)
    anyspec = pl.BlockSpec(memory_space=pl.ANY)
    return pl.pallas_call(
        kern,
        grid_spec=pltpu.PrefetchScalarGridSpec(
            num_scalar_prefetch=4,
            grid=(P // tm,),
            in_specs=[anyspec, anyspec, anyspec, anyspec],
            out_specs=anyspec,
            scratch_shapes=[pltpu.VMEM((2, tm * SUB, LANES), F32),
                            pltpu.VMEM((tm, D), BF16),
                            pltpu.VMEM((tm * SUB, LANES), F32),
                            pltpu.VMEM((tm, D), F32),
                            pltpu.VMEM((D, F), BF16),
                            pltpu.VMEM((D, F), BF16),
                            pltpu.VMEM((F, D), BF16),
                            pltpu.VMEM((2, D // W_PIECES, F), F32),
                            pltpu.VMEM((2, F // W_PIECES, D), F32),
                            pltpu.SemaphoreType.DMA((2,)),
                            pltpu.SemaphoreType.DMA(()),
                            pltpu.SemaphoreType.DMA((2,))],
        ),
        out_shape=jax.ShapeDtypeStruct((P * SUB, LANES), F32),
        compiler_params=_cparams(("arbitrary",)),
        name="expert_ffn",
    )(te, nt, src_tok, dst_slot, h_tiles, wg, wu, wd)


def _moe_combine_kernel(y0_ref, y1_ref, r_ref, h_ref, g_ref, beta_ref, ho_ref, hob_ref, *, tm):
    def rows(y_ref):
        return jnp.concatenate([y_ref[pl.ds(j, tm, stride=SUB), :] for j in range(SUB)], axis=1)

    r = r_ref[...]
    ff = rows(y0_ref) * r[:, 2:3] + rows(y1_ref) * r[:, 3:4]
    hn = _layer_norm(DN_ALPHA * h_ref[...] + ff, g_ref[...], beta_ref[...])
    ho_ref[...] = hn
    hob_ref[...] = hn.astype(BF16)


def moe_combine(y_tiles, route, h, g, beta, tm=512):
    T, D = h.shape
    tm = min(tm, T)
    nb = T // tm
    row = lambda i: (i, 0)
    const = lambda i: (0, 0)
    kern = functools.partial(_moe_combine_kernel, tm=tm)
    return pl.pallas_call(
        kern,
        grid=(nb,),
        in_specs=[pl.BlockSpec((tm * SUB, LANES), lambda i: (i, 0)),
                  pl.BlockSpec((tm * SUB, LANES), lambda i: (nb + i, 0)),
                  pl.BlockSpec((tm, LANES), row),
                  pl.BlockSpec((tm, D), row),
                  pl.BlockSpec((1, D), const), pl.BlockSpec((1, D), const)],
        out_specs=[pl.BlockSpec((tm, D), row), pl.BlockSpec((tm, D), row)],
        out_shape=[jax.ShapeDtypeStruct((T, D), F32), jax.ShapeDtypeStruct((T, D), BF16)],
        compiler_params=_cparams(("parallel",)),
        name="moe_combine",
    )(y_tiles, y_tiles, route, h, g.reshape(1, D), beta.reshape(1, D))


def moe_plan(route, T, tm):
    i32 = jnp.int32
    n_slots = TOP_K * T
    e_flat = route[:, 0:TOP_K].astype(i32).T.reshape(-1)
    onehot = (e_flat[:, None] == jnp.arange(N_EXPERTS, dtype=i32)[None, :]).astype(i32)
    csum = jnp.cumsum(onehot, axis=0)
    rank = jnp.sum(onehot * (csum - 1), axis=1)
    sizes = csum[-1]
    padded = ((sizes + tm - 1) // tm) * tm
    gend = jnp.cumsum(padded)
    gstart = gend - padded
    pos = gstart[e_flat] + rank
    n_rows = n_slots + N_EXPERTS * tm
    slot_plus1 = jnp.zeros((n_rows,), i32).at[pos].set(jnp.arange(1, n_slots + 1, dtype=i32))
    is_pad = slot_plus1 == 0
    pad_rank = jnp.cumsum(is_pad.astype(i32)) - 1
    dst_slot = jnp.where(is_pad, n_slots + pad_rank, slot_plus1 - 1)
    src_tok = jnp.where(is_pad, 0, (slot_plus1 - 1) % T)
    n_tiles = n_rows // tm
    nt = (gend[-1] // tm).astype(i32)
    tile_start = jnp.arange(n_tiles, dtype=i32) * tm
    te = jnp.minimum(jnp.sum((tile_start[:, None] >= gend[None, :]).astype(i32), axis=1), N_EXPERTS - 1)
    te = jnp.where(jnp.arange(n_tiles) < nt, te, te[jnp.maximum(nt - 1, 0)])
    return src_tok.astype(i32), dst_slot.astype(i32), te.astype(i32), nt.reshape(1)


def moe_ffn(h, h_tiles, w_router, wg, wu, wd, g, beta, tm=512):
    T, D = h.shape
    route = router_top2(h, w_router)
    src_tok, dst_slot, te, nt = moe_plan(route, T, tm)
    y_tiles = expert_ffn(h_tiles, te, nt, src_tok, dst_slot, wg, wu, wd, tm)
    return moe_combine(y_tiles, route, h, g, beta)


def permute_w_in(w_in, layer):
    widths = [RET_QK, RET_QK, RET_V, RET_V, HG_K, HG_K, HG_V, HG_V, SWA_Q, SWA_KV, SWA_KV,
              D_MODEL, D_MODEL, D_MODEL]
    starts = np.concatenate([[0], np.cumsum(widths)])
    names = ["rq", "rk", "rv", "rg", "hq", "hf", "hi", "hg", "sq", "sk", "sv", "ga", "gb", "gc"]
    off = dict(zip(names, starts[:-1].tolist()))
    wid = dict(zip(names, widths))

    def cols(name, lo=0, n=None):
        a = off[name] + lo
        return w_in[layer, :, a:a + (wid[name] if n is None else n)]

    pieces = [cols(n) for n in ("rq", "rk", "rv", "rg", "hq", "hi", "hg", "ga", "gb", "gc")]
    qw = SWA_G * SWA_HD
    for kvh in range(SWA_HKV):
        k = cols("sk", kvh * SWA_HD, SWA_HD)
        v = cols("sv", kvh * SWA_HD, SWA_HD)
        pieces += [cols("sq", kvh * qw, qw) * SWA_HD ** -0.5, k, v, v, k]
    main = jnp.concatenate(pieces, axis=1).astype(BF16)
    return main, cols("hf").astype(BF16)


def kernel(x, ln_in_g, ln_in_b, w_in, ret_w_out, hgrn_lower_bounds, hgrn_norm_g, hgrn_w_out, swa_sinks,
           swa_w_out, w_o, ln_mix_g, ln_mix_b, ffn_w_gate, ffn_w_up, ffn_w_down, moe_router, moe_w_gate,
           moe_w_up, moe_w_down, ln_ffn_g, ln_ffn_b):
    B, S, D = x.shape
    T = B * S
    assert D == D_MODEL and S % RET_CHUNK == 0 and S % (2 * SWA_CHUNK) == 0 and S >= 4 * SWA_CHUNK

    lb_all = jnp.cumsum(jax.nn.softmax(hgrn_lower_bounds.astype(F32), axis=0), axis=0)
    lb_all = lb_all - lb_all[0]
    tabs = retention_tables(S)

    h, hb = ln_in(x.reshape(T, D), ln_in_g, ln_in_b)
    for layer in range(DEPTH):
        w_main, w_hf = permute_w_in(w_in, layer)
        proj = matmul(hb, w_main, BF16, tn=N_PROJ // 4, name="in_proj")
        hf = matmul(hb, w_hf, F32, tn=HG_K, name="in_proj_forget")

        ya = retention_mixer(proj, tabs, B, S)
        yb = hgrn_mixer(proj, hf, lb_all[layer], hgrn_norm_g[layer], B, S)
        yc = swa_mixer(proj, swa_sinks[layer], B, S)
        dense = layer % 2 == 0
        h, aux = merge_project(ya, yb, yc, proj, h,
                               ret_w_out[layer].astype(BF16), hgrn_w_out[layer].astype(BF16),
                               swa_w_out[layer].astype(BF16), w_o[layer].astype(BF16),
                               ln_mix_g[layer], ln_mix_b[layer], token_tiles=not dense)

        j = layer // 2
        if dense:
            h, hb = dense_ffn(aux, h, ffn_w_gate[j].astype(BF16), ffn_w_up[j].astype(BF16),
                              ffn_w_down[j].astype(BF16), ln_ffn_g[layer], ln_ffn_b[layer])
        else:
            h, hb = moe_ffn(h, aux, moe_router[j], moe_w_gate[j], moe_w_up[j], moe_w_down[j],
                            ln_ffn_g[layer], ln_ffn_b[layer])
    return h.reshape(B, S, D)
```

```python
import functools

import numpy as np
import jax
import jax.numpy as jnp
from jax import lax
from jax.experimental import pallas as pl
from jax.experimental.pallas import tpu as pltpu

F32 = jnp.float32
BF16 = jnp.bfloat16

D_MODEL = 1024
RET_HEADS, RET_DK, RET_DV = 4, 128, 256
RET_QK, RET_V = RET_HEADS * RET_DK, RET_HEADS * RET_DV
ROPE_BASE = 10000.0
HG_HEADS, HG_DK, HG_DV = 8, 128, 128
HG_K, HG_V = HG_HEADS * HG_DK, HG_HEADS * HG_DV
SWA_HQ, SWA_HKV, SWA_HD = 16, 4, 64
SWA_G = SWA_HQ // SWA_HKV
SWA_Q, SWA_KV = SWA_HQ * SWA_HD, SWA_HKV * SWA_HD
SWA_CHUNK = 64
SWA_WIN_CHUNKS = 2
N_EXPERTS, TOP_K = 8, 2
DEPTH = 2
LN_EPS, RMS_EPS = 1e-5, 1e-6
DN_ALPHA = (2.0 * DEPTH) ** 0.25

VMEM_LIMIT_BYTES = 56 * 1024 * 1024
LANES = 128
SUB = 8

RET_CHUNK = 128
RET_BLOCK = 512
HG_CHUNK = 32
HG_BLOCK = 256
MIX_SPAN = 1024

OFF_RQ, OFF_RK, OFF_RV, OFF_RG = 0, 512, 1024, 2048
OFF_HQ, OFF_HI, OFF_HG = 3072, 4096, 5120
OFF_GA, OFF_GB, OFF_GC = 6144, 7168, 8192
OFF_SWA = 9216
SWA_GROUP_W = SWA_G * SWA_HD + 4 * SWA_HD
N_PROJ = OFF_SWA + SWA_HKV * SWA_GROUP_W


def _cparams(sem, vmem=VMEM_LIMIT_BYTES):
    return pltpu.CompilerParams(dimension_semantics=sem, vmem_limit_bytes=vmem)


def _layer_norm(x, g, b):
    mu = jnp.mean(x, -1, keepdims=True)
    xc = x - mu
    var = jnp.mean(xc * xc, -1, keepdims=True)
    return xc * lax.rsqrt(var + LN_EPS) * g + b


def _silu(x):
    return x * jax.nn.sigmoid(x)


def _ln_in_kernel(x_ref, g_ref, b_ref, h_ref, hb_ref):
    h = _layer_norm(x_ref[...], g_ref[...], b_ref[...])
    h_ref[...] = h
    hb_ref[...] = h.astype(BF16)


def ln_in(x2, g, b, tm=512):
    T, D = x2.shape
    tm = min(tm, T)
    return pl.pallas_call(
        _ln_in_kernel,
        grid=(T // tm,),
        in_specs=[pl.BlockSpec((tm, D), lambda i: (i, 0)),
                  pl.BlockSpec((1, D), lambda i: (0, 0)),
                  pl.BlockSpec((1, D), lambda i: (0, 0))],
        out_specs=[pl.BlockSpec((tm, D), lambda i: (i, 0)),
                   pl.BlockSpec((tm, D), lambda i: (i, 0))],
        out_shape=[jax.ShapeDtypeStruct((T, D), F32), jax.ShapeDtypeStruct((T, D), BF16)],
        compiler_params=_cparams(("parallel",)),
        name="ln_in",
    )(x2, g.reshape(1, D), b.reshape(1, D))


def _matmul_kernel(x_ref, w_ref, o_ref):
    o_ref[...] = jnp.dot(x_ref[...], w_ref[...], preferred_element_type=F32).astype(o_ref.dtype)


def matmul(x, w, out_dtype, tm=1024, tn=512, name="matmul"):
    T, K = x.shape
    N = w.shape[1]
    tm = min(tm, T)
    return pl.pallas_call(
        _matmul_kernel,
        grid=(T // tm, N // tn),
        in_specs=[pl.BlockSpec((tm, K), lambda i, j: (i, 0)),
                  pl.BlockSpec((K, tn), lambda i, j: (0, j))],
        out_specs=pl.BlockSpec((tm, tn), lambda i, j: (i, j)),
        out_shape=jax.ShapeDtypeStruct((T, N), out_dtype),
        compiler_params=_cparams(("parallel", "parallel")),
        name=name,
    )(x, w)


def _retention_block(q_ref, k_ref, v_ref, g_ref, cos_ref, sin_ref, tabs, o_ref, st_ref, start, *, chunk, block):
    dm, qd, kd, cd = tabs
    nc = block // chunk
    nt_dims = (((1,), (1,)), ((), ()))
    tn_dims = (((0,), (0,)), ((), ()))
    r = pl.ds(pl.multiple_of(start, block), block)
    cos = cos_ref[r, :]
    sin = sin_ref[r, :]
    q = q_ref[r, :].astype(F32)
    k = k_ref[r, :].astype(F32)
    qr = q * cos + pltpu.roll(q, RET_DK // 2, 1) * sin
    kr = k * cos + pltpu.roll(k, RET_DK // 2, 1) * sin
    v = v_ref[r, :]
    rows = [slice(c * chunk, (c + 1) * chunk) for c in range(nc)]
    upd = [lax.dot_general((kr[rc] * kd).astype(BF16), v[rc], tn_dims, preferred_element_type=F32)
           for rc in rows]
    att = [lax.dot_general(qr[rc].astype(BF16), kr[rc].astype(BF16), nt_dims,
                           preferred_element_type=F32) * dm for rc in rows]
    intra = [jnp.dot(a.astype(BF16), v[rc], preferred_element_type=F32) for a, rc in zip(att, rows)]
    q_in = [(qr[rc] * qd).astype(BF16) for rc in rows]
    st = st_ref[...]
    outs = []
    for c in range(nc):
        outs.append(intra[c] + jnp.dot(q_in[c], st.astype(BF16), preferred_element_type=F32))
        st = st * cd + upd[c]
    st_ref[...] = st
    o = jnp.concatenate(outs, axis=0)
    on = o * lax.rsqrt(jnp.mean(o * o, -1, keepdims=True) + RMS_EPS)
    g = g_ref[r, :].astype(F32)
    o_ref[r, :] = (_silu(g) * on).astype(o_ref.dtype)


def retention_tables(S):
    C = RET_CHUNK
    half = RET_DK // 2
    pos = jnp.arange(S, dtype=F32)
    inv = 1.0 / (ROPE_BASE ** jnp.linspace(0.0, 1.0, half, dtype=F32))
    ang = pos[:, None] * inv[None, :]
    cos, sin = jnp.cos(ang), jnp.sin(ang)
    log_gamma = jnp.log(1.0 - 2.0 ** (-5.0 - jnp.arange(RET_HEADS, dtype=F32)))
    idx = jnp.arange(C, dtype=F32)
    diff = idx[:, None] - idx[None, :]
    decay = jnp.where(diff[None] >= 0,
                      jnp.exp(jnp.maximum(diff, 0.0)[None] * log_gamma[:, None, None]), 0.0)
    scale = RET_DK ** -0.5
    qd = jnp.exp((idx + 1.0)[None, :] * log_gamma[:, None])
    kd = jnp.exp((C - 1.0 - idx)[None, :] * log_gamma[:, None]) * scale
    cd = jnp.exp(C * log_gamma)
    return {
        "cos": jnp.concatenate([cos, cos], axis=1),
        "sin": jnp.concatenate([-sin, sin], axis=1),
        "dm": decay * scale,
        "qd": jnp.broadcast_to(qd[:, :, None], (RET_HEADS, C, RET_DK)),
        "kd": jnp.broadcast_to(kd[:, :, None], (RET_HEADS, C, RET_DK)),
        "cd": jnp.broadcast_to(cd[:, None, None], (RET_HEADS, 1, RET_DV)),
    }


def _hgrn_block(q_ref, z_ref, i_ref, g_ref, lb, ng, o_ref, st_ref, start, lanes, consts, *, chunk, block):
    row, causal = consts
    half = chunk // 2
    nc = block // chunk
    nt_dims = (((1,), (1,)), ((), ()))
    tn_dims = (((0,), (0,)), ((), ()))

    def per_chunk(x, lo):
        return jnp.concatenate(
            [jnp.broadcast_to(x[c * chunk + lo:c * chunk + lo + 1, :], (chunk, x.shape[1])) for c in range(nc)],
            axis=0)

    r = pl.ds(pl.multiple_of(start, block), block)
    f = lb + (1.0 - lb) * jax.nn.sigmoid(z_ref[r, lanes])
    cum = jnp.log(f)
    shift = 1
    while shift < chunk:
        cum = cum + jnp.where(row >= shift, pltpu.roll(cum, shift, 0), 0.0)
        shift *= 2
    mid = per_chunk(cum, half - 1)
    last = per_chunk(cum, chunk - 1)
    qh = q_ref[r, lanes].astype(F32) * jnp.exp(cum - mid)
    kh = (1.0 - f) * jnp.exp(mid - cum)
    q_in = (qh * jnp.exp(mid)).astype(BF16)
    k_st = (kh * jnp.exp(last - mid)).astype(BF16)
    dec = jnp.exp(last)
    qh = qh.astype(BF16)
    kh = kh.astype(BF16)
    v = i_ref[r, lanes]
    rows = [slice(c * chunk, (c + 1) * chunk) for c in range(nc)]
    upd = [lax.dot_general(v[rc], k_st[rc], tn_dims, preferred_element_type=F32) for rc in rows]
    att = [lax.dot_general(qh[rc], kh[rc], nt_dims, preferred_element_type=F32) for rc in rows]
    intra = [jnp.dot(jnp.where(causal, a, 0.0).astype(BF16), v[rc], preferred_element_type=F32)
             for a, rc in zip(att, rows)]
    st = st_ref[...]
    outs = []
    for c, rc in enumerate(rows):
        inter = lax.dot_general(q_in[rc], st.astype(BF16), nt_dims, preferred_element_type=F32)
        outs.append(intra[c] + inter)
        st = st * dec[c * chunk:c * chunk + 1, :] + upd[c]
    st_ref[...] = st
    o = jnp.concatenate(outs, axis=0)
    on = o * lax.rsqrt(jnp.mean(o * o, -1, keepdims=True) + RMS_EPS) * ng
    g = g_ref[r, lanes].astype(F32)
    o_ref[r, lanes] = (_silu(g) * on).astype(o_ref.dtype)


def _swa_step(sink_ref, x_ref, o_ref, j, kvh, consts):
    lo_kv, lo_o, rel, zero = consts
    CH = SWA_CHUNK
    HD = SWA_HD
    QR = 2 * CH
    KW = 4 * CH
    QW = SWA_G * HD
    nt_dims = (((1,), (1,)), ((), ()))
    first = jnp.maximum(2 * j - SWA_WIN_CHUNKS, 0)
    rq = pl.ds(pl.multiple_of(j * QR, QR), QR)
    rk = pl.ds(pl.multiple_of(first * CH, CH), KW)
    kv = x_ref[rk, QW:QW + 2 * HD]
    vk = x_ref[rk, QW + 2 * HD:QW + 4 * HD]
    kbd = jnp.concatenate([jnp.where(lo_kv, kv, zero), jnp.where(lo_kv, zero, vk)], axis=0)
    vbd = jnp.concatenate([jnp.where(lo_kv, vk, zero), jnp.where(lo_kv, zero, kv)], axis=0)
    d = rel + (first - 2 * j)
    valid = d * (d + SWA_WIN_CHUNKS) <= 0
    scores = [lax.dot_general(x_ref[rq, p * 2 * HD:(p + 1) * 2 * HD], kbd, nt_dims,
                              preferred_element_type=F32) for p in range(SWA_G // 2)]
    for p, s in enumerate(scores):
        s = jnp.where(valid, s, -jnp.inf)
        probs, inv = [], []
        for hh in range(2):
            sink = sink_ref[kvh * SWA_G + 2 * p + hh]
            sh = s[:, hh * KW:(hh + 1) * KW]
            m = jnp.maximum(jnp.max(sh, -1, keepdims=True), sink)
            e = jnp.exp(sh - m)
            den = jnp.sum(e, -1, keepdims=True) + jnp.exp(sink - m)
            probs.append(e.astype(BF16))
            inv.append(1.0 / den)
        o = jnp.dot(jnp.concatenate(probs, axis=1), vbd, preferred_element_type=F32)
        o = o * jnp.where(lo_o, inv[0], inv[1])
        o_ref[rq, p * 2 * HD:(p + 1) * 2 * HD] = o.astype(o_ref.dtype)


def _mixers_kernel(sink_ref, rq_ref, rk_ref, rv_ref, rg_ref, cos_ref, sin_ref, dm_ref, qd_ref, kd_ref, cd_ref,
                   hq_ref, hz_ref, hi_ref, hg_ref, lb_ref, ng_ref, sx_ref, ya_ref, yb_ref, yc_ref,
                   st_r, st_h, *, span, nspans):
    kvh = pl.program_id(1)
    st_r[...] = jnp.zeros_like(st_r)
    st_h[...] = jnp.zeros_like(st_h)
    ret_tabs = (dm_ref[0], qd_ref[0], kd_ref[0], cd_ref[0])
    lb2 = lb_ref[0]
    ng = ng_ref[...]
    hg_consts = (lax.broadcasted_iota(jnp.int32, (HG_BLOCK, HG_DK), 0) % HG_CHUNK,
                 lax.broadcasted_iota(jnp.int32, (HG_CHUNK, HG_CHUNK), 0)
                 >= lax.broadcasted_iota(jnp.int32, (HG_CHUNK, HG_CHUNK), 1))
    QR, KW, HD = 2 * SWA_CHUNK, 4 * SWA_CHUNK, SWA_HD
    kcol = lax.broadcasted_iota(jnp.int32, (QR, 2 * KW), 1)
    qrow = lax.broadcasted_iota(jnp.int32, (QR, 2 * KW), 0)
    swa_consts = (lax.broadcasted_iota(jnp.int32, (KW, 2 * HD), 1) < HD,
                  lax.broadcasted_iota(jnp.int32, (QR, 2 * HD), 1) < HD,
                  (kcol % KW) // SWA_CHUNK - qrow // SWA_CHUNK,
                  jnp.zeros((KW, 2 * HD), BF16))
    n_swa = span // QR
    n_hg = span // HG_BLOCK
    n_ret = span // RET_BLOCK

    def body(u, carry):
        base = u * span
        for t in range(n_swa):
            _swa_step(sink_ref, sx_ref, yc_ref, u * n_swa + t, kvh, swa_consts)
            if t < 2 * n_hg:
                hh, blk = t % 2, t // 2
                lanes = slice(hh * HG_DK, (hh + 1) * HG_DK)
                _hgrn_block(hq_ref, hz_ref, hi_ref, hg_ref, lb2[:, lanes], ng, yb_ref, st_h.at[hh],
                            base + blk * HG_BLOCK, lanes, hg_consts, chunk=HG_CHUNK, block=HG_BLOCK)
            if t % (n_swa // n_ret) == 0:
                _retention_block(rq_ref, rk_ref, rv_ref, rg_ref, cos_ref, sin_ref, ret_tabs, ya_ref, st_r,
                                 base + (t // (n_swa // n_ret)) * RET_BLOCK, chunk=RET_CHUNK, block=RET_BLOCK)
        return carry

    lax.fori_loop(0, nspans, body, 0)


def token_mixers(proj, hf, tabs, lb, norm_g, sinks, B, S):
    T = B * S
    span = min(MIX_SPAN, S)
    C = RET_CHUNK
    hw = 2 * HG_DK
    ret = lambda off, w: (lambda b, h: (b, off // w + h))
    kern = functools.partial(_mixers_kernel, span=span, nspans=S // span)
    out = jax.ShapeDtypeStruct((T, D_MODEL), BF16)
    outspec = pl.BlockSpec((S, RET_DV), lambda b, h: (b, h))
    return pl.pallas_call(
        kern,
        grid=(B, RET_HEADS),
        in_specs=[pl.BlockSpec(memory_space=pltpu.SMEM),
                  pl.BlockSpec((S, RET_DK), ret(OFF_RQ, RET_DK)),
                  pl.BlockSpec((S, RET_DK), ret(OFF_RK, RET_DK)),
                  pl.BlockSpec((S, RET_DV), ret(OFF_RV, RET_DV)),
                  pl.BlockSpec((S, RET_DV), ret(OFF_RG, RET_DV)),
                  pl.BlockSpec((S, RET_DK), lambda b, h: (0, 0)),
                  pl.BlockSpec((S, RET_DK), lambda b, h: (0, 0)),
                  pl.BlockSpec((1, C, C), lambda b, h: (h, 0, 0)),
                  pl.BlockSpec((1, C, RET_DK), lambda b, h: (h, 0, 0)),
                  pl.BlockSpec((1, C, RET_DK), lambda b, h: (h, 0, 0)),
                  pl.BlockSpec((1, 1, RET_DV), lambda b, h: (h, 0, 0)),
                  pl.BlockSpec((S, hw), ret(OFF_HQ, hw)),
                  pl.BlockSpec((S, hw), lambda b, h: (b, h)),
                  pl.BlockSpec((S, hw), ret(OFF_HI, hw)),
                  pl.BlockSpec((S, hw), ret(OFF_HG, hw)),
                  pl.BlockSpec((1, 1, hw), lambda b, h: (h, 0, 0)),
                  pl.BlockSpec((1, HG_DV), lambda b, h: (0, 0)),
                  pl.BlockSpec((S, SWA_GROUP_W), ret(OFF_SWA, SWA_GROUP_W))],
        out_specs=[outspec, outspec, outspec],
        out_shape=[out, out, out],
        scratch_shapes=[pltpu.VMEM((RET_DK, RET_DV), F32), pltpu.VMEM((2, HG_DV, HG_DK), F32)],
        compiler_params=_cparams(("parallel", "parallel")),
        name="token_mixers",
    )(sinks.astype(F32), proj, proj, proj, proj, tabs["cos"], tabs["sin"], tabs["dm"], tabs["qd"], tabs["kd"],
      tabs["cd"], proj, hf, proj, proj, lb.reshape(RET_HEADS, 1, hw), norm_g.reshape(1, HG_DV), proj)


def _merge_kernel(a_ref, b_ref, c_ref, ga_ref, gb_ref, gc_ref, h_ref, wa_ref, wb_ref, wc_ref, wo_ref,
                  g_ref, beta_ref, ho_ref, aux_ref, *, tm, token_tiles):
    def branch(x_ref, gate_ref, w_ref):
        y = jnp.dot(x_ref[...], w_ref[...], preferred_element_type=F32)
        return jax.nn.sigmoid(gate_ref[...].astype(F32)) * y

    merged = branch(a_ref, ga_ref, wa_ref) + branch(b_ref, gb_ref, wb_ref) + branch(c_ref, gc_ref, wc_ref)
    mix = jnp.dot(merged.astype(BF16), wo_ref[...], preferred_element_type=F32)
    hn = _layer_norm(DN_ALPHA * h_ref[...] + mix, g_ref[...], beta_ref[...])
    ho_ref[...] = hn
    if token_tiles:
        for j in range(SUB):
            aux_ref[pl.ds(j, tm, stride=SUB), :] = hn[:, j * LANES:(j + 1) * LANES]
    else:
        aux_ref[...] = hn.astype(BF16)


def merge_project(ya, yb, yc, proj, h, wa, wb, wc, wo, g, beta, token_tiles, tm=512):
    T, D = h.shape
    tm = min(tm, T)
    row = lambda i: (i, 0)
    const = lambda i: (0, 0)
    wspec = pl.BlockSpec((D, D), const, pipeline_mode=pl.Buffered(1))
    if token_tiles:
        aux_spec = pl.BlockSpec((tm * SUB, LANES), row)
        aux_shape = jax.ShapeDtypeStruct((T * SUB, LANES), F32)
    else:
        aux_spec = pl.BlockSpec((tm, D), row)
        aux_shape = jax.ShapeDtypeStruct((T, D), BF16)
    kern = functools.partial(_merge_kernel, tm=tm, token_tiles=token_tiles)
    return pl.pallas_call(
        kern,
        grid=(T // tm,),
        in_specs=[pl.BlockSpec((tm, D), row), pl.BlockSpec((tm, D), row), pl.BlockSpec((tm, D), row),
                  pl.BlockSpec((tm, D), lambda i: (i, OFF_GA // D_MODEL)),
                  pl.BlockSpec((tm, D), lambda i: (i, OFF_GB // D_MODEL)),
                  pl.BlockSpec((tm, D), lambda i: (i, OFF_GC // D_MODEL)),
                  pl.BlockSpec((tm, D), row),
                  wspec, wspec, wspec, wspec,
                  pl.BlockSpec((1, D), const), pl.BlockSpec((1, D), const)],
        out_specs=[pl.BlockSpec((tm, D), row), aux_spec],
        out_shape=[jax.ShapeDtypeStruct((T, D), F32), aux_shape],
        compiler_params=_cparams(("parallel",)),
        name="merge_project",
    )(ya, yb, yc, proj, proj, proj, h, wa, wb, wc, wo, g.reshape(1, D), beta.reshape(1, D))


def _ffn_kernel(xb_ref, h_ref, wg_ref, wu_ref, wd_ref, g_ref, beta_ref, ho_ref, hob_ref):
    x = xb_ref[...]
    gate = jnp.dot(x, wg_ref[...], preferred_element_type=F32)
    up = jnp.dot(x, wu_ref[...], preferred_element_type=F32)
    a = (_silu(gate) * up).astype(BF16)
    ff = jnp.dot(a, wd_ref[...], preferred_element_type=F32)
    hn = _layer_norm(DN_ALPHA * h_ref[...] + ff, g_ref[...], beta_ref[...])
    ho_ref[...] = hn
    hob_ref[...] = hn.astype(BF16)


def dense_ffn(hb, h, wg, wu, wd, g, beta, tm=512):
    T, D = h.shape
    F = wg.shape[1]
    tm = min(tm, T)
    row = lambda i: (i, 0)
    const = lambda i: (0, 0)
    resident = pl.Buffered(1)
    return pl.pallas_call(
        _ffn_kernel,
        grid=(T // tm,),
        in_specs=[pl.BlockSpec((tm, D), row), pl.BlockSpec((tm, D), row),
                  pl.BlockSpec((D, F), const, pipeline_mode=resident),
                  pl.BlockSpec((D, F), const, pipeline_mode=resident),
                  pl.BlockSpec((F, D), const, pipeline_mode=resident),
                  pl.BlockSpec((1, D), const), pl.BlockSpec((1, D), const)],
        out_specs=[pl.BlockSpec((tm, D), row), pl.BlockSpec((tm, D), row)],
        out_shape=[jax.ShapeDtypeStruct((T, D), F32), jax.ShapeDtypeStruct((T, D), BF16)],
        compiler_params=_cparams(("parallel",)),
        name="dense_ffn",
    )(hb, h, wg, wu, wd, g.reshape(1, D), beta.reshape(1, D))


def _router_kernel(h_ref, w_ref, o_ref):
    logits = jnp.dot(h_ref[...], w_ref[...], preferred_element_type=F32, precision=lax.Precision.HIGHEST)
    lane = lax.broadcasted_iota(jnp.int32, logits.shape, 1)
    l1 = jnp.where(lane < N_EXPERTS, logits, -jnp.inf)
    m1 = jnp.max(l1, -1, keepdims=True)
    i1 = jnp.min(jnp.where(l1 == m1, lane, LANES), -1, keepdims=True)
    l2 = jnp.where(lane == i1, -jnp.inf, l1)
    m2 = jnp.max(l2, -1, keepdims=True)
    i2 = jnp.min(jnp.where(l2 == m2, lane, LANES), -1, keepdims=True)
    e = jnp.exp(m2 - m1)
    w1 = 1.0 / (1.0 + e)
    w2 = e / (1.0 + e)
    out = jnp.where(lane == 0, i1.astype(F32),
                    jnp.where(lane == 1, i2.astype(F32),
                              jnp.where(lane == 2, w1, jnp.where(lane == 3, w2, 0.0))))
    o_ref[...] = out


def router_top2(h, w_router, tm=512):
    T, D = h.shape
    tm = min(tm, T)
    wpad = jnp.zeros((D, LANES), F32).at[:, :N_EXPERTS].set(w_router.astype(F32))
    return pl.pallas_call(
        _router_kernel,
        grid=(T // tm,),
        in_specs=[pl.BlockSpec((tm, D), lambda i: (i, 0)), pl.BlockSpec((D, LANES), lambda i: (0, 0))],
        out_specs=pl.BlockSpec((tm, LANES), lambda i: (i, 0)),
        out_shape=jax.ShapeDtypeStruct((T, LANES), F32),
        compiler_params=_cparams(("parallel",)),
        name="router_top2",
    )(h, wpad)


W_PIECES = 8


def _expert_ffn_kernel(te_ref, nt_ref, src_ref, dst_ref, h_hbm, wg_hbm, wu_hbm, wd_hbm, y_hbm,
                       xbuf, xb_ref, ybuf, y_ref, wg_s, wu_s, wd_s, stage_in, stage_out, gsem, ssem, wsem,
                       *, tm):
    i = pl.program_id(0)
    n_i = pl.num_programs(0)
    active = i < nt_ref[0]
    buf = i % 2
    nxt = jnp.minimum(i + 1, n_i - 1)
    prv = jnp.maximum(i - 1, 0)
    expert = te_ref[i]
    new_expert = jnp.logical_and(active, jnp.logical_or(i == 0, expert != te_ref[prv]))

    def gather_row(tile, b, r):
        tok = src_ref[tile * tm + r]
        pltpu.make_async_copy(h_hbm.at[pl.ds(pl.multiple_of(tok * SUB, SUB), SUB), :],
                              xbuf.at[b, pl.ds(pl.multiple_of(r * SUB, SUB), SUB), :], gsem.at[b]).start()

    def scatter_row(tile, r):
        slot = dst_ref[tile * tm + r]
        pltpu.make_async_copy(ybuf.at[pl.ds(pl.multiple_of(r * SUB, SUB), SUB), :],
                              y_hbm.at[pl.ds(pl.multiple_of(slot * SUB, SUB), SUB), :], ssem).start()

    def wait_gather(b):
        pltpu.make_async_copy(h_hbm.at[pl.ds(0, tm * SUB), :], xbuf.at[b], gsem.at[b]).wait()

    def wait_scatter():
        pltpu.make_async_copy(ybuf, y_hbm.at[pl.ds(0, tm * SUB), :], ssem).wait()

    def looped(n, fn):
        def body(r, carry):
            fn(r)
            return carry
        lax.fori_loop(0, n, body, 0, unroll=8)

    def share_row(r):
        gather_row(nxt, 1 - buf, r)
        scatter_row(prv, r)

    @pl.when(i == 0)
    def _():
        looped(tm, lambda r: gather_row(0, 0, r))
        ybuf[...] = jnp.zeros_like(ybuf)

    @pl.when(new_expert)
    def _():
        pieces = []
        for w_hbm, w_s, stage in ((wg_hbm, wg_s, stage_in), (wu_hbm, wu_s, stage_in), (wd_hbm, wd_s, stage_out)):
            rows = w_s.shape[0] // W_PIECES
            for c in range(W_PIECES):
                pieces.append((w_hbm.at[expert, pl.ds(c * rows, rows), :], w_s.at[pl.ds(c * rows, rows), :], stage))
        copies = [pltpu.make_async_copy(src, stage.at[k % 2], wsem.at[k % 2])
                  for k, (src, _, stage) in enumerate(pieces)]
        copies[0].start()
        for k, (_, dst, stage) in enumerate(pieces):
            if k + 1 < len(pieces):
                copies[k + 1].start()
            copies[k].wait()
            dst[...] = stage[k % 2].astype(BF16)

    wait_gather(buf)
    for j in range(SUB):
        xb_ref[:, j * LANES:(j + 1) * LANES] = xbuf[buf, pl.ds(j, tm, stride=SUB), :].astype(BF16)

    @pl.when(active)
    def _():
        for r in range(tm):
            share_row(r)
        x = xb_ref[...]
        gate = jnp.dot(x, wg_s[...], preferred_element_type=F32)
        up = jnp.dot(x, wu_s[...], preferred_element_type=F32)
        a = (_silu(gate) * up).astype(BF16)
        y_ref[...] = jnp.dot(a, wd_s[...], preferred_element_type=F32)

    @pl.when(jnp.logical_not(active))
    def _():
        looped(tm, share_row)
        y_ref[...] = jnp.zeros_like(y_ref)

    wait_scatter()
    for j in range(SUB):
        ybuf[pl.ds(j, tm, stride=SUB), :] = y_ref[:, j * LANES:(j + 1) * LANES]

    @pl.when(i == n_i - 1)
    def _():
        looped(tm, lambda r: scatter_row(i, r))
        wait_scatter()
        wait_gather(1 - buf)


def expert_ffn(h_tiles, te, nt, src_tok, dst_slot, wg, wu, wd, tm):
    P = src_tok.shape[0]
    D = wg.shape[1]
    F = wg.shape[2]
    kern = functools.partial(_expert_ffn_kernel, tm=tm)
    anyspec = pl.BlockSpec(memory_space=pl.ANY)
    return pl.pallas_call(
        kern,
        grid_spec=pltpu.PrefetchScalarGridSpec(
            num_scalar_prefetch=4,
            grid=(P // tm,),
            in_specs=[anyspec, anyspec, anyspec, anyspec],
            out_specs=anyspec,
            scratch_shapes=[pltpu.VMEM((2, tm * SUB, LANES), F32),
                            pltpu.VMEM((tm, D), BF16),
                            pltpu.VMEM((tm * SUB, LANES), F32),
                            pltpu.VMEM((tm, D), F32),
                            pltpu.VMEM((D, F), BF16),
                            pltpu.VMEM((D, F), BF16),
                            pltpu.VMEM((F, D), BF16),
                            pltpu.VMEM((2, D // W_PIECES, F), F32),
                            pltpu.VMEM((2, F // W_PIECES, D), F32),
                            pltpu.SemaphoreType.DMA((2,)),
                            pltpu.SemaphoreType.DMA(()),
                            pltpu.SemaphoreType.DMA((2,))],
        ),
        out_shape=jax.ShapeDtypeStruct((P * SUB, LANES), F32),
        compiler_params=_cparams(("arbitrary",)),
        name="expert_ffn",
    )(te, nt, src_tok, dst_slot, h_tiles, wg, wu, wd)


def _moe_combine_kernel(y0_ref, y1_ref, r_ref, h_ref, g_ref, beta_ref, ho_ref, hob_ref, *, tm):
    def rows(y_ref):
        return jnp.concatenate([y_ref[pl.ds(j, tm, stride=SUB), :] for j in range(SUB)], axis=1)

    r = r_ref[...]
    ff = rows(y0_ref) * r[:, 2:3] + rows(y1_ref) * r[:, 3:4]
    hn = _layer_norm(DN_ALPHA * h_ref[...] + ff, g_ref[...], beta_ref[...])
    ho_ref[...] = hn
    hob_ref[...] = hn.astype(BF16)


def moe_combine(y_tiles, route, h, g, beta, tm=512):
    T, D = h.shape
    tm = min(tm, T)
    nb = T // tm
    row = lambda i: (i, 0)
    const = lambda i: (0, 0)
    kern = functools.partial(_moe_combine_kernel, tm=tm)
    return pl.pallas_call(
        kern,
        grid=(nb,),
        in_specs=[pl.BlockSpec((tm * SUB, LANES), lambda i: (i, 0)),
                  pl.BlockSpec((tm * SUB, LANES), lambda i: (nb + i, 0)),
                  pl.BlockSpec((tm, LANES), row),
                  pl.BlockSpec((tm, D), row),
                  pl.BlockSpec((1, D), const), pl.BlockSpec((1, D), const)],
        out_specs=[pl.BlockSpec((tm, D), row), pl.BlockSpec((tm, D), row)],
        out_shape=[jax.ShapeDtypeStruct((T, D), F32), jax.ShapeDtypeStruct((T, D), BF16)],
        compiler_params=_cparams(("parallel",)),
        name="moe_combine",
    )(y_tiles, y_tiles, route, h, g.reshape(1, D), beta.reshape(1, D))


def moe_plan(route, T, tm):
    i32 = jnp.int32
    n_slots = TOP_K * T
    e_flat = route[:, 0:TOP_K].astype(i32).T.reshape(-1)
    onehot = (e_flat[:, None] == jnp.arange(N_EXPERTS, dtype=i32)[None, :]).astype(i32)
    csum = jnp.cumsum(onehot, axis=0)
    rank = jnp.sum(onehot * (csum - 1), axis=1)
    sizes = csum[-1]
    padded = ((sizes + tm - 1) // tm) * tm
    gend = jnp.cumsum(padded)
    gstart = gend - padded
    pos = gstart[e_flat] + rank
    n_rows = n_slots + N_EXPERTS * tm
    slot_plus1 = jnp.zeros((n_rows,), i32).at[pos].set(jnp.arange(1, n_slots + 1, dtype=i32))
    is_pad = slot_plus1 == 0
    pad_rank = jnp.cumsum(is_pad.astype(i32)) - 1
    dst_slot = jnp.where(is_pad, n_slots + pad_rank, slot_plus1 - 1)
    src_tok = jnp.where(is_pad, 0, (slot_plus1 - 1) % T)
    n_tiles = n_rows // tm
    nt = (gend[-1] // tm).astype(i32)
    tile_start = jnp.arange(n_tiles, dtype=i32) * tm
    te = jnp.minimum(jnp.sum((tile_start[:, None] >= gend[None, :]).astype(i32), axis=1), N_EXPERTS - 1)
    te = jnp.where(jnp.arange(n_tiles) < nt, te, te[jnp.maximum(nt - 1, 0)])
    return src_tok.astype(i32), dst_slot.astype(i32), te.astype(i32), nt.reshape(1)


def moe_ffn(h, h_tiles, w_router, wg, wu, wd, g, beta, tm=512):
    T, D = h.shape
    route = router_top2(h, w_router)
    src_tok, dst_slot, te, nt = moe_plan(route, T, tm)
    y_tiles = expert_ffn(h_tiles, te, nt, src_tok, dst_slot, wg, wu, wd, tm)
    return moe_combine(y_tiles, route, h, g, beta)


def permute_w_in(w_in, layer):
    widths = [RET_QK, RET_QK, RET_V, RET_V, HG_K, HG_K, HG_V, HG_V, SWA_Q, SWA_KV, SWA_KV,
              D_MODEL, D_MODEL, D_MODEL]
    starts = np.concatenate([[0], np.cumsum(widths)])
    names = ["rq", "rk", "rv", "rg", "hq", "hf", "hi", "hg", "sq", "sk", "sv", "ga", "gb", "gc"]
    off = dict(zip(names, starts[:-1].tolist()))
    wid = dict(zip(names, widths))

    def cols(name, lo=0, n=None):
        a = off[name] + lo
        return w_in[layer, :, a:a + (wid[name] if n is None else n)]

    pieces = [cols(n) for n in ("rq", "rk", "rv", "rg", "hq", "hi", "hg", "ga", "gb", "gc")]
    qw = SWA_G * SWA_HD
    for kvh in range(SWA_HKV):
        k = cols("sk", kvh * SWA_HD, SWA_HD)
        v = cols("sv", kvh * SWA_HD, SWA_HD)
        pieces += [cols("sq", kvh * qw, qw) * SWA_HD ** -0.5, k, v, v, k]
    main = jnp.concatenate(pieces, axis=1).astype(BF16)
    return main, cols("hf").astype(BF16)


def kernel(x, ln_in_g, ln_in_b, w_in, ret_w_out, hgrn_lower_bounds, hgrn_norm_g, hgrn_w_out, swa_sinks,
           swa_w_out, w_o, ln_mix_g, ln_mix_b, ffn_w_gate, ffn_w_up, ffn_w_down, moe_router, moe_w_gate,
           moe_w_up, moe_w_down, ln_ffn_g, ln_ffn_b):
    B, S, D = x.shape
    T = B * S
    assert D == D_MODEL and S % RET_CHUNK == 0 and S % (2 * SWA_CHUNK) == 0 and S >= 4 * SWA_CHUNK

    lb_all = jnp.cumsum(jax.nn.softmax(hgrn_lower_bounds.astype(F32), axis=0), axis=0)
    lb_all = lb_all - lb_all[0]
    tabs = retention_tables(S)

    h, hb = ln_in(x.reshape(T, D), ln_in_g, ln_in_b)
    for layer in range(DEPTH):
        w_main, w_hf = permute_w_in(w_in, layer)
        proj = matmul(hb, w_main, BF16, tn=N_PROJ // 4, name="in_proj")
        hf = matmul(hb, w_hf, F32, tn=HG_K, name="in_proj_forget")

        ya, yb, yc = token_mixers(proj, hf, tabs, lb_all[layer], hgrn_norm_g[layer], swa_sinks[layer], B, S)
        dense = layer % 2 == 0
        h, aux = merge_project(ya, yb, yc, proj, h,
                               ret_w_out[layer].astype(BF16), hgrn_w_out[layer].astype(BF16),
                               swa_w_out[layer].astype(BF16), w_o[layer].astype(BF16),
                               ln_mix_g[layer], ln_mix_b[layer], token_tiles=not dense)

        j = layer // 2
        if dense:
            h, hb = dense_ffn(aux, h, ffn_w_gate[j].astype(BF16), ffn_w_up[j].astype(BF16),
                              ffn_w_down[j].astype(BF16), ln_ffn_g[layer], ln_ffn_b[layer])
        else:
            h, hb = moe_ffn(h, aux, moe_router[j], moe_w_gate[j], moe_w_up[j], moe_w_down[j],
                            ln_ffn_g[layer], ln_ffn_b[layer])
    return h.reshape(B, S, D)
```

```python
import functools

import numpy as np
import jax
import jax.numpy as jnp
from jax import lax
from jax.experimental import pallas as pl
from jax.experimental.pallas import tpu as pltpu

F32 = jnp.float32
BF16 = jnp.bfloat16

D_MODEL = 1024
RET_HEADS, RET_DK, RET_DV = 4, 128, 256
RET_QK, RET_V = RET_HEADS * RET_DK, RET_HEADS * RET_DV
ROPE_BASE = 10000.0
HG_HEADS, HG_DK, HG_DV = 8, 128, 128
HG_K, HG_V = HG_HEADS * HG_DK, HG_HEADS * HG_DV
SWA_HQ, SWA_HKV, SWA_HD = 16, 4, 64
SWA_G = SWA_HQ // SWA_HKV
SWA_Q, SWA_KV = SWA_HQ * SWA_HD, SWA_HKV * SWA_HD
SWA_CHUNK = 64
SWA_WIN_CHUNKS = 2
N_EXPERTS, TOP_K = 8, 2
DEPTH = 2
LN_EPS, RMS_EPS = 1e-5, 1e-6
DN_ALPHA = (2.0 * DEPTH) ** 0.25

VMEM_LIMIT_BYTES = 56 * 1024 * 1024
LANES = 128
SUB = 8

RET_CHUNK = 128
RET_BLOCK = 512
HG_CHUNK = 32
HG_BLOCK = 256
MIX_SPAN = 1024

OFF_RQ, OFF_RK, OFF_RV, OFF_RG = 0, 512, 1024, 2048
OFF_HQ, OFF_HI, OFF_HG = 3072, 4096, 5120
OFF_GA, OFF_GB, OFF_GC = 6144, 7168, 8192
OFF_SWA = 9216
SWA_GROUP_W = SWA_G * SWA_HD + 4 * SWA_HD
N_PROJ = OFF_SWA + SWA_HKV * SWA_GROUP_W


def _cparams(sem, vmem=VMEM_LIMIT_BYTES):
    return pltpu.CompilerParams(dimension_semantics=sem, vmem_limit_bytes=vmem)


def _layer_norm(x, g, b):
    mu = jnp.mean(x, -1, keepdims=True)
    xc = x - mu
    var = jnp.mean(xc * xc, -1, keepdims=True)
    return xc * lax.rsqrt(var + LN_EPS) * g + b


def _silu(x):
    return x * jax.nn.sigmoid(x)


def _ln_in_kernel(x_ref, g_ref, b_ref, h_ref, hb_ref):
    h = _layer_norm(x_ref[...], g_ref[...], b_ref[...])
    h_ref[...] = h
    hb_ref[...] = h.astype(BF16)


def ln_in(x2, g, b, tm=512):
    T, D = x2.shape
    tm = min(tm, T)
    return pl.pallas_call(
        _ln_in_kernel,
        grid=(T // tm,),
        in_specs=[pl.BlockSpec((tm, D), lambda i: (i, 0)),
                  pl.BlockSpec((1, D), lambda i: (0, 0)),
                  pl.BlockSpec((1, D), lambda i: (0, 0))],
        out_specs=[pl.BlockSpec((tm, D), lambda i: (i, 0)),
                   pl.BlockSpec((tm, D), lambda i: (i, 0))],
        out_shape=[jax.ShapeDtypeStruct((T, D), F32), jax.ShapeDtypeStruct((T, D), BF16)],
        compiler_params=_cparams(("parallel",)),
        name="ln_in",
    )(x2, g.reshape(1, D), b.reshape(1, D))


def _matmul_kernel(x_ref, w_ref, o_ref):
    o_ref[...] = jnp.dot(x_ref[...], w_ref[...], preferred_element_type=F32).astype(o_ref.dtype)


def matmul(x, w, out_dtype, tm=1024, tn=512, name="matmul"):
    T, K = x.shape
    N = w.shape[1]
    tm = min(tm, T)
    return pl.pallas_call(
        _matmul_kernel,
        grid=(T // tm, N // tn),
        in_specs=[pl.BlockSpec((tm, K), lambda i, j: (i, 0)),
                  pl.BlockSpec((K, tn), lambda i, j: (0, j))],
        out_specs=pl.BlockSpec((tm, tn), lambda i, j: (i, j)),
        out_shape=jax.ShapeDtypeStruct((T, N), out_dtype),
        compiler_params=_cparams(("parallel", "parallel")),
        name=name,
    )(x, w)


def _retention_block(q_ref, k_ref, v_ref, g_ref, cos_ref, sin_ref, tabs, o_ref, st_ref, start, *, chunk, block):
    dm, qd, kd, cd = tabs
    nc = block // chunk
    nt_dims = (((1,), (1,)), ((), ()))
    tn_dims = (((0,), (0,)), ((), ()))
    r = pl.ds(pl.multiple_of(start, block), block)
    cos = cos_ref[r, :]
    sin = sin_ref[r, :]
    q = q_ref[r, :].astype(F32)
    k = k_ref[r, :].astype(F32)
    qr = q * cos + pltpu.roll(q, RET_DK // 2, 1) * sin
    kr = k * cos + pltpu.roll(k, RET_DK // 2, 1) * sin
    v = v_ref[r, :]
    rows = [slice(c * chunk, (c + 1) * chunk) for c in range(nc)]
    upd = [lax.dot_general((kr[rc] * kd).astype(BF16), v[rc], tn_dims, preferred_element_type=F32)
           for rc in rows]
    att = [lax.dot_general(qr[rc].astype(BF16), kr[rc].astype(BF16), nt_dims,
                           preferred_element_type=F32) * dm for rc in rows]
    intra = [jnp.dot(a.astype(BF16), v[rc], preferred_element_type=F32) for a, rc in zip(att, rows)]
    q_in = [(qr[rc] * qd).astype(BF16) for rc in rows]
    st = st_ref[...]
    outs = []
    for c in range(nc):
        outs.append(intra[c] + jnp.dot(q_in[c], st.astype(BF16), preferred_element_type=F32))
        st = st * cd + upd[c]
    st_ref[...] = st
    o = jnp.concatenate(outs, axis=0)
    on = o * lax.rsqrt(jnp.mean(o * o, -1, keepdims=True) + RMS_EPS)
    g = g_ref[r, :].astype(F32)
    o_ref[r, :] = (_silu(g) * on).astype(o_ref.dtype)


def retention_tables(S):
    C = RET_CHUNK
    half = RET_DK // 2
    pos = jnp.arange(S, dtype=F32)
    inv = 1.0 / (ROPE_BASE ** jnp.linspace(0.0, 1.0, half, dtype=F32))
    ang = pos[:, None] * inv[None, :]
    cos, sin = jnp.cos(ang), jnp.sin(ang)
    log_gamma = jnp.log(1.0 - 2.0 ** (-5.0 - jnp.arange(RET_HEADS, dtype=F32)))
    idx = jnp.arange(C, dtype=F32)
    diff = idx[:, None] - idx[None, :]
    decay = jnp.where(diff[None] >= 0,
                      jnp.exp(jnp.maximum(diff, 0.0)[None] * log_gamma[:, None, None]), 0.0)
    scale = RET_DK ** -0.5
    qd = jnp.exp((idx + 1.0)[None, :] * log_gamma[:, None])
    kd = jnp.exp((C - 1.0 - idx)[None, :] * log_gamma[:, None]) * scale
    cd = jnp.exp(C * log_gamma)
    return {
        "cos": jnp.concatenate([cos, cos], axis=1),
        "sin": jnp.concatenate([-sin, sin], axis=1),
        "dm": decay * scale,
        "qd": jnp.broadcast_to(qd[:, :, None], (RET_HEADS, C, RET_DK)),
        "kd": jnp.broadcast_to(kd[:, :, None], (RET_HEADS, C, RET_DK)),
        "cd": jnp.broadcast_to(cd[:, None, None], (RET_HEADS, 1, RET_DV)),
    }


def _hgrn_block(q_ref, z_ref, i_ref, g_ref, lb, ng, o_ref, st_ref, start, lanes, consts, *, chunk, block):
    row, causal = consts
    half = chunk // 2
    nc = block // chunk
    nt_dims = (((1,), (1,)), ((), ()))
    tn_dims = (((0,), (0,)), ((), ()))

    def per_chunk(x, lo):
        return jnp.concatenate(
            [jnp.broadcast_to(x[c * chunk + lo:c * chunk + lo + 1, :], (chunk, x.shape[1])) for c in range(nc)],
            axis=0)

    r = pl.ds(pl.multiple_of(start, block), block)
    f = lb + (1.0 - lb) * jax.nn.sigmoid(z_ref[r, lanes])
    cum = jnp.log(f)
    shift = 1
    while shift < chunk:
        cum = cum + jnp.where(row >= shift, pltpu.roll(cum, shift, 0), 0.0)
        shift *= 2
    mid = per_chunk(cum, half - 1)
    last = per_chunk(cum, chunk - 1)
    qh = q_ref[r, lanes].astype(F32) * jnp.exp(cum - mid)
    kh = (1.0 - f) * jnp.exp(mid - cum)
    q_in = (qh * jnp.exp(mid)).astype(BF16)
    k_st = (kh * jnp.exp(last - mid)).astype(BF16)
    dec = jnp.exp(last)
    qh = qh.astype(BF16)
    kh = kh.astype(BF16)
    v = i_ref[r, lanes]
    rows = [slice(c * chunk, (c + 1) * chunk) for c in range(nc)]
    upd = [lax.dot_general(v[rc], k_st[rc], tn_dims, preferred_element_type=F32) for rc in rows]
    att = [lax.dot_general(qh[rc], kh[rc], nt_dims, preferred_element_type=F32) for rc in rows]
    intra = [jnp.dot(jnp.where(causal, a, 0.0).astype(BF16), v[rc], preferred_element_type=F32)
             for a, rc in zip(att, rows)]
    st = st_ref[...]
    outs = []
    for c, rc in enumerate(rows):
        inter = lax.dot_general(q_in[rc], st.astype(BF16), nt_dims, preferred_element_type=F32)
        outs.append(intra[c] + inter)
        st = st * dec[c * chunk:c * chunk + 1, :] + upd[c]
    st_ref[...] = st
    o = jnp.concatenate(outs, axis=0)
    on = o * lax.rsqrt(jnp.mean(o * o, -1, keepdims=True) + RMS_EPS) * ng
    g = g_ref[r, lanes].astype(F32)
    o_ref[r, lanes] = (_silu(g) * on).astype(o_ref.dtype)


def _swa_step(sink_ref, x_ref, o_ref, j, kvh, consts):
    lo_kv, lo_o, rel, zero = consts
    CH = SWA_CHUNK
    HD = SWA_HD
    QR = 2 * CH
    KW = 4 * CH
    QW = SWA_G * HD
    nt_dims = (((1,), (1,)), ((), ()))
    first = jnp.maximum(2 * j - SWA_WIN_CHUNKS, 0)
    rq = pl.ds(pl.multiple_of(j * QR, QR), QR)
    rk = pl.ds(pl.multiple_of(first * CH, CH), KW)
    kv = x_ref[rk, QW:QW + 2 * HD]
    vk = x_ref[rk, QW + 2 * HD:QW + 4 * HD]
    kbd = jnp.concatenate([jnp.where(lo_kv, kv, zero), jnp.where(lo_kv, zero, vk)], axis=0)
    vbd = jnp.concatenate([jnp.where(lo_kv, vk, zero), jnp.where(lo_kv, zero, kv)], axis=0)
    d = rel + (first - 2 * j)
    valid = d * (d + SWA_WIN_CHUNKS) <= 0
    scores = [lax.dot_general(x_ref[rq, p * 2 * HD:(p + 1) * 2 * HD], kbd, nt_dims,
                              preferred_element_type=F32) for p in range(SWA_G // 2)]
    for p, s in enumerate(scores):
        s = jnp.where(valid, s, -jnp.inf)
        probs, inv = [], []
        for hh in range(2):
            sink = sink_ref[kvh * SWA_G + 2 * p + hh]
            sh = s[:, hh * KW:(hh + 1) * KW]
            m = jnp.maximum(jnp.max(sh, -1, keepdims=True), sink)
            e = jnp.exp(sh - m)
            den = jnp.sum(e, -1, keepdims=True) + jnp.exp(sink - m)
            probs.append(e.astype(BF16))
            inv.append(1.0 / den)
        o = jnp.dot(jnp.concatenate(probs, axis=1), vbd, preferred_element_type=F32)
        o = o * jnp.where(lo_o, inv[0], inv[1])
        o_ref[rq, p * 2 * HD:(p + 1) * 2 * HD] = o.astype(o_ref.dtype)


def _mixers_kernel(sink_ref, rq_ref, rk_ref, rv_ref, rg_ref, cos_ref, sin_ref, dm_ref, qd_ref, kd_ref, cd_ref,
                   hq_ref, hz_ref, hi_ref, hg_ref, lb_ref, ng_ref, sx_ref, ya_ref, yb_ref, yc_ref,
                   st_r, st_h, *, span, nspans):
    kvh = pl.program_id(1)
    st_r[...] = jnp.zeros_like(st_r)
    st_h[...] = jnp.zeros_like(st_h)
    ret_tabs = (dm_ref[0], qd_ref[0], kd_ref[0], cd_ref[0])
    lb2 = lb_ref[0]
    ng = ng_ref[...]
    hg_consts = (lax.broadcasted_iota(jnp.int32, (HG_BLOCK, HG_DK), 0) % HG_CHUNK,
                 lax.broadcasted_iota(jnp.int32, (HG_CHUNK, HG_CHUNK), 0)
                 >= lax.broadcasted_iota(jnp.int32, (HG_CHUNK, HG_CHUNK), 1))
    QR, KW, HD = 2 * SWA_CHUNK, 4 * SWA_CHUNK, SWA_HD
    kcol = lax.broadcasted_iota(jnp.int32, (QR, 2 * KW), 1)
    qrow = lax.broadcasted_iota(jnp.int32, (QR, 2 * KW), 0)
    swa_consts = (lax.broadcasted_iota(jnp.int32, (KW, 2 * HD), 1) < HD,
                  lax.broadcasted_iota(jnp.int32, (QR, 2 * HD), 1) < HD,
                  (kcol % KW) // SWA_CHUNK - qrow // SWA_CHUNK,
                  jnp.zeros((KW, 2 * HD), BF16))
    n_swa = span // QR
    n_hg = span // HG_BLOCK
    n_ret = span // RET_BLOCK

    def body(u, carry):
        base = u * span
        for t in range(n_swa):
            _swa_step(sink_ref, sx_ref, yc_ref, u * n_swa + t, kvh, swa_consts)
            if t < 2 * n_hg:
                hh, blk = t % 2, t // 2
                lanes = slice(hh * HG_DK, (hh + 1) * HG_DK)
                _hgrn_block(hq_ref, hz_ref, hi_ref, hg_ref, lb2[:, lanes], ng, yb_ref, st_h.at[hh],
                            base + blk * HG_BLOCK, lanes, hg_consts, chunk=HG_CHUNK, block=HG_BLOCK)
            if t % (n_swa // n_ret) == 0:
                _retention_block(rq_ref, rk_ref, rv_ref, rg_ref, cos_ref, sin_ref, ret_tabs, ya_ref, st_r,
                                 base + (t // (n_swa // n_ret)) * RET_BLOCK, chunk=RET_CHUNK, block=RET_BLOCK)
        return carry

    lax.fori_loop(0, nspans, body, 0)


def token_mixers(proj, hf, tabs, lb, norm_g, sinks, B, S):
    T = B * S
    span = min(MIX_SPAN, S)
    C = RET_CHUNK
    hw = 2 * HG_DK
    ret = lambda off, w: (lambda b, h: (b, off // w + h))
    kern = functools.partial(_mixers_kernel, span=span, nspans=S // span)
    out = jax.ShapeDtypeStruct((T, D_MODEL), BF16)
    outspec = pl.BlockSpec((S, RET_DV), lambda b, h: (b, h))
    return pl.pallas_call(
        kern,
        grid=(B, RET_HEADS),
        in_specs=[pl.BlockSpec(memory_space=pltpu.SMEM),
                  pl.BlockSpec((S, RET_DK), ret(OFF_RQ, RET_DK)),
                  pl.BlockSpec((S, RET_DK), ret(OFF_RK, RET_DK)),
                  pl.BlockSpec((S, RET_DV), ret(OFF_RV, RET_DV)),
                  pl.BlockSpec((S, RET_DV), ret(OFF_RG, RET_DV)),
                  pl.BlockSpec((S, RET_DK), lambda b, h: (0, 0)),
                  pl.BlockSpec((S, RET_DK), lambda b, h: (0, 0)),
                  pl.BlockSpec((1, C, C), lambda b, h: (h, 0, 0)),
                  pl.BlockSpec((1, C, RET_DK), lambda b, h: (h, 0, 0)),
                  pl.BlockSpec((1, C, RET_DK), lambda b, h: (h, 0, 0)),
                  pl.BlockSpec((1, 1, RET_DV), lambda b, h: (h, 0, 0)),
                  pl.BlockSpec((S, hw), ret(OFF_HQ, hw)),
                  pl.BlockSpec((S, hw), lambda b, h: (b, h)),
                  pl.BlockSpec((S, hw), ret(OFF_HI, hw)),
                  pl.BlockSpec((S, hw), ret(OFF_HG, hw)),
                  pl.BlockSpec((1, 1, hw), lambda b, h: (h, 0, 0)),
                  pl.BlockSpec((1, HG_DV), lambda b, h: (0, 0)),
                  pl.BlockSpec((S, SWA_GROUP_W), ret(OFF_SWA, SWA_GROUP_W))],
        out_specs=[outspec, outspec, outspec],
        out_shape=[out, out, out],
        scratch_shapes=[pltpu.VMEM((RET_DK, RET_DV), F32), pltpu.VMEM((2, HG_DV, HG_DK), F32)],
        compiler_params=_cparams(("parallel", "parallel")),
        name="token_mixers",
    )(sinks.astype(F32), proj, proj, proj, proj, tabs["cos"], tabs["sin"], tabs["dm"], tabs["qd"], tabs["kd"],
      tabs["cd"], proj, hf, proj, proj, lb.reshape(RET_HEADS, 1, hw), norm_g.reshape(1, HG_DV), proj)


def _merge_kernel(a_ref, b_ref, c_ref, ga_ref, gb_ref, gc_ref, h_ref, wa_ref, wb_ref, wc_ref, wo_ref,
                  g_ref, beta_ref, ho_ref, aux_ref, *, tm, token_tiles):
    def branch(x_ref, gate_ref, w_ref):
        y = jnp.dot(x_ref[...], w_ref[...], preferred_element_type=F32)
        return jax.nn.sigmoid(gate_ref[...].astype(F32)) * y

    merged = branch(a_ref, ga_ref, wa_ref) + branch(b_ref, gb_ref, wb_ref) + branch(c_ref, gc_ref, wc_ref)
    mix = jnp.dot(merged.astype(BF16), wo_ref[...], preferred_element_type=F32)
    hn = _layer_norm(DN_ALPHA * h_ref[...] + mix, g_ref[...], beta_ref[...])
    ho_ref[...] = hn
    if token_tiles:
        for j in range(SUB):
            aux_ref[pl.ds(j, tm, stride=SUB), :] = hn[:, j * LANES:(j + 1) * LANES]
    else:
        aux_ref[...] = hn.astype(BF16)


def merge_project(ya, yb, yc, proj, h, wa, wb, wc, wo, g, beta, token_tiles, tm=512):
    T, D = h.shape
    tm = min(tm, T)
    row = lambda i: (i, 0)
    const = lambda i: (0, 0)
    wspec = pl.BlockSpec((D, D), const, pipeline_mode=pl.Buffered(1))
    if token_tiles:
        aux_spec = pl.BlockSpec((tm * SUB, LANES), row)
        aux_shape = jax.ShapeDtypeStruct((T * SUB, LANES), F32)
    else:
        aux_spec = pl.BlockSpec((tm, D), row)
        aux_shape = jax.ShapeDtypeStruct((T, D), BF16)
    kern = functools.partial(_merge_kernel, tm=tm, token_tiles=token_tiles)
    return pl.pallas_call(
        kern,
        grid=(T // tm,),
        in_specs=[pl.BlockSpec((tm, D), row), pl.BlockSpec((tm, D), row), pl.BlockSpec((tm, D), row),
                  pl.BlockSpec((tm, D), lambda i: (i, OFF_GA // D_MODEL)),
                  pl.BlockSpec((tm, D), lambda i: (i, OFF_GB // D_MODEL)),
                  pl.BlockSpec((tm, D), lambda i: (i, OFF_GC // D_MODEL)),
                  pl.BlockSpec((tm, D), row),
                  wspec, wspec, wspec, wspec,
                  pl.BlockSpec((1, D), const), pl.BlockSpec((1, D), const)],
        out_specs=[pl.BlockSpec((tm, D), row), aux_spec],
        out_shape=[jax.ShapeDtypeStruct((T, D), F32), aux_shape],
        compiler_params=_cparams(("parallel",)),
        name="merge_project",
    )(ya, yb, yc, proj, proj, proj, h, wa, wb, wc, wo, g.reshape(1, D), beta.reshape(1, D))


def _ffn_kernel(xb_ref, h_ref, wg_ref, wu_ref, wd_ref, g_ref, beta_ref, ho_ref, hob_ref):
    x = xb_ref[...]
    gate = jnp.dot(x, wg_ref[...], preferred_element_type=F32)
    up = jnp.dot(x, wu_ref[...], preferred_element_type=F32)
    a = (_silu(gate) * up).astype(BF16)
    ff = jnp.dot(a, wd_ref[...], preferred_element_type=F32)
    hn = _layer_norm(DN_ALPHA * h_ref[...] + ff, g_ref[...], beta_ref[...])
    ho_ref[...] = hn
    hob_ref[...] = hn.astype(BF16)


def dense_ffn(hb, h, wg, wu, wd, g, beta, tm=512):
    T, D = h.shape
    F = wg.shape[1]
    tm = min(tm, T)
    row = lambda i: (i, 0)
    const = lambda i: (0, 0)
    resident = pl.Buffered(1)
    return pl.pallas_call(
        _ffn_kernel,
        grid=(T // tm,),
        in_specs=[pl.BlockSpec((tm, D), row), pl.BlockSpec((tm, D), row),
                  pl.BlockSpec((D, F), const, pipeline_mode=resident),
                  pl.BlockSpec((D, F), const, pipeline_mode=resident),
                  pl.BlockSpec((F, D), const, pipeline_mode=resident),
                  pl.BlockSpec((1, D), const), pl.BlockSpec((1, D), const)],
        out_specs=[pl.BlockSpec((tm, D), row), pl.BlockSpec((tm, D), row)],
        out_shape=[jax.ShapeDtypeStruct((T, D), F32), jax.ShapeDtypeStruct((T, D), BF16)],
        compiler_params=_cparams(("parallel",)),
        name="dense_ffn",
    )(hb, h, wg, wu, wd, g.reshape(1, D), beta.reshape(1, D))


def _router_kernel(h_ref, w_ref, o_ref):
    logits = jnp.dot(h_ref[...], w_ref[...], preferred_element_type=F32, precision=lax.Precision.HIGHEST)
    lane = lax.broadcasted_iota(jnp.int32, logits.shape, 1)
    l1 = jnp.where(lane < N_EXPERTS, logits, -jnp.inf)
    m1 = jnp.max(l1, -1, keepdims=True)
    i1 = jnp.min(jnp.where(l1 == m1, lane, LANES), -1, keepdims=True)
    l2 = jnp.where(lane == i1, -jnp.inf, l1)
    m2 = jnp.max(l2, -1, keepdims=True)
    i2 = jnp.min(jnp.where(l2 == m2, lane, LANES), -1, keepdims=True)
    e = jnp.exp(m2 - m1)
    w1 = 1.0 / (1.0 + e)
    w2 = e / (1.0 + e)
    out = jnp.where(lane == 0, i1.astype(F32),
                    jnp.where(lane == 1, i2.astype(F32),
                              jnp.where(lane == 2, w1, jnp.where(lane == 3, w2, 0.0))))
    o_ref[...] = out


def router_top2(h, w_router, tm=512):
    T, D = h.shape
    tm = min(tm, T)
    wpad = jnp.zeros((D, LANES), F32).at[:, :N_EXPERTS].set(w_router.astype(F32))
    return pl.pallas_call(
        _router_kernel,
        grid=(T // tm,),
        in_specs=[pl.BlockSpec((tm, D), lambda i: (i, 0)), pl.BlockSpec((D, LANES), lambda i: (0, 0))],
        out_specs=pl.BlockSpec((tm, LANES), lambda i: (i, 0)),
        out_shape=jax.ShapeDtypeStruct((T, LANES), F32),
        compiler_params=_cparams(("parallel",)),
        name="router_top2",
    )(h, wpad)


W_PIECES = 8


def _expert_ffn_kernel(te_ref, nt_ref, src_ref, dst_ref, h_hbm, wg_hbm, wu_hbm, wd_hbm, y_hbm,
                       xbuf, xb_ref, ybuf, y_ref, wg_s, wu_s, wd_s, stage_in, stage_out, gsem, ssem, wsem,
                       *, tm):
    i = pl.program_id(0)
    n_i = pl.num_programs(0)
    active = i < nt_ref[0]
    buf = i % 2
    nxt = jnp.minimum(i + 1, n_i - 1)
    prv = jnp.maximum(i - 1, 0)
    expert = te_ref[i]
    new_expert = jnp.logical_and(active, jnp.logical_or(i == 0, expert != te_ref[prv]))

    def dma_queue(r):
        return r % 2 if isinstance(r, int) else 0

    def gather_row(tile, b, r):
        tok = src_ref[tile * tm + r]
        pltpu.make_async_copy(h_hbm.at[pl.ds(pl.multiple_of(tok * SUB, SUB), SUB), :],
                              xbuf.at[b, pl.ds(pl.multiple_of(r * SUB, SUB), SUB), :],
                              gsem.at[b]).start(priority=dma_queue(r))

    def scatter_row(tile, r):
        slot = dst_ref[tile * tm + r]
        pltpu.make_async_copy(ybuf.at[pl.ds(pl.multiple_of(r * SUB, SUB), SUB), :],
                              y_hbm.at[pl.ds(pl.multiple_of(slot * SUB, SUB), SUB), :],
                              ssem).start(priority=dma_queue(r))

    def wait_gather(b):
        pltpu.make_async_copy(h_hbm.at[pl.ds(0, tm * SUB), :], xbuf.at[b], gsem.at[b]).wait()

    def wait_scatter():
        pltpu.make_async_copy(ybuf, y_hbm.at[pl.ds(0, tm * SUB), :], ssem).wait()

    def looped(n, fn):
        def body(r, carry):
            fn(r)
            return carry
        lax.fori_loop(0, n, body, 0, unroll=8)

    def share_row(r):
        gather_row(nxt, 1 - buf, r)
        scatter_row(prv, r)

    @pl.when(i == 0)
    def _():
        looped(tm, lambda r: gather_row(0, 0, r))
        ybuf[...] = jnp.zeros_like(ybuf)

    @pl.when(new_expert)
    def _():
        pieces = []
        for w_hbm, w_s, stage in ((wg_hbm, wg_s, stage_in), (wu_hbm, wu_s, stage_in), (wd_hbm, wd_s, stage_out)):
            rows = w_s.shape[0] // W_PIECES
            for c in range(W_PIECES):
                pieces.append((w_hbm.at[expert, pl.ds(c * rows, rows), :], w_s.at[pl.ds(c * rows, rows), :], stage))
        copies = [pltpu.make_async_copy(src, stage.at[k % 2], wsem.at[k % 2])
                  for k, (src, _, stage) in enumerate(pieces)]
        copies[0].start()
        for k, (_, dst, stage) in enumerate(pieces):
            if k + 1 < len(pieces):
                copies[k + 1].start()
            copies[k].wait()
            dst[...] = stage[k % 2].astype(BF16)

    wait_gather(buf)
    for j in range(SUB):
        xb_ref[:, j * LANES:(j + 1) * LANES] = xbuf[buf, pl.ds(j, tm, stride=SUB), :].astype(BF16)

    @pl.when(active)
    def _():
        for r in range(tm):
            share_row(r)
        x = xb_ref[...]
        gate = jnp.dot(x, wg_s[...], preferred_element_type=F32)
        up = jnp.dot(x, wu_s[...], preferred_element_type=F32)
        a = (_silu(gate) * up).astype(BF16)
        y_ref[...] = jnp.dot(a, wd_s[...], preferred_element_type=F32)

    @pl.when(jnp.logical_not(active))
    def _():
        looped(tm, share_row)
        y_ref[...] = jnp.zeros_like(y_ref)

    wait_scatter()
    for j in range(SUB):
        ybuf[pl.ds(j, tm, stride=SUB), :] = y_ref[:, j * LANES:(j + 1) * LANES]

    @pl.when(i == n_i - 1)
    def _():
        looped(tm, lambda r: scatter_row(i, r))
        wait_scatter()
        wait_gather(1 - buf)


def expert_ffn(h_tiles, te, nt, src_tok, dst_slot, wg, wu, wd, tm):
    P = src_tok.shape[0]
    D = wg.shape[1]
    F = wg.shape[2]
    kern = functools.partial(_expert_ffn_kernel, tm=tm)
    anyspec = pl.BlockSpec(memory_space=pl.ANY)
    return pl.pallas_call(
        kern,
        grid_spec=pltpu.PrefetchScalarGridSpec(
            num_scalar_prefetch=4,
            grid=(P // tm,),
            in_specs=[anyspec, anyspec, anyspec, anyspec],
            out_specs=anyspec,
            scratch_shapes=[pltpu.VMEM((2, tm * SUB, LANES), F32),
                            pltpu.VMEM((tm, D), BF16),
                            pltpu.VMEM((tm * SUB, LANES), F32),
                            pltpu.VMEM((tm, D), F32),
                            pltpu.VMEM((D, F), BF16),
                            pltpu.VMEM((D, F), BF16),
                            pltpu.VMEM((F, D), BF16),
                            pltpu.VMEM((2, D // W_PIECES, F), F32),
                            pltpu.VMEM((2, F // W_PIECES, D), F32),
                            pltpu.SemaphoreType.DMA((2,)),
                            pltpu.SemaphoreType.DMA(()),
                            pltpu.SemaphoreType.DMA((2,))],
        ),
        out_shape=jax.ShapeDtypeStruct((P * SUB, LANES), F32),
        compiler_params=_cparams(("arbitrary",)),
        name="expert_ffn",
    )(te, nt, src_tok, dst_slot, h_tiles, wg, wu, wd)


def _moe_combine_kernel(y0_ref, y1_ref, r_ref, h_ref, g_ref, beta_ref, ho_ref, hob_ref, *, tm):
    def rows(y_ref):
        return jnp.concatenate([y_ref[pl.ds(j, tm, stride=SUB), :] for j in range(SUB)], axis=1)

    r = r_ref[...]
    ff = rows(y0_ref) * r[:, 2:3] + rows(y1_ref) * r[:, 3:4]
    hn = _layer_norm(DN_ALPHA * h_ref[...] + ff, g_ref[...], beta_ref[...])
    ho_ref[...] = hn
    hob_ref[...] = hn.astype(BF16)


def moe_combine(y_tiles, route, h, g, beta, tm=512):
    T, D = h.shape
    tm = min(tm, T)
    nb = T // tm
    row = lambda i: (i, 0)
    const = lambda i: (0, 0)
    kern = functools.partial(_moe_combine_kernel, tm=tm)
    return pl.pallas_call(
        kern,
        grid=(nb,),
        in_specs=[pl.BlockSpec((tm * SUB, LANES), lambda i: (i, 0)),
                  pl.BlockSpec((tm * SUB, LANES), lambda i: (nb + i, 0)),
                  pl.BlockSpec((tm, LANES), row),
                  pl.BlockSpec((tm, D), row),
                  pl.BlockSpec((1, D), const), pl.BlockSpec((1, D), const)],
        out_specs=[pl.BlockSpec((tm, D), row), pl.BlockSpec((tm, D), row)],
        out_shape=[jax.ShapeDtypeStruct((T, D), F32), jax.ShapeDtypeStruct((T, D), BF16)],
        compiler_params=_cparams(("parallel",)),
        name="moe_combine",
    )(y_tiles, y_tiles, route, h, g.reshape(1, D), beta.reshape(1, D))


def moe_plan(route, T, tm):
    i32 = jnp.int32
    n_slots = TOP_K * T
    e_flat = route[:, 0:TOP_K].astype(i32).T.reshape(-1)
    onehot = (e_flat[:, None] == jnp.arange(N_EXPERTS, dtype=i32)[None, :]).astype(i32)
    csum = jnp.cumsum(onehot, axis=0)
    rank = jnp.sum(onehot * (csum - 1), axis=1)
    sizes = csum[-1]
    padded = ((sizes + tm - 1) // tm) * tm
    gend = jnp.cumsum(padded)
    gstart = gend - padded
    pos = gstart[e_flat] + rank
    n_rows = n_slots + N_EXPERTS * tm
    slot_plus1 = jnp.zeros((n_rows,), i32).at[pos].set(jnp.arange(1, n_slots + 1, dtype=i32))
    is_pad = slot_plus1 == 0
    pad_rank = jnp.cumsum(is_pad.astype(i32)) - 1
    dst_slot = jnp.where(is_pad, n_slots + pad_rank, slot_plus1 - 1)
    src_tok = jnp.where(is_pad, 0, (slot_plus1 - 1) % T)
    n_tiles = n_rows // tm
    nt = (gend[-1] // tm).astype(i32)
    tile_start = jnp.arange(n_tiles, dtype=i32) * tm
    te = jnp.minimum(jnp.sum((tile_start[:, None] >= gend[None, :]).astype(i32), axis=1), N_EXPERTS - 1)
    te = jnp.where(jnp.arange(n_tiles) < nt, te, te[jnp.maximum(nt - 1, 0)])
    return src_tok.astype(i32), dst_slot.astype(i32), te.astype(i32), nt.reshape(1)


def moe_ffn(h, h_tiles, w_router, wg, wu, wd, g, beta, tm=512):
    T, D = h.shape
    route = router_top2(h, w_router)
    src_tok, dst_slot, te, nt = moe_plan(route, T, tm)
    y_tiles = expert_ffn(h_tiles, te, nt, src_tok, dst_slot, wg, wu, wd, tm)
    return moe_combine(y_tiles, route, h, g, beta)


def permute_w_in(w_in, layer):
    widths = [RET_QK, RET_QK, RET_V, RET_V, HG_K, HG_K, HG_V, HG_V, SWA_Q, SWA_KV, SWA_KV,
              D_MODEL, D_MODEL, D_MODEL]
    starts = np.concatenate([[0], np.cumsum(widths)])
    names = ["rq", "rk", "rv", "rg", "hq", "hf", "hi", "hg", "sq", "sk", "sv", "ga", "gb", "gc"]
    off = dict(zip(names, starts[:-1].tolist()))
    wid = dict(zip(names, widths))

    def cols(name, lo=0, n=None):
        a = off[name] + lo
        return w_in[layer, :, a:a + (wid[name] if n is None else n)].astype(BF16)

    pieces = [cols(n) for n in ("rq", "rk", "rv", "rg", "hq", "hi", "hg", "ga", "gb", "gc")]
    qw = SWA_G * SWA_HD
    for kvh in range(SWA_HKV):
        k = cols("sk", kvh * SWA_HD, SWA_HD)
        v = cols("sv", kvh * SWA_HD, SWA_HD)
        pieces += [cols("sq", kvh * qw, qw) * SWA_HD ** -0.5, k, v, v, k]
    return jnp.concatenate(pieces, axis=1), cols("hf")


def kernel(x, ln_in_g, ln_in_b, w_in, ret_w_out, hgrn_lower_bounds, hgrn_norm_g, hgrn_w_out, swa_sinks,
           swa_w_out, w_o, ln_mix_g, ln_mix_b, ffn_w_gate, ffn_w_up, ffn_w_down, moe_router, moe_w_gate,
           moe_w_up, moe_w_down, ln_ffn_g, ln_ffn_b):
    B, S, D = x.shape
    T = B * S
    assert D == D_MODEL and S % RET_CHUNK == 0 and S % (2 * SWA_CHUNK) == 0 and S >= 4 * SWA_CHUNK

    lb_all = jnp.cumsum(jax.nn.softmax(hgrn_lower_bounds.astype(F32), axis=0), axis=0)
    lb_all = lb_all - lb_all[0]
    tabs = retention_tables(S)

    h, hb = ln_in(x.reshape(T, D), ln_in_g, ln_in_b)
    for layer in range(DEPTH):
        w_main, w_hf = permute_w_in(w_in, layer)
        proj = matmul(hb, w_main, BF16, tn=N_PROJ // 4, name="in_proj")
        hf = matmul(hb, w_hf, F32, tn=HG_K, name="in_proj_forget")

        ya, yb, yc = token_mixers(proj, hf, tabs, lb_all[layer], hgrn_norm_g[layer], swa_sinks[layer], B, S)
        dense = layer % 2 == 0
        h, aux = merge_project(ya, yb, yc, proj, h,
                               ret_w_out[layer].astype(BF16), hgrn_w_out[layer].astype(BF16),
                               swa_w_out[layer].astype(BF16), w_o[layer].astype(BF16),
                               ln_mix_g[layer], ln_mix_b[layer], token_tiles=not dense)

        j = layer // 2
        if dense:
            h, hb = dense_ffn(aux, h, ffn_w_gate[j].astype(BF16), ffn_w_up[j].astype(BF16),
                              ffn_w_down[j].astype(BF16), ln_ffn_g[layer], ln_ffn_b[layer])
        else:
            h, hb = moe_ffn(h, aux, moe_router[j], moe_w_gate[j], moe_w_up[j], moe_w_down[j],
                            ln_ffn_g[layer], ln_ffn_b[layer])
    return h.reshape(B, S, D)
```

```python
import functools

import numpy as np
import jax
import jax.numpy as jnp
from jax import lax
from jax.experimental import pallas as pl
from jax.experimental.pallas import tpu as pltpu

F32 = jnp.float32
BF16 = jnp.bfloat16

D_MODEL = 1024
RET_HEADS, RET_DK, RET_DV = 4, 128, 256
RET_QK, RET_V = RET_HEADS * RET_DK, RET_HEADS * RET_DV
ROPE_BASE = 10000.0
HG_HEADS, HG_DK, HG_DV = 8, 128, 128
HG_K, HG_V = HG_HEADS * HG_DK, HG_HEADS * HG_DV
SWA_HQ, SWA_HKV, SWA_HD = 16, 4, 64
SWA_G = SWA_HQ // SWA_HKV
SWA_Q, SWA_KV = SWA_HQ * SWA_HD, SWA_HKV * SWA_HD
SWA_CHUNK = 64
SWA_WIN_CHUNKS = 2
N_EXPERTS, TOP_K = 8, 2
DEPTH = 2
LN_EPS, RMS_EPS = 1e-5, 1e-6
DN_ALPHA = (2.0 * DEPTH) ** 0.25

VMEM_LIMIT_BYTES = 56 * 1024 * 1024
LANES = 128
SUB = 8

RET_CHUNK = 128
RET_BLOCK = 512
HG_CHUNK = 32
HG_BLOCK = 256
MIX_SPAN = 1024

OFF_RQ, OFF_RK, OFF_RV, OFF_RG = 0, 512, 1024, 2048
OFF_HQ, OFF_HI, OFF_HG = 3072, 4096, 5120
OFF_GA, OFF_GB, OFF_GC = 6144, 7168, 8192
OFF_SWA = 9216
SWA_GROUP_W = SWA_G * SWA_HD + 4 * SWA_HD
N_PROJ = OFF_SWA + SWA_HKV * SWA_GROUP_W


def _cparams(sem, vmem=VMEM_LIMIT_BYTES):
    return pltpu.CompilerParams(dimension_semantics=sem, vmem_limit_bytes=vmem)


def _layer_norm(x, g, b):
    mu = jnp.mean(x, -1, keepdims=True)
    xc = x - mu
    var = jnp.mean(xc * xc, -1, keepdims=True)
    return xc * lax.rsqrt(var + LN_EPS) * g + b


def _silu(x):
    return x * jax.nn.sigmoid(x)


def _ln_in_kernel(x_ref, g_ref, b_ref, h_ref, hb_ref):
    h = _layer_norm(x_ref[...], g_ref[...], b_ref[...])
    h_ref[...] = h
    hb_ref[...] = h.astype(BF16)


def ln_in(x2, g, b, tm=512):
    T, D = x2.shape
    tm = min(tm, T)
    return pl.pallas_call(
        _ln_in_kernel,
        grid=(T // tm,),
        in_specs=[pl.BlockSpec((tm, D), lambda i: (i, 0)),
                  pl.BlockSpec((1, D), lambda i: (0, 0)),
                  pl.BlockSpec((1, D), lambda i: (0, 0))],
        out_specs=[pl.BlockSpec((tm, D), lambda i: (i, 0)),
                   pl.BlockSpec((tm, D), lambda i: (i, 0))],
        out_shape=[jax.ShapeDtypeStruct((T, D), F32), jax.ShapeDtypeStruct((T, D), BF16)],
        compiler_params=_cparams(("parallel",)),
        name="ln_in",
    )(x2, g.reshape(1, D), b.reshape(1, D))


def _matmul_kernel(x_ref, w_ref, o_ref):
    o_ref[...] = jnp.dot(x_ref[...], w_ref[...], preferred_element_type=F32).astype(o_ref.dtype)


def matmul(x, w, out_dtype, tm=1024, tn=512, name="matmul"):
    T, K = x.shape
    N = w.shape[1]
    tm = min(tm, T)
    return pl.pallas_call(
        _matmul_kernel,
        grid=(T // tm, N // tn),
        in_specs=[pl.BlockSpec((tm, K), lambda i, j: (i, 0)),
                  pl.BlockSpec((K, tn), lambda i, j: (0, j))],
        out_specs=pl.BlockSpec((tm, tn), lambda i, j: (i, j)),
        out_shape=jax.ShapeDtypeStruct((T, N), out_dtype),
        compiler_params=_cparams(("parallel", "parallel")),
        name=name,
    )(x, w)


def _retention_block(q_ref, k_ref, v_ref, g_ref, cos_ref, sin_ref, tabs, o_ref, st_ref, start, *, chunk, block):
    dm, qd, kd, cd = tabs
    nc = block // chunk
    nt_dims = (((1,), (1,)), ((), ()))
    tn_dims = (((0,), (0,)), ((), ()))
    r = pl.ds(pl.multiple_of(start, block), block)
    cos = cos_ref[r, :]
    sin = sin_ref[r, :]
    q = q_ref[r, :].astype(F32)
    k = k_ref[r, :].astype(F32)
    qr = q * cos + pltpu.roll(q, RET_DK // 2, 1) * sin
    kr = k * cos + pltpu.roll(k, RET_DK // 2, 1) * sin
    v = v_ref[r, :]
    rows = [slice(c * chunk, (c + 1) * chunk) for c in range(nc)]
    upd = [lax.dot_general((kr[rc] * kd).astype(BF16), v[rc], tn_dims, preferred_element_type=F32)
           for rc in rows]
    att = [lax.dot_general(qr[rc].astype(BF16), kr[rc].astype(BF16), nt_dims,
                           preferred_element_type=F32) * dm for rc in rows]
    intra = [jnp.dot(a.astype(BF16), v[rc], preferred_element_type=F32) for a, rc in zip(att, rows)]
    q_in = [(qr[rc] * qd).astype(BF16) for rc in rows]
    st = st_ref[...]
    outs = []
    for c in range(nc):
        outs.append(intra[c] + jnp.dot(q_in[c], st.astype(BF16), preferred_element_type=F32))
        st = st * cd + upd[c]
    st_ref[...] = st
    o = jnp.concatenate(outs, axis=0)
    on = o * lax.rsqrt(jnp.mean(o * o, -1, keepdims=True) + RMS_EPS)
    g = g_ref[r, :].astype(F32)
    o_ref[r, :] = (_silu(g) * on).astype(o_ref.dtype)


def retention_tables(S):
    C = RET_CHUNK
    half = RET_DK // 2
    pos = jnp.arange(S, dtype=F32)
    inv = 1.0 / (ROPE_BASE ** jnp.linspace(0.0, 1.0, half, dtype=F32))
    ang = pos[:, None] * inv[None, :]
    cos, sin = jnp.cos(ang), jnp.sin(ang)
    log_gamma = jnp.log(1.0 - 2.0 ** (-5.0 - jnp.arange(RET_HEADS, dtype=F32)))
    idx = jnp.arange(C, dtype=F32)
    diff = idx[:, None] - idx[None, :]
    decay = jnp.where(diff[None] >= 0,
                      jnp.exp(jnp.maximum(diff, 0.0)[None] * log_gamma[:, None, None]), 0.0)
    scale = RET_DK ** -0.5
    qd = jnp.exp((idx + 1.0)[None, :] * log_gamma[:, None])
    kd = jnp.exp((C - 1.0 - idx)[None, :] * log_gamma[:, None]) * scale
    cd = jnp.exp(C * log_gamma)
    return {
        "cos": jnp.concatenate([cos, cos], axis=1),
        "sin": jnp.concatenate([-sin, sin], axis=1),
        "dm": decay * scale,
        "qd": jnp.broadcast_to(qd[:, :, None], (RET_HEADS, C, RET_DK)),
        "kd": jnp.broadcast_to(kd[:, :, None], (RET_HEADS, C, RET_DK)),
        "cd": jnp.broadcast_to(cd[:, None, None], (RET_HEADS, 1, RET_DV)),
    }


def _hgrn_block(q_ref, z_ref, i_ref, g_ref, lb, ng, o_ref, st_ref, start, lanes, consts, *, chunk, block):
    row, causal = consts
    half = chunk // 2
    nc = block // chunk
    nt_dims = (((1,), (1,)), ((), ()))
    tn_dims = (((0,), (0,)), ((), ()))

    def chunk_rows(x, lo):
        return jnp.concatenate([x[c * chunk + lo:c * chunk + lo + 1, :] for c in range(nc)], axis=0)

    def spread(xc):
        return jnp.concatenate(
            [jnp.broadcast_to(xc[c:c + 1, :], (chunk, xc.shape[1])) for c in range(nc)], axis=0)

    r = pl.ds(pl.multiple_of(start, block), block)
    f = lb + (1.0 - lb) * jax.nn.sigmoid(z_ref[r, lanes])
    cum = jnp.log(f)
    shift = 1
    while shift < chunk:
        cum = cum + jnp.where(row >= shift, pltpu.roll(cum, shift, 0), 0.0)
        shift *= 2
    mid_c = chunk_rows(cum, half - 1)
    last_c = chunk_rows(cum, chunk - 1)
    mid = spread(mid_c)
    qh = q_ref[r, lanes].astype(F32) * jnp.exp(cum - mid)
    kh = (1.0 - f) * jnp.exp(mid - cum)
    q_in = (qh * spread(jnp.exp(mid_c))).astype(BF16)
    k_st = (kh * spread(jnp.exp(last_c - mid_c))).astype(BF16)
    dec = jnp.exp(last_c)
    qh = qh.astype(BF16)
    kh = kh.astype(BF16)
    v = i_ref[r, lanes]
    rows = [slice(c * chunk, (c + 1) * chunk) for c in range(nc)]
    upd = [lax.dot_general(v[rc], k_st[rc], tn_dims, preferred_element_type=F32) for rc in rows]
    att = [lax.dot_general(qh[rc], kh[rc], nt_dims, preferred_element_type=F32) for rc in rows]
    intra = [jnp.dot(jnp.where(causal, a, 0.0).astype(BF16), v[rc], preferred_element_type=F32)
             for a, rc in zip(att, rows)]
    st = st_ref[...]
    outs = []
    for c, rc in enumerate(rows):
        inter = lax.dot_general(q_in[rc], st.astype(BF16), nt_dims, preferred_element_type=F32)
        outs.append(intra[c] + inter)
        st = st * dec[c:c + 1, :] + upd[c]
    st_ref[...] = st
    o = jnp.concatenate(outs, axis=0)
    on = o * lax.rsqrt(jnp.mean(o * o, -1, keepdims=True) + RMS_EPS) * ng
    g = g_ref[r, lanes].astype(F32)
    o_ref[r, lanes] = (_silu(g) * on).astype(o_ref.dtype)


def _swa_step(sink_ref, x_ref, o_ref, j, kvh, consts):
    lo_kv, lo_o, rel, zero = consts
    CH = SWA_CHUNK
    HD = SWA_HD
    QR = 2 * CH
    KW = 4 * CH
    QW = SWA_G * HD
    nt_dims = (((1,), (1,)), ((), ()))
    first = jnp.maximum(2 * j - SWA_WIN_CHUNKS, 0)
    rq = pl.ds(pl.multiple_of(j * QR, QR), QR)
    rk = pl.ds(pl.multiple_of(first * CH, CH), KW)
    kv = x_ref[rk, QW:QW + 2 * HD]
    vk = x_ref[rk, QW + 2 * HD:QW + 4 * HD]
    kbd = jnp.concatenate([jnp.where(lo_kv, kv, zero), jnp.where(lo_kv, zero, vk)], axis=0)
    vbd = jnp.concatenate([jnp.where(lo_kv, vk, zero), jnp.where(lo_kv, zero, kv)], axis=0)
    valid = lax.bitcast_convert_type(rel + (first - 2 * j + SWA_WIN_CHUNKS), jnp.uint32) <= SWA_WIN_CHUNKS
    scores = [lax.dot_general(x_ref[rq, p * 2 * HD:(p + 1) * 2 * HD], kbd, nt_dims,
                              preferred_element_type=F32) for p in range(SWA_G // 2)]
    for p, s in enumerate(scores):
        s = jnp.where(valid, s, -jnp.inf)
        probs, inv = [], []
        for hh in range(2):
            sink = sink_ref[kvh * SWA_G + 2 * p + hh]
            sh = s[:, hh * KW:(hh + 1) * KW]
            m = jnp.maximum(jnp.max(sh, -1, keepdims=True), sink)
            e = jnp.exp(sh - m)
            den = jnp.sum(e, -1, keepdims=True) + jnp.exp(sink - m)
            probs.append(e.astype(BF16))
            inv.append(1.0 / den)
        o = jnp.dot(jnp.concatenate(probs, axis=1), vbd, preferred_element_type=F32)
        o = o * jnp.where(lo_o, inv[0], inv[1])
        o_ref[rq, p * 2 * HD:(p + 1) * 2 * HD] = o.astype(o_ref.dtype)


def _mixers_kernel(sink_ref, rq_ref, rk_ref, rv_ref, rg_ref, cos_ref, sin_ref, dm_ref, qd_ref, kd_ref, cd_ref,
                   hq_ref, hz_ref, hi_ref, hg_ref, lb_ref, ng_ref, sx_ref, ya_ref, yb_ref, yc_ref,
                   st_r, st_h, *, span, nspans):
    kvh = pl.program_id(1)
    st_r[...] = jnp.zeros_like(st_r)
    st_h[...] = jnp.zeros_like(st_h)
    ret_tabs = (dm_ref[0], qd_ref[0], kd_ref[0], cd_ref[0])
    lb2 = lb_ref[0]
    ng = ng_ref[...]
    hg_consts = (lax.broadcasted_iota(jnp.int32, (HG_BLOCK, HG_DK), 0) % HG_CHUNK,
                 lax.broadcasted_iota(jnp.int32, (HG_CHUNK, HG_CHUNK), 0)
                 >= lax.broadcasted_iota(jnp.int32, (HG_CHUNK, HG_CHUNK), 1))
    QR, KW, HD = 2 * SWA_CHUNK, 4 * SWA_CHUNK, SWA_HD
    kcol = lax.broadcasted_iota(jnp.int32, (QR, 2 * KW), 1)
    qrow = lax.broadcasted_iota(jnp.int32, (QR, 2 * KW), 0)
    swa_consts = (lax.broadcasted_iota(jnp.int32, (KW, 2 * HD), 1) < HD,
                  lax.broadcasted_iota(jnp.int32, (QR, 2 * HD), 1) < HD,
                  (kcol % KW) // SWA_CHUNK - qrow // SWA_CHUNK,
                  jnp.zeros((KW, 2 * HD), BF16))
    n_swa = span // QR
    n_hg = span // HG_BLOCK
    n_ret = span // RET_BLOCK

    def body(u, carry):
        base = u * span
        for t in range(n_swa):
            _swa_step(sink_ref, sx_ref, yc_ref, u * n_swa + t, kvh, swa_consts)
            if t < 2 * n_hg:
                hh, blk = t % 2, t // 2
                lanes = slice(hh * HG_DK, (hh + 1) * HG_DK)
                _hgrn_block(hq_ref, hz_ref, hi_ref, hg_ref, lb2[:, lanes], ng, yb_ref, st_h.at[hh],
                            base + blk * HG_BLOCK, lanes, hg_consts, chunk=HG_CHUNK, block=HG_BLOCK)
            if t % (n_swa // n_ret) == 0:
                _retention_block(rq_ref, rk_ref, rv_ref, rg_ref, cos_ref, sin_ref, ret_tabs, ya_ref, st_r,
                                 base + (t // (n_swa // n_ret)) * RET_BLOCK, chunk=RET_CHUNK, block=RET_BLOCK)
        return carry

    lax.fori_loop(0, nspans, body, 0)


def token_mixers(proj, hf, tabs, lb, norm_g, sinks, B, S):
    T = B * S
    span = min(MIX_SPAN, S)
    C = RET_CHUNK
    hw = 2 * HG_DK
    ret = lambda off, w: (lambda b, h: (b, off // w + h))
    kern = functools.partial(_mixers_kernel, span=span, nspans=S // span)
    out = jax.ShapeDtypeStruct((T, D_MODEL), BF16)
    outspec = pl.BlockSpec((S, RET_DV), lambda b, h: (b, h))
    return pl.pallas_call(
        kern,
        grid=(B, RET_HEADS),
        in_specs=[pl.BlockSpec(memory_space=pltpu.SMEM),
                  pl.BlockSpec((S, RET_DK), ret(OFF_RQ, RET_DK)),
                  pl.BlockSpec((S, RET_DK), ret(OFF_RK, RET_DK)),
                  pl.BlockSpec((S, RET_DV), ret(OFF_RV, RET_DV)),
                  pl.BlockSpec((S, RET_DV), ret(OFF_RG, RET_DV)),
                  pl.BlockSpec((S, RET_DK), lambda b, h: (0, 0)),
                  pl.BlockSpec((S, RET_DK), lambda b, h: (0, 0)),
                  pl.BlockSpec((1, C, C), lambda b, h: (h, 0, 0)),
                  pl.BlockSpec((1, C, RET_DK), lambda b, h: (h, 0, 0)),
                  pl.BlockSpec((1, C, RET_DK), lambda b, h: (h, 0, 0)),
                  pl.BlockSpec((1, 1, RET_DV), lambda b, h: (h, 0, 0)),
                  pl.BlockSpec((S, hw), ret(OFF_HQ, hw)),
                  pl.BlockSpec((S, hw), lambda b, h: (b, h)),
                  pl.BlockSpec((S, hw), ret(OFF_HI, hw)),
                  pl.BlockSpec((S, hw), ret(OFF_HG, hw)),
                  pl.BlockSpec((1, 1, hw), lambda b, h: (h, 0, 0)),
                  pl.BlockSpec((1, HG_DV), lambda b, h: (0, 0)),
                  pl.BlockSpec((S, SWA_GROUP_W), ret(OFF_SWA, SWA_GROUP_W))],
        out_specs=[outspec, outspec, outspec],
        out_shape=[out, out, out],
        scratch_shapes=[pltpu.VMEM((RET_DK, RET_DV), F32), pltpu.VMEM((2, HG_DV, HG_DK), F32)],
        compiler_params=_cparams(("parallel", "parallel")),
        name="token_mixers",
    )(sinks.astype(F32), proj, proj, proj, proj, tabs["cos"], tabs["sin"], tabs["dm"], tabs["qd"], tabs["kd"],
      tabs["cd"], proj, hf, proj, proj, lb.reshape(RET_HEADS, 1, hw), norm_g.reshape(1, HG_DV), proj)


def _merge_kernel(a_ref, b_ref, c_ref, ga_ref, gb_ref, gc_ref, h_ref, wa_ref, wb_ref, wc_ref, wo_ref,
                  g_ref, beta_ref, ho_ref, aux_ref, *, tm, token_tiles):
    def branch(x_ref, gate_ref, w_ref):
        y = jnp.dot(x_ref[...], w_ref[...], preferred_element_type=F32)
        return jax.nn.sigmoid(gate_ref[...].astype(F32)) * y

    merged = branch(a_ref, ga_ref, wa_ref) + branch(b_ref, gb_ref, wb_ref) + branch(c_ref, gc_ref, wc_ref)
    mix = jnp.dot(merged.astype(BF16), wo_ref[...], preferred_element_type=F32)
    hn = _layer_norm(DN_ALPHA * h_ref[...] + mix, g_ref[...], beta_ref[...])
    ho_ref[...] = hn
    if token_tiles:
        for j in range(SUB):
            aux_ref[pl.ds(j, tm, stride=SUB), :] = hn[:, j * LANES:(j + 1) * LANES]
    else:
        aux_ref[...] = hn.astype(BF16)


def merge_project(ya, yb, yc, proj, h, wa, wb, wc, wo, g, beta, token_tiles, tm=512):
    T, D = h.shape
    tm = min(tm, T)
    row = lambda i: (i, 0)
    const = lambda i: (0, 0)
    wspec = pl.BlockSpec((D, D), const, pipeline_mode=pl.Buffered(1))
    if token_tiles:
        aux_spec = pl.BlockSpec((tm * SUB, LANES), row)
        aux_shape = jax.ShapeDtypeStruct((T * SUB, LANES), F32)
    else:
        aux_spec = pl.BlockSpec((tm, D), row)
        aux_shape = jax.ShapeDtypeStruct((T, D), BF16)
    kern = functools.partial(_merge_kernel, tm=tm, token_tiles=token_tiles)
    return pl.pallas_call(
        kern,
        grid=(T // tm,),
        in_specs=[pl.BlockSpec((tm, D), row), pl.BlockSpec((tm, D), row), pl.BlockSpec((tm, D), row),
                  pl.BlockSpec((tm, D), lambda i: (i, OFF_GA // D_MODEL)),
                  pl.BlockSpec((tm, D), lambda i: (i, OFF_GB // D_MODEL)),
                  pl.BlockSpec((tm, D), lambda i: (i, OFF_GC // D_MODEL)),
                  pl.BlockSpec((tm, D), row),
                  wspec, wspec, wspec, wspec,
                  pl.BlockSpec((1, D), const), pl.BlockSpec((1, D), const)],
        out_specs=[pl.BlockSpec((tm, D), row), aux_spec],
        out_shape=[jax.ShapeDtypeStruct((T, D), F32), aux_shape],
        compiler_params=_cparams(("parallel",)),
        name="merge_project",
    )(ya, yb, yc, proj, proj, proj, h, wa, wb, wc, wo, g.reshape(1, D), beta.reshape(1, D))


def _ffn_kernel(xb_ref, h_ref, wg_ref, wu_ref, wd_ref, g_ref, beta_ref, ho_ref, hob_ref):
    x = xb_ref[...]
    gate = jnp.dot(x, wg_ref[...], preferred_element_type=F32)
    up = jnp.dot(x, wu_ref[...], preferred_element_type=F32)
    a = (_silu(gate) * up).astype(BF16)
    ff = jnp.dot(a, wd_ref[...], preferred_element_type=F32)
    hn = _layer_norm(DN_ALPHA * h_ref[...] + ff, g_ref[...], beta_ref[...])
    ho_ref[...] = hn
    hob_ref[...] = hn.astype(BF16)


def dense_ffn(hb, h, wg, wu, wd, g, beta, tm=512):
    T, D = h.shape
    F = wg.shape[1]
    tm = min(tm, T)
    row = lambda i: (i, 0)
    const = lambda i: (0, 0)
    resident = pl.Buffered(1)
    return pl.pallas_call(
        _ffn_kernel,
        grid=(T // tm,),
        in_specs=[pl.BlockSpec((tm, D), row), pl.BlockSpec((tm, D), row),
                  pl.BlockSpec((D, F), const, pipeline_mode=resident),
                  pl.BlockSpec((D, F), const, pipeline_mode=resident),
                  pl.BlockSpec((F, D), const, pipeline_mode=resident),
                  pl.BlockSpec((1, D), const), pl.BlockSpec((1, D), const)],
        out_specs=[pl.BlockSpec((tm, D), row), pl.BlockSpec((tm, D), row)],
        out_shape=[jax.ShapeDtypeStruct((T, D), F32), jax.ShapeDtypeStruct((T, D), BF16)],
        compiler_params=_cparams(("parallel",)),
        name="dense_ffn",
    )(hb, h, wg, wu, wd, g.reshape(1, D), beta.reshape(1, D))


def _route_top2(h, w_hi, w_lo):
    h_hi = h.astype(BF16)
    h_lo = (h - h_hi.astype(F32)).astype(BF16)
    logits = (jnp.dot(h_hi, w_hi, preferred_element_type=F32) + jnp.dot(h_lo, w_hi, preferred_element_type=F32)
              + jnp.dot(h_hi, w_lo, preferred_element_type=F32))
    lane = lax.broadcasted_iota(jnp.int32, logits.shape, 1)
    l1 = jnp.where(lane < N_EXPERTS, logits, -jnp.inf)
    m1 = jnp.max(l1, -1, keepdims=True)
    i1 = jnp.min(jnp.where(l1 == m1, lane, LANES), -1, keepdims=True)
    l2 = jnp.where(lane == i1, -jnp.inf, l1)
    m2 = jnp.max(l2, -1, keepdims=True)
    i2 = jnp.min(jnp.where(l2 == m2, lane, LANES), -1, keepdims=True)
    e = jnp.exp(m2 - m1)
    w1 = 1.0 / (1.0 + e)
    w2 = e / (1.0 + e)
    return jnp.where(lane == 0, i1.astype(F32),
                     jnp.where(lane == 1, i2.astype(F32),
                               jnp.where(lane == 2, w1, jnp.where(lane == 3, w2, 0.0))))


def _router_kernel(h_ref, w_ref, o_ref):
    o_ref[...] = _route_top2(h_ref[...], w_ref[0], w_ref[1])


def router_top2(h, w_router, tm=512):
    T, D = h.shape
    tm = min(tm, T)
    w_pad = jnp.zeros((D, LANES), F32).at[:, :N_EXPERTS].set(w_router.astype(F32))
    w_hi = w_pad.astype(BF16)
    w_lo = (w_pad - w_hi.astype(F32)).astype(BF16)
    return pl.pallas_call(
        _router_kernel,
        grid=(T // tm,),
        in_specs=[pl.BlockSpec((tm, D), lambda i: (i, 0)), pl.BlockSpec((2, D, LANES), lambda i: (0, 0, 0))],
        out_specs=pl.BlockSpec((tm, LANES), lambda i: (i, 0)),
        out_shape=jax.ShapeDtypeStruct((T, LANES), F32),
        compiler_params=_cparams(("parallel",)),
        name="router_top2",
    )(h, jnp.stack([w_hi, w_lo]))


W_PIECES = 8


def _expert_ffn_kernel(te_ref, nt_ref, src_ref, dst_ref, h_hbm, wg_hbm, wu_hbm, wd_hbm, y_hbm,
                       xbuf, xb_ref, ybuf, y_ref, wg_s, wu_s, wd_s, stage_in, stage_out, gsem, ssem, wsem,
                       *, tm):
    i = pl.program_id(0)
    n_i = pl.num_programs(0)
    active = i < nt_ref[0]
    buf = i % 2
    nxt = jnp.minimum(i + 1, n_i - 1)
    prv = jnp.maximum(i - 1, 0)
    expert = te_ref[i]
    new_expert = jnp.logical_and(active, jnp.logical_or(i == 0, expert != te_ref[prv]))

    def gather_row(tile, b, r):
        tok = src_ref[tile * tm + r]
        pltpu.make_async_copy(h_hbm.at[pl.ds(pl.multiple_of(tok * SUB, SUB), SUB), :],
                              xbuf.at[b, pl.ds(pl.multiple_of(r * SUB, SUB), SUB), :], gsem.at[b]).start()

    def scatter_row(tile, r):
        slot = dst_ref[tile * tm + r]
        pltpu.make_async_copy(ybuf.at[pl.ds(pl.multiple_of(r * SUB, SUB), SUB), :],
                              y_hbm.at[pl.ds(pl.multiple_of(slot * SUB, SUB), SUB), :], ssem).start()

    def wait_gather(b):
        pltpu.make_async_copy(h_hbm.at[pl.ds(0, tm * SUB), :], xbuf.at[b], gsem.at[b]).wait()

    def wait_scatter():
        pltpu.make_async_copy(ybuf, y_hbm.at[pl.ds(0, tm * SUB), :], ssem).wait()

    def looped(n, fn):
        def body(r, carry):
            fn(r)
            return carry
        lax.fori_loop(0, n, body, 0, unroll=8)

    def share_row(r):
        gather_row(nxt, 1 - buf, r)
        scatter_row(prv, r)

    @pl.when(i == 0)
    def _():
        looped(tm, lambda r: gather_row(0, 0, r))
        ybuf[...] = jnp.zeros_like(ybuf)

    @pl.when(new_expert)
    def _():
        pieces = []
        for w_hbm, w_s, stage in ((wg_hbm, wg_s, stage_in), (wu_hbm, wu_s, stage_in), (wd_hbm, wd_s, stage_out)):
            rows = w_s.shape[0] // W_PIECES
            for c in range(W_PIECES):
                pieces.append((w_hbm.at[expert, pl.ds(c * rows, rows), :], w_s.at[pl.ds(c * rows, rows), :], stage))
        copies = [pltpu.make_async_copy(src, stage.at[k % 2], wsem.at[k % 2])
                  for k, (src, _, stage) in enumerate(pieces)]
        copies[0].start()
        for k, (_, dst, stage) in enumerate(pieces):
            if k + 1 < len(pieces):
                copies[k + 1].start()
            copies[k].wait()
            dst[...] = stage[k % 2].astype(BF16)

    wait_gather(buf)
    for j in range(SUB):
        xb_ref[:, j * LANES:(j + 1) * LANES] = xbuf[buf, pl.ds(j, tm, stride=SUB), :].astype(BF16)

    @pl.when(active)
    def _():
        for r in range(tm):
            share_row(r)
        x = xb_ref[...]
        gate = jnp.dot(x, wg_s[...], preferred_element_type=F32)
        up = jnp.dot(x, wu_s[...], preferred_element_type=F32)
        a = (_silu(gate) * up).astype(BF16)
        y_ref[...] = jnp.dot(a, wd_s[...], preferred_element_type=F32)

    @pl.when(jnp.logical_not(active))
    def _():
        looped(tm, share_row)
        y_ref[...] = jnp.zeros_like(y_ref)

    wait_scatter()
    for j in range(SUB):
        ybuf[pl.ds(j, tm, stride=SUB), :] = y_ref[:, j * LANES:(j + 1) * LANES]

    @pl.when(i == n_i - 1)
    def _():
        looped(tm, lambda r: scatter_row(i, r))
        wait_scatter()
        wait_gather(1 - buf)


def expert_ffn(h_tiles, te, nt, src_tok, dst_slot, wg, wu, wd, tm):
    P = src_tok.shape[0]
    D = wg.shape[1]
    F = wg.shape[2]
    kern = functools.partial(_expert_ffn_kernel, tm=tm)
    anyspec = pl.BlockSpec(memory_space=pl.ANY)
    return pl.pallas_call(
        kern,
        grid_spec=pltpu.PrefetchScalarGridSpec(
            num_scalar_prefetch=4,
            grid=(P // tm,),
            in_specs=[anyspec, anyspec, anyspec, anyspec],
            out_specs=anyspec,
            scratch_shapes=[pltpu.VMEM((2, tm * SUB, LANES), F32),
                            pltpu.VMEM((tm, D), BF16),
                            pltpu.VMEM((tm * SUB, LANES), F32),
                            pltpu.VMEM((tm, D), F32),
                            pltpu.VMEM((D, F), BF16),
                            pltpu.VMEM((D, F), BF16),
                            pltpu.VMEM((F, D), BF16),
                            pltpu.VMEM((2, D // W_PIECES, F), F32),
                            pltpu.VMEM((2, F // W_PIECES, D), F32),
                            pltpu.SemaphoreType.DMA((2,)),
                            pltpu.SemaphoreType.DMA(()),
                            pltpu.SemaphoreType.DMA((2,))],
        ),
        out_shape=jax.ShapeDtypeStruct((P * SUB, LANES), F32),
        compiler_params=_cparams(("arbitrary",)),
        name="expert_ffn",
    )(te, nt, src_tok, dst_slot, h_tiles, wg, wu, wd)


def _moe_combine_kernel(y0_ref, y1_ref, r_ref, h_ref, g_ref, beta_ref, ho_ref, hob_ref, *, tm):
    def rows(y_ref):
        return jnp.concatenate([y_ref[pl.ds(j, tm, stride=SUB), :] for j in range(SUB)], axis=1)

    r = r_ref[...]
    ff = rows(y0_ref) * r[:, 2:3] + rows(y1_ref) * r[:, 3:4]
    hn = _layer_norm(DN_ALPHA * h_ref[...] + ff, g_ref[...], beta_ref[...])
    ho_ref[...] = hn
    hob_ref[...] = hn.astype(BF16)


def moe_combine(y_tiles, route, h, g, beta, tm=512):
    T, D = h.shape
    tm = min(tm, T)
    nb = T // tm
    row = lambda i: (i, 0)
    const = lambda i: (0, 0)
    kern = functools.partial(_moe_combine_kernel, tm=tm)
    return pl.pallas_call(
        kern,
        grid=(nb,),
        in_specs=[pl.BlockSpec((tm * SUB, LANES), lambda i: (i, 0)),
                  pl.BlockSpec((tm * SUB, LANES), lambda i: (nb + i, 0)),
                  pl.BlockSpec((tm, LANES), row),
                  pl.BlockSpec((tm, D), row),
                  pl.BlockSpec((1, D), const), pl.BlockSpec((1, D), const)],
        out_specs=[pl.BlockSpec((tm, D), row), pl.BlockSpec((tm, D), row)],
        out_shape=[jax.ShapeDtypeStruct((T, D), F32), jax.ShapeDtypeStruct((T, D), BF16)],
        compiler_params=_cparams(("parallel",)),
        name="moe_combine",
    )(y_tiles, y_tiles, route, h, g.reshape(1, D), beta.reshape(1, D))


def moe_plan(route, T, tm):
    i32 = jnp.int32
    n_slots = TOP_K * T
    e_flat = route[:, 0:TOP_K].astype(i32).T.reshape(-1)
    onehot = (e_flat[:, None] == jnp.arange(N_EXPERTS, dtype=i32)[None, :]).astype(i32)
    csum = jnp.cumsum(onehot, axis=0)
    rank = jnp.sum(onehot * (csum - 1), axis=1)
    sizes = csum[-1]
    padded = ((sizes + tm - 1) // tm) * tm
    gend = jnp.cumsum(padded)
    gstart = gend - padded
    pos = gstart[e_flat] + rank
    n_rows = n_slots + N_EXPERTS * tm
    slot_plus1 = jnp.zeros((n_rows,), i32).at[pos].set(jnp.arange(1, n_slots + 1, dtype=i32))
    is_pad = slot_plus1 == 0
    pad_rank = jnp.cumsum(is_pad.astype(i32)) - 1
    dst_slot = jnp.where(is_pad, n_slots + pad_rank, slot_plus1 - 1)
    src_tok = jnp.where(is_pad, 0, (slot_plus1 - 1) % T)
    n_tiles = n_rows // tm
    nt = (gend[-1] // tm).astype(i32)
    tile_start = jnp.arange(n_tiles, dtype=i32) * tm
    te = jnp.minimum(jnp.sum((tile_start[:, None] >= gend[None, :]).astype(i32), axis=1), N_EXPERTS - 1)
    te = jnp.where(jnp.arange(n_tiles) < nt, te, te[jnp.maximum(nt - 1, 0)])
    return src_tok.astype(i32), dst_slot.astype(i32), te.astype(i32), nt.reshape(1)


def moe_ffn(h, h_tiles, w_router, wg, wu, wd, g, beta, tm=512):
    T, D = h.shape
    route = router_top2(h, w_router)
    src_tok, dst_slot, te, nt = moe_plan(route, T, tm)
    y_tiles = expert_ffn(h_tiles, te, nt, src_tok, dst_slot, wg, wu, wd, tm)
    return moe_combine(y_tiles, route, h, g, beta)


def permute_w_in(w_in, layer):
    widths = [RET_QK, RET_QK, RET_V, RET_V, HG_K, HG_K, HG_V, HG_V, SWA_Q, SWA_KV, SWA_KV,
              D_MODEL, D_MODEL, D_MODEL]
    starts = np.concatenate([[0], np.cumsum(widths)])
    names = ["rq", "rk", "rv", "rg", "hq", "hf", "hi", "hg", "sq", "sk", "sv", "ga", "gb", "gc"]
    off = dict(zip(names, starts[:-1].tolist()))
    wid = dict(zip(names, widths))

    def cols(name, lo=0, n=None):
        a = off[name] + lo
        return w_in[layer, :, a:a + (wid[name] if n is None else n)]

    pieces = [cols(n) for n in ("rq", "rk", "rv", "rg", "hq", "hi", "hg", "ga", "gb", "gc")]
    qw = SWA_G * SWA_HD
    for kvh in range(SWA_HKV):
        k = cols("sk", kvh * SWA_HD, SWA_HD)
        v = cols("sv", kvh * SWA_HD, SWA_HD)
        pieces += [cols("sq", kvh * qw, qw) * SWA_HD ** -0.5, k, v, v, k]
    main = jnp.concatenate(pieces, axis=1).astype(BF16)
    return main, cols("hf").astype(BF16)


def kernel(x, ln_in_g, ln_in_b, w_in, ret_w_out, hgrn_lower_bounds, hgrn_norm_g, hgrn_w_out, swa_sinks,
           swa_w_out, w_o, ln_mix_g, ln_mix_b, ffn_w_gate, ffn_w_up, ffn_w_down, moe_router, moe_w_gate,
           moe_w_up, moe_w_down, ln_ffn_g, ln_ffn_b):
    B, S, D = x.shape
    T = B * S
    assert D == D_MODEL and S % RET_CHUNK == 0 and S % (2 * SWA_CHUNK) == 0 and S >= 4 * SWA_CHUNK

    lb_all = jnp.cumsum(jax.nn.softmax(hgrn_lower_bounds.astype(F32), axis=0), axis=0)
    lb_all = lb_all - lb_all[0]
    tabs = retention_tables(S)

    h, hb = ln_in(x.reshape(T, D), ln_in_g, ln_in_b)
    for layer in range(DEPTH):
        w_main, w_hf = permute_w_in(w_in, layer)
        proj = matmul(hb, w_main, BF16, tn=N_PROJ // 4, name="in_proj")
        hf = matmul(hb, w_hf, F32, tn=HG_K, name="in_proj_forget")

        ya, yb, yc = token_mixers(proj, hf, tabs, lb_all[layer], hgrn_norm_g[layer], swa_sinks[layer], B, S)
        dense = layer % 2 == 0
        h, aux = merge_project(ya, yb, yc, proj, h,
                               ret_w_out[layer].astype(BF16), hgrn_w_out[layer].astype(BF16),
                               swa_w_out[layer].astype(BF16), w_o[layer].astype(BF16),
                               ln_mix_g[layer], ln_mix_b[layer], token_tiles=not dense)

        j = layer // 2
        if dense:
            h, hb = dense_ffn(aux, h, ffn_w_gate[j].astype(BF16), ffn_w_up[j].astype(BF16),
                              ffn_w_down[j].astype(BF16), ln_ffn_g[layer], ln_ffn_b[layer])
        else:
            h, hb = moe_ffn(h, aux, moe_router[j], moe_w_gate[j], moe_w_up[j], moe_w_down[j],
                            ln_ffn_g[layer], ln_ffn_b[layer])
    return h.reshape(B, S, D)
```

```python
import functools

import numpy as np
import jax
import jax.numpy as jnp
from jax import lax
from jax.experimental import pallas as pl
from jax.experimental.pallas import tpu as pltpu

F32 = jnp.float32
BF16 = jnp.bfloat16

D_MODEL = 1024
RET_HEADS, RET_DK, RET_DV = 4, 128, 256
RET_QK, RET_V = RET_HEADS * RET_DK, RET_HEADS * RET_DV
ROPE_BASE = 10000.0
HG_HEADS, HG_DK, HG_DV = 8, 128, 128
HG_K, HG_V = HG_HEADS * HG_DK, HG_HEADS * HG_DV
SWA_HQ, SWA_HKV, SWA_HD = 16, 4, 64
SWA_G = SWA_HQ // SWA_HKV
SWA_Q, SWA_KV = SWA_HQ * SWA_HD, SWA_HKV * SWA_HD
SWA_CHUNK = 64
SWA_WIN_CHUNKS = 2
N_EXPERTS, TOP_K = 8, 2
DEPTH = 2
LN_EPS, RMS_EPS = 1e-5, 1e-6
DN_ALPHA = (2.0 * DEPTH) ** 0.25

VMEM_LIMIT_BYTES = 56 * 1024 * 1024
LANES = 128
SUB = 8

RET_CHUNK = 128
RET_BLOCK = 512
HG_CHUNK = 32
HG_BLOCK = 256
MIX_SPAN = 1024

OFF_RQ, OFF_RK, OFF_RV, OFF_RG = 0, 512, 1024, 2048
OFF_HQ, OFF_HI, OFF_HG = 3072, 4096, 5120
OFF_GA, OFF_GB, OFF_GC = 6144, 7168, 8192
OFF_SWA = 9216
SWA_GROUP_W = SWA_G * SWA_HD + 4 * SWA_HD
N_PROJ = OFF_SWA + SWA_HKV * SWA_GROUP_W


def _cparams(sem, vmem=VMEM_LIMIT_BYTES):
    return pltpu.CompilerParams(dimension_semantics=sem, vmem_limit_bytes=vmem)


def _layer_norm(x, g, b):
    mu = jnp.mean(x, -1, keepdims=True)
    xc = x - mu
    var = jnp.mean(xc * xc, -1, keepdims=True)
    return xc * lax.rsqrt(var + LN_EPS) * g + b


def _silu(x):
    return x * jax.nn.sigmoid(x)


def _ln_in_kernel(x_ref, g_ref, b_ref, h_ref, hb_ref):
    h = _layer_norm(x_ref[...], g_ref[...], b_ref[...])
    h_ref[...] = h
    hb_ref[...] = h.astype(BF16)


def ln_in(x2, g, b, tm=512):
    T, D = x2.shape
    tm = min(tm, T)
    return pl.pallas_call(
        _ln_in_kernel,
        grid=(T // tm,),
        in_specs=[pl.BlockSpec((tm, D), lambda i: (i, 0)),
                  pl.BlockSpec((1, D), lambda i: (0, 0)),
                  pl.BlockSpec((1, D), lambda i: (0, 0))],
        out_specs=[pl.BlockSpec((tm, D), lambda i: (i, 0)),
                   pl.BlockSpec((tm, D), lambda i: (i, 0))],
        out_shape=[jax.ShapeDtypeStruct((T, D), F32), jax.ShapeDtypeStruct((T, D), BF16)],
        compiler_params=_cparams(("parallel",)),
        name="ln_in",
    )(x2, g.reshape(1, D), b.reshape(1, D))


def _matmul_kernel(x_ref, w_ref, o_ref):
    o_ref[...] = jnp.dot(x_ref[...], w_ref[...], preferred_element_type=F32).astype(o_ref.dtype)


def matmul(x, w, out_dtype, tm=1024, tn=512, name="matmul"):
    T, K = x.shape
    N = w.shape[1]
    tm = min(tm, T)
    return pl.pallas_call(
        _matmul_kernel,
        grid=(T // tm, N // tn),
        in_specs=[pl.BlockSpec((tm, K), lambda i, j: (i, 0)),
                  pl.BlockSpec((K, tn), lambda i, j: (0, j))],
        out_specs=pl.BlockSpec((tm, tn), lambda i, j: (i, j)),
        out_shape=jax.ShapeDtypeStruct((T, N), out_dtype),
        compiler_params=_cparams(("parallel", "parallel")),
        name=name,
    )(x, w)


def _retention_block(q_ref, k_ref, v_ref, g_ref, cos_ref, sin_ref, tabs, o_ref, st_ref, start, *, chunk, block):
    dm, qd, kd, cd = tabs
    nc = block // chunk
    nt_dims = (((1,), (1,)), ((), ()))
    tn_dims = (((0,), (0,)), ((), ()))
    r = pl.ds(pl.multiple_of(start, block), block)
    cos = cos_ref[r, :]
    sin = sin_ref[r, :]
    q = q_ref[r, :].astype(F32)
    k = k_ref[r, :].astype(F32)
    qr = q * cos + pltpu.roll(q, RET_DK // 2, 1) * sin
    kr = k * cos + pltpu.roll(k, RET_DK // 2, 1) * sin
    v = v_ref[r, :]
    rows = [slice(c * chunk, (c + 1) * chunk) for c in range(nc)]
    upd = [lax.dot_general((kr[rc] * kd).astype(BF16), v[rc], tn_dims, preferred_element_type=F32)
           for rc in rows]
    att = [lax.dot_general(qr[rc].astype(BF16), kr[rc].astype(BF16), nt_dims,
                           preferred_element_type=F32) * dm for rc in rows]
    intra = [jnp.dot(a.astype(BF16), v[rc], preferred_element_type=F32) for a, rc in zip(att, rows)]
    q_in = [(qr[rc] * qd).astype(BF16) for rc in rows]
    st = st_ref[...]
    outs = []
    for c in range(nc):
        outs.append(intra[c] + jnp.dot(q_in[c], st.astype(BF16), preferred_element_type=F32))
        st = st * cd + upd[c]
    st_ref[...] = st
    o = jnp.concatenate(outs, axis=0)
    on = o * lax.rsqrt(jnp.mean(o * o, -1, keepdims=True) + RMS_EPS)
    g = g_ref[r, :].astype(F32)
    o_ref[r, :] = (_silu(g) * on).astype(o_ref.dtype)


def retention_tables(S):
    C = RET_CHUNK
    half = RET_DK // 2
    pos = jnp.arange(S, dtype=F32)
    inv = 1.0 / (ROPE_BASE ** jnp.linspace(0.0, 1.0, half, dtype=F32))
    ang = pos[:, None] * inv[None, :]
    cos, sin = jnp.cos(ang), jnp.sin(ang)
    log_gamma = jnp.log(1.0 - 2.0 ** (-5.0 - jnp.arange(RET_HEADS, dtype=F32)))
    idx = jnp.arange(C, dtype=F32)
    diff = idx[:, None] - idx[None, :]
    decay = jnp.where(diff[None] >= 0,
                      jnp.exp(jnp.maximum(diff, 0.0)[None] * log_gamma[:, None, None]), 0.0)
    scale = RET_DK ** -0.5
    qd = jnp.exp((idx + 1.0)[None, :] * log_gamma[:, None])
    kd = jnp.exp((C - 1.0 - idx)[None, :] * log_gamma[:, None]) * scale
    cd = jnp.exp(C * log_gamma)
    return {
        "cos": jnp.concatenate([cos, cos], axis=1),
        "sin": jnp.concatenate([-sin, sin], axis=1),
        "dm": decay * scale,
        "qd": jnp.broadcast_to(qd[:, :, None], (RET_HEADS, C, RET_DK)),
        "kd": jnp.broadcast_to(kd[:, :, None], (RET_HEADS, C, RET_DK)),
        "cd": jnp.broadcast_to(cd[:, None, None], (RET_HEADS, 1, RET_DV)),
    }


def _hgrn_block(q_ref, z_ref, i_ref, g_ref, lb, ng, o_ref, st_ref, start, lanes, consts, *, chunk, block):
    row, causal = consts
    half = chunk // 2
    nc = block // chunk
    nt_dims = (((1,), (1,)), ((), ()))
    tn_dims = (((0,), (0,)), ((), ()))

    def chunk_rows(x, lo):
        return jnp.concatenate([x[c * chunk + lo:c * chunk + lo + 1, :] for c in range(nc)], axis=0)

    def spread(xc):
        return jnp.concatenate(
            [jnp.broadcast_to(xc[c:c + 1, :], (chunk, xc.shape[1])) for c in range(nc)], axis=0)

    r = pl.ds(pl.multiple_of(start, block), block)
    f = lb + (1.0 - lb) * jax.nn.sigmoid(z_ref[r, lanes])
    cum = jnp.log(f)
    shift = 1
    while shift < chunk:
        cum = cum + jnp.where(row >= shift, pltpu.roll(cum, shift, 0), 0.0)
        shift *= 2
    mid_c = chunk_rows(cum, half - 1)
    last_c = chunk_rows(cum, chunk - 1)
    mid = spread(mid_c)
    qh = q_ref[r, lanes].astype(F32) * jnp.exp(cum - mid)
    kh = (1.0 - f) * jnp.exp(mid - cum)
    q_in = (qh * spread(jnp.exp(mid_c))).astype(BF16)
    k_st = (kh * spread(jnp.exp(last_c - mid_c))).astype(BF16)
    dec = jnp.exp(last_c)
    qh = qh.astype(BF16)
    kh = kh.astype(BF16)
    v = i_ref[r, lanes]
    rows = [slice(c * chunk, (c + 1) * chunk) for c in range(nc)]
    upd = [lax.dot_general(v[rc], k_st[rc], tn_dims, preferred_element_type=F32) for rc in rows]
    att = [lax.dot_general(qh[rc], kh[rc], nt_dims, preferred_element_type=F32) for rc in rows]
    intra = [jnp.dot(jnp.where(causal, a, 0.0).astype(BF16), v[rc], preferred_element_type=F32)
             for a, rc in zip(att, rows)]
    st = st_ref[...]
    outs = []
    for c, rc in enumerate(rows):
        inter = lax.dot_general(q_in[rc], st.astype(BF16), nt_dims, preferred_element_type=F32)
        outs.append(intra[c] + inter)
        st = st * dec[c:c + 1, :] + upd[c]
    st_ref[...] = st
    o = jnp.concatenate(outs, axis=0)
    on = o * lax.rsqrt(jnp.mean(o * o, -1, keepdims=True) + RMS_EPS) * ng
    g = g_ref[r, lanes].astype(F32)
    o_ref[r, lanes] = (_silu(g) * on).astype(o_ref.dtype)


def _swa_step(sink_ref, x_ref, o_ref, j, kvh, consts):
    lo_kv, lo_o, rel, zero = consts
    CH = SWA_CHUNK
    HD = SWA_HD
    QR = 2 * CH
    KW = 4 * CH
    QW = SWA_G * HD
    nt_dims = (((1,), (1,)), ((), ()))
    first = jnp.maximum(2 * j - SWA_WIN_CHUNKS, 0)
    rq = pl.ds(pl.multiple_of(j * QR, QR), QR)
    rk = pl.ds(pl.multiple_of(first * CH, CH), KW)
    kv = x_ref[rk, QW:QW + 2 * HD]
    vk = x_ref[rk, QW + 2 * HD:QW + 4 * HD]
    kbd = jnp.concatenate([jnp.where(lo_kv, kv, zero), jnp.where(lo_kv, zero, vk)], axis=0)
    vbd = jnp.concatenate([jnp.where(lo_kv, vk, zero), jnp.where(lo_kv, zero, kv)], axis=0)
    valid = lax.bitcast_convert_type(rel + (first - 2 * j + SWA_WIN_CHUNKS), jnp.uint32) <= SWA_WIN_CHUNKS
    scores = [lax.dot_general(x_ref[rq, p * 2 * HD:(p + 1) * 2 * HD], kbd, nt_dims,
                              preferred_element_type=F32) for p in range(SWA_G // 2)]
    for p, s in enumerate(scores):
        s = jnp.where(valid, s, -jnp.inf)
        probs, inv = [], []
        for hh in range(2):
            sink = sink_ref[kvh * SWA_G + 2 * p + hh]
            sh = s[:, hh * KW:(hh + 1) * KW]
            m = jnp.maximum(jnp.max(sh, -1, keepdims=True), sink)
            e = jnp.exp(sh - m)
            den = jnp.sum(e, -1, keepdims=True) + jnp.exp(sink - m)
            probs.append(e.astype(BF16))
            inv.append(1.0 / den)
        o = jnp.dot(jnp.concatenate(probs, axis=1), vbd, preferred_element_type=F32)
        o = o * jnp.where(lo_o, inv[0], inv[1])
        o_ref[rq, p * 2 * HD:(p + 1) * 2 * HD] = o.astype(o_ref.dtype)


def _mixers_kernel(sink_ref, rq_ref, rk_ref, rv_ref, rg_ref, cos_ref, sin_ref, dm_ref, qd_ref, kd_ref, cd_ref,
                   hq_ref, xb_ref, wf_ref, hi_ref, hg_ref, lb_ref, ng_ref, sx_ref, ya_ref, yb_ref, yc_ref,
                   st_r, st_h, hz_ref, *, span, nspans):
    kvh = pl.program_id(1)
    st_r[...] = jnp.zeros_like(st_r)
    st_h[...] = jnp.zeros_like(st_h)
    hz_ref[...] = jnp.dot(xb_ref[...], wf_ref[...], preferred_element_type=F32)
    ret_tabs = (dm_ref[0], qd_ref[0], kd_ref[0], cd_ref[0])
    lb2 = lb_ref[0]
    ng = ng_ref[...]
    hg_consts = (lax.broadcasted_iota(jnp.int32, (HG_BLOCK, HG_DK), 0) % HG_CHUNK,
                 lax.broadcasted_iota(jnp.int32, (HG_CHUNK, HG_CHUNK), 0)
                 >= lax.broadcasted_iota(jnp.int32, (HG_CHUNK, HG_CHUNK), 1))
    QR, KW, HD = 2 * SWA_CHUNK, 4 * SWA_CHUNK, SWA_HD
    kcol = lax.broadcasted_iota(jnp.int32, (QR, 2 * KW), 1)
    qrow = lax.broadcasted_iota(jnp.int32, (QR, 2 * KW), 0)
    swa_consts = (lax.broadcasted_iota(jnp.int32, (KW, 2 * HD), 1) < HD,
                  lax.broadcasted_iota(jnp.int32, (QR, 2 * HD), 1) < HD,
                  (kcol % KW) // SWA_CHUNK - qrow // SWA_CHUNK,
                  jnp.zeros((KW, 2 * HD), BF16))
    n_swa = span // QR
    n_hg = span // HG_BLOCK
    n_ret = span // RET_BLOCK

    def body(u, carry):
        base = u * span
        for t in range(n_swa):
            _swa_step(sink_ref, sx_ref, yc_ref, u * n_swa + t, kvh, swa_consts)
            if t < 2 * n_hg:
                hh, blk = t % 2, t // 2
                lanes = slice(hh * HG_DK, (hh + 1) * HG_DK)
                _hgrn_block(hq_ref, hz_ref, hi_ref, hg_ref, lb2[:, lanes], ng, yb_ref, st_h.at[hh],
                            base + blk * HG_BLOCK, lanes, hg_consts, chunk=HG_CHUNK, block=HG_BLOCK)
            if t % (n_swa // n_ret) == 0:
                _retention_block(rq_ref, rk_ref, rv_ref, rg_ref, cos_ref, sin_ref, ret_tabs, ya_ref, st_r,
                                 base + (t // (n_swa // n_ret)) * RET_BLOCK, chunk=RET_CHUNK, block=RET_BLOCK)
        return carry

    lax.fori_loop(0, nspans, body, 0)


def token_mixers(proj, hb, w_hf, tabs, lb, norm_g, sinks, B, S):
    T = B * S
    D = hb.shape[1]
    span = min(MIX_SPAN, S)
    C = RET_CHUNK
    hw = 2 * HG_DK
    ret = lambda off, w: (lambda b, h: (b, off // w + h))
    kern = functools.partial(_mixers_kernel, span=span, nspans=S // span)
    out = jax.ShapeDtypeStruct((T, D_MODEL), BF16)
    outspec = pl.BlockSpec((S, RET_DV), lambda b, h: (b, h))
    return pl.pallas_call(
        kern,
        grid=(B, RET_HEADS),
        in_specs=[pl.BlockSpec(memory_space=pltpu.SMEM),
                  pl.BlockSpec((S, RET_DK), ret(OFF_RQ, RET_DK)),
                  pl.BlockSpec((S, RET_DK), ret(OFF_RK, RET_DK)),
                  pl.BlockSpec((S, RET_DV), ret(OFF_RV, RET_DV)),
                  pl.BlockSpec((S, RET_DV), ret(OFF_RG, RET_DV)),
                  pl.BlockSpec((S, RET_DK), lambda b, h: (0, 0)),
                  pl.BlockSpec((S, RET_DK), lambda b, h: (0, 0)),
                  pl.BlockSpec((1, C, C), lambda b, h: (h, 0, 0)),
                  pl.BlockSpec((1, C, RET_DK), lambda b, h: (h, 0, 0)),
                  pl.BlockSpec((1, C, RET_DK), lambda b, h: (h, 0, 0)),
                  pl.BlockSpec((1, 1, RET_DV), lambda b, h: (h, 0, 0)),
                  pl.BlockSpec((S, hw), ret(OFF_HQ, hw)),
                  pl.BlockSpec((S, D), lambda b, h: (b, 0)),
                  pl.BlockSpec((D, hw), lambda b, h: (0, h)),
                  pl.BlockSpec((S, hw), ret(OFF_HI, hw)),
                  pl.BlockSpec((S, hw), ret(OFF_HG, hw)),
                  pl.BlockSpec((1, 1, hw), lambda b, h: (h, 0, 0)),
                  pl.BlockSpec((1, HG_DV), lambda b, h: (0, 0)),
                  pl.BlockSpec((S, SWA_GROUP_W), ret(OFF_SWA, SWA_GROUP_W))],
        out_specs=[outspec, outspec, outspec],
        out_shape=[out, out, out],
        scratch_shapes=[pltpu.VMEM((RET_DK, RET_DV), F32), pltpu.VMEM((2, HG_DV, HG_DK), F32),
                        pltpu.VMEM((S, hw), F32)],
        compiler_params=_cparams(("parallel", "parallel")),
        name="token_mixers",
    )(sinks.astype(F32), proj, proj, proj, proj, tabs["cos"], tabs["sin"], tabs["dm"], tabs["qd"], tabs["kd"],
      tabs["cd"], proj, hb, w_hf, proj, proj, lb.reshape(RET_HEADS, 1, hw), norm_g.reshape(1, HG_DV), proj)


def _merge_kernel(a_ref, b_ref, c_ref, ga_ref, gb_ref, gc_ref, h_ref, wa_ref, wb_ref, wc_ref, wo_ref,
                  g_ref, beta_ref, ho_ref, aux_ref, *, tm, token_tiles):
    def branch(x_ref, gate_ref, w_ref):
        y = jnp.dot(x_ref[...], w_ref[...], preferred_element_type=F32)
        return jax.nn.sigmoid(gate_ref[...].astype(F32)) * y

    merged = branch(a_ref, ga_ref, wa_ref) + branch(b_ref, gb_ref, wb_ref) + branch(c_ref, gc_ref, wc_ref)
    mix = jnp.dot(merged.astype(BF16), wo_ref[...], preferred_element_type=F32)
    hn = _layer_norm(DN_ALPHA * h_ref[...] + mix, g_ref[...], beta_ref[...])
    ho_ref[...] = hn
    if token_tiles:
        for j in range(SUB):
            aux_ref[pl.ds(j, tm, stride=SUB), :] = hn[:, j * LANES:(j + 1) * LANES]
    else:
        aux_ref[...] = hn.astype(BF16)


def merge_project(ya, yb, yc, proj, h, wa, wb, wc, wo, g, beta, token_tiles, tm=512):
    T, D = h.shape
    tm = min(tm, T)
    row = lambda i: (i, 0)
    const = lambda i: (0, 0)
    wspec = pl.BlockSpec((D, D), const, pipeline_mode=pl.Buffered(1))
    if token_tiles:
        aux_spec = pl.BlockSpec((tm * SUB, LANES), row)
        aux_shape = jax.ShapeDtypeStruct((T * SUB, LANES), F32)
    else:
        aux_spec = pl.BlockSpec((tm, D), row)
        aux_shape = jax.ShapeDtypeStruct((T, D), BF16)
    kern = functools.partial(_merge_kernel, tm=tm, token_tiles=token_tiles)
    return pl.pallas_call(
        kern,
        grid=(T // tm,),
        in_specs=[pl.BlockSpec((tm, D), row), pl.BlockSpec((tm, D), row), pl.BlockSpec((tm, D), row),
                  pl.BlockSpec((tm, D), lambda i: (i, OFF_GA // D_MODEL)),
                  pl.BlockSpec((tm, D), lambda i: (i, OFF_GB // D_MODEL)),
                  pl.BlockSpec((tm, D), lambda i: (i, OFF_GC // D_MODEL)),
                  pl.BlockSpec((tm, D), row),
                  wspec, wspec, wspec, wspec,
                  pl.BlockSpec((1, D), const), pl.BlockSpec((1, D), const)],
        out_specs=[pl.BlockSpec((tm, D), row), aux_spec],
        out_shape=[jax.ShapeDtypeStruct((T, D), F32), aux_shape],
        compiler_params=_cparams(("parallel",)),
        name="merge_project",
    )(ya, yb, yc, proj, proj, proj, h, wa, wb, wc, wo, g.reshape(1, D), beta.reshape(1, D))


def _ffn_kernel(xb_ref, h_ref, wg_ref, wu_ref, wd_ref, g_ref, beta_ref, ho_ref, hob_ref):
    x = xb_ref[...]
    gate = jnp.dot(x, wg_ref[...], preferred_element_type=F32)
    up = jnp.dot(x, wu_ref[...], preferred_element_type=F32)
    a = (_silu(gate) * up).astype(BF16)
    ff = jnp.dot(a, wd_ref[...], preferred_element_type=F32)
    hn = _layer_norm(DN_ALPHA * h_ref[...] + ff, g_ref[...], beta_ref[...])
    ho_ref[...] = hn
    hob_ref[...] = hn.astype(BF16)


def dense_ffn(hb, h, wg, wu, wd, g, beta, tm=512):
    T, D = h.shape
    F = wg.shape[1]
    tm = min(tm, T)
    row = lambda i: (i, 0)
    const = lambda i: (0, 0)
    resident = pl.Buffered(1)
    return pl.pallas_call(
        _ffn_kernel,
        grid=(T // tm,),
        in_specs=[pl.BlockSpec((tm, D), row), pl.BlockSpec((tm, D), row),
                  pl.BlockSpec((D, F), const, pipeline_mode=resident),
                  pl.BlockSpec((D, F), const, pipeline_mode=resident),
                  pl.BlockSpec((F, D), const, pipeline_mode=resident),
                  pl.BlockSpec((1, D), const), pl.BlockSpec((1, D), const)],
        out_specs=[pl.BlockSpec((tm, D), row), pl.BlockSpec((tm, D), row)],
        out_shape=[jax.ShapeDtypeStruct((T, D), F32), jax.ShapeDtypeStruct((T, D), BF16)],
        compiler_params=_cparams(("parallel",)),
        name="dense_ffn",
    )(hb, h, wg, wu, wd, g.reshape(1, D), beta.reshape(1, D))


def _route_top2(h, w_hi, w_lo):
    h_hi = h.astype(BF16)
    h_lo = (h - h_hi.astype(F32)).astype(BF16)
    logits = (jnp.dot(h_hi, w_hi, preferred_element_type=F32) + jnp.dot(h_lo, w_hi, preferred_element_type=F32)
              + jnp.dot(h_hi, w_lo, preferred_element_type=F32))
    lane = lax.broadcasted_iota(jnp.int32, logits.shape, 1)
    l1 = jnp.where(lane < N_EXPERTS, logits, -jnp.inf)
    m1 = jnp.max(l1, -1, keepdims=True)
    i1 = jnp.min(jnp.where(l1 == m1, lane, LANES), -1, keepdims=True)
    l2 = jnp.where(lane == i1, -jnp.inf, l1)
    m2 = jnp.max(l2, -1, keepdims=True)
    i2 = jnp.min(jnp.where(l2 == m2, lane, LANES), -1, keepdims=True)
    e = jnp.exp(m2 - m1)
    w1 = 1.0 / (1.0 + e)
    w2 = e / (1.0 + e)
    return jnp.where(lane == 0, i1.astype(F32),
                     jnp.where(lane == 1, i2.astype(F32),
                               jnp.where(lane == 2, w1, jnp.where(lane == 3, w2, 0.0))))


def _router_kernel(h_ref, w_ref, o_ref):
    o_ref[...] = _route_top2(h_ref[...], w_ref[0], w_ref[1])


def router_top2(h, w_router, tm=512):
    T, D = h.shape
    tm = min(tm, T)
    w_pad = jnp.zeros((D, LANES), F32).at[:, :N_EXPERTS].set(w_router.astype(F32))
    w_hi = w_pad.astype(BF16)
    w_lo = (w_pad - w_hi.astype(F32)).astype(BF16)
    return pl.pallas_call(
        _router_kernel,
        grid=(T // tm,),
        in_specs=[pl.BlockSpec((tm, D), lambda i: (i, 0)), pl.BlockSpec((2, D, LANES), lambda i: (0, 0, 0))],
        out_specs=pl.BlockSpec((tm, LANES), lambda i: (i, 0)),
        out_shape=jax.ShapeDtypeStruct((T, LANES), F32),
        compiler_params=_cparams(("parallel",)),
        name="router_top2",
    )(h, jnp.stack([w_hi, w_lo]))


W_PIECES = 8


def _expert_ffn_kernel(te_ref, nt_ref, src_ref, dst_ref, h_hbm, wg_hbm, wu_hbm, wd_hbm, y_hbm,
                       xbuf, xb_ref, ybuf, y_ref, wg_s, wu_s, wd_s, stage_in, stage_out, gsem, ssem, wsem,
                       *, tm):
    i = pl.program_id(0)
    n_i = pl.num_programs(0)
    active = i < nt_ref[0]
    buf = i % 2
    nxt = jnp.minimum(i + 1, n_i - 1)
    prv = jnp.maximum(i - 1, 0)
    expert = te_ref[i]
    new_expert = jnp.logical_and(active, jnp.logical_or(i == 0, expert != te_ref[prv]))

    def gather_row(tile, b, r):
        tok = src_ref[tile * tm + r]
        pltpu.make_async_copy(h_hbm.at[pl.ds(pl.multiple_of(tok * SUB, SUB), SUB), :],
                              xbuf.at[b, pl.ds(pl.multiple_of(r * SUB, SUB), SUB), :], gsem.at[b]).start()

    def scatter_row(tile, r):
        slot = dst_ref[tile * tm + r]
        pltpu.make_async_copy(ybuf.at[pl.ds(pl.multiple_of(r * SUB, SUB), SUB), :],
                              y_hbm.at[pl.ds(pl.multiple_of(slot * SUB, SUB), SUB), :], ssem).start()

    def wait_gather(b):
        pltpu.make_async_copy(h_hbm.at[pl.ds(0, tm * SUB), :], xbuf.at[b], gsem.at[b]).wait()

    def wait_scatter():
        pltpu.make_async_copy(ybuf, y_hbm.at[pl.ds(0, tm * SUB), :], ssem).wait()

    def looped(n, fn):
        def body(r, carry):
            fn(r)
            return carry
        lax.fori_loop(0, n, body, 0, unroll=8)

    def share_row(r):
        gather_row(nxt, 1 - buf, r)
        scatter_row(prv, r)

    @pl.when(i == 0)
    def _():
        looped(tm, lambda r: gather_row(0, 0, r))
        ybuf[...] = jnp.zeros_like(ybuf)

    @pl.when(new_expert)
    def _():
        pieces = []
        for w_hbm, w_s, stage in ((wg_hbm, wg_s, stage_in), (wu_hbm, wu_s, stage_in), (wd_hbm, wd_s, stage_out)):
            rows = w_s.shape[0] // W_PIECES
            for c in range(W_PIECES):
                pieces.append((w_hbm.at[expert, pl.ds(c * rows, rows), :], w_s.at[pl.ds(c * rows, rows), :], stage))
        copies = [pltpu.make_async_copy(src, stage.at[k % 2], wsem.at[k % 2])
                  for k, (src, _, stage) in enumerate(pieces)]
        copies[0].start()
        for k, (_, dst, stage) in enumerate(pieces):
            if k + 1 < len(pieces):
                copies[k + 1].start()
            copies[k].wait()
            dst[...] = stage[k % 2].astype(BF16)

    wait_gather(buf)
    for j in range(SUB):
        xb_ref[:, j * LANES:(j + 1) * LANES] = xbuf[buf, pl.ds(j, tm, stride=SUB), :].astype(BF16)

    @pl.when(active)
    def _():
        for r in range(tm):
            share_row(r)
        x = xb_ref[...]
        gate = jnp.dot(x, wg_s[...], preferred_element_type=F32)
        up = jnp.dot(x, wu_s[...], preferred_element_type=F32)
        a = (_silu(gate) * up).astype(BF16)
        y_ref[...] = jnp.dot(a, wd_s[...], preferred_element_type=F32)

    @pl.when(jnp.logical_not(active))
    def _():
        looped(tm, share_row)
        y_ref[...] = jnp.zeros_like(y_ref)

    wait_scatter()
    for j in range(SUB):
        ybuf[pl.ds(j, tm, stride=SUB), :] = y_ref[:, j * LANES:(j + 1) * LANES]

    @pl.when(i == n_i - 1)
    def _():
        looped(tm, lambda r: scatter_row(i, r))
        wait_scatter()
        wait_gather(1 - buf)


def expert_ffn(h_tiles, te, nt, src_tok, dst_slot, wg, wu, wd, tm):
    P = src_tok.shape[0]
    D = wg.shape[1]
    F = wg.shape[2]
    kern = functools.partial(_expert_ffn_kernel, tm=tm)
    anyspec = pl.BlockSpec(memory_space=pl.ANY)
    return pl.pallas_call(
        kern,
        grid_spec=pltpu.PrefetchScalarGridSpec(
            num_scalar_prefetch=4,
            grid=(P // tm,),
            in_specs=[anyspec, anyspec, anyspec, anyspec],
            out_specs=anyspec,
            scratch_shapes=[pltpu.VMEM((2, tm * SUB, LANES), F32),
                            pltpu.VMEM((tm, D), BF16),
                            pltpu.VMEM((tm * SUB, LANES), F32),
                            pltpu.VMEM((tm, D), F32),
                            pltpu.VMEM((D, F), BF16),
                            pltpu.VMEM((D, F), BF16),
                            pltpu.VMEM((F, D), BF16),
                            pltpu.VMEM((2, D // W_PIECES, F), F32),
                            pltpu.VMEM((2, F // W_PIECES, D), F32),
                            pltpu.SemaphoreType.DMA((2,)),
                            pltpu.SemaphoreType.DMA(()),
                            pltpu.SemaphoreType.DMA((2,))],
        ),
        out_shape=jax.ShapeDtypeStruct((P * SUB, LANES), F32),
        compiler_params=_cparams(("arbitrary",)),
        name="expert_ffn",
    )(te, nt, src_tok, dst_slot, h_tiles, wg, wu, wd)


def _moe_combine_kernel(y0_ref, y1_ref, r_ref, h_ref, g_ref, beta_ref, ho_ref, hob_ref, *, tm):
    def rows(y_ref):
        return jnp.concatenate([y_ref[pl.ds(j, tm, stride=SUB), :] for j in range(SUB)], axis=1)

    r = r_ref[...]
    ff = rows(y0_ref) * r[:, 2:3] + rows(y1_ref) * r[:, 3:4]
    hn = _layer_norm(DN_ALPHA * h_ref[...] + ff, g_ref[...], beta_ref[...])
    ho_ref[...] = hn
    hob_ref[...] = hn.astype(BF16)


def moe_combine(y_tiles, route, h, g, beta, tm=512):
    T, D = h.shape
    tm = min(tm, T)
    nb = T // tm
    row = lambda i: (i, 0)
    const = lambda i: (0, 0)
    kern = functools.partial(_moe_combine_kernel, tm=tm)
    return pl.pallas_call(
        kern,
        grid=(nb,),
        in_specs=[pl.BlockSpec((tm * SUB, LANES), lambda i: (i, 0)),
                  pl.BlockSpec((tm * SUB, LANES), lambda i: (nb + i, 0)),
                  pl.BlockSpec((tm, LANES), row),
                  pl.BlockSpec((tm, D), row),
                  pl.BlockSpec((1, D), const), pl.BlockSpec((1, D), const)],
        out_specs=[pl.BlockSpec((tm, D), row), pl.BlockSpec((tm, D), row)],
        out_shape=[jax.ShapeDtypeStruct((T, D), F32), jax.ShapeDtypeStruct((T, D), BF16)],
        compiler_params=_cparams(("parallel",)),
        name="moe_combine",
    )(y_tiles, y_tiles, route, h, g.reshape(1, D), beta.reshape(1, D))


def moe_plan(route, T, tm):
    i32 = jnp.int32
    n_slots = TOP_K * T
    experts = jnp.arange(N_EXPERTS, dtype=i32)
    e_flat = route[:, 0:TOP_K].astype(i32).T.reshape(-1)
    order = jnp.argsort(e_flat, stable=True).astype(i32)
    sizes = jnp.sum((e_flat[:, None] == experts[None, :]).astype(i32), axis=0)
    padded = ((sizes + tm - 1) // tm) * tm
    gend = jnp.cumsum(padded)
    gstart = gend - padded
    cstart = jnp.cumsum(sizes) - sizes
    n_rows = n_slots + N_EXPERTS * tm
    row = jnp.arange(n_rows, dtype=i32)
    group = jnp.sum((row[:, None] >= gend[None, :]).astype(i32), axis=1)
    onehot = (jnp.minimum(group, N_EXPERTS - 1)[:, None] == experts[None, :]).astype(i32)
    look = lambda table: jnp.sum(onehot * table[None, :], axis=1)
    local = row - look(gstart)
    real = jnp.logical_and(group < N_EXPERTS, local < look(sizes))
    slot = order[jnp.clip(look(cstart) + local, 0, n_slots - 1)]
    pad_rank = jnp.where(group < N_EXPERTS, look(gstart - cstart) + local - look(sizes), row - n_slots)
    dst_slot = jnp.where(real, slot, n_slots + pad_rank)
    src_tok = jnp.where(real, slot % T, 0)
    n_tiles = n_rows // tm
    nt = (gend[-1] // tm).astype(i32)
    tile_start = jnp.arange(n_tiles, dtype=i32) * tm
    te = jnp.minimum(jnp.sum((tile_start[:, None] >= gend[None, :]).astype(i32), axis=1), N_EXPERTS - 1)
    te = jnp.where(jnp.arange(n_tiles) < nt, te, te[jnp.maximum(nt - 1, 0)])
    return src_tok.astype(i32), dst_slot.astype(i32), te.astype(i32), nt.reshape(1)


def moe_ffn(h, h_tiles, w_router, wg, wu, wd, g, beta, tm=512):
    T, D = h.shape
    route = router_top2(h, w_router)
    src_tok, dst_slot, te, nt = moe_plan(route, T, tm)
    y_tiles = expert_ffn(h_tiles, te, nt, src_tok, dst_slot, wg, wu, wd, tm)
    return moe_combine(y_tiles, route, h, g, beta)


def permute_w_in(w_in, layer):
    widths = [RET_QK, RET_QK, RET_V, RET_V, HG_K, HG_K, HG_V, HG_V, SWA_Q, SWA_KV, SWA_KV,
              D_MODEL, D_MODEL, D_MODEL]
    starts = np.concatenate([[0], np.cumsum(widths)])
    names = ["rq", "rk", "rv", "rg", "hq", "hf", "hi", "hg", "sq", "sk", "sv", "ga", "gb", "gc"]
    off = dict(zip(names, starts[:-1].tolist()))
    wid = dict(zip(names, widths))

    def cols(name, lo=0, n=None):
        a = off[name] + lo
        return w_in[layer, :, a:a + (wid[name] if n is None else n)]

    pieces = [cols(n) for n in ("rq", "rk", "rv", "rg", "hq", "hi", "hg", "ga", "gb", "gc")]
    qw = SWA_G * SWA_HD
    for kvh in range(SWA_HKV):
        k = cols("sk", kvh * SWA_HD, SWA_HD)
        v = cols("sv", kvh * SWA_HD, SWA_HD)
        pieces += [cols("sq", kvh * qw, qw) * SWA_HD ** -0.5, k, v, v, k]
    main = jnp.concatenate(pieces, axis=1).astype(BF16)
    return main, cols("hf").astype(BF16)


def kernel(x, ln_in_g, ln_in_b, w_in, ret_w_out, hgrn_lower_bounds, hgrn_norm_g, hgrn_w_out, swa_sinks,
           swa_w_out, w_o, ln_mix_g, ln_mix_b, ffn_w_gate, ffn_w_up, ffn_w_down, moe_router, moe_w_gate,
           moe_w_up, moe_w_down, ln_ffn_g, ln_ffn_b):
    B, S, D = x.shape
    T = B * S
    assert D == D_MODEL and S % RET_CHUNK == 0 and S % (2 * SWA_CHUNK) == 0 and S >= 4 * SWA_CHUNK

    lb_all = jnp.cumsum(jax.nn.softmax(hgrn_lower_bounds.astype(F32), axis=0), axis=0)
    lb_all = lb_all - lb_all[0]
    tabs = retention_tables(S)

    h, hb = ln_in(x.reshape(T, D), ln_in_g, ln_in_b)
    for layer in range(DEPTH):
        w_main, w_hf = permute_w_in(w_in, layer)
        proj = matmul(hb, w_main, BF16, tn=N_PROJ // 4, name="in_proj")
        ya, yb, yc = token_mixers(proj, hb, w_hf, tabs, lb_all[layer], hgrn_norm_g[layer], swa_sinks[layer],
                                  B, S)
        dense = layer % 2 == 0
        h, aux = merge_project(ya, yb, yc, proj, h,
                               ret_w_out[layer].astype(BF16), hgrn_w_out[layer].astype(BF16),
                               swa_w_out[layer].astype(BF16), w_o[layer].astype(BF16),
                               ln_mix_g[layer], ln_mix_b[layer], token_tiles=not dense)

        j = layer // 2
        if dense:
            h, hb = dense_ffn(aux, h, ffn_w_gate[j].astype(BF16), ffn_w_up[j].astype(BF16),
                              ffn_w_down[j].astype(BF16), ln_ffn_g[layer], ln_ffn_b[layer])
        else:
            h, hb = moe_ffn(h, aux, moe_router[j], moe_w_gate[j], moe_w_up[j], moe_w_down[j],
                            ln_ffn_g[layer], ln_ffn_b[layer])
    return h.reshape(B, S, D)
```

```python
import functools

import numpy as np
import jax
import jax.numpy as jnp
from jax import lax
from jax.experimental import pallas as pl
from jax.experimental.pallas import tpu as pltpu

F32 = jnp.float32
BF16 = jnp.bfloat16

D_MODEL = 1024
RET_HEADS, RET_DK, RET_DV = 4, 128, 256
RET_QK, RET_V = RET_HEADS * RET_DK, RET_HEADS * RET_DV
ROPE_BASE = 10000.0
HG_HEADS, HG_DK, HG_DV = 8, 128, 128
HG_K, HG_V = HG_HEADS * HG_DK, HG_HEADS * HG_DV
SWA_HQ, SWA_HKV, SWA_HD = 16, 4, 64
SWA_G = SWA_HQ // SWA_HKV
SWA_Q, SWA_KV = SWA_HQ * SWA_HD, SWA_HKV * SWA_HD
SWA_CHUNK = 64
SWA_WIN_CHUNKS = 2
N_EXPERTS, TOP_K = 8, 2
DEPTH = 2
LN_EPS, RMS_EPS = 1e-5, 1e-6
DN_ALPHA = (2.0 * DEPTH) ** 0.25

VMEM_LIMIT_BYTES = 56 * 1024 * 1024
LANES = 128
SUB = 8

RET_CHUNK = 128
RET_BLOCK = 512
HG_CHUNK = 32
HG_BLOCK = 256
MIX_SPAN = 1024

OFF_RQ, OFF_RK, OFF_RV, OFF_RG = 0, 512, 1024, 2048
OFF_HQ, OFF_HI, OFF_HG = 3072, 4096, 5120
OFF_GA, OFF_GB, OFF_GC = 6144, 7168, 8192
OFF_SWA = 9216
SWA_GROUP_W = SWA_G * SWA_HD + 4 * SWA_HD
N_PROJ = OFF_SWA + SWA_HKV * SWA_GROUP_W


def _cparams(sem, vmem=VMEM_LIMIT_BYTES):
    return pltpu.CompilerParams(dimension_semantics=sem, vmem_limit_bytes=vmem)


def _layer_norm(x, g, b):
    mu = jnp.mean(x, -1, keepdims=True)
    xc = x - mu
    var = jnp.mean(xc * xc, -1, keepdims=True)
    return xc * lax.rsqrt(var + LN_EPS) * g + b


def _silu(x):
    return x * jax.nn.sigmoid(x)


def _ln_in_kernel(x_ref, g_ref, b_ref, h_ref, hb_ref):
    h = _layer_norm(x_ref[...], g_ref[...], b_ref[...])
    h_ref[...] = h
    hb_ref[...] = h.astype(BF16)


def ln_in(x2, g, b, tm=512):
    T, D = x2.shape
    tm = min(tm, T)
    return pl.pallas_call(
        _ln_in_kernel,
        grid=(T // tm,),
        in_specs=[pl.BlockSpec((tm, D), lambda i: (i, 0)),
                  pl.BlockSpec((1, D), lambda i: (0, 0)),
                  pl.BlockSpec((1, D), lambda i: (0, 0))],
        out_specs=[pl.BlockSpec((tm, D), lambda i: (i, 0)),
                   pl.BlockSpec((tm, D), lambda i: (i, 0))],
        out_shape=[jax.ShapeDtypeStruct((T, D), F32), jax.ShapeDtypeStruct((T, D), BF16)],
        compiler_params=_cparams(("parallel",)),
        name="ln_in",
    )(x2, g.reshape(1, D), b.reshape(1, D))


def _matmul_kernel(x_ref, w_ref, o_ref, wb_ref):
    @pl.when(pl.program_id(1) == 0)
    def _():
        wb_ref[...] = w_ref[...].astype(BF16)

    o_ref[...] = jnp.dot(x_ref[...], wb_ref[...], preferred_element_type=F32).astype(o_ref.dtype)


def matmul(x, w, out_dtype, tm=1024, tn=512, name="matmul"):
    T, K = x.shape
    N = w.shape[1]
    tm = min(tm, T)
    return pl.pallas_call(
        _matmul_kernel,
        grid=(N // tn, T // tm),
        in_specs=[pl.BlockSpec((tm, K), lambda j, i: (i, 0)),
                  pl.BlockSpec((K, tn), lambda j, i: (0, j))],
        out_specs=pl.BlockSpec((tm, tn), lambda j, i: (i, j)),
        out_shape=jax.ShapeDtypeStruct((T, N), out_dtype),
        scratch_shapes=[pltpu.VMEM((K, tn), BF16)],
        compiler_params=_cparams(("arbitrary", "arbitrary")),
        name=name,
    )(x, w)


def _retention_block(q_ref, k_ref, v_ref, g_ref, cos_ref, sin_ref, tabs, o_ref, st_ref, start, *, chunk, block):
    dm, qd, kd, cd = tabs
    nc = block // chunk
    nt_dims = (((1,), (1,)), ((), ()))
    tn_dims = (((0,), (0,)), ((), ()))
    r = pl.ds(pl.multiple_of(start, block), block)
    cos = cos_ref[r, :]
    sin = sin_ref[r, :]
    q = q_ref[r, :].astype(F32)
    k = k_ref[r, :].astype(F32)
    qr = q * cos + pltpu.roll(q, RET_DK // 2, 1) * sin
    kr = k * cos + pltpu.roll(k, RET_DK // 2, 1) * sin
    v = v_ref[r, :]
    rows = [slice(c * chunk, (c + 1) * chunk) for c in range(nc)]
    upd = [lax.dot_general((kr[rc] * kd).astype(BF16), v[rc], tn_dims, preferred_element_type=F32)
           for rc in rows]
    att = [lax.dot_general(qr[rc].astype(BF16), kr[rc].astype(BF16), nt_dims,
                           preferred_element_type=F32) * dm for rc in rows]
    intra = [jnp.dot(a.astype(BF16), v[rc], preferred_element_type=F32) for a, rc in zip(att, rows)]
    q_in = [(qr[rc] * qd).astype(BF16) for rc in rows]
    st = st_ref[...]
    outs = []
    for c in range(nc):
        outs.append(intra[c] + jnp.dot(q_in[c], st.astype(BF16), preferred_element_type=F32))
        st = st * cd + upd[c]
    st_ref[...] = st
    o = jnp.concatenate(outs, axis=0)
    on = o * lax.rsqrt(jnp.mean(o * o, -1, keepdims=True) + RMS_EPS)
    g = g_ref[r, :].astype(F32)
    o_ref[r, :] = (_silu(g) * on).astype(o_ref.dtype)


def retention_tables(S):
    C = RET_CHUNK
    half = RET_DK // 2
    pos = jnp.arange(S, dtype=F32)
    inv = 1.0 / (ROPE_BASE ** jnp.linspace(0.0, 1.0, half, dtype=F32))
    ang = pos[:, None] * inv[None, :]
    cos, sin = jnp.cos(ang), jnp.sin(ang)
    log_gamma = jnp.log(1.0 - 2.0 ** (-5.0 - jnp.arange(RET_HEADS, dtype=F32)))
    idx = jnp.arange(C, dtype=F32)
    diff = idx[:, None] - idx[None, :]
    decay = jnp.where(diff[None] >= 0,
                      jnp.exp(jnp.maximum(diff, 0.0)[None] * log_gamma[:, None, None]), 0.0)
    scale = RET_DK ** -0.5
    qd = jnp.exp((idx + 1.0)[None, :] * log_gamma[:, None])
    kd = jnp.exp((C - 1.0 - idx)[None, :] * log_gamma[:, None]) * scale
    cd = jnp.exp(C * log_gamma)
    return {
        "cos": jnp.concatenate([cos, cos], axis=1),
        "sin": jnp.concatenate([-sin, sin], axis=1),
        "dm": decay * scale,
        "qd": jnp.broadcast_to(qd[:, :, None], (RET_HEADS, C, RET_DK)),
        "kd": jnp.broadcast_to(kd[:, :, None], (RET_HEADS, C, RET_DK)),
        "cd": jnp.broadcast_to(cd[:, None, None], (RET_HEADS, 1, RET_DV)),
    }


def _hgrn_block(q_ref, z_ref, i_ref, g_ref, lb, ng, o_ref, st_ref, start, lanes, consts, *, chunk, block):
    row, causal = consts
    half = chunk // 2
    nc = block // chunk
    nt_dims = (((1,), (1,)), ((), ()))
    tn_dims = (((0,), (0,)), ((), ()))

    def chunk_rows(x, lo):
        return jnp.concatenate([x[c * chunk + lo:c * chunk + lo + 1, :] for c in range(nc)], axis=0)

    def spread(xc):
        return jnp.concatenate(
            [jnp.broadcast_to(xc[c:c + 1, :], (chunk, xc.shape[1])) for c in range(nc)], axis=0)

    r = pl.ds(pl.multiple_of(start, block), block)
    f = lb + (1.0 - lb) * jax.nn.sigmoid(z_ref[r, lanes])
    cum = jnp.log(f)
    shift = 1
    while shift < chunk:
        cum = cum + jnp.where(row >= shift, pltpu.roll(cum, shift, 0), 0.0)
        shift *= 2
    mid_c = chunk_rows(cum, half - 1)
    last_c = chunk_rows(cum, chunk - 1)
    mid = spread(mid_c)
    qh = q_ref[r, lanes].astype(F32) * jnp.exp(cum - mid)
    kh = (1.0 - f) * jnp.exp(mid - cum)
    q_in = (qh * spread(jnp.exp(mid_c))).astype(BF16)
    k_st = (kh * spread(jnp.exp(last_c - mid_c))).astype(BF16)
    dec = jnp.exp(last_c)
    qh = qh.astype(BF16)
    kh = kh.astype(BF16)
    v = i_ref[r, lanes]
    rows = [slice(c * chunk, (c + 1) * chunk) for c in range(nc)]
    upd = [lax.dot_general(v[rc], k_st[rc], tn_dims, preferred_element_type=F32) for rc in rows]
    att = [lax.dot_general(qh[rc], kh[rc], nt_dims, preferred_element_type=F32) for rc in rows]
    intra = [jnp.dot(jnp.where(causal, a, 0.0).astype(BF16), v[rc], preferred_element_type=F32)
             for a, rc in zip(att, rows)]
    st = st_ref[...]
    outs = []
    for c, rc in enumerate(rows):
        inter = lax.dot_general(q_in[rc], st.astype(BF16), nt_dims, preferred_element_type=F32)
        outs.append(intra[c] + inter)
        st = st * dec[c:c + 1, :] + upd[c]
    st_ref[...] = st
    o = jnp.concatenate(outs, axis=0)
    on = o * lax.rsqrt(jnp.mean(o * o, -1, keepdims=True) + RMS_EPS) * ng
    g = g_ref[r, lanes].astype(F32)
    o_ref[r, lanes] = (_silu(g) * on).astype(o_ref.dtype)


def _swa_step(sink_ref, x_ref, o_ref, j, kvh, consts):
    lo_kv, lo_o, rel, zero = consts
    CH = SWA_CHUNK
    HD = SWA_HD
    QR = 2 * CH
    KW = 4 * CH
    QW = SWA_G * HD
    nt_dims = (((1,), (1,)), ((), ()))
    first = jnp.maximum(2 * j - SWA_WIN_CHUNKS, 0)
    rq = pl.ds(pl.multiple_of(j * QR, QR), QR)
    rk = pl.ds(pl.multiple_of(first * CH, CH), KW)
    kv = x_ref[rk, QW:QW + 2 * HD]
    vk = x_ref[rk, QW + 2 * HD:QW + 4 * HD]
    kbd = jnp.concatenate([jnp.where(lo_kv, kv, zero), jnp.where(lo_kv, zero, vk)], axis=0)
    vbd = jnp.concatenate([jnp.where(lo_kv, vk, zero), jnp.where(lo_kv, zero, kv)], axis=0)
    valid = lax.bitcast_convert_type(rel + (first - 2 * j + SWA_WIN_CHUNKS), jnp.uint32) <= SWA_WIN_CHUNKS
    scores = [lax.dot_general(x_ref[rq, p * 2 * HD:(p + 1) * 2 * HD], kbd, nt_dims,
                              preferred_element_type=F32) for p in range(SWA_G // 2)]
    for p, s in enumerate(scores):
        s = jnp.where(valid, s, -jnp.inf)
        probs, inv = [], []
        for hh in range(2):
            sink = sink_ref[kvh * SWA_G + 2 * p + hh]
            sh = s[:, hh * KW:(hh + 1) * KW]
            m = jnp.maximum(jnp.max(sh, -1, keepdims=True), sink)
            e = jnp.exp(sh - m)
            den = jnp.sum(e, -1, keepdims=True) + jnp.exp(sink - m)
            probs.append(e.astype(BF16))
            inv.append(1.0 / den)
        o = jnp.dot(jnp.concatenate(probs, axis=1), vbd, preferred_element_type=F32)
        o = o * jnp.where(lo_o, inv[0], inv[1])
        o_ref[rq, p * 2 * HD:(p + 1) * 2 * HD] = o.astype(o_ref.dtype)


def _mixers_kernel(sink_ref, rq_ref, rk_ref, rv_ref, rg_ref, cos_ref, sin_ref, dm_ref, qd_ref, kd_ref, cd_ref,
                   hq_ref, xb_ref, wf_ref, hi_ref, hg_ref, lb_ref, ng_ref, sx_ref, ya_ref, yb_ref, yc_ref,
                   st_r, st_h, hz_ref, *, span, nspans):
    kvh = pl.program_id(1)
    st_r[...] = jnp.zeros_like(st_r)
    st_h[...] = jnp.zeros_like(st_h)
    hz_ref[...] = jnp.dot(xb_ref[...], wf_ref[...], preferred_element_type=F32)
    ret_tabs = (dm_ref[0], qd_ref[0], kd_ref[0], cd_ref[0])
    lb2 = lb_ref[0]
    ng = ng_ref[...]
    hg_consts = (lax.broadcasted_iota(jnp.int32, (HG_BLOCK, HG_DK), 0) % HG_CHUNK,
                 lax.broadcasted_iota(jnp.int32, (HG_CHUNK, HG_CHUNK), 0)
                 >= lax.broadcasted_iota(jnp.int32, (HG_CHUNK, HG_CHUNK), 1))
    QR, KW, HD = 2 * SWA_CHUNK, 4 * SWA_CHUNK, SWA_HD
    kcol = lax.broadcasted_iota(jnp.int32, (QR, 2 * KW), 1)
    qrow = lax.broadcasted_iota(jnp.int32, (QR, 2 * KW), 0)
    swa_consts = (lax.broadcasted_iota(jnp.int32, (KW, 2 * HD), 1) < HD,
                  lax.broadcasted_iota(jnp.int32, (QR, 2 * HD), 1) < HD,
                  (kcol % KW) // SWA_CHUNK - qrow // SWA_CHUNK,
                  jnp.zeros((KW, 2 * HD), BF16))
    n_swa = span // QR
    n_hg = span // HG_BLOCK
    n_ret = span // RET_BLOCK

    def body(u, carry):
        base = u * span
        for t in range(n_swa):
            _swa_step(sink_ref, sx_ref, yc_ref, u * n_swa + t, kvh, swa_consts)
            if t < 2 * n_hg:
                hh, blk = t % 2, t // 2
                lanes = slice(hh * HG_DK, (hh + 1) * HG_DK)
                _hgrn_block(hq_ref, hz_ref, hi_ref, hg_ref, lb2[:, lanes], ng, yb_ref, st_h.at[hh],
                            base + blk * HG_BLOCK, lanes, hg_consts, chunk=HG_CHUNK, block=HG_BLOCK)
            if t % (n_swa // n_ret) == 0:
                _retention_block(rq_ref, rk_ref, rv_ref, rg_ref, cos_ref, sin_ref, ret_tabs, ya_ref, st_r,
                                 base + (t // (n_swa // n_ret)) * RET_BLOCK, chunk=RET_CHUNK, block=RET_BLOCK)
        return carry

    lax.fori_loop(0, nspans, body, 0)


def token_mixers(proj, hb, w_hf, tabs, lb, norm_g, sinks, B, S):
    T = B * S
    D = hb.shape[1]
    span = min(MIX_SPAN, S)
    C = RET_CHUNK
    hw = 2 * HG_DK
    ret = lambda off, w: (lambda b, h: (b, off // w + h))
    kern = functools.partial(_mixers_kernel, span=span, nspans=S // span)
    out = jax.ShapeDtypeStruct((T, D_MODEL), BF16)
    outspec = pl.BlockSpec((S, RET_DV), lambda b, h: (b, h))
    return pl.pallas_call(
        kern,
        grid=(B, RET_HEADS),
        in_specs=[pl.BlockSpec(memory_space=pltpu.SMEM),
                  pl.BlockSpec((S, RET_DK), ret(OFF_RQ, RET_DK)),
                  pl.BlockSpec((S, RET_DK), ret(OFF_RK, RET_DK)),
                  pl.BlockSpec((S, RET_DV), ret(OFF_RV, RET_DV)),
                  pl.BlockSpec((S, RET_DV), ret(OFF_RG, RET_DV)),
                  pl.BlockSpec((S, RET_DK), lambda b, h: (0, 0)),
                  pl.BlockSpec((S, RET_DK), lambda b, h: (0, 0)),
                  pl.BlockSpec((1, C, C), lambda b, h: (h, 0, 0)),
                  pl.BlockSpec((1, C, RET_DK), lambda b, h: (h, 0, 0)),
                  pl.BlockSpec((1, C, RET_DK), lambda b, h: (h, 0, 0)),
                  pl.BlockSpec((1, 1, RET_DV), lambda b, h: (h, 0, 0)),
                  pl.BlockSpec((S, hw), ret(OFF_HQ, hw)),
                  pl.BlockSpec((S, D), lambda b, h: (b, 0)),
                  pl.BlockSpec((D, hw), lambda b, h: (0, h)),
                  pl.BlockSpec((S, hw), ret(OFF_HI, hw)),
                  pl.BlockSpec((S, hw), ret(OFF_HG, hw)),
                  pl.BlockSpec((1, 1, hw), lambda b, h: (h, 0, 0)),
                  pl.BlockSpec((1, HG_DV), lambda b, h: (0, 0)),
                  pl.BlockSpec((S, SWA_GROUP_W), ret(OFF_SWA, SWA_GROUP_W))],
        out_specs=[outspec, outspec, outspec],
        out_shape=[out, out, out],
        scratch_shapes=[pltpu.VMEM((RET_DK, RET_DV), F32), pltpu.VMEM((2, HG_DV, HG_DK), F32),
                        pltpu.VMEM((S, hw), F32)],
        compiler_params=_cparams(("parallel", "parallel")),
        name="token_mixers",
    )(sinks.astype(F32), proj, proj, proj, proj, tabs["cos"], tabs["sin"], tabs["dm"], tabs["qd"], tabs["kd"],
      tabs["cd"], proj, hb, w_hf, proj, proj, lb.reshape(RET_HEADS, 1, hw), norm_g.reshape(1, HG_DV), proj)


def _merge_kernel(a_ref, b_ref, c_ref, ga_ref, gb_ref, gc_ref, h_ref, wa_ref, wb_ref, wc_ref, wo_ref,
                  g_ref, beta_ref, ho_ref, aux_ref, *, tm, token_tiles):
    def branch(x_ref, gate_ref, w_ref):
        y = jnp.dot(x_ref[...], w_ref[...], preferred_element_type=F32)
        return jax.nn.sigmoid(gate_ref[...].astype(F32)) * y

    merged = branch(a_ref, ga_ref, wa_ref) + branch(b_ref, gb_ref, wb_ref) + branch(c_ref, gc_ref, wc_ref)
    mix = jnp.dot(merged.astype(BF16), wo_ref[...], preferred_element_type=F32)
    hn = _layer_norm(DN_ALPHA * h_ref[...] + mix, g_ref[...], beta_ref[...])
    ho_ref[...] = hn
    if token_tiles:
        for j in range(SUB):
            aux_ref[pl.ds(j, tm, stride=SUB), :] = hn[:, j * LANES:(j + 1) * LANES]
    else:
        aux_ref[...] = hn.astype(BF16)


def merge_project(ya, yb, yc, proj, h, wa, wb, wc, wo, g, beta, token_tiles, tm=512):
    T, D = h.shape
    tm = min(tm, T)
    row = lambda i: (i, 0)
    const = lambda i: (0, 0)
    wspec = pl.BlockSpec((D, D), const, pipeline_mode=pl.Buffered(1))
    if token_tiles:
        aux_spec = pl.BlockSpec((tm * SUB, LANES), row)
        aux_shape = jax.ShapeDtypeStruct((T * SUB, LANES), F32)
    else:
        aux_spec = pl.BlockSpec((tm, D), row)
        aux_shape = jax.ShapeDtypeStruct((T, D), BF16)
    kern = functools.partial(_merge_kernel, tm=tm, token_tiles=token_tiles)
    return pl.pallas_call(
        kern,
        grid=(T // tm,),
        in_specs=[pl.BlockSpec((tm, D), row), pl.BlockSpec((tm, D), row), pl.BlockSpec((tm, D), row),
                  pl.BlockSpec((tm, D), lambda i: (i, OFF_GA // D_MODEL)),
                  pl.BlockSpec((tm, D), lambda i: (i, OFF_GB // D_MODEL)),
                  pl.BlockSpec((tm, D), lambda i: (i, OFF_GC // D_MODEL)),
                  pl.BlockSpec((tm, D), row),
                  wspec, wspec, wspec, wspec,
                  pl.BlockSpec((1, D), const), pl.BlockSpec((1, D), const)],
        out_specs=[pl.BlockSpec((tm, D), row), aux_spec],
        out_shape=[jax.ShapeDtypeStruct((T, D), F32), aux_shape],
        compiler_params=_cparams(("parallel",)),
        name="merge_project",
    )(ya, yb, yc, proj, proj, proj, h, wa, wb, wc, wo, g.reshape(1, D), beta.reshape(1, D))


def _ffn_kernel(xb_ref, h_ref, wg_ref, wu_ref, wd_ref, g_ref, beta_ref, ho_ref, hob_ref):
    x = xb_ref[...]
    gate = jnp.dot(x, wg_ref[...], preferred_element_type=F32)
    up = jnp.dot(x, wu_ref[...], preferred_element_type=F32)
    a = (_silu(gate) * up).astype(BF16)
    ff = jnp.dot(a, wd_ref[...], preferred_element_type=F32)
    hn = _layer_norm(DN_ALPHA * h_ref[...] + ff, g_ref[...], beta_ref[...])
    ho_ref[...] = hn
    hob_ref[...] = hn.astype(BF16)


def dense_ffn(hb, h, wg, wu, wd, g, beta, tm=512):
    T, D = h.shape
    F = wg.shape[1]
    tm = min(tm, T)
    row = lambda i: (i, 0)
    const = lambda i: (0, 0)
    resident = pl.Buffered(1)
    return pl.pallas_call(
        _ffn_kernel,
        grid=(T // tm,),
        in_specs=[pl.BlockSpec((tm, D), row), pl.BlockSpec((tm, D), row),
                  pl.BlockSpec((D, F), const, pipeline_mode=resident),
                  pl.BlockSpec((D, F), const, pipeline_mode=resident),
                  pl.BlockSpec((F, D), const, pipeline_mode=resident),
                  pl.BlockSpec((1, D), const), pl.BlockSpec((1, D), const)],
        out_specs=[pl.BlockSpec((tm, D), row), pl.BlockSpec((tm, D), row)],
        out_shape=[jax.ShapeDtypeStruct((T, D), F32), jax.ShapeDtypeStruct((T, D), BF16)],
        compiler_params=_cparams(("parallel",)),
        name="dense_ffn",
    )(hb, h, wg, wu, wd, g.reshape(1, D), beta.reshape(1, D))


def _route_top2(h, w_hi, w_lo):
    h_hi = h.astype(BF16)
    h_lo = (h - h_hi.astype(F32)).astype(BF16)
    logits = (jnp.dot(h_hi, w_hi, preferred_element_type=F32) + jnp.dot(h_lo, w_hi, preferred_element_type=F32)
              + jnp.dot(h_hi, w_lo, preferred_element_type=F32))
    lane = lax.broadcasted_iota(jnp.int32, logits.shape, 1)
    l1 = jnp.where(lane < N_EXPERTS, logits, -jnp.inf)
    m1 = jnp.max(l1, -1, keepdims=True)
    i1 = jnp.min(jnp.where(l1 == m1, lane, LANES), -1, keepdims=True)
    l2 = jnp.where(lane == i1, -jnp.inf, l1)
    m2 = jnp.max(l2, -1, keepdims=True)
    i2 = jnp.min(jnp.where(l2 == m2, lane, LANES), -1, keepdims=True)
    e = jnp.exp(m2 - m1)
    w1 = 1.0 / (1.0 + e)
    w2 = e / (1.0 + e)
    return jnp.where(lane == 0, i1.astype(F32),
                     jnp.where(lane == 1, i2.astype(F32),
                               jnp.where(lane == 2, w1, jnp.where(lane == 3, w2, 0.0))))


def _router_kernel(h_ref, w_ref, o_ref):
    o_ref[...] = _route_top2(h_ref[...], w_ref[0], w_ref[1])


def router_top2(h, w_router, tm=512):
    T, D = h.shape
    tm = min(tm, T)
    w_pad = jnp.zeros((D, LANES), F32).at[:, :N_EXPERTS].set(w_router.astype(F32))
    w_hi = w_pad.astype(BF16)
    w_lo = (w_pad - w_hi.astype(F32)).astype(BF16)
    return pl.pallas_call(
        _router_kernel,
        grid=(T // tm,),
        in_specs=[pl.BlockSpec((tm, D), lambda i: (i, 0)), pl.BlockSpec((2, D, LANES), lambda i: (0, 0, 0))],
        out_specs=pl.BlockSpec((tm, LANES), lambda i: (i, 0)),
        out_shape=jax.ShapeDtypeStruct((T, LANES), F32),
        compiler_params=_cparams(("parallel",)),
        name="router_top2",
    )(h, jnp.stack([w_hi, w_lo]))


W_PIECES = 8


def _expert_ffn_kernel(te_ref, nt_ref, src_ref, dst_ref, h_hbm, wg_hbm, wu_hbm, wd_hbm, y_hbm,
                       xbuf, xb_ref, ybuf, y_ref, wg_s, wu_s, wd_s, stage_in, stage_out, gsem, ssem, wsem,
                       *, tm):
    i = pl.program_id(0)
    n_i = pl.num_programs(0)
    active = i < nt_ref[0]
    buf = i % 2
    nxt = jnp.minimum(i + 1, n_i - 1)
    prv = jnp.maximum(i - 1, 0)
    expert = te_ref[i]
    new_expert = jnp.logical_and(active, jnp.logical_or(i == 0, expert != te_ref[prv]))

    def gather_row(tile, b, r):
        tok = src_ref[tile * tm + r]
        pltpu.make_async_copy(h_hbm.at[pl.ds(pl.multiple_of(tok * SUB, SUB), SUB), :],
                              xbuf.at[b, pl.ds(pl.multiple_of(r * SUB, SUB), SUB), :], gsem.at[b]).start()

    def scatter_row(tile, r):
        slot = dst_ref[tile * tm + r]
        pltpu.make_async_copy(ybuf.at[pl.ds(pl.multiple_of(r * SUB, SUB), SUB), :],
                              y_hbm.at[pl.ds(pl.multiple_of(slot * SUB, SUB), SUB), :], ssem).start()

    def wait_gather(b):
        pltpu.make_async_copy(h_hbm.at[pl.ds(0, tm * SUB), :], xbuf.at[b], gsem.at[b]).wait()

    def wait_scatter():
        pltpu.make_async_copy(ybuf, y_hbm.at[pl.ds(0, tm * SUB), :], ssem).wait()

    def looped(n, fn):
        def body(r, carry):
            fn(r)
            return carry
        lax.fori_loop(0, n, body, 0, unroll=8)

    def share_row(r):
        gather_row(nxt, 1 - buf, r)
        scatter_row(prv, r)

    @pl.when(i == 0)
    def _():
        looped(tm, lambda r: gather_row(0, 0, r))
        ybuf[...] = jnp.zeros_like(ybuf)

    @pl.when(new_expert)
    def _():
        pieces = []
        for w_hbm, w_s, stage in ((wg_hbm, wg_s, stage_in), (wu_hbm, wu_s, stage_in), (wd_hbm, wd_s, stage_out)):
            rows = w_s.shape[0] // W_PIECES
            for c in range(W_PIECES):
                pieces.append((w_hbm.at[expert, pl.ds(c * rows, rows), :], w_s.at[pl.ds(c * rows, rows), :], stage))
        copies = [pltpu.make_async_copy(src, stage.at[k % 2], wsem.at[k % 2])
                  for k, (src, _, stage) in enumerate(pieces)]
        copies[0].start()
        for k, (_, dst, stage) in enumerate(pieces):
            if k + 1 < len(pieces):
                copies[k + 1].start()
            copies[k].wait()
            dst[...] = stage[k % 2].astype(BF16)

    wait_gather(buf)
    for j in range(SUB):
        xb_ref[:, j * LANES:(j + 1) * LANES] = xbuf[buf, pl.ds(j, tm, stride=SUB), :].astype(BF16)

    @pl.when(active)
    def _():
        for r in range(tm):
            share_row(r)
        x = xb_ref[...]
        gate = jnp.dot(x, wg_s[...], preferred_element_type=F32)
        up = jnp.dot(x, wu_s[...], preferred_element_type=F32)
        a = (_silu(gate) * up).astype(BF16)
        y_ref[...] = jnp.dot(a, wd_s[...], preferred_element_type=F32)

    @pl.when(jnp.logical_not(active))
    def _():
        looped(tm, share_row)
        y_ref[...] = jnp.zeros_like(y_ref)

    wait_scatter()
    for j in range(SUB):
        ybuf[pl.ds(j, tm, stride=SUB), :] = y_ref[:, j * LANES:(j + 1) * LANES]

    @pl.when(i == n_i - 1)
    def _():
        looped(tm, lambda r: scatter_row(i, r))
        wait_scatter()
        wait_gather(1 - buf)


def expert_ffn(h_tiles, te, nt, src_tok, dst_slot, wg, wu, wd, tm):
    P = src_tok.shape[0]
    D = wg.shape[1]
    F = wg.shape[2]
    kern = functools.partial(_expert_ffn_kernel, tm=tm)
    anyspec = pl.BlockSpec(memory_space=pl.ANY)
    return pl.pallas_call(
        kern,
        grid_spec=pltpu.PrefetchScalarGridSpec(
            num_scalar_prefetch=4,
            grid=(P // tm,),
            in_specs=[anyspec, anyspec, anyspec, anyspec],
            out_specs=anyspec,
            scratch_shapes=[pltpu.VMEM((2, tm * SUB, LANES), F32),
                            pltpu.VMEM((tm, D), BF16),
                            pltpu.VMEM((tm * SUB, LANES), F32),
                            pltpu.VMEM((tm, D), F32),
                            pltpu.VMEM((D, F), BF16),
                            pltpu.VMEM((D, F), BF16),
                            pltpu.VMEM((F, D), BF16),
                            pltpu.VMEM((2, D // W_PIECES, F), F32),
                            pltpu.VMEM((2, F // W_PIECES, D), F32),
                            pltpu.SemaphoreType.DMA((2,)),
                            pltpu.SemaphoreType.DMA(()),
                            pltpu.SemaphoreType.DMA((2,))],
        ),
        out_shape=jax.ShapeDtypeStruct((P * SUB, LANES), F32),
        compiler_params=_cparams(("arbitrary",)),
        name="expert_ffn",
    )(te, nt, src_tok, dst_slot, h_tiles, wg, wu, wd)


def _moe_combine_kernel(y0_ref, y1_ref, r_ref, h_ref, g_ref, beta_ref, ho_ref, hob_ref, *, tm):
    def rows(y_ref):
        return jnp.concatenate([y_ref[pl.ds(j, tm, stride=SUB), :] for j in range(SUB)], axis=1)

    r = r_ref[...]
    ff = rows(y0_ref) * r[:, 2:3] + rows(y1_ref) * r[:, 3:4]
    hn = _layer_norm(DN_ALPHA * h_ref[...] + ff, g_ref[...], beta_ref[...])
    ho_ref[...] = hn
    hob_ref[...] = hn.astype(BF16)


def moe_combine(y_tiles, route, h, g, beta, tm=512):
    T, D = h.shape
    tm = min(tm, T)
    nb = T // tm
    row = lambda i: (i, 0)
    const = lambda i: (0, 0)
    kern = functools.partial(_moe_combine_kernel, tm=tm)
    return pl.pallas_call(
        kern,
        grid=(nb,),
        in_specs=[pl.BlockSpec((tm * SUB, LANES), lambda i: (i, 0)),
                  pl.BlockSpec((tm * SUB, LANES), lambda i: (nb + i, 0)),
                  pl.BlockSpec((tm, LANES), row),
                  pl.BlockSpec((tm, D), row),
                  pl.BlockSpec((1, D), const), pl.BlockSpec((1, D), const)],
        out_specs=[pl.BlockSpec((tm, D), row), pl.BlockSpec((tm, D), row)],
        out_shape=[jax.ShapeDtypeStruct((T, D), F32), jax.ShapeDtypeStruct((T, D), BF16)],
        compiler_params=_cparams(("parallel",)),
        name="moe_combine",
    )(y_tiles, y_tiles, route, h, g.reshape(1, D), beta.reshape(1, D))


def moe_plan(route, T, tm):
    i32 = jnp.int32
    n_slots = TOP_K * T
    experts = jnp.arange(N_EXPERTS, dtype=i32)
    e_flat = route[:, 0:TOP_K].astype(i32).T.reshape(-1)
    order = jnp.argsort(e_flat, stable=True).astype(i32)
    sizes = jnp.sum((e_flat[:, None] == experts[None, :]).astype(i32), axis=0)
    padded = ((sizes + tm - 1) // tm) * tm
    gend = jnp.cumsum(padded)
    gstart = gend - padded
    cstart = jnp.cumsum(sizes) - sizes
    n_rows = n_slots + N_EXPERTS * tm
    row = jnp.arange(n_rows, dtype=i32)
    group = jnp.sum((row[:, None] >= gend[None, :]).astype(i32), axis=1)
    onehot = (jnp.minimum(group, N_EXPERTS - 1)[:, None] == experts[None, :]).astype(i32)
    look = lambda table: jnp.sum(onehot * table[None, :], axis=1)
    local = row - look(gstart)
    real = jnp.logical_and(group < N_EXPERTS, local < look(sizes))
    slot = order[jnp.clip(look(cstart) + local, 0, n_slots - 1)]
    pad_rank = jnp.where(group < N_EXPERTS, look(gstart - cstart) + local - look(sizes), row - n_slots)
    dst_slot = jnp.where(real, slot, n_slots + pad_rank)
    src_tok = jnp.where(real, slot % T, 0)
    n_tiles = n_rows // tm
    nt = (gend[-1] // tm).astype(i32)
    tile_start = jnp.arange(n_tiles, dtype=i32) * tm
    te = jnp.minimum(jnp.sum((tile_start[:, None] >= gend[None, :]).astype(i32), axis=1), N_EXPERTS - 1)
    te = jnp.where(jnp.arange(n_tiles) < nt, te, te[jnp.maximum(nt - 1, 0)])
    return src_tok.astype(i32), dst_slot.astype(i32), te.astype(i32), nt.reshape(1)


def moe_ffn(h, h_tiles, w_router, wg, wu, wd, g, beta, tm=512):
    T, D = h.shape
    route = router_top2(h, w_router)
    src_tok, dst_slot, te, nt = moe_plan(route, T, tm)
    y_tiles = expert_ffn(h_tiles, te, nt, src_tok, dst_slot, wg, wu, wd, tm)
    return moe_combine(y_tiles, route, h, g, beta)


def permute_w_in(w_in, layer):
    widths = [RET_QK, RET_QK, RET_V, RET_V, HG_K, HG_K, HG_V, HG_V, SWA_Q, SWA_KV, SWA_KV,
              D_MODEL, D_MODEL, D_MODEL]
    starts = np.concatenate([[0], np.cumsum(widths)])
    names = ["rq", "rk", "rv", "rg", "hq", "hf", "hi", "hg", "sq", "sk", "sv", "ga", "gb", "gc"]
    off = dict(zip(names, starts[:-1].tolist()))
    wid = dict(zip(names, widths))

    def cols(name, lo=0, n=None):
        a = off[name] + lo
        return w_in[layer, :, a:a + (wid[name] if n is None else n)]

    pieces = [cols(n) for n in ("rq", "rk", "rv", "rg", "hq", "hi", "hg", "ga", "gb", "gc")]
    qw = SWA_G * SWA_HD
    for kvh in range(SWA_HKV):
        k = cols("sk", kvh * SWA_HD, SWA_HD)
        v = cols("sv", kvh * SWA_HD, SWA_HD)
        pieces += [cols("sq", kvh * qw, qw) * SWA_HD ** -0.5, k, v, v, k]
    return jnp.concatenate(pieces, axis=1), cols("hf").astype(BF16)


def kernel(x, ln_in_g, ln_in_b, w_in, ret_w_out, hgrn_lower_bounds, hgrn_norm_g, hgrn_w_out, swa_sinks,
           swa_w_out, w_o, ln_mix_g, ln_mix_b, ffn_w_gate, ffn_w_up, ffn_w_down, moe_router, moe_w_gate,
           moe_w_up, moe_w_down, ln_ffn_g, ln_ffn_b):
    B, S, D = x.shape
    T = B * S
    assert D == D_MODEL and S % RET_CHUNK == 0 and S % (2 * SWA_CHUNK) == 0 and S >= 4 * SWA_CHUNK

    lb_all = jnp.cumsum(jax.nn.softmax(hgrn_lower_bounds.astype(F32), axis=0), axis=0)
    lb_all = lb_all - lb_all[0]
    tabs = retention_tables(S)

    h, hb = ln_in(x.reshape(T, D), ln_in_g, ln_in_b)
    for layer in range(DEPTH):
        w_main, w_hf = permute_w_in(w_in, layer)
        proj = matmul(hb, w_main, BF16, tn=N_PROJ // 4, name="in_proj")
        ya, yb, yc = token_mixers(proj, hb, w_hf, tabs, lb_all[layer], hgrn_norm_g[layer], swa_sinks[layer],
                                  B, S)
        dense = layer % 2 == 0
        h, aux = merge_project(ya, yb, yc, proj, h,
                               ret_w_out[layer].astype(BF16), hgrn_w_out[layer].astype(BF16),
                               swa_w_out[layer].astype(BF16), w_o[layer].astype(BF16),
                               ln_mix_g[layer], ln_mix_b[layer], token_tiles=not dense)

        j = layer // 2
        if dense:
            h, hb = dense_ffn(aux, h, ffn_w_gate[j].astype(BF16), ffn_w_up[j].astype(BF16),
                              ffn_w_down[j].astype(BF16), ln_ffn_g[layer], ln_ffn_b[layer])
        else:
            h, hb = moe_ffn(h, aux, moe_router[j], moe_w_gate[j], moe_w_up[j], moe_w_down[j],
                            ln_ffn_g[layer], ln_ffn_b[layer])
    return h.reshape(B, S, D)
```

```python
import functools

import numpy as np
import jax
import jax.numpy as jnp
from jax import lax
from jax.experimental import pallas as pl
from jax.experimental.pallas import tpu as pltpu

F32 = jnp.float32
BF16 = jnp.bfloat16

D_MODEL = 1024
RET_HEADS, RET_DK, RET_DV = 4, 128, 256
RET_QK, RET_V = RET_HEADS * RET_DK, RET_HEADS * RET_DV
ROPE_BASE = 10000.0
HG_HEADS, HG_DK, HG_DV = 8, 128, 128
HG_K, HG_V = HG_HEADS * HG_DK, HG_HEADS * HG_DV
SWA_HQ, SWA_HKV, SWA_HD = 16, 4, 64
SWA_G = SWA_HQ // SWA_HKV
SWA_Q, SWA_KV = SWA_HQ * SWA_HD, SWA_HKV * SWA_HD
SWA_CHUNK = 64
SWA_WIN_CHUNKS = 2
N_EXPERTS, TOP_K = 8, 2
DEPTH = 2
LN_EPS, RMS_EPS = 1e-5, 1e-6
DN_ALPHA = (2.0 * DEPTH) ** 0.25

VMEM_LIMIT_BYTES = 56 * 1024 * 1024
LANES = 128
SUB = 8

RET_CHUNK = 128
RET_BLOCK = 512
HG_CHUNK = 32
HG_BLOCK = 256
MIX_SPAN = 1024

OFF_RQ, OFF_RK, OFF_RV, OFF_RG = 0, 512, 1024, 2048
OFF_HQ, OFF_HI, OFF_HG = 3072, 4096, 5120
OFF_GA, OFF_GB, OFF_GC = 6144, 7168, 8192
OFF_SWA = 9216
SWA_GROUP_W = SWA_G * SWA_HD + 4 * SWA_HD
N_PROJ = OFF_SWA + SWA_HKV * SWA_GROUP_W


def _cparams(sem, vmem=VMEM_LIMIT_BYTES):
    return pltpu.CompilerParams(dimension_semantics=sem, vmem_limit_bytes=vmem)


def _layer_norm(x, g, b):
    mu = jnp.mean(x, -1, keepdims=True)
    xc = x - mu
    var = jnp.mean(xc * xc, -1, keepdims=True)
    return xc * lax.rsqrt(var + LN_EPS) * g + b


def _silu(x):
    return x * jax.nn.sigmoid(x)


def _ln_in_kernel(x_ref, g_ref, b_ref, h_ref, hb_ref):
    h = _layer_norm(x_ref[...], g_ref[...], b_ref[...])
    h_ref[...] = h
    hb_ref[...] = h.astype(BF16)


def ln_in(x2, g, b, tm=512):
    T, D = x2.shape
    tm = min(tm, T)
    return pl.pallas_call(
        _ln_in_kernel,
        grid=(T // tm,),
        in_specs=[pl.BlockSpec((tm, D), lambda i: (i, 0)),
                  pl.BlockSpec((1, D), lambda i: (0, 0)),
                  pl.BlockSpec((1, D), lambda i: (0, 0))],
        out_specs=[pl.BlockSpec((tm, D), lambda i: (i, 0)),
                   pl.BlockSpec((tm, D), lambda i: (i, 0))],
        out_shape=[jax.ShapeDtypeStruct((T, D), F32), jax.ShapeDtypeStruct((T, D), BF16)],
        compiler_params=_cparams(("parallel",)),
        name="ln_in",
    )(x2, g.reshape(1, D), b.reshape(1, D))


def _matmul_kernel(x_ref, w_ref, o_ref):
    o_ref[...] = jnp.dot(x_ref[...], w_ref[...], preferred_element_type=F32).astype(o_ref.dtype)


def matmul(x, w, out_dtype, tm=1024, tn=512, name="matmul"):
    T, K = x.shape
    N = w.shape[1]
    tm = min(tm, T)
    return pl.pallas_call(
        _matmul_kernel,
        grid=(T // tm, N // tn),
        in_specs=[pl.BlockSpec((tm, K), lambda i, j: (i, 0)),
                  pl.BlockSpec((K, tn), lambda i, j: (0, j))],
        out_specs=pl.BlockSpec((tm, tn), lambda i, j: (i, j)),
        out_shape=jax.ShapeDtypeStruct((T, N), out_dtype),
        compiler_params=_cparams(("parallel", "parallel")),
        name=name,
    )(x, w)


def _retention_block(q_ref, k_ref, v_ref, g_ref, cos_ref, sin_ref, tabs, o_ref, st_ref, start, *, chunk, block):
    dm, qd, kd, cd = tabs
    nc = block // chunk
    nt_dims = (((1,), (1,)), ((), ()))
    tn_dims = (((0,), (0,)), ((), ()))
    r = pl.ds(pl.multiple_of(start, block), block)
    cos = cos_ref[r, :]
    sin = sin_ref[r, :]
    q = q_ref[r, :].astype(F32)
    k = k_ref[r, :].astype(F32)
    qr = q * cos + pltpu.roll(q, RET_DK // 2, 1) * sin
    kr = k * cos + pltpu.roll(k, RET_DK // 2, 1) * sin
    v = v_ref[r, :]
    rows = [slice(c * chunk, (c + 1) * chunk) for c in range(nc)]
    upd = [lax.dot_general((kr[rc] * kd).astype(BF16), v[rc], tn_dims, preferred_element_type=F32)
           for rc in rows]
    att = [lax.dot_general(qr[rc].astype(BF16), kr[rc].astype(BF16), nt_dims,
                           preferred_element_type=F32) * dm for rc in rows]
    intra = [jnp.dot(a.astype(BF16), v[rc], preferred_element_type=F32) for a, rc in zip(att, rows)]
    q_in = [(qr[rc] * qd).astype(BF16) for rc in rows]
    st = st_ref[...]
    outs = []
    for c in range(nc):
        outs.append(intra[c] + jnp.dot(q_in[c], st.astype(BF16), preferred_element_type=F32))
        st = st * cd + upd[c]
    st_ref[...] = st
    o = jnp.concatenate(outs, axis=0)
    on = o * lax.rsqrt(jnp.mean(o * o, -1, keepdims=True) + RMS_EPS)
    g = g_ref[r, :].astype(F32)
    o_ref[r, :] = (_silu(g) * on).astype(o_ref.dtype)


def retention_tables(S):
    C = RET_CHUNK
    half = RET_DK // 2
    pos = jnp.arange(S, dtype=F32)
    inv = 1.0 / (ROPE_BASE ** jnp.linspace(0.0, 1.0, half, dtype=F32))
    ang = pos[:, None] * inv[None, :]
    cos, sin = jnp.cos(ang), jnp.sin(ang)
    log_gamma = jnp.log(1.0 - 2.0 ** (-5.0 - jnp.arange(RET_HEADS, dtype=F32)))
    idx = jnp.arange(C, dtype=F32)
    diff = idx[:, None] - idx[None, :]
    decay = jnp.where(diff[None] >= 0,
                      jnp.exp(jnp.maximum(diff, 0.0)[None] * log_gamma[:, None, None]), 0.0)
    scale = RET_DK ** -0.5
    qd = jnp.exp((idx + 1.0)[None, :] * log_gamma[:, None])
    kd = jnp.exp((C - 1.0 - idx)[None, :] * log_gamma[:, None]) * scale
    cd = jnp.exp(C * log_gamma)
    return {
        "cos": jnp.concatenate([cos, cos], axis=1),
        "sin": jnp.concatenate([-sin, sin], axis=1),
        "dm": decay * scale,
        "qd": jnp.broadcast_to(qd[:, :, None], (RET_HEADS, C, RET_DK)),
        "kd": jnp.broadcast_to(kd[:, :, None], (RET_HEADS, C, RET_DK)),
        "cd": jnp.broadcast_to(cd[:, None, None], (RET_HEADS, 1, RET_DV)),
    }


def _hgrn_block(q_ref, z_ref, i_ref, g_ref, lb, ng, o_ref, st_ref, start, lanes, consts, *, chunk, block):
    row, causal = consts
    half = chunk // 2
    nc = block // chunk
    nt_dims = (((1,), (1,)), ((), ()))
    tn_dims = (((0,), (0,)), ((), ()))

    def chunk_rows(x, lo):
        return jnp.concatenate([x[c * chunk + lo:c * chunk + lo + 1, :] for c in range(nc)], axis=0)

    def spread(xc):
        return jnp.concatenate(
            [jnp.broadcast_to(xc[c:c + 1, :], (chunk, xc.shape[1])) for c in range(nc)], axis=0)

    r = pl.ds(pl.multiple_of(start, block), block)
    f = lb + (1.0 - lb) * jax.nn.sigmoid(z_ref[r, lanes])
    cum = jnp.log(f)
    shift = 1
    while shift < chunk:
        cum = cum + jnp.where(row >= shift, pltpu.roll(cum, shift, 0), 0.0)
        shift *= 2
    mid_c = chunk_rows(cum, half - 1)
    last_c = chunk_rows(cum, chunk - 1)
    mid = spread(mid_c)
    qh = q_ref[r, lanes].astype(F32) * jnp.exp(cum - mid)
    kh = (1.0 - f) * jnp.exp(mid - cum)
    q_in = (qh * spread(jnp.exp(mid_c))).astype(BF16)
    k_st = (kh * spread(jnp.exp(last_c - mid_c))).astype(BF16)
    dec = jnp.exp(last_c)
    qh = qh.astype(BF16)
    kh = kh.astype(BF16)
    v = i_ref[r, lanes]
    rows = [slice(c * chunk, (c + 1) * chunk) for c in range(nc)]
    upd = [lax.dot_general(v[rc], k_st[rc], tn_dims, preferred_element_type=F32) for rc in rows]
    att = [lax.dot_general(qh[rc], kh[rc], nt_dims, preferred_element_type=F32) for rc in rows]
    intra = [jnp.dot(jnp.where(causal, a, 0.0).astype(BF16), v[rc], preferred_element_type=F32)
             for a, rc in zip(att, rows)]
    st = st_ref[...]
    outs = []
    for c, rc in enumerate(rows):
        inter = lax.dot_general(q_in[rc], st.astype(BF16), nt_dims, preferred_element_type=F32)
        outs.append(intra[c] + inter)
        st = st * dec[c:c + 1, :] + upd[c]
    st_ref[...] = st
    o = jnp.concatenate(outs, axis=0)
    on = o * lax.rsqrt(jnp.mean(o * o, -1, keepdims=True) + RMS_EPS) * ng
    g = g_ref[r, lanes].astype(F32)
    o_ref[r, lanes] = (_silu(g) * on).astype(o_ref.dtype)


def _swa_step(sink_ref, x_ref, o_ref, j, kvh, consts):
    lo_kv, lo_o, rel, zero = consts
    CH = SWA_CHUNK
    HD = SWA_HD
    QR = 2 * CH
    KW = 4 * CH
    QW = SWA_G * HD
    nt_dims = (((1,), (1,)), ((), ()))
    first = jnp.maximum(2 * j - SWA_WIN_CHUNKS, 0)
    rq = pl.ds(pl.multiple_of(j * QR, QR), QR)
    rk = pl.ds(pl.multiple_of(first * CH, CH), KW)
    kv = x_ref[rk, QW:QW + 2 * HD]
    vk = x_ref[rk, QW + 2 * HD:QW + 4 * HD]
    kbd = jnp.concatenate([jnp.where(lo_kv, kv, zero), jnp.where(lo_kv, zero, vk)], axis=0)
    vbd = jnp.concatenate([jnp.where(lo_kv, vk, zero), jnp.where(lo_kv, zero, kv)], axis=0)
    valid = lax.bitcast_convert_type(rel + (first - 2 * j + SWA_WIN_CHUNKS), jnp.uint32) <= SWA_WIN_CHUNKS
    scores = [lax.dot_general(x_ref[rq, p * 2 * HD:(p + 1) * 2 * HD], kbd, nt_dims,
                              preferred_element_type=F32) for p in range(SWA_G // 2)]
    for p, s in enumerate(scores):
        s = jnp.where(valid, s, -jnp.inf)
        probs, inv = [], []
        for hh in range(2):
            sink = sink_ref[kvh * SWA_G + 2 * p + hh]
            sh = s[:, hh * KW:(hh + 1) * KW]
            m = jnp.maximum(jnp.max(sh, -1, keepdims=True), sink)
            e = jnp.exp(sh - m)
            den = jnp.sum(e, -1, keepdims=True) + jnp.exp(sink - m)
            probs.append(e.astype(BF16))
            inv.append(1.0 / den)
        o = jnp.dot(jnp.concatenate(probs, axis=1), vbd, preferred_element_type=F32)
        o = o * jnp.where(lo_o, inv[0], inv[1])
        o_ref[rq, p * 2 * HD:(p + 1) * 2 * HD] = o.astype(o_ref.dtype)


def _mixers_kernel(sink_ref, rq_ref, rk_ref, rv_ref, rg_ref, cos_ref, sin_ref, dm_ref, qd_ref, kd_ref, cd_ref,
                   hq_ref, xb_ref, wf_ref, hi_ref, hg_ref, lb_ref, ng_ref, sx_ref, ya_ref, yb_ref, yc_ref,
                   st_r, st_h, hz_ref, *, span, nspans):
    kvh = pl.program_id(1)
    st_r[...] = jnp.zeros_like(st_r)
    st_h[...] = jnp.zeros_like(st_h)
    hz_ref[...] = jnp.dot(xb_ref[...], wf_ref[...].astype(BF16), preferred_element_type=F32)
    ret_tabs = (dm_ref[0], qd_ref[0], kd_ref[0], cd_ref[0])
    lb2 = lb_ref[0]
    ng = ng_ref[...]
    hg_consts = (lax.broadcasted_iota(jnp.int32, (HG_BLOCK, HG_DK), 0) % HG_CHUNK,
                 lax.broadcasted_iota(jnp.int32, (HG_CHUNK, HG_CHUNK), 0)
                 >= lax.broadcasted_iota(jnp.int32, (HG_CHUNK, HG_CHUNK), 1))
    QR, KW, HD = 2 * SWA_CHUNK, 4 * SWA_CHUNK, SWA_HD
    kcol = lax.broadcasted_iota(jnp.int32, (QR, 2 * KW), 1)
    qrow = lax.broadcasted_iota(jnp.int32, (QR, 2 * KW), 0)
    swa_consts = (lax.broadcasted_iota(jnp.int32, (KW, 2 * HD), 1) < HD,
                  lax.broadcasted_iota(jnp.int32, (QR, 2 * HD), 1) < HD,
                  (kcol % KW) // SWA_CHUNK - qrow // SWA_CHUNK,
                  jnp.zeros((KW, 2 * HD), BF16))
    n_swa = span // QR
    n_hg = span // HG_BLOCK
    n_ret = span // RET_BLOCK

    def body(u, carry):
        base = u * span
        for t in range(n_swa):
            _swa_step(sink_ref, sx_ref, yc_ref, u * n_swa + t, kvh, swa_consts)
            if t < 2 * n_hg:
                hh, blk = t % 2, t // 2
                lanes = slice(hh * HG_DK, (hh + 1) * HG_DK)
                _hgrn_block(hq_ref, hz_ref, hi_ref, hg_ref, lb2[:, lanes], ng, yb_ref, st_h.at[hh],
                            base + blk * HG_BLOCK, lanes, hg_consts, chunk=HG_CHUNK, block=HG_BLOCK)
            if t % (n_swa // n_ret) == 0:
                _retention_block(rq_ref, rk_ref, rv_ref, rg_ref, cos_ref, sin_ref, ret_tabs, ya_ref, st_r,
                                 base + (t // (n_swa // n_ret)) * RET_BLOCK, chunk=RET_CHUNK, block=RET_BLOCK)
        return carry

    lax.fori_loop(0, nspans, body, 0)


def token_mixers(proj, hb, w_hf, tabs, lb, norm_g, sinks, B, S):
    T = B * S
    D = hb.shape[1]
    span = min(MIX_SPAN, S)
    C = RET_CHUNK
    hw = 2 * HG_DK
    ret = lambda off, w: (lambda b, h: (b, off // w + h))
    kern = functools.partial(_mixers_kernel, span=span, nspans=S // span)
    out = jax.ShapeDtypeStruct((T, D_MODEL), BF16)
    outspec = pl.BlockSpec((S, RET_DV), lambda b, h: (b, h))
    return pl.pallas_call(
        kern,
        grid=(B, RET_HEADS),
        in_specs=[pl.BlockSpec(memory_space=pltpu.SMEM),
                  pl.BlockSpec((S, RET_DK), ret(OFF_RQ, RET_DK)),
                  pl.BlockSpec((S, RET_DK), ret(OFF_RK, RET_DK)),
                  pl.BlockSpec((S, RET_DV), ret(OFF_RV, RET_DV)),
                  pl.BlockSpec((S, RET_DV), ret(OFF_RG, RET_DV)),
                  pl.BlockSpec((S, RET_DK), lambda b, h: (0, 0)),
                  pl.BlockSpec((S, RET_DK), lambda b, h: (0, 0)),
                  pl.BlockSpec((1, C, C), lambda b, h: (h, 0, 0)),
                  pl.BlockSpec((1, C, RET_DK), lambda b, h: (h, 0, 0)),
                  pl.BlockSpec((1, C, RET_DK), lambda b, h: (h, 0, 0)),
                  pl.BlockSpec((1, 1, RET_DV), lambda b, h: (h, 0, 0)),
                  pl.BlockSpec((S, hw), ret(OFF_HQ, hw)),
                  pl.BlockSpec((S, D), lambda b, h: (b, 0)),
                  pl.BlockSpec((D, hw), lambda b, h: (0, h)),
                  pl.BlockSpec((S, hw), ret(OFF_HI, hw)),
                  pl.BlockSpec((S, hw), ret(OFF_HG, hw)),
                  pl.BlockSpec((1, 1, hw), lambda b, h: (h, 0, 0)),
                  pl.BlockSpec((1, HG_DV), lambda b, h: (0, 0)),
                  pl.BlockSpec((S, SWA_GROUP_W), ret(OFF_SWA, SWA_GROUP_W))],
        out_specs=[outspec, outspec, outspec],
        out_shape=[out, out, out],
        scratch_shapes=[pltpu.VMEM((RET_DK, RET_DV), F32), pltpu.VMEM((2, HG_DV, HG_DK), F32),
                        pltpu.VMEM((S, hw), F32)],
        compiler_params=_cparams(("parallel", "parallel")),
        name="token_mixers",
    )(sinks.astype(F32), proj, proj, proj, proj, tabs["cos"], tabs["sin"], tabs["dm"], tabs["qd"], tabs["kd"],
      tabs["cd"], proj, hb, w_hf, proj, proj, lb.reshape(RET_HEADS, 1, hw), norm_g.reshape(1, HG_DV), proj)


def _merge_kernel(a_ref, b_ref, c_ref, ga_ref, gb_ref, gc_ref, h_ref, wa_ref, wb_ref, wc_ref, wo_ref,
                  g_ref, beta_ref, ho_ref, aux_ref, *, tm, token_tiles):
    def branch(x_ref, gate_ref, w_ref):
        y = jnp.dot(x_ref[...], w_ref[...], preferred_element_type=F32)
        return jax.nn.sigmoid(gate_ref[...].astype(F32)) * y

    merged = branch(a_ref, ga_ref, wa_ref) + branch(b_ref, gb_ref, wb_ref) + branch(c_ref, gc_ref, wc_ref)
    mix = jnp.dot(merged.astype(BF16), wo_ref[...], preferred_element_type=F32)
    hn = _layer_norm(DN_ALPHA * h_ref[...] + mix, g_ref[...], beta_ref[...])
    ho_ref[...] = hn
    if token_tiles:
        for j in range(SUB):
            aux_ref[pl.ds(j, tm, stride=SUB), :] = hn[:, j * LANES:(j + 1) * LANES]
    else:
        aux_ref[...] = hn.astype(BF16)


def merge_project(ya, yb, yc, proj, h, wa, wb, wc, wo, g, beta, token_tiles, tm=512):
    T, D = h.shape
    tm = min(tm, T)
    row = lambda i: (i, 0)
    const = lambda i: (0, 0)
    wspec = pl.BlockSpec((D, D), const, pipeline_mode=pl.Buffered(1))
    if token_tiles:
        aux_spec = pl.BlockSpec((tm * SUB, LANES), row)
        aux_shape = jax.ShapeDtypeStruct((T * SUB, LANES), F32)
    else:
        aux_spec = pl.BlockSpec((tm, D), row)
        aux_shape = jax.ShapeDtypeStruct((T, D), BF16)
    kern = functools.partial(_merge_kernel, tm=tm, token_tiles=token_tiles)
    return pl.pallas_call(
        kern,
        grid=(T // tm,),
        in_specs=[pl.BlockSpec((tm, D), row), pl.BlockSpec((tm, D), row), pl.BlockSpec((tm, D), row),
                  pl.BlockSpec((tm, D), lambda i: (i, OFF_GA // D_MODEL)),
                  pl.BlockSpec((tm, D), lambda i: (i, OFF_GB // D_MODEL)),
                  pl.BlockSpec((tm, D), lambda i: (i, OFF_GC // D_MODEL)),
                  pl.BlockSpec((tm, D), row),
                  wspec, wspec, wspec, wspec,
                  pl.BlockSpec((1, D), const), pl.BlockSpec((1, D), const)],
        out_specs=[pl.BlockSpec((tm, D), row), aux_spec],
        out_shape=[jax.ShapeDtypeStruct((T, D), F32), aux_shape],
        compiler_params=_cparams(("parallel",)),
        name="merge_project",
    )(ya, yb, yc, proj, proj, proj, h, wa, wb, wc, wo, g.reshape(1, D), beta.reshape(1, D))


def _ffn_kernel(xb_ref, h_ref, wg_ref, wu_ref, wd_ref, g_ref, beta_ref, ho_ref, hob_ref):
    x = xb_ref[...]
    gate = jnp.dot(x, wg_ref[...], preferred_element_type=F32)
    up = jnp.dot(x, wu_ref[...], preferred_element_type=F32)
    a = (_silu(gate) * up).astype(BF16)
    ff = jnp.dot(a, wd_ref[...], preferred_element_type=F32)
    hn = _layer_norm(DN_ALPHA * h_ref[...] + ff, g_ref[...], beta_ref[...])
    ho_ref[...] = hn
    hob_ref[...] = hn.astype(BF16)


def dense_ffn(hb, h, wg, wu, wd, g, beta, tm=512):
    T, D = h.shape
    F = wg.shape[1]
    tm = min(tm, T)
    row = lambda i: (i, 0)
    const = lambda i: (0, 0)
    resident = pl.Buffered(1)
    return pl.pallas_call(
        _ffn_kernel,
        grid=(T // tm,),
        in_specs=[pl.BlockSpec((tm, D), row), pl.BlockSpec((tm, D), row),
                  pl.BlockSpec((D, F), const, pipeline_mode=resident),
                  pl.BlockSpec((D, F), const, pipeline_mode=resident),
                  pl.BlockSpec((F, D), const, pipeline_mode=resident),
                  pl.BlockSpec((1, D), const), pl.BlockSpec((1, D), const)],
        out_specs=[pl.BlockSpec((tm, D), row), pl.BlockSpec((tm, D), row)],
        out_shape=[jax.ShapeDtypeStruct((T, D), F32), jax.ShapeDtypeStruct((T, D), BF16)],
        compiler_params=_cparams(("parallel",)),
        name="dense_ffn",
    )(hb, h, wg, wu, wd, g.reshape(1, D), beta.reshape(1, D))


def _route_top2(h, w_hi, w_lo):
    h_hi = h.astype(BF16)
    h_lo = (h - h_hi.astype(F32)).astype(BF16)
    logits = (jnp.dot(h_hi, w_hi, preferred_element_type=F32) + jnp.dot(h_lo, w_hi, preferred_element_type=F32)
              + jnp.dot(h_hi, w_lo, preferred_element_type=F32))
    lane = lax.broadcasted_iota(jnp.int32, logits.shape, 1)
    l1 = jnp.where(lane < N_EXPERTS, logits, -jnp.inf)
    m1 = jnp.max(l1, -1, keepdims=True)
    i1 = jnp.min(jnp.where(l1 == m1, lane, LANES), -1, keepdims=True)
    l2 = jnp.where(lane == i1, -jnp.inf, l1)
    m2 = jnp.max(l2, -1, keepdims=True)
    i2 = jnp.min(jnp.where(l2 == m2, lane, LANES), -1, keepdims=True)
    e = jnp.exp(m2 - m1)
    w1 = 1.0 / (1.0 + e)
    w2 = e / (1.0 + e)
    return jnp.where(lane == 0, i1.astype(F32),
                     jnp.where(lane == 1, i2.astype(F32),
                               jnp.where(lane == 2, w1, jnp.where(lane == 3, w2, 0.0))))


def _router_kernel(h_ref, w_ref, o_ref):
    o_ref[...] = _route_top2(h_ref[...], w_ref[0], w_ref[1])


def router_top2(h, w_router, tm=512):
    T, D = h.shape
    tm = min(tm, T)
    w_pad = jnp.zeros((D, LANES), F32).at[:, :N_EXPERTS].set(w_router.astype(F32))
    w_hi = w_pad.astype(BF16)
    w_lo = (w_pad - w_hi.astype(F32)).astype(BF16)
    return pl.pallas_call(
        _router_kernel,
        grid=(T // tm,),
        in_specs=[pl.BlockSpec((tm, D), lambda i: (i, 0)), pl.BlockSpec((2, D, LANES), lambda i: (0, 0, 0))],
        out_specs=pl.BlockSpec((tm, LANES), lambda i: (i, 0)),
        out_shape=jax.ShapeDtypeStruct((T, LANES), F32),
        compiler_params=_cparams(("parallel",)),
        name="router_top2",
    )(h, jnp.stack([w_hi, w_lo]))


W_PIECES = 8


def _expert_ffn_kernel(te_ref, nt_ref, src_ref, dst_ref, h_hbm, wg_hbm, wu_hbm, wd_hbm, y_hbm,
                       xbuf, xb_ref, ybuf, y_ref, wg_s, wu_s, wd_s, stage_in, stage_out, gsem, ssem, wsem,
                       *, tm):
    i = pl.program_id(0)
    n_i = pl.num_programs(0)
    active = i < nt_ref[0]
    buf = i % 2
    nxt = jnp.minimum(i + 1, n_i - 1)
    prv = jnp.maximum(i - 1, 0)
    expert = te_ref[i]
    new_expert = jnp.logical_and(active, jnp.logical_or(i == 0, expert != te_ref[prv]))

    def gather_row(tile, b, r):
        tok = src_ref[tile * tm + r]
        pltpu.make_async_copy(h_hbm.at[pl.ds(pl.multiple_of(tok * SUB, SUB), SUB), :],
                              xbuf.at[b, pl.ds(pl.multiple_of(r * SUB, SUB), SUB), :], gsem.at[b]).start()

    def scatter_row(tile, r):
        slot = dst_ref[tile * tm + r]
        pltpu.make_async_copy(ybuf.at[pl.ds(pl.multiple_of(r * SUB, SUB), SUB), :],
                              y_hbm.at[pl.ds(pl.multiple_of(slot * SUB, SUB), SUB), :], ssem).start()

    def wait_gather(b):
        pltpu.make_async_copy(h_hbm.at[pl.ds(0, tm * SUB), :], xbuf.at[b], gsem.at[b]).wait()

    def wait_scatter():
        pltpu.make_async_copy(ybuf, y_hbm.at[pl.ds(0, tm * SUB), :], ssem).wait()

    def looped(n, fn):
        def body(r, carry):
            fn(r)
            return carry
        lax.fori_loop(0, n, body, 0, unroll=8)

    def share_row(r):
        gather_row(nxt, 1 - buf, r)
        scatter_row(prv, r)

    @pl.when(i == 0)
    def _():
        looped(tm, lambda r: gather_row(0, 0, r))
        ybuf[...] = jnp.zeros_like(ybuf)

    @pl.when(new_expert)
    def _():
        pieces = []
        for w_hbm, w_s, stage in ((wg_hbm, wg_s, stage_in), (wu_hbm, wu_s, stage_in), (wd_hbm, wd_s, stage_out)):
            rows = w_s.shape[0] // W_PIECES
            for c in range(W_PIECES):
                pieces.append((w_hbm.at[expert, pl.ds(c * rows, rows), :], w_s.at[pl.ds(c * rows, rows), :], stage))
        copies = [pltpu.make_async_copy(src, stage.at[k % 2], wsem.at[k % 2])
                  for k, (src, _, stage) in enumerate(pieces)]
        copies[0].start()
        for k, (_, dst, stage) in enumerate(pieces):
            if k + 1 < len(pieces):
                copies[k + 1].start()
            copies[k].wait()
            dst[...] = stage[k % 2].astype(BF16)

    wait_gather(buf)
    for j in range(SUB):
        xb_ref[:, j * LANES:(j + 1) * LANES] = xbuf[buf, pl.ds(j, tm, stride=SUB), :].astype(BF16)

    @pl.when(active)
    def _():
        for r in range(tm):
            share_row(r)
        x = xb_ref[...]
        gate = jnp.dot(x, wg_s[...], preferred_element_type=F32)
        up = jnp.dot(x, wu_s[...], preferred_element_type=F32)
        a = (_silu(gate) * up).astype(BF16)
        y_ref[...] = jnp.dot(a, wd_s[...], preferred_element_type=F32)

    @pl.when(jnp.logical_not(active))
    def _():
        looped(tm, share_row)
        y_ref[...] = jnp.zeros_like(y_ref)

    wait_scatter()
    for j in range(SUB):
        ybuf[pl.ds(j, tm, stride=SUB), :] = y_ref[:, j * LANES:(j + 1) * LANES]

    @pl.when(i == n_i - 1)
    def _():
        looped(tm, lambda r: scatter_row(i, r))
        wait_scatter()
        wait_gather(1 - buf)


def expert_ffn(h_tiles, te, nt, src_tok, dst_slot, wg, wu, wd, tm):
    P = src_tok.shape[0]
    D = wg.shape[1]
    F = wg.shape[2]
    kern = functools.partial(_expert_ffn_kernel, tm=tm)
    anyspec = pl.BlockSpec(memory_space=pl.ANY)
    return pl.pallas_call(
        kern,
        grid_spec=pltpu.PrefetchScalarGridSpec(
            num_scalar_prefetch=4,
            grid=(P // tm,),
            in_specs=[anyspec, anyspec, anyspec, anyspec],
            out_specs=anyspec,
            scratch_shapes=[pltpu.VMEM((2, tm * SUB, LANES), F32),
                            pltpu.VMEM((tm, D), BF16),
                            pltpu.VMEM((tm * SUB, LANES), F32),
                            pltpu.VMEM((tm, D), F32),
                            pltpu.VMEM((D, F), BF16),
                            pltpu.VMEM((D, F), BF16),
                            pltpu.VMEM((F, D), BF16),
                            pltpu.VMEM((2, D // W_PIECES, F), F32),
                            pltpu.VMEM((2, F // W_PIECES, D), F32),
                            pltpu.SemaphoreType.DMA((2,)),
                            pltpu.SemaphoreType.DMA(()),
                            pltpu.SemaphoreType.DMA((2,))],
        ),
        out_shape=jax.ShapeDtypeStruct((P * SUB, LANES), F32),
        compiler_params=_cparams(("arbitrary",)),
        name="expert_ffn",
    )(te, nt, src_tok, dst_slot, h_tiles, wg, wu, wd)


def _moe_combine_kernel(y0_ref, y1_ref, r_ref, h_ref, g_ref, beta_ref, ho_ref, hob_ref, *, tm):
    def rows(y_ref):
        return jnp.concatenate([y_ref[pl.ds(j, tm, stride=SUB), :] for j in range(SUB)], axis=1)

    r = r_ref[...]
    ff = rows(y0_ref) * r[:, 2:3] + rows(y1_ref) * r[:, 3:4]
    hn = _layer_norm(DN_ALPHA * h_ref[...] + ff, g_ref[...], beta_ref[...])
    ho_ref[...] = hn
    hob_ref[...] = hn.astype(BF16)


def moe_combine(y_tiles, route, h, g, beta, tm=512):
    T, D = h.shape
    tm = min(tm, T)
    nb = T // tm
    row = lambda i: (i, 0)
    const = lambda i: (0, 0)
    kern = functools.partial(_moe_combine_kernel, tm=tm)
    return pl.pallas_call(
        kern,
        grid=(nb,),
        in_specs=[pl.BlockSpec((tm * SUB, LANES), lambda i: (i, 0)),
                  pl.BlockSpec((tm * SUB, LANES), lambda i: (nb + i, 0)),
                  pl.BlockSpec((tm, LANES), row),
                  pl.BlockSpec((tm, D), row),
                  pl.BlockSpec((1, D), const), pl.BlockSpec((1, D), const)],
        out_specs=[pl.BlockSpec((tm, D), row), pl.BlockSpec((tm, D), row)],
        out_shape=[jax.ShapeDtypeStruct((T, D), F32), jax.ShapeDtypeStruct((T, D), BF16)],
        compiler_params=_cparams(("parallel",)),
        name="moe_combine",
    )(y_tiles, y_tiles, route, h, g.reshape(1, D), beta.reshape(1, D))


def moe_plan(route, T, tm):
    i32 = jnp.int32
    n_slots = TOP_K * T
    experts = jnp.arange(N_EXPERTS, dtype=i32)
    e_flat = route[:, 0:TOP_K].astype(i32).T.reshape(-1)
    order = jnp.argsort(e_flat, stable=True).astype(i32)
    sizes = jnp.sum((e_flat[:, None] == experts[None, :]).astype(i32), axis=0)
    padded = ((sizes + tm - 1) // tm) * tm
    gend = jnp.cumsum(padded)
    gstart = gend - padded
    cstart = jnp.cumsum(sizes) - sizes
    n_rows = n_slots + N_EXPERTS * tm
    row = jnp.arange(n_rows, dtype=i32)
    group = jnp.sum((row[:, None] >= gend[None, :]).astype(i32), axis=1)
    onehot = (jnp.minimum(group, N_EXPERTS - 1)[:, None] == experts[None, :]).astype(i32)
    look = lambda table: jnp.sum(onehot * table[None, :], axis=1)
    local = row - look(gstart)
    real = jnp.logical_and(group < N_EXPERTS, local < look(sizes))
    slot = order[jnp.clip(look(cstart) + local, 0, n_slots - 1)]
    pad_rank = jnp.where(group < N_EXPERTS, look(gstart - cstart) + local - look(sizes), row - n_slots)
    dst_slot = jnp.where(real, slot, n_slots + pad_rank)
    src_tok = jnp.where(real, slot % T, 0)
    n_tiles = n_rows // tm
    nt = (gend[-1] // tm).astype(i32)
    tile_start = jnp.arange(n_tiles, dtype=i32) * tm
    te = jnp.minimum(jnp.sum((tile_start[:, None] >= gend[None, :]).astype(i32), axis=1), N_EXPERTS - 1)
    te = jnp.where(jnp.arange(n_tiles) < nt, te, te[jnp.maximum(nt - 1, 0)])
    return src_tok.astype(i32), dst_slot.astype(i32), te.astype(i32), nt.reshape(1)


def moe_ffn(h, h_tiles, w_router, wg, wu, wd, g, beta, tm=512):
    T, D = h.shape
    route = router_top2(h, w_router)
    src_tok, dst_slot, te, nt = moe_plan(route, T, tm)
    y_tiles = expert_ffn(h_tiles, te, nt, src_tok, dst_slot, wg, wu, wd, tm)
    return moe_combine(y_tiles, route, h, g, beta)


def permute_w_in(w_in, layer):
    widths = [RET_QK, RET_QK, RET_V, RET_V, HG_K, HG_K, HG_V, HG_V, SWA_Q, SWA_KV, SWA_KV,
              D_MODEL, D_MODEL, D_MODEL]
    starts = np.concatenate([[0], np.cumsum(widths)])
    names = ["rq", "rk", "rv", "rg", "hq", "hf", "hi", "hg", "sq", "sk", "sv", "ga", "gb", "gc"]
    off = dict(zip(names, starts[:-1].tolist()))
    wid = dict(zip(names, widths))

    def cols(name, lo=0, n=None):
        a = off[name] + lo
        return w_in[layer, :, a:a + (wid[name] if n is None else n)]

    pieces = [cols(n) for n in ("rq", "rk", "rv", "rg", "hq", "hi", "hg", "ga", "gb", "gc")]
    qw = SWA_G * SWA_HD
    for kvh in range(SWA_HKV):
        k = cols("sk", kvh * SWA_HD, SWA_HD)
        v = cols("sv", kvh * SWA_HD, SWA_HD)
        pieces += [cols("sq", kvh * qw, qw) * SWA_HD ** -0.5, k, v, v, k]
    main = jnp.concatenate(pieces, axis=1).astype(BF16)
    return main, cols("hf")


def kernel(x, ln_in_g, ln_in_b, w_in, ret_w_out, hgrn_lower_bounds, hgrn_norm_g, hgrn_w_out, swa_sinks,
           swa_w_out, w_o, ln_mix_g, ln_mix_b, ffn_w_gate, ffn_w_up, ffn_w_down, moe_router, moe_w_gate,
           moe_w_up, moe_w_down, ln_ffn_g, ln_ffn_b):
    B, S, D = x.shape
    T = B * S
    assert D == D_MODEL and S % RET_CHUNK == 0 and S % (2 * SWA_CHUNK) == 0 and S >= 4 * SWA_CHUNK

    lb_all = jnp.cumsum(jax.nn.softmax(hgrn_lower_bounds.astype(F32), axis=0), axis=0)
    lb_all = lb_all - lb_all[0]
    tabs = retention_tables(S)

    h, hb = ln_in(x.reshape(T, D), ln_in_g, ln_in_b)
    for layer in range(DEPTH):
        w_main, w_hf = permute_w_in(w_in, layer)
        proj = matmul(hb, w_main, BF16, tn=N_PROJ // 4, name="in_proj")
        ya, yb, yc = token_mixers(proj, hb, w_hf, tabs, lb_all[layer], hgrn_norm_g[layer], swa_sinks[layer],
                                  B, S)
        dense = layer % 2 == 0
        h, aux = merge_project(ya, yb, yc, proj, h,
                               ret_w_out[layer].astype(BF16), hgrn_w_out[layer].astype(BF16),
                               swa_w_out[layer].astype(BF16), w_o[layer].astype(BF16),
                               ln_mix_g[layer], ln_mix_b[layer], token_tiles=not dense)

        j = layer // 2
        if dense:
            h, hb = dense_ffn(aux, h, ffn_w_gate[j].astype(BF16), ffn_w_up[j].astype(BF16),
                              ffn_w_down[j].astype(BF16), ln_ffn_g[layer], ln_ffn_b[layer])
        else:
            h, hb = moe_ffn(h, aux, moe_router[j], moe_w_gate[j], moe_w_up[j], moe_w_down[j],
                            ln_ffn_g[layer], ln_ffn_b[layer])
    return h.reshape(B, S, D)
```

```python
import functools

import numpy as np
import jax
import jax.numpy as jnp
from jax import lax
from jax.experimental import pallas as pl
from jax.experimental.pallas import tpu as pltpu

F32 = jnp.float32
BF16 = jnp.bfloat16

D_MODEL = 1024
RET_HEADS, RET_DK, RET_DV = 4, 128, 256
RET_QK, RET_V = RET_HEADS * RET_DK, RET_HEADS * RET_DV
ROPE_BASE = 10000.0
HG_HEADS, HG_DK, HG_DV = 8, 128, 128
HG_K, HG_V = HG_HEADS * HG_DK, HG_HEADS * HG_DV
SWA_HQ, SWA_HKV, SWA_HD = 16, 4, 64
SWA_G = SWA_HQ // SWA_HKV
SWA_Q, SWA_KV = SWA_HQ * SWA_HD, SWA_HKV * SWA_HD
SWA_CHUNK = 64
SWA_WIN_CHUNKS = 2
N_EXPERTS, TOP_K = 8, 2
DEPTH = 2
LN_EPS, RMS_EPS = 1e-5, 1e-6
DN_ALPHA = (2.0 * DEPTH) ** 0.25

VMEM_LIMIT_BYTES = 56 * 1024 * 1024
LANES = 128
SUB = 8

RET_CHUNK = 128
RET_BLOCK = 512
HG_CHUNK = 32
HG_BLOCK = 256
MIX_SPAN = 1024

OFF_RQ, OFF_RK, OFF_RV, OFF_RG = 0, 512, 1024, 2048
OFF_HQ, OFF_HI, OFF_HG = 3072, 4096, 5120
OFF_GA, OFF_GB, OFF_GC = 6144, 7168, 8192
OFF_SWA = 9216
OFF_SWA_K = OFF_SWA + SWA_Q
OFF_SWA_V = OFF_SWA_K + SWA_KV
N_PROJ = OFF_SWA_V + SWA_KV
SWA_HEAD_ORDER = [(2 * m + half) * SWA_G + g for m in range(SWA_HKV // 2) for g in range(SWA_G) for half in (0, 1)]


def _cparams(sem, vmem=VMEM_LIMIT_BYTES):
    return pltpu.CompilerParams(dimension_semantics=sem, vmem_limit_bytes=vmem)


def _layer_norm(x, g, b):
    mu = jnp.mean(x, -1, keepdims=True)
    xc = x - mu
    var = jnp.mean(xc * xc, -1, keepdims=True)
    return xc * lax.rsqrt(var + LN_EPS) * g + b


def _silu(x):
    return x * jax.nn.sigmoid(x)


def _ln_in_kernel(x_ref, g_ref, b_ref, h_ref, hb_ref):
    h = _layer_norm(x_ref[...], g_ref[...], b_ref[...])
    h_ref[...] = h
    hb_ref[...] = h.astype(BF16)


def ln_in(x2, g, b, tm=512):
    T, D = x2.shape
    tm = min(tm, T)
    return pl.pallas_call(
        _ln_in_kernel,
        grid=(T // tm,),
        in_specs=[pl.BlockSpec((tm, D), lambda i: (i, 0)),
                  pl.BlockSpec((1, D), lambda i: (0, 0)),
                  pl.BlockSpec((1, D), lambda i: (0, 0))],
        out_specs=[pl.BlockSpec((tm, D), lambda i: (i, 0)),
                   pl.BlockSpec((tm, D), lambda i: (i, 0))],
        out_shape=[jax.ShapeDtypeStruct((T, D), F32), jax.ShapeDtypeStruct((T, D), BF16)],
        compiler_params=_cparams(("parallel",)),
        name="ln_in",
    )(x2, g.reshape(1, D), b.reshape(1, D))


def _matmul_kernel(x_ref, w_ref, o_ref):
    o_ref[...] = jnp.dot(x_ref[...], w_ref[...], preferred_element_type=F32).astype(o_ref.dtype)


def matmul(x, w, out_dtype, tm=1024, tn=512, name="matmul"):
    T, K = x.shape
    N = w.shape[1]
    tm = min(tm, T)
    return pl.pallas_call(
        _matmul_kernel,
        grid=(T // tm, N // tn),
        in_specs=[pl.BlockSpec((tm, K), lambda i, j: (i, 0)),
                  pl.BlockSpec((K, tn), lambda i, j: (0, j))],
        out_specs=pl.BlockSpec((tm, tn), lambda i, j: (i, j)),
        out_shape=jax.ShapeDtypeStruct((T, N), out_dtype),
        compiler_params=_cparams(("parallel", "parallel")),
        name=name,
    )(x, w)


def _retention_block(q_ref, k_ref, v_ref, g_ref, cos_ref, sin_ref, tabs, o_ref, st_ref, start, *, chunk, block):
    dm, qd, kd, cd = tabs
    nc = block // chunk
    nt_dims = (((1,), (1,)), ((), ()))
    tn_dims = (((0,), (0,)), ((), ()))
    r = pl.ds(pl.multiple_of(start, block), block)
    cos = cos_ref[r, :]
    sin = sin_ref[r, :]
    q = q_ref[r, :].astype(F32)
    k = k_ref[r, :].astype(F32)
    qr = q * cos + pltpu.roll(q, RET_DK // 2, 1) * sin
    kr = k * cos + pltpu.roll(k, RET_DK // 2, 1) * sin
    v = v_ref[r, :]
    rows = [slice(c * chunk, (c + 1) * chunk) for c in range(nc)]
    upd = [lax.dot_general((kr[rc] * kd).astype(BF16), v[rc], tn_dims, preferred_element_type=F32)
           for rc in rows]
    att = [lax.dot_general(qr[rc].astype(BF16), kr[rc].astype(BF16), nt_dims,
                           preferred_element_type=F32) * dm for rc in rows]
    intra = [jnp.dot(a.astype(BF16), v[rc], preferred_element_type=F32) for a, rc in zip(att, rows)]
    q_in = [(qr[rc] * qd).astype(BF16) for rc in rows]
    st = st_ref[...]
    outs = []
    for c in range(nc):
        outs.append(intra[c] + jnp.dot(q_in[c], st.astype(BF16), preferred_element_type=F32))
        st = st * cd + upd[c]
    st_ref[...] = st
    o = jnp.concatenate(outs, axis=0)
    on = o * lax.rsqrt(jnp.mean(o * o, -1, keepdims=True) + RMS_EPS)
    g = g_ref[r, :].astype(F32)
    o_ref[r, :] = (_silu(g) * on).astype(o_ref.dtype)


def retention_tables(S):
    C = RET_CHUNK
    half = RET_DK // 2
    pos = jnp.arange(S, dtype=F32)
    inv = 1.0 / (ROPE_BASE ** jnp.linspace(0.0, 1.0, half, dtype=F32))
    ang = pos[:, None] * inv[None, :]
    cos, sin = jnp.cos(ang), jnp.sin(ang)
    log_gamma = jnp.log(1.0 - 2.0 ** (-5.0 - jnp.arange(RET_HEADS, dtype=F32)))
    idx = jnp.arange(C, dtype=F32)
    diff = idx[:, None] - idx[None, :]
    decay = jnp.where(diff[None] >= 0,
                      jnp.exp(jnp.maximum(diff, 0.0)[None] * log_gamma[:, None, None]), 0.0)
    scale = RET_DK ** -0.5
    qd = jnp.exp((idx + 1.0)[None, :] * log_gamma[:, None])
    kd = jnp.exp((C - 1.0 - idx)[None, :] * log_gamma[:, None]) * scale
    cd = jnp.exp(C * log_gamma)
    return {
        "cos": jnp.concatenate([cos, cos], axis=1),
        "sin": jnp.concatenate([-sin, sin], axis=1),
        "dm": decay * scale,
        "qd": jnp.broadcast_to(qd[:, :, None], (RET_HEADS, C, RET_DK)),
        "kd": jnp.broadcast_to(kd[:, :, None], (RET_HEADS, C, RET_DK)),
        "cd": jnp.broadcast_to(cd[:, None, None], (RET_HEADS, 1, RET_DV)),
    }


def _hgrn_block(q_ref, z_ref, i_ref, g_ref, lb, ng, o_ref, st_ref, start, lanes, consts, *, chunk, block):
    row, causal = consts
    half = chunk // 2
    nc = block // chunk
    nt_dims = (((1,), (1,)), ((), ()))
    tn_dims = (((0,), (0,)), ((), ()))

    def chunk_rows(x, lo):
        return jnp.concatenate([x[c * chunk + lo:c * chunk + lo + 1, :] for c in range(nc)], axis=0)

    def spread(xc):
        return jnp.concatenate(
            [jnp.broadcast_to(xc[c:c + 1, :], (chunk, xc.shape[1])) for c in range(nc)], axis=0)

    r = pl.ds(pl.multiple_of(start, block), block)
    f = lb + (1.0 - lb) * jax.nn.sigmoid(z_ref[r, lanes])
    cum = jnp.log(f)
    shift = 1
    while shift < chunk:
        cum = cum + jnp.where(row >= shift, pltpu.roll(cum, shift, 0), 0.0)
        shift *= 2
    mid_c = chunk_rows(cum, half - 1)
    last_c = chunk_rows(cum, chunk - 1)
    mid = spread(mid_c)
    qh = q_ref[r, lanes].astype(F32) * jnp.exp(cum - mid)
    kh = (1.0 - f) * jnp.exp(mid - cum)
    q_in = (qh * spread(jnp.exp(mid_c))).astype(BF16)
    k_st = (kh * spread(jnp.exp(last_c - mid_c))).astype(BF16)
    dec = jnp.exp(last_c)
    qh = qh.astype(BF16)
    kh = kh.astype(BF16)
    v = i_ref[r, lanes]
    rows = [slice(c * chunk, (c + 1) * chunk) for c in range(nc)]
    upd = [lax.dot_general(v[rc], k_st[rc], tn_dims, preferred_element_type=F32) for rc in rows]
    att = [lax.dot_general(qh[rc], kh[rc], nt_dims, preferred_element_type=F32) for rc in rows]
    intra = [jnp.dot(jnp.where(causal, a, 0.0).astype(BF16), v[rc], preferred_element_type=F32)
             for a, rc in zip(att, rows)]
    st = st_ref[...]
    outs = []
    for c, rc in enumerate(rows):
        inter = lax.dot_general(q_in[rc], st.astype(BF16), nt_dims, preferred_element_type=F32)
        outs.append(intra[c] + inter)
        st = st * dec[c:c + 1, :] + upd[c]
    st_ref[...] = st
    o = jnp.concatenate(outs, axis=0)
    on = o * lax.rsqrt(jnp.mean(o * o, -1, keepdims=True) + RMS_EPS) * ng
    g = g_ref[r, lanes].astype(F32)
    o_ref[r, lanes] = (_silu(g) * on).astype(o_ref.dtype)


def _swa_step(sink_ref, q_ref, k_ref, v_ref, o_ref, j, step, consts):
    lo_kv, lo_o, rel, zero = consts
    CH = SWA_CHUNK
    HD = SWA_HD
    QR = 2 * CH
    KW = 4 * CH
    nt_dims = (((1,), (1,)), ((), ()))
    first = jnp.maximum(2 * j - SWA_WIN_CHUNKS, 0)
    rq = pl.ds(pl.multiple_of(j * QR, QR), QR)
    rk = pl.ds(pl.multiple_of(first * CH, CH), KW)
    kk = k_ref[rk, :]
    vv = v_ref[rk, :]
    kbd = jnp.concatenate([jnp.where(lo_kv, kk, zero), jnp.where(lo_kv, zero, kk)], axis=0)
    vbd = jnp.concatenate([jnp.where(lo_kv, vv, zero), jnp.where(lo_kv, zero, vv)], axis=0)
    valid = lax.bitcast_convert_type(rel + (first - 2 * j + SWA_WIN_CHUNKS), jnp.uint32) <= SWA_WIN_CHUNKS
    scores = [lax.dot_general(q_ref[rq, p * 2 * HD:(p + 1) * 2 * HD], kbd, nt_dims,
                              preferred_element_type=F32) for p in range(SWA_G // 2)]
    pair = step // 2
    head0 = (step % 2) * (SWA_G // 2)
    for p, s in enumerate(scores):
        s = jnp.where(valid, s, -jnp.inf)
        probs, inv = [], []
        for hh in range(2):
            sink = sink_ref[(2 * pair + hh) * SWA_G + head0 + p]
            sh = s[:, hh * KW:(hh + 1) * KW]
            m = jnp.maximum(jnp.max(sh, -1, keepdims=True), sink)
            e = jnp.exp(sh - m)
            den = jnp.sum(e, -1, keepdims=True) + jnp.exp(sink - m)
            probs.append(e.astype(BF16))
            inv.append(1.0 / den)
        o = jnp.dot(jnp.concatenate(probs, axis=1), vbd, preferred_element_type=F32)
        o = o * jnp.where(lo_o, inv[0], inv[1])
        o_ref[rq, p * 2 * HD:(p + 1) * 2 * HD] = o.astype(o_ref.dtype)


def _mixers_kernel(sink_ref, rq_ref, rk_ref, rv_ref, rg_ref, cos_ref, sin_ref, dm_ref, qd_ref, kd_ref, cd_ref,
                   hq_ref, xb_ref, wf_ref, hi_ref, hg_ref, lb_ref, ng_ref, sq_ref, sk_ref, sv_ref,
                   ya_ref, yb_ref, yc_ref,
                   st_r, st_h, hz_ref, *, span, nspans):
    kvh = pl.program_id(1)
    st_r[...] = jnp.zeros_like(st_r)
    st_h[...] = jnp.zeros_like(st_h)
    hz_ref[...] = jnp.dot(xb_ref[...], wf_ref[...].astype(BF16), preferred_element_type=F32)
    ret_tabs = (dm_ref[0], qd_ref[0], kd_ref[0], cd_ref[0])
    lb2 = lb_ref[0]
    ng = ng_ref[...]
    hg_consts = (lax.broadcasted_iota(jnp.int32, (HG_BLOCK, HG_DK), 0) % HG_CHUNK,
                 lax.broadcasted_iota(jnp.int32, (HG_CHUNK, HG_CHUNK), 0)
                 >= lax.broadcasted_iota(jnp.int32, (HG_CHUNK, HG_CHUNK), 1))
    QR, KW, HD = 2 * SWA_CHUNK, 4 * SWA_CHUNK, SWA_HD
    kcol = lax.broadcasted_iota(jnp.int32, (QR, 2 * KW), 1)
    qrow = lax.broadcasted_iota(jnp.int32, (QR, 2 * KW), 0)
    swa_consts = (lax.broadcasted_iota(jnp.int32, (KW, 2 * HD), 1) < HD,
                  lax.broadcasted_iota(jnp.int32, (QR, 2 * HD), 1) < HD,
                  (kcol % KW) // SWA_CHUNK - qrow // SWA_CHUNK,
                  jnp.zeros((KW, 2 * HD), BF16))
    n_swa = span // QR
    n_hg = span // HG_BLOCK
    n_ret = span // RET_BLOCK

    def body(u, carry):
        base = u * span
        for t in range(n_swa):
            _swa_step(sink_ref, sq_ref, sk_ref, sv_ref, yc_ref, u * n_swa + t, kvh, swa_consts)
            if t < 2 * n_hg:
                hh, blk = t % 2, t // 2
                lanes = slice(hh * HG_DK, (hh + 1) * HG_DK)
                _hgrn_block(hq_ref, hz_ref, hi_ref, hg_ref, lb2[:, lanes], ng, yb_ref, st_h.at[hh],
                            base + blk * HG_BLOCK, lanes, hg_consts, chunk=HG_CHUNK, block=HG_BLOCK)
            if t % (n_swa // n_ret) == 0:
                _retention_block(rq_ref, rk_ref, rv_ref, rg_ref, cos_ref, sin_ref, ret_tabs, ya_ref, st_r,
                                 base + (t // (n_swa // n_ret)) * RET_BLOCK, chunk=RET_CHUNK, block=RET_BLOCK)
        return carry

    lax.fori_loop(0, nspans, body, 0)


def token_mixers(proj, hb, w_hf, tabs, lb, norm_g, sinks, B, S):
    T = B * S
    D = hb.shape[1]
    span = min(MIX_SPAN, S)
    C = RET_CHUNK
    hw = 2 * HG_DK
    ret = lambda off, w: (lambda b, h: (b, off // w + h))
    kern = functools.partial(_mixers_kernel, span=span, nspans=S // span)
    out = jax.ShapeDtypeStruct((T, D_MODEL), BF16)
    outspec = pl.BlockSpec((S, RET_DV), lambda b, h: (b, h))
    return pl.pallas_call(
        kern,
        grid=(B, RET_HEADS),
        in_specs=[pl.BlockSpec(memory_space=pltpu.SMEM),
                  pl.BlockSpec((S, RET_DK), ret(OFF_RQ, RET_DK)),
                  pl.BlockSpec((S, RET_DK), ret(OFF_RK, RET_DK)),
                  pl.BlockSpec((S, RET_DV), ret(OFF_RV, RET_DV)),
                  pl.BlockSpec((S, RET_DV), ret(OFF_RG, RET_DV)),
                  pl.BlockSpec((S, RET_DK), lambda b, h: (0, 0)),
                  pl.BlockSpec((S, RET_DK), lambda b, h: (0, 0)),
                  pl.BlockSpec((1, C, C), lambda b, h: (h, 0, 0)),
                  pl.BlockSpec((1, C, RET_DK), lambda b, h: (h, 0, 0)),
                  pl.BlockSpec((1, C, RET_DK), lambda b, h: (h, 0, 0)),
                  pl.BlockSpec((1, 1, RET_DV), lambda b, h: (h, 0, 0)),
                  pl.BlockSpec((S, hw), ret(OFF_HQ, hw)),
                  pl.BlockSpec((S, D), lambda b, h: (b, 0)),
                  pl.BlockSpec((D, hw), lambda b, h: (0, h)),
                  pl.BlockSpec((S, hw), ret(OFF_HI, hw)),
                  pl.BlockSpec((S, hw), ret(OFF_HG, hw)),
                  pl.BlockSpec((1, 1, hw), lambda b, h: (h, 0, 0)),
                  pl.BlockSpec((1, HG_DV), lambda b, h: (0, 0)),
                  pl.BlockSpec((S, 2 * LANES), ret(OFF_SWA, 2 * LANES)),
                  pl.BlockSpec((S, LANES), lambda b, h: (b, OFF_SWA_K // LANES + h // 2)),
                  pl.BlockSpec((S, LANES), lambda b, h: (b, OFF_SWA_V // LANES + h // 2))],
        out_specs=[outspec, outspec, outspec],
        out_shape=[out, out, out],
        scratch_shapes=[pltpu.VMEM((RET_DK, RET_DV), F32), pltpu.VMEM((2, HG_DV, HG_DK), F32),
                        pltpu.VMEM((S, hw), F32)],
        compiler_params=_cparams(("parallel", "parallel")),
        name="token_mixers",
    )(sinks.astype(F32), proj, proj, proj, proj, tabs["cos"], tabs["sin"], tabs["dm"], tabs["qd"], tabs["kd"],
      tabs["cd"], proj, hb, w_hf, proj, proj, lb.reshape(RET_HEADS, 1, hw), norm_g.reshape(1, HG_DV),
      proj, proj, proj)


def _merge_kernel(a_ref, b_ref, c_ref, ga_ref, gb_ref, gc_ref, h_ref, wa_ref, wb_ref, wc_ref, wo_ref,
                  g_ref, beta_ref, ho_ref, aux_ref, *, tm, token_tiles):
    def branch(x_ref, gate_ref, w_ref):
        y = jnp.dot(x_ref[...], w_ref[...], preferred_element_type=F32)
        return jax.nn.sigmoid(gate_ref[...].astype(F32)) * y

    merged = branch(a_ref, ga_ref, wa_ref) + branch(b_ref, gb_ref, wb_ref) + branch(c_ref, gc_ref, wc_ref)
    mix = jnp.dot(merged.astype(BF16), wo_ref[...], preferred_element_type=F32)
    hn = _layer_norm(DN_ALPHA * h_ref[...] + mix, g_ref[...], beta_ref[...])
    ho_ref[...] = hn
    if token_tiles:
        for j in range(SUB):
            aux_ref[pl.ds(j, tm, stride=SUB), :] = hn[:, j * LANES:(j + 1) * LANES]
    else:
        aux_ref[...] = hn.astype(BF16)


def merge_project(ya, yb, yc, proj, h, wa, wb, wc, wo, g, beta, token_tiles, tm=512):
    T, D = h.shape
    tm = min(tm, T)
    row = lambda i: (i, 0)
    const = lambda i: (0, 0)
    wspec = pl.BlockSpec((D, D), const, pipeline_mode=pl.Buffered(1))
    if token_tiles:
        aux_spec = pl.BlockSpec((tm * SUB, LANES), row)
        aux_shape = jax.ShapeDtypeStruct((T * SUB, LANES), F32)
    else:
        aux_spec = pl.BlockSpec((tm, D), row)
        aux_shape = jax.ShapeDtypeStruct((T, D), BF16)
    kern = functools.partial(_merge_kernel, tm=tm, token_tiles=token_tiles)
    return pl.pallas_call(
        kern,
        grid=(T // tm,),
        in_specs=[pl.BlockSpec((tm, D), row), pl.BlockSpec((tm, D), row), pl.BlockSpec((tm, D), row),
                  pl.BlockSpec((tm, D), lambda i: (i, OFF_GA // D_MODEL)),
                  pl.BlockSpec((tm, D), lambda i: (i, OFF_GB // D_MODEL)),
                  pl.BlockSpec((tm, D), lambda i: (i, OFF_GC // D_MODEL)),
                  pl.BlockSpec((tm, D), row),
                  wspec, wspec, wspec, wspec,
                  pl.BlockSpec((1, D), const), pl.BlockSpec((1, D), const)],
        out_specs=[pl.BlockSpec((tm, D), row), aux_spec],
        out_shape=[jax.ShapeDtypeStruct((T, D), F32), aux_shape],
        compiler_params=_cparams(("parallel",)),
        name="merge_project",
    )(ya, yb, yc, proj, proj, proj, h, wa, wb, wc, wo, g.reshape(1, D), beta.reshape(1, D))


def _ffn_kernel(xb_ref, h_ref, wg_ref, wu_ref, wd_ref, g_ref, beta_ref, ho_ref, hob_ref):
    x = xb_ref[...]
    gate = jnp.dot(x, wg_ref[...], preferred_element_type=F32)
    up = jnp.dot(x, wu_ref[...], preferred_element_type=F32)
    a = (_silu(gate) * up).astype(BF16)
    ff = jnp.dot(a, wd_ref[...], preferred_element_type=F32)
    hn = _layer_norm(DN_ALPHA * h_ref[...] + ff, g_ref[...], beta_ref[...])
    ho_ref[...] = hn
    hob_ref[...] = hn.astype(BF16)


def dense_ffn(hb, h, wg, wu, wd, g, beta, tm=512):
    T, D = h.shape
    F = wg.shape[1]
    tm = min(tm, T)
    row = lambda i: (i, 0)
    const = lambda i: (0, 0)
    resident = pl.Buffered(1)
    return pl.pallas_call(
        _ffn_kernel,
        grid=(T // tm,),
        in_specs=[pl.BlockSpec((tm, D), row), pl.BlockSpec((tm, D), row),
                  pl.BlockSpec((D, F), const, pipeline_mode=resident),
                  pl.BlockSpec((D, F), const, pipeline_mode=resident),
                  pl.BlockSpec((F, D), const, pipeline_mode=resident),
                  pl.BlockSpec((1, D), const), pl.BlockSpec((1, D), const)],
        out_specs=[pl.BlockSpec((tm, D), row), pl.BlockSpec((tm, D), row)],
        out_shape=[jax.ShapeDtypeStruct((T, D), F32), jax.ShapeDtypeStruct((T, D), BF16)],
        compiler_params=_cparams(("parallel",)),
        name="dense_ffn",
    )(hb, h, wg, wu, wd, g.reshape(1, D), beta.reshape(1, D))


def _route_top2(h, w_hi, w_lo):
    h_hi = h.astype(BF16)
    h_lo = (h - h_hi.astype(F32)).astype(BF16)
    logits = (jnp.dot(h_hi, w_hi, preferred_element_type=F32) + jnp.dot(h_lo, w_hi, preferred_element_type=F32)
              + jnp.dot(h_hi, w_lo, preferred_element_type=F32))
    lane = lax.broadcasted_iota(jnp.int32, logits.shape, 1)
    l1 = jnp.where(lane < N_EXPERTS, logits, -jnp.inf)
    m1 = jnp.max(l1, -1, keepdims=True)
    i1 = jnp.min(jnp.where(l1 == m1, lane, LANES), -1, keepdims=True)
    l2 = jnp.where(lane == i1, -jnp.inf, l1)
    m2 = jnp.max(l2, -1, keepdims=True)
    i2 = jnp.min(jnp.where(l2 == m2, lane, LANES), -1, keepdims=True)
    e = jnp.exp(m2 - m1)
    w1 = 1.0 / (1.0 + e)
    w2 = e / (1.0 + e)
    return jnp.where(lane == 0, i1.astype(F32),
                     jnp.where(lane == 1, i2.astype(F32),
                               jnp.where(lane == 2, w1, jnp.where(lane == 3, w2, 0.0))))


def _router_kernel(h_ref, w_ref, o_ref):
    o_ref[...] = _route_top2(h_ref[...], w_ref[0], w_ref[1])


def router_top2(h, w_router, tm=512):
    T, D = h.shape
    tm = min(tm, T)
    w_pad = jnp.zeros((D, LANES), F32).at[:, :N_EXPERTS].set(w_router.astype(F32))
    w_hi = w_pad.astype(BF16)
    w_lo = (w_pad - w_hi.astype(F32)).astype(BF16)
    return pl.pallas_call(
        _router_kernel,
        grid=(T // tm,),
        in_specs=[pl.BlockSpec((tm, D), lambda i: (i, 0)), pl.BlockSpec((2, D, LANES), lambda i: (0, 0, 0))],
        out_specs=pl.BlockSpec((tm, LANES), lambda i: (i, 0)),
        out_shape=jax.ShapeDtypeStruct((T, LANES), F32),
        compiler_params=_cparams(("parallel",)),
        name="router_top2",
    )(h, jnp.stack([w_hi, w_lo]))


W_PIECES = 8


def _expert_ffn_kernel(te_ref, nt_ref, src_ref, dst_ref, h_hbm, wg_hbm, wu_hbm, wd_hbm, y_hbm,
                       xbuf, xb_ref, ybuf, y_ref, wg_s, wu_s, wd_s, stage_in, stage_out, gsem, ssem, wsem,
                       *, tm):
    i = pl.program_id(0)
    n_i = pl.num_programs(0)
    active = i < nt_ref[0]
    buf = i % 2
    nxt = jnp.minimum(i + 1, n_i - 1)
    prv = jnp.maximum(i - 1, 0)
    expert = te_ref[i]
    new_expert = jnp.logical_and(active, jnp.logical_or(i == 0, expert != te_ref[prv]))

    def gather_row(tile, b, r):
        tok = src_ref[tile * tm + r]
        pltpu.make_async_copy(h_hbm.at[pl.ds(pl.multiple_of(tok * SUB, SUB), SUB), :],
                              xbuf.at[b, pl.ds(pl.multiple_of(r * SUB, SUB), SUB), :], gsem.at[b]).start()

    def scatter_row(tile, r):
        slot = dst_ref[tile * tm + r]
        pltpu.make_async_copy(ybuf.at[pl.ds(pl.multiple_of(r * SUB, SUB), SUB), :],
                              y_hbm.at[pl.ds(pl.multiple_of(slot * SUB, SUB), SUB), :], ssem).start()

    def wait_gather(b):
        pltpu.make_async_copy(h_hbm.at[pl.ds(0, tm * SUB), :], xbuf.at[b], gsem.at[b]).wait()

    def wait_scatter():
        pltpu.make_async_copy(ybuf, y_hbm.at[pl.ds(0, tm * SUB), :], ssem).wait()

    def looped(n, fn):
        def body(r, carry):
            fn(r)
            return carry
        lax.fori_loop(0, n, body, 0, unroll=8)

    def share_row(r):
        gather_row(nxt, 1 - buf, r)
        scatter_row(prv, r)

    @pl.when(i == 0)
    def _():
        looped(tm, lambda r: gather_row(0, 0, r))
        ybuf[...] = jnp.zeros_like(ybuf)

    @pl.when(new_expert)
    def _():
        pieces = []
        for w_hbm, w_s, stage in ((wg_hbm, wg_s, stage_in), (wu_hbm, wu_s, stage_in), (wd_hbm, wd_s, stage_out)):
            rows = w_s.shape[0] // W_PIECES
            for c in range(W_PIECES):
                pieces.append((w_hbm.at[expert, pl.ds(c * rows, rows), :], w_s.at[pl.ds(c * rows, rows), :], stage))
        copies = [pltpu.make_async_copy(src, stage.at[k % 2], wsem.at[k % 2])
                  for k, (src, _, stage) in enumerate(pieces)]
        copies[0].start()
        for k, (_, dst, stage) in enumerate(pieces):
            if k + 1 < len(pieces):
                copies[k + 1].start()
            copies[k].wait()
            dst[...] = stage[k % 2].astype(BF16)

    wait_gather(buf)
    for j in range(SUB):
        xb_ref[:, j * LANES:(j + 1) * LANES] = xbuf[buf, pl.ds(j, tm, stride=SUB), :].astype(BF16)

    @pl.when(active)
    def _():
        for r in range(tm):
            share_row(r)
        x = xb_ref[...]
        gate = jnp.dot(x, wg_s[...], preferred_element_type=F32)
        up = jnp.dot(x, wu_s[...], preferred_element_type=F32)
        a = (_silu(gate) * up).astype(BF16)
        y_ref[...] = jnp.dot(a, wd_s[...], preferred_element_type=F32)

    @pl.when(jnp.logical_not(active))
    def _():
        looped(tm, share_row)
        y_ref[...] = jnp.zeros_like(y_ref)

    wait_scatter()
    for j in range(SUB):
        ybuf[pl.ds(j, tm, stride=SUB), :] = y_ref[:, j * LANES:(j + 1) * LANES]

    @pl.when(i == n_i - 1)
    def _():
        looped(tm, lambda r: scatter_row(i, r))
        wait_scatter()
        wait_gather(1 - buf)


def expert_ffn(h_tiles, te, nt, src_tok, dst_slot, wg, wu, wd, tm):
    P = src_tok.shape[0]
    D = wg.shape[1]
    F = wg.shape[2]
    kern = functools.partial(_expert_ffn_kernel, tm=tm)
    anyspec = pl.BlockSpec(memory_space=pl.ANY)
    return pl.pallas_call(
        kern,
        grid_spec=pltpu.PrefetchScalarGridSpec(
            num_scalar_prefetch=4,
            grid=(P // tm,),
            in_specs=[anyspec, anyspec, anyspec, anyspec],
            out_specs=anyspec,
            scratch_shapes=[pltpu.VMEM((2, tm * SUB, LANES), F32),
                            pltpu.VMEM((tm, D), BF16),
                            pltpu.VMEM((tm * SUB, LANES), F32),
                            pltpu.VMEM((tm, D), F32),
                            pltpu.VMEM((D, F), BF16),
                            pltpu.VMEM((D, F), BF16),
                            pltpu.VMEM((F, D), BF16),
                            pltpu.VMEM((2, D // W_PIECES, F), F32),
                            pltpu.VMEM((2, F // W_PIECES, D), F32),
                            pltpu.SemaphoreType.DMA((2,)),
                            pltpu.SemaphoreType.DMA(()),
                            pltpu.SemaphoreType.DMA((2,))],
        ),
        out_shape=jax.ShapeDtypeStruct((P * SUB, LANES), F32),
        compiler_params=_cparams(("arbitrary",)),
        name="expert_ffn",
    )(te, nt, src_tok, dst_slot, h_tiles, wg, wu, wd)


def _moe_combine_kernel(y0_ref, y1_ref, r_ref, h_ref, g_ref, beta_ref, ho_ref, hob_ref, *, tm):
    def rows(y_ref):
        return jnp.concatenate([y_ref[pl.ds(j, tm, stride=SUB), :] for j in range(SUB)], axis=1)

    r = r_ref[...]
    ff = rows(y0_ref) * r[:, 2:3] + rows(y1_ref) * r[:, 3:4]
    hn = _layer_norm(DN_ALPHA * h_ref[...] + ff, g_ref[...], beta_ref[...])
    ho_ref[...] = hn
    hob_ref[...] = hn.astype(BF16)


def moe_combine(y_tiles, route, h, g, beta, tm=512):
    T, D = h.shape
    tm = min(tm, T)
    nb = T // tm
    row = lambda i: (i, 0)
    const = lambda i: (0, 0)
    kern = functools.partial(_moe_combine_kernel, tm=tm)
    return pl.pallas_call(
        kern,
        grid=(nb,),
        in_specs=[pl.BlockSpec((tm * SUB, LANES), lambda i: (i, 0)),
                  pl.BlockSpec((tm * SUB, LANES), lambda i: (nb + i, 0)),
                  pl.BlockSpec((tm, LANES), row),
                  pl.BlockSpec((tm, D), row),
                  pl.BlockSpec((1, D), const), pl.BlockSpec((1, D), const)],
        out_specs=[pl.BlockSpec((tm, D), row), pl.BlockSpec((tm, D), row)],
        out_shape=[jax.ShapeDtypeStruct((T, D), F32), jax.ShapeDtypeStruct((T, D), BF16)],
        compiler_params=_cparams(("parallel",)),
        name="moe_combine",
    )(y_tiles, y_tiles, route, h, g.reshape(1, D), beta.reshape(1, D))


def moe_plan(route, T, tm):
    i32 = jnp.int32
    n_slots = TOP_K * T
    experts = jnp.arange(N_EXPERTS, dtype=i32)
    e_flat = route[:, 0:TOP_K].astype(i32).T.reshape(-1)
    order = jnp.argsort(e_flat, stable=True).astype(i32)
    sizes = jnp.sum((e_flat[:, None] == experts[None, :]).astype(i32), axis=0)
    padded = ((sizes + tm - 1) // tm) * tm
    gend = jnp.cumsum(padded)
    gstart = gend - padded
    cstart = jnp.cumsum(sizes) - sizes
    n_rows = n_slots + N_EXPERTS * tm
    row = jnp.arange(n_rows, dtype=i32)
    group = jnp.sum((row[:, None] >= gend[None, :]).astype(i32), axis=1)
    onehot = (jnp.minimum(group, N_EXPERTS - 1)[:, None] == experts[None, :]).astype(i32)
    look = lambda table: jnp.sum(onehot * table[None, :], axis=1)
    local = row - look(gstart)
    real = jnp.logical_and(group < N_EXPERTS, local < look(sizes))
    slot = order[jnp.clip(look(cstart) + local, 0, n_slots - 1)]
    pad_rank = jnp.where(group < N_EXPERTS, look(gstart - cstart) + local - look(sizes), row - n_slots)
    dst_slot = jnp.where(real, slot, n_slots + pad_rank)
    src_tok = jnp.where(real, slot % T, 0)
    n_tiles = n_rows // tm
    nt = (gend[-1] // tm).astype(i32)
    tile_start = jnp.arange(n_tiles, dtype=i32) * tm
    te = jnp.minimum(jnp.sum((tile_start[:, None] >= gend[None, :]).astype(i32), axis=1), N_EXPERTS - 1)
    te = jnp.where(jnp.arange(n_tiles) < nt, te, te[jnp.maximum(nt - 1, 0)])
    return src_tok.astype(i32), dst_slot.astype(i32), te.astype(i32), nt.reshape(1)


def moe_ffn(h, h_tiles, w_router, wg, wu, wd, g, beta, tm=512):
    T, D = h.shape
    route = router_top2(h, w_router)
    src_tok, dst_slot, te, nt = moe_plan(route, T, tm)
    y_tiles = expert_ffn(h_tiles, te, nt, src_tok, dst_slot, wg, wu, wd, tm)
    return moe_combine(y_tiles, route, h, g, beta)


def permute_w_in(w_in, layer):
    widths = [RET_QK, RET_QK, RET_V, RET_V, HG_K, HG_K, HG_V, HG_V, SWA_Q, SWA_KV, SWA_KV,
              D_MODEL, D_MODEL, D_MODEL]
    starts = np.concatenate([[0], np.cumsum(widths)])
    names = ["rq", "rk", "rv", "rg", "hq", "hf", "hi", "hg", "sq", "sk", "sv", "ga", "gb", "gc"]
    off = dict(zip(names, starts[:-1].tolist()))
    wid = dict(zip(names, widths))

    def cols(name, lo=0, n=None):
        a = off[name] + lo
        return w_in[layer, :, a:a + (wid[name] if n is None else n)]

    pieces = [cols(n) for n in ("rq", "rk", "rv", "rg", "hq", "hi", "hg", "ga", "gb", "gc")]
    pieces += [cols("sq", head * SWA_HD, SWA_HD) * SWA_HD ** -0.5 for head in SWA_HEAD_ORDER]
    pieces += [cols("sk"), cols("sv")]
    main = jnp.concatenate(pieces, axis=1).astype(BF16)
    return main, cols("hf")


def swa_rows_paired(w_out):
    return jnp.concatenate([w_out[head * SWA_HD:(head + 1) * SWA_HD, :] for head in SWA_HEAD_ORDER], axis=0)


def kernel(x, ln_in_g, ln_in_b, w_in, ret_w_out, hgrn_lower_bounds, hgrn_norm_g, hgrn_w_out, swa_sinks,
           swa_w_out, w_o, ln_mix_g, ln_mix_b, ffn_w_gate, ffn_w_up, ffn_w_down, moe_router, moe_w_gate,
           moe_w_up, moe_w_down, ln_ffn_g, ln_ffn_b):
    B, S, D = x.shape
    T = B * S
    assert D == D_MODEL and S % RET_CHUNK == 0 and S % (2 * SWA_CHUNK) == 0 and S >= 4 * SWA_CHUNK

    lb_all = jnp.cumsum(jax.nn.softmax(hgrn_lower_bounds.astype(F32), axis=0), axis=0)
    lb_all = lb_all - lb_all[0]
    tabs = retention_tables(S)

    h, hb = ln_in(x.reshape(T, D), ln_in_g, ln_in_b)
    for layer in range(DEPTH):
        w_main, w_hf = permute_w_in(w_in, layer)
        proj = matmul(hb, w_main, BF16, tn=N_PROJ // 4, name="in_proj")
        ya, yb, yc = token_mixers(proj, hb, w_hf, tabs, lb_all[layer], hgrn_norm_g[layer], swa_sinks[layer],
                                  B, S)
        dense = layer % 2 == 0
        h, aux = merge_project(ya, yb, yc, proj, h,
                               ret_w_out[layer].astype(BF16), hgrn_w_out[layer].astype(BF16),
                               swa_rows_paired(swa_w_out[layer]).astype(BF16), w_o[layer].astype(BF16),
                               ln_mix_g[layer], ln_mix_b[layer], token_tiles=not dense)

        j = layer // 2
        if dense:
            h, hb = dense_ffn(aux, h, ffn_w_gate[j].astype(BF16), ffn_w_up[j].astype(BF16),
                              ffn_w_down[j].astype(BF16), ln_ffn_g[layer], ln_ffn_b[layer])
        else:
            h, hb = moe_ffn(h, aux, moe_router[j], moe_w_gate[j], moe_w_up[j], moe_w_down[j],
                            ln_ffn_g[layer], ln_ffn_b[layer])
    return h.reshape(B, S, D)
```

```python
import functools

import numpy as np
import jax
import jax.numpy as jnp
from jax import lax
from jax.experimental import pallas as pl
from jax.experimental.pallas import tpu as pltpu

F32 = jnp.float32
BF16 = jnp.bfloat16

D_MODEL = 1024
RET_HEADS, RET_DK, RET_DV = 4, 128, 256
RET_QK, RET_V = RET_HEADS * RET_DK, RET_HEADS * RET_DV
ROPE_BASE = 10000.0
HG_HEADS, HG_DK, HG_DV = 8, 128, 128
HG_K, HG_V = HG_HEADS * HG_DK, HG_HEADS * HG_DV
SWA_HQ, SWA_HKV, SWA_HD = 16, 4, 64
SWA_G = SWA_HQ // SWA_HKV
SWA_Q, SWA_KV = SWA_HQ * SWA_HD, SWA_HKV * SWA_HD
SWA_CHUNK = 64
SWA_WIN_CHUNKS = 2
N_EXPERTS, TOP_K = 8, 2
DEPTH = 2
LN_EPS, RMS_EPS = 1e-5, 1e-6
DN_ALPHA = (2.0 * DEPTH) ** 0.25

VMEM_LIMIT_BYTES = 56 * 1024 * 1024
LANES = 128
SUB = 8

RET_CHUNK = 128
RET_BLOCK = 512
HG_CHUNK = 32
HG_BLOCK = 256
MIX_SPAN = 1024

OFF_RQ, OFF_RK, OFF_RV, OFF_RG = 0, 512, 1024, 2048
OFF_HQ, OFF_HI, OFF_HG = 3072, 4096, 5120
OFF_GA, OFF_GB, OFF_GC = 6144, 7168, 8192
OFF_SWA = 9216
OFF_SWA_K = OFF_SWA + SWA_Q
OFF_SWA_V = OFF_SWA_K + SWA_KV
N_PROJ = OFF_SWA_V + SWA_KV
SWA_HEAD_ORDER = [(2 * m + half) * SWA_G + g for m in range(SWA_HKV // 2) for g in range(SWA_G) for half in (0, 1)]


def _cparams(sem, vmem=VMEM_LIMIT_BYTES):
    return pltpu.CompilerParams(dimension_semantics=sem, vmem_limit_bytes=vmem)


def _layer_norm(x, g, b):
    mu = jnp.mean(x, -1, keepdims=True)
    xc = x - mu
    var = jnp.mean(xc * xc, -1, keepdims=True)
    return xc * lax.rsqrt(var + LN_EPS) * g + b


def _silu(x):
    return x * jax.nn.sigmoid(x)


def _matmul_kernel(x_ref, w_ref, o_ref):
    o_ref[...] = jnp.dot(x_ref[...], w_ref[...], preferred_element_type=F32).astype(o_ref.dtype)


def matmul(x, w, out_dtype, tm=1024, tn=512, name="matmul"):
    T, K = x.shape
    N = w.shape[1]
    tm = min(tm, T)
    return pl.pallas_call(
        _matmul_kernel,
        grid=(T // tm, N // tn),
        in_specs=[pl.BlockSpec((tm, K), lambda i, j: (i, 0)),
                  pl.BlockSpec((K, tn), lambda i, j: (0, j))],
        out_specs=pl.BlockSpec((tm, tn), lambda i, j: (i, j)),
        out_shape=jax.ShapeDtypeStruct((T, N), out_dtype),
        compiler_params=_cparams(("parallel", "parallel")),
        name=name,
    )(x, w)


def _ln_matmul_kernel(x_ref, g_ref, b_ref, w_ref, o_ref, h_ref, hb_ref):
    @pl.when(pl.program_id(1) == 0)
    def _():
        h = _layer_norm(x_ref[...], g_ref[...], b_ref[...])
        h_ref[...] = h
        hb_ref[...] = h.astype(BF16)

    o_ref[...] = jnp.dot(hb_ref[...], w_ref[...], preferred_element_type=F32).astype(o_ref.dtype)


def ln_matmul(x2, g, b, w, tm=1024, tn=512, name="ln_matmul"):
    T, D = x2.shape
    N = w.shape[1]
    tm = min(tm, T)
    row = lambda i, j: (i, 0)
    const = lambda i, j: (0, 0)
    return pl.pallas_call(
        _ln_matmul_kernel,
        grid=(T // tm, N // tn),
        in_specs=[pl.BlockSpec((tm, D), row), pl.BlockSpec((1, D), const), pl.BlockSpec((1, D), const),
                  pl.BlockSpec((D, tn), lambda i, j: (0, j))],
        out_specs=[pl.BlockSpec((tm, tn), lambda i, j: (i, j)), pl.BlockSpec((tm, D), row),
                   pl.BlockSpec((tm, D), row)],
        out_shape=[jax.ShapeDtypeStruct((T, N), BF16), jax.ShapeDtypeStruct((T, D), F32),
                   jax.ShapeDtypeStruct((T, D), BF16)],
        compiler_params=_cparams(("parallel", "arbitrary")),
        name=name,
    )(x2, g.reshape(1, D), b.reshape(1, D), w)


def _retention_block(q_ref, k_ref, v_ref, g_ref, cos_ref, sin_ref, tabs, o_ref, st_ref, start, *, chunk, block):
    dm, qd, kd, cd = tabs
    nc = block // chunk
    nt_dims = (((1,), (1,)), ((), ()))
    tn_dims = (((0,), (0,)), ((), ()))
    r = pl.ds(pl.multiple_of(start, block), block)
    cos = cos_ref[r, :]
    sin = sin_ref[r, :]
    q = q_ref[r, :].astype(F32)
    k = k_ref[r, :].astype(F32)
    qr = q * cos + pltpu.roll(q, RET_DK // 2, 1) * sin
    kr = k * cos + pltpu.roll(k, RET_DK // 2, 1) * sin
    v = v_ref[r, :]
    rows = [slice(c * chunk, (c + 1) * chunk) for c in range(nc)]
    upd = [lax.dot_general((kr[rc] * kd).astype(BF16), v[rc], tn_dims, preferred_element_type=F32)
           for rc in rows]
    att = [lax.dot_general(qr[rc].astype(BF16), kr[rc].astype(BF16), nt_dims,
                           preferred_element_type=F32) * dm for rc in rows]
    intra = [jnp.dot(a.astype(BF16), v[rc], preferred_element_type=F32) for a, rc in zip(att, rows)]
    q_in = [(qr[rc] * qd).astype(BF16) for rc in rows]
    st = st_ref[...]
    outs = []
    for c in range(nc):
        outs.append(intra[c] + jnp.dot(q_in[c], st.astype(BF16), preferred_element_type=F32))
        st = st * cd + upd[c]
    st_ref[...] = st
    o = jnp.concatenate(outs, axis=0)
    on = o * lax.rsqrt(jnp.mean(o * o, -1, keepdims=True) + RMS_EPS)
    g = g_ref[r, :].astype(F32)
    o_ref[r, :] = (_silu(g) * on).astype(o_ref.dtype)


def retention_tables(S):
    C = RET_CHUNK
    half = RET_DK // 2
    pos = jnp.arange(S, dtype=F32)
    inv = 1.0 / (ROPE_BASE ** jnp.linspace(0.0, 1.0, half, dtype=F32))
    ang = pos[:, None] * inv[None, :]
    cos, sin = jnp.cos(ang), jnp.sin(ang)
    log_gamma = jnp.log(1.0 - 2.0 ** (-5.0 - jnp.arange(RET_HEADS, dtype=F32)))
    idx = jnp.arange(C, dtype=F32)
    diff = idx[:, None] - idx[None, :]
    decay = jnp.where(diff[None] >= 0,
                      jnp.exp(jnp.maximum(diff, 0.0)[None] * log_gamma[:, None, None]), 0.0)
    scale = RET_DK ** -0.5
    qd = jnp.exp((idx + 1.0)[None, :] * log_gamma[:, None])
    kd = jnp.exp((C - 1.0 - idx)[None, :] * log_gamma[:, None]) * scale
    cd = jnp.exp(C * log_gamma)
    return {
        "cos": jnp.concatenate([cos, cos], axis=1),
        "sin": jnp.concatenate([-sin, sin], axis=1),
        "dm": decay * scale,
        "qd": jnp.broadcast_to(qd[:, :, None], (RET_HEADS, C, RET_DK)),
        "kd": jnp.broadcast_to(kd[:, :, None], (RET_HEADS, C, RET_DK)),
        "cd": jnp.broadcast_to(cd[:, None, None], (RET_HEADS, 1, RET_DV)),
    }


def _hgrn_block(q_ref, z_ref, i_ref, g_ref, lb, ng, o_ref, st_ref, start, lanes, consts, *, chunk, block):
    row, causal = consts
    half = chunk // 2
    nc = block // chunk
    nt_dims = (((1,), (1,)), ((), ()))
    tn_dims = (((0,), (0,)), ((), ()))

    def chunk_rows(x, lo):
        return jnp.concatenate([x[c * chunk + lo:c * chunk + lo + 1, :] for c in range(nc)], axis=0)

    def spread(xc):
        return jnp.concatenate(
            [jnp.broadcast_to(xc[c:c + 1, :], (chunk, xc.shape[1])) for c in range(nc)], axis=0)

    r = pl.ds(pl.multiple_of(start, block), block)
    f = lb + (1.0 - lb) * jax.nn.sigmoid(z_ref[r, lanes])
    cum = jnp.log(f)
    shift = 1
    while shift < chunk:
        cum = cum + jnp.where(row >= shift, pltpu.roll(cum, shift, 0), 0.0)
        shift *= 2
    mid_c = chunk_rows(cum, half - 1)
    last_c = chunk_rows(cum, chunk - 1)
    mid = spread(mid_c)
    qh = q_ref[r, lanes].astype(F32) * jnp.exp(cum - mid)
    kh = (1.0 - f) * jnp.exp(mid - cum)
    q_in = (qh * spread(jnp.exp(mid_c))).astype(BF16)
    k_st = (kh * spread(jnp.exp(last_c - mid_c))).astype(BF16)
    dec = jnp.exp(last_c)
    qh = qh.astype(BF16)
    kh = kh.astype(BF16)
    v = i_ref[r, lanes]
    rows = [slice(c * chunk, (c + 1) * chunk) for c in range(nc)]
    upd = [lax.dot_general(v[rc], k_st[rc], tn_dims, preferred_element_type=F32) for rc in rows]
    att = [lax.dot_general(qh[rc], kh[rc], nt_dims, preferred_element_type=F32) for rc in rows]
    intra = [jnp.dot(jnp.where(causal, a, 0.0).astype(BF16), v[rc], preferred_element_type=F32)
             for a, rc in zip(att, rows)]
    st = st_ref[...]
    outs = []
    for c, rc in enumerate(rows):
        inter = lax.dot_general(q_in[rc], st.astype(BF16), nt_dims, preferred_element_type=F32)
        outs.append(intra[c] + inter)
        st = st * dec[c:c + 1, :] + upd[c]
    st_ref[...] = st
    o = jnp.concatenate(outs, axis=0)
    on = o * lax.rsqrt(jnp.mean(o * o, -1, keepdims=True) + RMS_EPS) * ng
    g = g_ref[r, lanes].astype(F32)
    o_ref[r, lanes] = (_silu(g) * on).astype(o_ref.dtype)


def _swa_step(sink_ref, q_ref, k_ref, v_ref, o_ref, j, step, consts):
    lo_kv, lo_o, rel, zero = consts
    CH = SWA_CHUNK
    HD = SWA_HD
    QR = 2 * CH
    KW = 4 * CH
    nt_dims = (((1,), (1,)), ((), ()))
    first = jnp.maximum(2 * j - SWA_WIN_CHUNKS, 0)
    rq = pl.ds(pl.multiple_of(j * QR, QR), QR)
    rk = pl.ds(pl.multiple_of(first * CH, CH), KW)
    kk = k_ref[rk, :]
    vv = v_ref[rk, :]
    kbd = jnp.concatenate([jnp.where(lo_kv, kk, zero), jnp.where(lo_kv, zero, kk)], axis=0)
    vbd = jnp.concatenate([jnp.where(lo_kv, vv, zero), jnp.where(lo_kv, zero, vv)], axis=0)
    valid = lax.bitcast_convert_type(rel + (first - 2 * j + SWA_WIN_CHUNKS), jnp.uint32) <= SWA_WIN_CHUNKS
    scores = [lax.dot_general(q_ref[rq, p * 2 * HD:(p + 1) * 2 * HD], kbd, nt_dims,
                              preferred_element_type=F32) for p in range(SWA_G // 2)]
    pair = step // 2
    head0 = (step % 2) * (SWA_G // 2)
    for p, s in enumerate(scores):
        s = jnp.where(valid, s, -jnp.inf)
        probs, inv = [], []
        for hh in range(2):
            sink = sink_ref[(2 * pair + hh) * SWA_G + head0 + p]
            sh = s[:, hh * KW:(hh + 1) * KW]
            m = jnp.maximum(jnp.max(sh, -1, keepdims=True), sink)
            e = jnp.exp(sh - m)
            den = jnp.sum(e, -1, keepdims=True) + jnp.exp(sink - m)
            probs.append(e.astype(BF16))
            inv.append(1.0 / den)
        o = jnp.dot(jnp.concatenate(probs, axis=1), vbd, preferred_element_type=F32)
        o = o * jnp.where(lo_o, inv[0], inv[1])
        o_ref[rq, p * 2 * HD:(p + 1) * 2 * HD] = o.astype(o_ref.dtype)


def _mixers_kernel(sink_ref, rq_ref, rk_ref, rv_ref, rg_ref, cos_ref, sin_ref, dm_ref, qd_ref, kd_ref, cd_ref,
                   hq_ref, xb_ref, wf_ref, hi_ref, hg_ref, lb_ref, ng_ref, sq_ref, sk_ref, sv_ref,
                   ya_ref, yb_ref, yc_ref,
                   st_r, st_h, hz_ref, *, span, nspans):
    kvh = pl.program_id(1)
    st_r[...] = jnp.zeros_like(st_r)
    st_h[...] = jnp.zeros_like(st_h)
    hz_ref[...] = jnp.dot(xb_ref[...], wf_ref[...].astype(BF16), preferred_element_type=F32)
    ret_tabs = (dm_ref[0], qd_ref[0], kd_ref[0], cd_ref[0])
    lb2 = lb_ref[0]
    ng = ng_ref[...]
    hg_consts = (lax.broadcasted_iota(jnp.int32, (HG_BLOCK, HG_DK), 0) % HG_CHUNK,
                 lax.broadcasted_iota(jnp.int32, (HG_CHUNK, HG_CHUNK), 0)
                 >= lax.broadcasted_iota(jnp.int32, (HG_CHUNK, HG_CHUNK), 1))
    QR, KW, HD = 2 * SWA_CHUNK, 4 * SWA_CHUNK, SWA_HD
    kcol = lax.broadcasted_iota(jnp.int32, (QR, 2 * KW), 1)
    qrow = lax.broadcasted_iota(jnp.int32, (QR, 2 * KW), 0)
    swa_consts = (lax.broadcasted_iota(jnp.int32, (KW, 2 * HD), 1) < HD,
                  lax.broadcasted_iota(jnp.int32, (QR, 2 * HD), 1) < HD,
                  (kcol % KW) // SWA_CHUNK - qrow // SWA_CHUNK,
                  jnp.zeros((KW, 2 * HD), BF16))
    n_swa = span // QR
    n_hg = span // HG_BLOCK
    n_ret = span // RET_BLOCK

    def body(u, carry):
        base = u * span
        for t in range(n_swa):
            _swa_step(sink_ref, sq_ref, sk_ref, sv_ref, yc_ref, u * n_swa + t, kvh, swa_consts)
            if t < 2 * n_hg:
                hh, blk = t % 2, t // 2
                lanes = slice(hh * HG_DK, (hh + 1) * HG_DK)
                _hgrn_block(hq_ref, hz_ref, hi_ref, hg_ref, lb2[:, lanes], ng, yb_ref, st_h.at[hh],
                            base + blk * HG_BLOCK, lanes, hg_consts, chunk=HG_CHUNK, block=HG_BLOCK)
            if t % (n_swa // n_ret) == 0:
                _retention_block(rq_ref, rk_ref, rv_ref, rg_ref, cos_ref, sin_ref, ret_tabs, ya_ref, st_r,
                                 base + (t // (n_swa // n_ret)) * RET_BLOCK, chunk=RET_CHUNK, block=RET_BLOCK)
        return carry

    lax.fori_loop(0, nspans, body, 0)


def token_mixers(proj, hb, w_hf, tabs, lb, norm_g, sinks, B, S):
    T = B * S
    D = hb.shape[1]
    span = min(MIX_SPAN, S)
    C = RET_CHUNK
    hw = 2 * HG_DK
    ret = lambda off, w: (lambda b, h: (b, off // w + h))
    kern = functools.partial(_mixers_kernel, span=span, nspans=S // span)
    out = jax.ShapeDtypeStruct((T, D_MODEL), BF16)
    outspec = pl.BlockSpec((S, RET_DV), lambda b, h: (b, h))
    return pl.pallas_call(
        kern,
        grid=(B, RET_HEADS),
        in_specs=[pl.BlockSpec(memory_space=pltpu.SMEM),
                  pl.BlockSpec((S, RET_DK), ret(OFF_RQ, RET_DK)),
                  pl.BlockSpec((S, RET_DK), ret(OFF_RK, RET_DK)),
                  pl.BlockSpec((S, RET_DV), ret(OFF_RV, RET_DV)),
                  pl.BlockSpec((S, RET_DV), ret(OFF_RG, RET_DV)),
                  pl.BlockSpec((S, RET_DK), lambda b, h: (0, 0)),
                  pl.BlockSpec((S, RET_DK), lambda b, h: (0, 0)),
                  pl.BlockSpec((1, C, C), lambda b, h: (h, 0, 0)),
                  pl.BlockSpec((1, C, RET_DK), lambda b, h: (h, 0, 0)),
                  pl.BlockSpec((1, C, RET_DK), lambda b, h: (h, 0, 0)),
                  pl.BlockSpec((1, 1, RET_DV), lambda b, h: (h, 0, 0)),
                  pl.BlockSpec((S, hw), ret(OFF_HQ, hw)),
                  pl.BlockSpec((S, D), lambda b, h: (b, 0)),
                  pl.BlockSpec((D, hw), lambda b, h: (0, h)),
                  pl.BlockSpec((S, hw), ret(OFF_HI, hw)),
                  pl.BlockSpec((S, hw), ret(OFF_HG, hw)),
                  pl.BlockSpec((1, 1, hw), lambda b, h: (h, 0, 0)),
                  pl.BlockSpec((1, HG_DV), lambda b, h: (0, 0)),
                  pl.BlockSpec((S, 2 * LANES), ret(OFF_SWA, 2 * LANES)),
                  pl.BlockSpec((S, LANES), lambda b, h: (b, OFF_SWA_K // LANES + h // 2)),
                  pl.BlockSpec((S, LANES), lambda b, h: (b, OFF_SWA_V // LANES + h // 2))],
        out_specs=[outspec, outspec, outspec],
        out_shape=[out, out, out],
        scratch_shapes=[pltpu.VMEM((RET_DK, RET_DV), F32), pltpu.VMEM((2, HG_DV, HG_DK), F32),
                        pltpu.VMEM((S, hw), F32)],
        compiler_params=_cparams(("parallel", "parallel")),
        name="token_mixers",
    )(sinks.astype(F32), proj, proj, proj, proj, tabs["cos"], tabs["sin"], tabs["dm"], tabs["qd"], tabs["kd"],
      tabs["cd"], proj, hb, w_hf, proj, proj, lb.reshape(RET_HEADS, 1, hw), norm_g.reshape(1, HG_DV),
      proj, proj, proj)


def _merge_kernel(a_ref, b_ref, c_ref, ga_ref, gb_ref, gc_ref, h_ref, wa_ref, wb_ref, wc_ref, wo_ref,
                  g_ref, beta_ref, ho_ref, aux_ref, *, tm, token_tiles):
    def branch(x_ref, gate_ref, w_ref):
        y = jnp.dot(x_ref[...], w_ref[...], preferred_element_type=F32)
        return jax.nn.sigmoid(gate_ref[...].astype(F32)) * y

    merged = branch(a_ref, ga_ref, wa_ref) + branch(b_ref, gb_ref, wb_ref) + branch(c_ref, gc_ref, wc_ref)
    mix = jnp.dot(merged.astype(BF16), wo_ref[...], preferred_element_type=F32)
    hn = _layer_norm(DN_ALPHA * h_ref[...] + mix, g_ref[...], beta_ref[...])
    ho_ref[...] = hn
    if token_tiles:
        for j in range(SUB):
            aux_ref[pl.ds(j, tm, stride=SUB), :] = hn[:, j * LANES:(j + 1) * LANES]
    else:
        aux_ref[...] = hn.astype(BF16)


def merge_project(ya, yb, yc, proj, h, wa, wb, wc, wo, g, beta, token_tiles, tm=512):
    T, D = h.shape
    tm = min(tm, T)
    row = lambda i: (i, 0)
    const = lambda i: (0, 0)
    wspec = pl.BlockSpec((D, D), const, pipeline_mode=pl.Buffered(1))
    if token_tiles:
        aux_spec = pl.BlockSpec((tm * SUB, LANES), row)
        aux_shape = jax.ShapeDtypeStruct((T * SUB, LANES), F32)
    else:
        aux_spec = pl.BlockSpec((tm, D), row)
        aux_shape = jax.ShapeDtypeStruct((T, D), BF16)
    kern = functools.partial(_merge_kernel, tm=tm, token_tiles=token_tiles)
    return pl.pallas_call(
        kern,
        grid=(T // tm,),
        in_specs=[pl.BlockSpec((tm, D), row), pl.BlockSpec((tm, D), row), pl.BlockSpec((tm, D), row),
                  pl.BlockSpec((tm, D), lambda i: (i, OFF_GA // D_MODEL)),
                  pl.BlockSpec((tm, D), lambda i: (i, OFF_GB // D_MODEL)),
                  pl.BlockSpec((tm, D), lambda i: (i, OFF_GC // D_MODEL)),
                  pl.BlockSpec((tm, D), row),
                  wspec, wspec, wspec, wspec,
                  pl.BlockSpec((1, D), const), pl.BlockSpec((1, D), const)],
        out_specs=[pl.BlockSpec((tm, D), row), aux_spec],
        out_shape=[jax.ShapeDtypeStruct((T, D), F32), aux_shape],
        compiler_params=_cparams(("parallel",)),
        name="merge_project",
    )(ya, yb, yc, proj, proj, proj, h, wa, wb, wc, wo, g.reshape(1, D), beta.reshape(1, D))


def _ffn_kernel(xb_ref, h_ref, wg_ref, wu_ref, wd_ref, g_ref, beta_ref, ho_ref, hob_ref):
    x = xb_ref[...]
    gate = jnp.dot(x, wg_ref[...], preferred_element_type=F32)
    up = jnp.dot(x, wu_ref[...], preferred_element_type=F32)
    a = (_silu(gate) * up).astype(BF16)
    ff = jnp.dot(a, wd_ref[...], preferred_element_type=F32)
    hn = _layer_norm(DN_ALPHA * h_ref[...] + ff, g_ref[...], beta_ref[...])
    ho_ref[...] = hn
    hob_ref[...] = hn.astype(BF16)


def dense_ffn(hb, h, wg, wu, wd, g, beta, tm=512):
    T, D = h.shape
    F = wg.shape[1]
    tm = min(tm, T)
    row = lambda i: (i, 0)
    const = lambda i: (0, 0)
    resident = pl.Buffered(1)
    return pl.pallas_call(
        _ffn_kernel,
        grid=(T // tm,),
        in_specs=[pl.BlockSpec((tm, D), row), pl.BlockSpec((tm, D), row),
                  pl.BlockSpec((D, F), const, pipeline_mode=resident),
                  pl.BlockSpec((D, F), const, pipeline_mode=resident),
                  pl.BlockSpec((F, D), const, pipeline_mode=resident),
                  pl.BlockSpec((1, D), const), pl.BlockSpec((1, D), const)],
        out_specs=[pl.BlockSpec((tm, D), row), pl.BlockSpec((tm, D), row)],
        out_shape=[jax.ShapeDtypeStruct((T, D), F32), jax.ShapeDtypeStruct((T, D), BF16)],
        compiler_params=_cparams(("parallel",)),
        name="dense_ffn",
    )(hb, h, wg, wu, wd, g.reshape(1, D), beta.reshape(1, D))


def _route_top2(h, w_hi, w_lo):
    h_hi = h.astype(BF16)
    h_lo = (h - h_hi.astype(F32)).astype(BF16)
    logits = (jnp.dot(h_hi, w_hi, preferred_element_type=F32) + jnp.dot(h_lo, w_hi, preferred_element_type=F32)
              + jnp.dot(h_hi, w_lo, preferred_element_type=F32))
    lane = lax.broadcasted_iota(jnp.int32, logits.shape, 1)
    l1 = jnp.where(lane < N_EXPERTS, logits, -jnp.inf)
    m1 = jnp.max(l1, -1, keepdims=True)
    i1 = jnp.min(jnp.where(l1 == m1, lane, LANES), -1, keepdims=True)
    l2 = jnp.where(lane == i1, -jnp.inf, l1)
    m2 = jnp.max(l2, -1, keepdims=True)
    i2 = jnp.min(jnp.where(l2 == m2, lane, LANES), -1, keepdims=True)
    e = jnp.exp(m2 - m1)
    w1 = 1.0 / (1.0 + e)
    w2 = e / (1.0 + e)
    return jnp.where(lane == 0, i1.astype(F32),
                     jnp.where(lane == 1, i2.astype(F32),
                               jnp.where(lane == 2, w1, jnp.where(lane == 3, w2, 0.0))))


def _router_kernel(h_ref, w_ref, o_ref):
    o_ref[...] = _route_top2(h_ref[...], w_ref[0], w_ref[1])


def router_top2(h, w_router, tm=512):
    T, D = h.shape
    tm = min(tm, T)
    w_pad = jnp.zeros((D, LANES), F32).at[:, :N_EXPERTS].set(w_router.astype(F32))
    w_hi = w_pad.astype(BF16)
    w_lo = (w_pad - w_hi.astype(F32)).astype(BF16)
    return pl.pallas_call(
        _router_kernel,
        grid=(T // tm,),
        in_specs=[pl.BlockSpec((tm, D), lambda i: (i, 0)), pl.BlockSpec((2, D, LANES), lambda i: (0, 0, 0))],
        out_specs=pl.BlockSpec((tm, LANES), lambda i: (i, 0)),
        out_shape=jax.ShapeDtypeStruct((T, LANES), F32),
        compiler_params=_cparams(("parallel",)),
        name="router_top2",
    )(h, jnp.stack([w_hi, w_lo]))


W_PIECES = 8


def _expert_ffn_kernel(te_ref, nt_ref, src_ref, dst_ref, h_hbm, wg_hbm, wu_hbm, wd_hbm, y_hbm,
                       xbuf, xb_ref, ybuf, y_ref, wg_s, wu_s, wd_s, stage_in, stage_out, gsem, ssem, wsem,
                       *, tm):
    i = pl.program_id(0)
    n_i = pl.num_programs(0)
    active = i < nt_ref[0]
    buf = i % 2
    nxt = jnp.minimum(i + 1, n_i - 1)
    prv = jnp.maximum(i - 1, 0)
    expert = te_ref[i]
    new_expert = jnp.logical_and(active, jnp.logical_or(i == 0, expert != te_ref[prv]))

    def gather_row(tile, b, r):
        tok = src_ref[tile * tm + r]
        pltpu.make_async_copy(h_hbm.at[pl.ds(pl.multiple_of(tok * SUB, SUB), SUB), :],
                              xbuf.at[b, pl.ds(pl.multiple_of(r * SUB, SUB), SUB), :], gsem.at[b]).start()

    def scatter_row(tile, r):
        slot = dst_ref[tile * tm + r]
        pltpu.make_async_copy(ybuf.at[pl.ds(pl.multiple_of(r * SUB, SUB), SUB), :],
                              y_hbm.at[pl.ds(pl.multiple_of(slot * SUB, SUB), SUB), :], ssem).start()

    def wait_gather(b):
        pltpu.make_async_copy(h_hbm.at[pl.ds(0, tm * SUB), :], xbuf.at[b], gsem.at[b]).wait()

    def wait_scatter():
        pltpu.make_async_copy(ybuf, y_hbm.at[pl.ds(0, tm * SUB), :], ssem).wait()

    def looped(n, fn):
        def body(r, carry):
            fn(r)
            return carry
        lax.fori_loop(0, n, body, 0, unroll=8)

    def share_row(r):
        gather_row(nxt, 1 - buf, r)
        scatter_row(prv, r)

    @pl.when(i == 0)
    def _():
        looped(tm, lambda r: gather_row(0, 0, r))
        ybuf[...] = jnp.zeros_like(ybuf)

    @pl.when(new_expert)
    def _():
        pieces = []
        for w_hbm, w_s, stage in ((wg_hbm, wg_s, stage_in), (wu_hbm, wu_s, stage_in), (wd_hbm, wd_s, stage_out)):
            rows = w_s.shape[0] // W_PIECES
            for c in range(W_PIECES):
                pieces.append((w_hbm.at[expert, pl.ds(c * rows, rows), :], w_s.at[pl.ds(c * rows, rows), :], stage))
        copies = [pltpu.make_async_copy(src, stage.at[k % 2], wsem.at[k % 2])
                  for k, (src, _, stage) in enumerate(pieces)]
        copies[0].start()
        for k, (_, dst, stage) in enumerate(pieces):
            if k + 1 < len(pieces):
                copies[k + 1].start()
            copies[k].wait()
            dst[...] = stage[k % 2].astype(BF16)

    wait_gather(buf)
    for j in range(SUB):
        xb_ref[:, j * LANES:(j + 1) * LANES] = xbuf[buf, pl.ds(j, tm, stride=SUB), :].astype(BF16)

    @pl.when(active)
    def _():
        for r in range(tm):
            share_row(r)
        x = xb_ref[...]
        gate = jnp.dot(x, wg_s[...], preferred_element_type=F32)
        up = jnp.dot(x, wu_s[...], preferred_element_type=F32)
        a = (_silu(gate) * up).astype(BF16)
        y_ref[...] = jnp.dot(a, wd_s[...], preferred_element_type=F32)

    @pl.when(jnp.logical_not(active))
    def _():
        looped(tm, share_row)
        y_ref[...] = jnp.zeros_like(y_ref)

    wait_scatter()
    for j in range(SUB):
        ybuf[pl.ds(j, tm, stride=SUB), :] = y_ref[:, j * LANES:(j + 1) * LANES]

    @pl.when(i == n_i - 1)
    def _():
        looped(tm, lambda r: scatter_row(i, r))
        wait_scatter()
        wait_gather(1 - buf)


def expert_ffn(h_tiles, te, nt, src_tok, dst_slot, wg, wu, wd, tm):
    P = src_tok.shape[0]
    D = wg.shape[1]
    F = wg.shape[2]
    kern = functools.partial(_expert_ffn_kernel, tm=tm)
    anyspec = pl.BlockSpec(memory_space=pl.ANY)
    return pl.pallas_call(
        kern,
        grid_spec=pltpu.PrefetchScalarGridSpec(
            num_scalar_prefetch=4,
            grid=(P // tm,),
            in_specs=[anyspec, anyspec, anyspec, anyspec],
            out_specs=anyspec,
            scratch_shapes=[pltpu.VMEM((2, tm * SUB, LANES), F32),
                            pltpu.VMEM((tm, D), BF16),
                            pltpu.VMEM((tm * SUB, LANES), F32),
                            pltpu.VMEM((tm, D), F32),
                            pltpu.VMEM((D, F), BF16),
                            pltpu.VMEM((D, F), BF16),
                            pltpu.VMEM((F, D), BF16),
                            pltpu.VMEM((2, D // W_PIECES, F), F32),
                            pltpu.VMEM((2, F // W_PIECES, D), F32),
                            pltpu.SemaphoreType.DMA((2,)),
                            pltpu.SemaphoreType.DMA(()),
                            pltpu.SemaphoreType.DMA((2,))],
        ),
        out_shape=jax.ShapeDtypeStruct((P * SUB, LANES), F32),
        compiler_params=_cparams(("arbitrary",)),
        name="expert_ffn",
    )(te, nt, src_tok, dst_slot, h_tiles, wg, wu, wd)


def _moe_combine_kernel(y0_ref, y1_ref, r_ref, h_ref, g_ref, beta_ref, ho_ref, hob_ref, *, tm):
    def rows(y_ref):
        return jnp.concatenate([y_ref[pl.ds(j, tm, stride=SUB), :] for j in range(SUB)], axis=1)

    r = r_ref[...]
    ff = rows(y0_ref) * r[:, 2:3] + rows(y1_ref) * r[:, 3:4]
    hn = _layer_norm(DN_ALPHA * h_ref[...] + ff, g_ref[...], beta_ref[...])
    ho_ref[...] = hn
    hob_ref[...] = hn.astype(BF16)


def moe_combine(y_tiles, route, h, g, beta, tm=512):
    T, D = h.shape
    tm = min(tm, T)
    nb = T // tm
    row = lambda i: (i, 0)
    const = lambda i: (0, 0)
    kern = functools.partial(_moe_combine_kernel, tm=tm)
    return pl.pallas_call(
        kern,
        grid=(nb,),
        in_specs=[pl.BlockSpec((tm * SUB, LANES), lambda i: (i, 0)),
                  pl.BlockSpec((tm * SUB, LANES), lambda i: (nb + i, 0)),
                  pl.BlockSpec((tm, LANES), row),
                  pl.BlockSpec((tm, D), row),
                  pl.BlockSpec((1, D), const), pl.BlockSpec((1, D), const)],
        out_specs=[pl.BlockSpec((tm, D), row), pl.BlockSpec((tm, D), row)],
        out_shape=[jax.ShapeDtypeStruct((T, D), F32), jax.ShapeDtypeStruct((T, D), BF16)],
        compiler_params=_cparams(("parallel",)),
        name="moe_combine",
    )(y_tiles, y_tiles, route, h, g.reshape(1, D), beta.reshape(1, D))


def moe_plan(route, T, tm):
    i32 = jnp.int32
    n_slots = TOP_K * T
    experts = jnp.arange(N_EXPERTS, dtype=i32)
    e_flat = route[:, 0:TOP_K].astype(i32).T.reshape(-1)
    order = jnp.argsort(e_flat, stable=True).astype(i32)
    sizes = jnp.sum((e_flat[:, None] == experts[None, :]).astype(i32), axis=0)
    padded = ((sizes + tm - 1) // tm) * tm
    gend = jnp.cumsum(padded)
    gstart = gend - padded
    cstart = jnp.cumsum(sizes) - sizes
    n_rows = n_slots + N_EXPERTS * tm
    row = jnp.arange(n_rows, dtype=i32)
    group = jnp.sum((row[:, None] >= gend[None, :]).astype(i32), axis=1)
    onehot = (jnp.minimum(group, N_EXPERTS - 1)[:, None] == experts[None, :]).astype(i32)
    look = lambda table: jnp.sum(onehot * table[None, :], axis=1)
    local = row - look(gstart)
    real = jnp.logical_and(group < N_EXPERTS, local < look(sizes))
    slot = order[jnp.clip(look(cstart) + local, 0, n_slots - 1)]
    pad_rank = jnp.where(group < N_EXPERTS, look(gstart - cstart) + local - look(sizes), row - n_slots)
    dst_slot = jnp.where(real, slot, n_slots + pad_rank)
    src_tok = jnp.where(real, slot % T, 0)
    n_tiles = n_rows // tm
    nt = (gend[-1] // tm).astype(i32)
    tile_start = jnp.arange(n_tiles, dtype=i32) * tm
    te = jnp.minimum(jnp.sum((tile_start[:, None] >= gend[None, :]).astype(i32), axis=1), N_EXPERTS - 1)
    te = jnp.where(jnp.arange(n_tiles) < nt, te, te[jnp.maximum(nt - 1, 0)])
    return src_tok.astype(i32), dst_slot.astype(i32), te.astype(i32), nt.reshape(1)


def moe_ffn(h, h_tiles, w_router, wg, wu, wd, g, beta, tm=512):
    T, D = h.shape
    route = router_top2(h, w_router)
    src_tok, dst_slot, te, nt = moe_plan(route, T, tm)
    y_tiles = expert_ffn(h_tiles, te, nt, src_tok, dst_slot, wg, wu, wd, tm)
    return moe_combine(y_tiles, route, h, g, beta)


def permute_w_in(w_in, layer):
    widths = [RET_QK, RET_QK, RET_V, RET_V, HG_K, HG_K, HG_V, HG_V, SWA_Q, SWA_KV, SWA_KV,
              D_MODEL, D_MODEL, D_MODEL]
    starts = np.concatenate([[0], np.cumsum(widths)])
    names = ["rq", "rk", "rv", "rg", "hq", "hf", "hi", "hg", "sq", "sk", "sv", "ga", "gb", "gc"]
    off = dict(zip(names, starts[:-1].tolist()))
    wid = dict(zip(names, widths))

    def cols(name, lo=0, n=None):
        a = off[name] + lo
        return w_in[layer, :, a:a + (wid[name] if n is None else n)]

    pieces = [cols(n) for n in ("rq", "rk", "rv", "rg", "hq", "hi", "hg", "ga", "gb", "gc")]
    pieces += [cols("sq", head * SWA_HD, SWA_HD) * SWA_HD ** -0.5 for head in SWA_HEAD_ORDER]
    pieces += [cols("sk"), cols("sv")]
    main = jnp.concatenate(pieces, axis=1).astype(BF16)
    return main, cols("hf")


def swa_rows_paired(w_out):
    return jnp.concatenate([w_out[head * SWA_HD:(head + 1) * SWA_HD, :] for head in SWA_HEAD_ORDER], axis=0)


def kernel(x, ln_in_g, ln_in_b, w_in, ret_w_out, hgrn_lower_bounds, hgrn_norm_g, hgrn_w_out, swa_sinks,
           swa_w_out, w_o, ln_mix_g, ln_mix_b, ffn_w_gate, ffn_w_up, ffn_w_down, moe_router, moe_w_gate,
           moe_w_up, moe_w_down, ln_ffn_g, ln_ffn_b):
    B, S, D = x.shape
    T = B * S
    assert D == D_MODEL and S % RET_CHUNK == 0 and S % (2 * SWA_CHUNK) == 0 and S >= 4 * SWA_CHUNK

    lb_all = jnp.cumsum(jax.nn.softmax(hgrn_lower_bounds.astype(F32), axis=0), axis=0)
    lb_all = lb_all - lb_all[0]
    tabs = retention_tables(S)

    for layer in range(DEPTH):
        w_main, w_hf = permute_w_in(w_in, layer)
        if layer == 0:
            proj, h, hb = ln_matmul(x.reshape(T, D), ln_in_g, ln_in_b, w_main, tn=N_PROJ // 4, name="ln_in_proj")
        else:
            proj = matmul(hb, w_main, BF16, tn=N_PROJ // 4, name="in_proj")
        ya, yb, yc = token_mixers(proj, hb, w_hf, tabs, lb_all[layer], hgrn_norm_g[layer], swa_sinks[layer],
                                  B, S)
        dense = layer % 2 == 0
        h, aux = merge_project(ya, yb, yc, proj, h,
                               ret_w_out[layer].astype(BF16), hgrn_w_out[layer].astype(BF16),
                               swa_rows_paired(swa_w_out[layer]).astype(BF16), w_o[layer].astype(BF16),
                               ln_mix_g[layer], ln_mix_b[layer], token_tiles=not dense)

        j = layer // 2
        if dense:
            h, hb = dense_ffn(aux, h, ffn_w_gate[j].astype(BF16), ffn_w_up[j].astype(BF16),
                              ffn_w_down[j].astype(BF16), ln_ffn_g[layer], ln_ffn_b[layer])
        else:
            h, hb = moe_ffn(h, aux, moe_router[j], moe_w_gate[j], moe_w_up[j], moe_w_down[j],
                            ln_ffn_g[layer], ln_ffn_b[layer])
    return h.reshape(B, S, D)
```

```python
import functools

import numpy as np
import jax
import jax.numpy as jnp
from jax import lax
from jax.experimental import pallas as pl
from jax.experimental.pallas import tpu as pltpu

F32 = jnp.float32
BF16 = jnp.bfloat16

D_MODEL = 1024
RET_HEADS, RET_DK, RET_DV = 4, 128, 256
RET_QK, RET_V = RET_HEADS * RET_DK, RET_HEADS * RET_DV
ROPE_BASE = 10000.0
HG_HEADS, HG_DK, HG_DV = 8, 128, 128
HG_K, HG_V = HG_HEADS * HG_DK, HG_HEADS * HG_DV
SWA_HQ, SWA_HKV, SWA_HD = 16, 4, 64
SWA_G = SWA_HQ // SWA_HKV
SWA_Q, SWA_KV = SWA_HQ * SWA_HD, SWA_HKV * SWA_HD
SWA_CHUNK = 64
SWA_WIN_CHUNKS = 2
N_EXPERTS, TOP_K = 8, 2
DEPTH = 2
LN_EPS, RMS_EPS = 1e-5, 1e-6
DN_ALPHA = (2.0 * DEPTH) ** 0.25

VMEM_LIMIT_BYTES = 56 * 1024 * 1024
LANES = 128
SUB = 8

RET_CHUNK = 128
RET_BLOCK = 512
HG_CHUNK = 32
HG_BLOCK = 256
MIX_SPAN = 1024

OFF_RQ, OFF_RK, OFF_RV, OFF_RG = 0, 512, 1024, 2048
OFF_HQ, OFF_HI, OFF_HG = 3072, 4096, 5120
OFF_GA, OFF_GB, OFF_GC = 6144, 7168, 8192
OFF_SWA = 9216
OFF_SWA_K = OFF_SWA + SWA_Q
OFF_SWA_V = OFF_SWA_K + SWA_KV
N_PROJ = OFF_SWA_V + SWA_KV
SWA_HEAD_ORDER = [(2 * m + half) * SWA_G + g for m in range(SWA_HKV // 2) for g in range(SWA_G) for half in (0, 1)]


def _cparams(sem, vmem=VMEM_LIMIT_BYTES):
    return pltpu.CompilerParams(dimension_semantics=sem, vmem_limit_bytes=vmem)


def _layer_norm(x, g, b):
    mu = jnp.mean(x, -1, keepdims=True)
    xc = x - mu
    var = jnp.mean(xc * xc, -1, keepdims=True)
    return xc * lax.rsqrt(var + LN_EPS) * g + b


def _silu(x):
    return x * jax.nn.sigmoid(x)


def _matmul_kernel(x_ref, w_ref, o_ref):
    o_ref[...] = jnp.dot(x_ref[...], w_ref[...], preferred_element_type=F32).astype(o_ref.dtype)


def matmul(x, w, out_dtype, tm=1024, tn=512, name="matmul"):
    T, K = x.shape
    N = w.shape[1]
    tm = min(tm, T)
    return pl.pallas_call(
        _matmul_kernel,
        grid=(T // tm, N // tn),
        in_specs=[pl.BlockSpec((tm, K), lambda i, j: (i, 0)),
                  pl.BlockSpec((K, tn), lambda i, j: (0, j))],
        out_specs=pl.BlockSpec((tm, tn), lambda i, j: (i, j)),
        out_shape=jax.ShapeDtypeStruct((T, N), out_dtype),
        compiler_params=_cparams(("parallel", "parallel")),
        name=name,
    )(x, w)


def _ln_matmul_kernel(x_ref, g_ref, b_ref, w_ref, o_ref, h_ref, hb_ref):
    @pl.when(pl.program_id(1) == 0)
    def _():
        h = _layer_norm(x_ref[...], g_ref[...], b_ref[...])
        h_ref[...] = h
        hb_ref[...] = h.astype(BF16)

    o_ref[...] = jnp.dot(hb_ref[...], w_ref[...], preferred_element_type=F32).astype(o_ref.dtype)


def ln_matmul(x2, g, b, w, tm=1024, tn=512, name="ln_matmul"):
    T, D = x2.shape
    N = w.shape[1]
    tm = min(tm, T)
    row = lambda i, j: (i, 0)
    const = lambda i, j: (0, 0)
    return pl.pallas_call(
        _ln_matmul_kernel,
        grid=(T // tm, N // tn),
        in_specs=[pl.BlockSpec((tm, D), row), pl.BlockSpec((1, D), const), pl.BlockSpec((1, D), const),
                  pl.BlockSpec((D, tn), lambda i, j: (0, j))],
        out_specs=[pl.BlockSpec((tm, tn), lambda i, j: (i, j)), pl.BlockSpec((tm, D), row),
                   pl.BlockSpec((tm, D), row)],
        out_shape=[jax.ShapeDtypeStruct((T, N), BF16), jax.ShapeDtypeStruct((T, D), F32),
                   jax.ShapeDtypeStruct((T, D), BF16)],
        compiler_params=_cparams(("parallel", "arbitrary")),
        name=name,
    )(x2, g.reshape(1, D), b.reshape(1, D), w)


def _retention_block(q_ref, k_ref, v_ref, g_ref, cos_ref, sin_ref, tabs, o_ref, st_ref, start, *, chunk, block):
    dm, qd, kd, cd = tabs
    nc = block // chunk
    nt_dims = (((1,), (1,)), ((), ()))
    tn_dims = (((0,), (0,)), ((), ()))
    r = pl.ds(pl.multiple_of(start, block), block)
    cos = cos_ref[r, :]
    sin = sin_ref[r, :]
    q = q_ref[r, :].astype(F32)
    k = k_ref[r, :].astype(F32)
    qr = q * cos + pltpu.roll(q, RET_DK // 2, 1) * sin
    kr = k * cos + pltpu.roll(k, RET_DK // 2, 1) * sin
    v = v_ref[r, :]
    rows = [slice(c * chunk, (c + 1) * chunk) for c in range(nc)]
    upd = [lax.dot_general((kr[rc] * kd).astype(BF16), v[rc], tn_dims, preferred_element_type=F32)
           for rc in rows]
    att = [lax.dot_general(qr[rc].astype(BF16), kr[rc].astype(BF16), nt_dims,
                           preferred_element_type=F32) * dm for rc in rows]
    intra = [jnp.dot(a.astype(BF16), v[rc], preferred_element_type=F32) for a, rc in zip(att, rows)]
    q_in = [(qr[rc] * qd).astype(BF16) for rc in rows]
    st = st_ref[...]
    outs = []
    for c in range(nc):
        outs.append(intra[c] + jnp.dot(q_in[c], st.astype(BF16), preferred_element_type=F32))
        st = st * cd + upd[c]
    st_ref[...] = st
    o = jnp.concatenate(outs, axis=0)
    on = o * lax.rsqrt(jnp.mean(o * o, -1, keepdims=True) + RMS_EPS)
    g = g_ref[r, :].astype(F32)
    o_ref[r, :] = (_silu(g) * on).astype(o_ref.dtype)


def retention_tables(S):
    C = RET_CHUNK
    half = RET_DK // 2
    pos = jnp.arange(S, dtype=F32)
    inv = 1.0 / (ROPE_BASE ** jnp.linspace(0.0, 1.0, half, dtype=F32))
    ang = pos[:, None] * inv[None, :]
    cos, sin = jnp.cos(ang), jnp.sin(ang)
    log_gamma = jnp.log(1.0 - 2.0 ** (-5.0 - jnp.arange(RET_HEADS, dtype=F32)))
    idx = jnp.arange(C, dtype=F32)
    diff = idx[:, None] - idx[None, :]
    decay = jnp.where(diff[None] >= 0,
                      jnp.exp(jnp.maximum(diff, 0.0)[None] * log_gamma[:, None, None]), 0.0)
    scale = RET_DK ** -0.5
    qd = jnp.exp((idx + 1.0)[None, :] * log_gamma[:, None])
    kd = jnp.exp((C - 1.0 - idx)[None, :] * log_gamma[:, None]) * scale
    cd = jnp.exp(C * log_gamma)
    return {
        "cos": jnp.concatenate([cos, cos], axis=1),
        "sin": jnp.concatenate([-sin, sin], axis=1),
        "dm": decay * scale,
        "qd": jnp.broadcast_to(qd[:, :, None], (RET_HEADS, C, RET_DK)),
        "kd": jnp.broadcast_to(kd[:, :, None], (RET_HEADS, C, RET_DK)),
        "cd": jnp.broadcast_to(cd[:, None, None], (RET_HEADS, 1, RET_DV)),
    }


def _hgrn_block(q_ref, z_ref, i_ref, g_ref, lb, ng, o_ref, st_ref, start, lanes, consts, *, chunk, block):
    row, causal = consts
    half = chunk // 2
    nc = block // chunk
    nt_dims = (((1,), (1,)), ((), ()))
    tn_dims = (((0,), (0,)), ((), ()))

    def chunk_rows(x, lo):
        return jnp.concatenate([x[c * chunk + lo:c * chunk + lo + 1, :] for c in range(nc)], axis=0)

    def spread(xc):
        return jnp.concatenate(
            [jnp.broadcast_to(xc[c:c + 1, :], (chunk, xc.shape[1])) for c in range(nc)], axis=0)

    r = pl.ds(pl.multiple_of(start, block), block)
    f = lb + (1.0 - lb) * jax.nn.sigmoid(z_ref[r, lanes])
    cum = jnp.log(f)
    shift = 1
    while shift < chunk:
        cum = cum + jnp.where(row >= shift, pltpu.roll(cum, shift, 0), 0.0)
        shift *= 2
    mid_c = chunk_rows(cum, half - 1)
    last_c = chunk_rows(cum, chunk - 1)
    mid = spread(mid_c)
    qh = q_ref[r, lanes].astype(F32) * jnp.exp(cum - mid)
    kh = (1.0 - f) * jnp.exp(mid - cum)
    q_in = (qh * spread(jnp.exp(mid_c))).astype(BF16)
    k_st = (kh * spread(jnp.exp(last_c - mid_c))).astype(BF16)
    dec = jnp.exp(last_c)
    qh = qh.astype(BF16)
    kh = kh.astype(BF16)
    v = i_ref[r, lanes]
    rows = [slice(c * chunk, (c + 1) * chunk) for c in range(nc)]
    upd = [lax.dot_general(v[rc], k_st[rc], tn_dims, preferred_element_type=F32) for rc in rows]
    att = [lax.dot_general(qh[rc], kh[rc], nt_dims, preferred_element_type=F32) for rc in rows]
    intra = [jnp.dot(jnp.where(causal, a, 0.0).astype(BF16), v[rc], preferred_element_type=F32)
             for a, rc in zip(att, rows)]
    st = st_ref[...]
    outs = []
    for c, rc in enumerate(rows):
        inter = lax.dot_general(q_in[rc], st.astype(BF16), nt_dims, preferred_element_type=F32)
        outs.append(intra[c] + inter)
        st = st * dec[c:c + 1, :] + upd[c]
    st_ref[...] = st
    o = jnp.concatenate(outs, axis=0)
    on = o * lax.rsqrt(jnp.mean(o * o, -1, keepdims=True) + RMS_EPS) * ng
    g = g_ref[r, lanes].astype(F32)
    o_ref[r, lanes] = (_silu(g) * on).astype(o_ref.dtype)


def _swa_step(sink_ref, q_ref, k_ref, v_ref, o_ref, j, step, consts):
    lo_kv, lo_o, rel, zero = consts
    CH = SWA_CHUNK
    HD = SWA_HD
    QR = 2 * CH
    KW = 4 * CH
    nt_dims = (((1,), (1,)), ((), ()))
    first = jnp.maximum(2 * j - SWA_WIN_CHUNKS, 0)
    rq = pl.ds(pl.multiple_of(j * QR, QR), QR)
    rk = pl.ds(pl.multiple_of(first * CH, CH), KW)
    kk = k_ref[rk, :]
    vv = v_ref[rk, :]
    kbd = jnp.concatenate([jnp.where(lo_kv, kk, zero), jnp.where(lo_kv, zero, kk)], axis=0)
    vbd = jnp.concatenate([jnp.where(lo_kv, vv, zero), jnp.where(lo_kv, zero, vv)], axis=0)
    valid = lax.bitcast_convert_type(rel + (first - 2 * j + SWA_WIN_CHUNKS), jnp.uint32) <= SWA_WIN_CHUNKS
    scores = [lax.dot_general(q_ref[rq, p * 2 * HD:(p + 1) * 2 * HD], kbd, nt_dims,
                              preferred_element_type=F32) for p in range(SWA_G // 2)]
    pair = step // 2
    head0 = (step % 2) * (SWA_G // 2)
    for p, s in enumerate(scores):
        s = jnp.where(valid, s, -jnp.inf)
        probs, inv = [], []
        for hh in range(2):
            sink = sink_ref[(2 * pair + hh) * SWA_G + head0 + p]
            sh = s[:, hh * KW:(hh + 1) * KW]
            m = jnp.maximum(jnp.max(sh, -1, keepdims=True), sink)
            e = jnp.exp(sh - m)
            den = jnp.sum(e, -1, keepdims=True) + jnp.exp(sink - m)
            probs.append(e.astype(BF16))
            inv.append(1.0 / den)
        o = jnp.dot(jnp.concatenate(probs, axis=1), vbd, preferred_element_type=F32)
        o = o * jnp.where(lo_o, inv[0], inv[1])
        o_ref[rq, p * 2 * HD:(p + 1) * 2 * HD] = o.astype(o_ref.dtype)


def _mixers_kernel(sink_ref, rq_ref, rk_ref, rv_ref, rg_ref, cos_ref, sin_ref, dm_ref, qd_ref, kd_ref, cd_ref,
                   hq_ref, xb_ref, wf_ref, hi_ref, hg_ref, lb_ref, ng_ref, sq_ref, sk_ref, sv_ref,
                   ya_ref, yb_ref, yc_ref,
                   st_r, st_h, hz_ref, *, span, nspans):
    kvh = pl.program_id(1)
    st_r[...] = jnp.zeros_like(st_r)
    st_h[...] = jnp.zeros_like(st_h)
    hz_ref[...] = jnp.dot(xb_ref[...], wf_ref[...].astype(BF16), preferred_element_type=F32)
    ret_tabs = (dm_ref[0], qd_ref[0], kd_ref[0], cd_ref[0])
    lb2 = lb_ref[0]
    ng = ng_ref[...]
    hg_consts = (lax.broadcasted_iota(jnp.int32, (HG_BLOCK, HG_DK), 0) % HG_CHUNK,
                 lax.broadcasted_iota(jnp.int32, (HG_CHUNK, HG_CHUNK), 0)
                 >= lax.broadcasted_iota(jnp.int32, (HG_CHUNK, HG_CHUNK), 1))
    QR, KW, HD = 2 * SWA_CHUNK, 4 * SWA_CHUNK, SWA_HD
    kcol = lax.broadcasted_iota(jnp.int32, (QR, 2 * KW), 1)
    qrow = lax.broadcasted_iota(jnp.int32, (QR, 2 * KW), 0)
    swa_consts = (lax.broadcasted_iota(jnp.int32, (KW, 2 * HD), 1) < HD,
                  lax.broadcasted_iota(jnp.int32, (QR, 2 * HD), 1) < HD,
                  (kcol % KW) // SWA_CHUNK - qrow // SWA_CHUNK,
                  jnp.zeros((KW, 2 * HD), BF16))
    n_swa = span // QR
    n_hg = span // HG_BLOCK
    n_ret = span // RET_BLOCK

    def body(u, carry):
        base = u * span
        for t in range(2 * n_hg):
            hh, blk = t % 2, t // 2
            lanes = slice(hh * HG_DK, (hh + 1) * HG_DK)
            _hgrn_block(hq_ref, hz_ref, hi_ref, hg_ref, lb2[:, lanes], ng, yb_ref, st_h.at[hh],
                        base + blk * HG_BLOCK, lanes, hg_consts, chunk=HG_CHUNK, block=HG_BLOCK)
            if t % (2 * n_hg // n_ret) == 0:
                _retention_block(rq_ref, rk_ref, rv_ref, rg_ref, cos_ref, sin_ref, ret_tabs, ya_ref, st_r,
                                 base + (t // (2 * n_hg // n_ret)) * RET_BLOCK, chunk=RET_CHUNK, block=RET_BLOCK)
        for t in range(n_swa):
            _swa_step(sink_ref, sq_ref, sk_ref, sv_ref, yc_ref, u * n_swa + t, kvh, swa_consts)
        return carry

    lax.fori_loop(0, nspans, body, 0)


def token_mixers(proj, hb, w_hf, tabs, lb, norm_g, sinks, B, S):
    T = B * S
    D = hb.shape[1]
    span = min(MIX_SPAN, S)
    C = RET_CHUNK
    hw = 2 * HG_DK
    ret = lambda off, w: (lambda b, h: (b, off // w + h))
    kern = functools.partial(_mixers_kernel, span=span, nspans=S // span)
    out = jax.ShapeDtypeStruct((T, D_MODEL), BF16)
    outspec = pl.BlockSpec((S, RET_DV), lambda b, h: (b, h))
    return pl.pallas_call(
        kern,
        grid=(B, RET_HEADS),
        in_specs=[pl.BlockSpec(memory_space=pltpu.SMEM),
                  pl.BlockSpec((S, RET_DK), ret(OFF_RQ, RET_DK)),
                  pl.BlockSpec((S, RET_DK), ret(OFF_RK, RET_DK)),
                  pl.BlockSpec((S, RET_DV), ret(OFF_RV, RET_DV)),
                  pl.BlockSpec((S, RET_DV), ret(OFF_RG, RET_DV)),
                  pl.BlockSpec((S, RET_DK), lambda b, h: (0, 0)),
                  pl.BlockSpec((S, RET_DK), lambda b, h: (0, 0)),
                  pl.BlockSpec((1, C, C), lambda b, h: (h, 0, 0)),
                  pl.BlockSpec((1, C, RET_DK), lambda b, h: (h, 0, 0)),
                  pl.BlockSpec((1, C, RET_DK), lambda b, h: (h, 0, 0)),
                  pl.BlockSpec((1, 1, RET_DV), lambda b, h: (h, 0, 0)),
                  pl.BlockSpec((S, hw), ret(OFF_HQ, hw)),
                  pl.BlockSpec((S, D), lambda b, h: (b, 0)),
                  pl.BlockSpec((D, hw), lambda b, h: (0, h)),
                  pl.BlockSpec((S, hw), ret(OFF_HI, hw)),
                  pl.BlockSpec((S, hw), ret(OFF_HG, hw)),
                  pl.BlockSpec((1, 1, hw), lambda b, h: (h, 0, 0)),
                  pl.BlockSpec((1, HG_DV), lambda b, h: (0, 0)),
                  pl.BlockSpec((S, 2 * LANES), ret(OFF_SWA, 2 * LANES)),
                  pl.BlockSpec((S, LANES), lambda b, h: (b, OFF_SWA_K // LANES + h // 2)),
                  pl.BlockSpec((S, LANES), lambda b, h: (b, OFF_SWA_V // LANES + h // 2))],
        out_specs=[outspec, outspec, outspec],
        out_shape=[out, out, out],
        scratch_shapes=[pltpu.VMEM((RET_DK, RET_DV), F32), pltpu.VMEM((2, HG_DV, HG_DK), F32),
                        pltpu.VMEM((S, hw), F32)],
        compiler_params=_cparams(("parallel", "parallel")),
        name="token_mixers",
    )(sinks.astype(F32), proj, proj, proj, proj, tabs["cos"], tabs["sin"], tabs["dm"], tabs["qd"], tabs["kd"],
      tabs["cd"], proj, hb, w_hf, proj, proj, lb.reshape(RET_HEADS, 1, hw), norm_g.reshape(1, HG_DV),
      proj, proj, proj)


def _merge_kernel(a_ref, b_ref, c_ref, ga_ref, gb_ref, gc_ref, h_ref, wa_ref, wb_ref, wc_ref, wo_ref,
                  g_ref, beta_ref, ho_ref, aux_ref, *, tm, token_tiles):
    def branch(x_ref, gate_ref, w_ref):
        y = jnp.dot(x_ref[...], w_ref[...], preferred_element_type=F32)
        return jax.nn.sigmoid(gate_ref[...].astype(F32)) * y

    merged = branch(a_ref, ga_ref, wa_ref) + branch(b_ref, gb_ref, wb_ref) + branch(c_ref, gc_ref, wc_ref)
    mix = jnp.dot(merged.astype(BF16), wo_ref[...], preferred_element_type=F32)
    hn = _layer_norm(DN_ALPHA * h_ref[...] + mix, g_ref[...], beta_ref[...])
    ho_ref[...] = hn
    if token_tiles:
        for j in range(SUB):
            aux_ref[pl.ds(j, tm, stride=SUB), :] = hn[:, j * LANES:(j + 1) * LANES]
    else:
        aux_ref[...] = hn.astype(BF16)


def merge_project(ya, yb, yc, proj, h, wa, wb, wc, wo, g, beta, token_tiles, tm=512):
    T, D = h.shape
    tm = min(tm, T)
    row = lambda i: (i, 0)
    const = lambda i: (0, 0)
    wspec = pl.BlockSpec((D, D), const, pipeline_mode=pl.Buffered(1))
    if token_tiles:
        aux_spec = pl.BlockSpec((tm * SUB, LANES), row)
        aux_shape = jax.ShapeDtypeStruct((T * SUB, LANES), F32)
    else:
        aux_spec = pl.BlockSpec((tm, D), row)
        aux_shape = jax.ShapeDtypeStruct((T, D), BF16)
    kern = functools.partial(_merge_kernel, tm=tm, token_tiles=token_tiles)
    return pl.pallas_call(
        kern,
        grid=(T // tm,),
        in_specs=[pl.BlockSpec((tm, D), row), pl.BlockSpec((tm, D), row), pl.BlockSpec((tm, D), row),
                  pl.BlockSpec((tm, D), lambda i: (i, OFF_GA // D_MODEL)),
                  pl.BlockSpec((tm, D), lambda i: (i, OFF_GB // D_MODEL)),
                  pl.BlockSpec((tm, D), lambda i: (i, OFF_GC // D_MODEL)),
                  pl.BlockSpec((tm, D), row),
                  wspec, wspec, wspec, wspec,
                  pl.BlockSpec((1, D), const), pl.BlockSpec((1, D), const)],
        out_specs=[pl.BlockSpec((tm, D), row), aux_spec],
        out_shape=[jax.ShapeDtypeStruct((T, D), F32), aux_shape],
        compiler_params=_cparams(("parallel",)),
        name="merge_project",
    )(ya, yb, yc, proj, proj, proj, h, wa, wb, wc, wo, g.reshape(1, D), beta.reshape(1, D))


def _ffn_kernel(xb_ref, h_ref, wg_ref, wu_ref, wd_ref, g_ref, beta_ref, ho_ref, hob_ref):
    x = xb_ref[...]
    gate = jnp.dot(x, wg_ref[...], preferred_element_type=F32)
    up = jnp.dot(x, wu_ref[...], preferred_element_type=F32)
    a = (_silu(gate) * up).astype(BF16)
    ff = jnp.dot(a, wd_ref[...], preferred_element_type=F32)
    hn = _layer_norm(DN_ALPHA * h_ref[...] + ff, g_ref[...], beta_ref[...])
    ho_ref[...] = hn
    hob_ref[...] = hn.astype(BF16)


def dense_ffn(hb, h, wg, wu, wd, g, beta, tm=512):
    T, D = h.shape
    F = wg.shape[1]
    tm = min(tm, T)
    row = lambda i: (i, 0)
    const = lambda i: (0, 0)
    resident = pl.Buffered(1)
    return pl.pallas_call(
        _ffn_kernel,
        grid=(T // tm,),
        in_specs=[pl.BlockSpec((tm, D), row), pl.BlockSpec((tm, D), row),
                  pl.BlockSpec((D, F), const, pipeline_mode=resident),
                  pl.BlockSpec((D, F), const, pipeline_mode=resident),
                  pl.BlockSpec((F, D), const, pipeline_mode=resident),
                  pl.BlockSpec((1, D), const), pl.BlockSpec((1, D), const)],
        out_specs=[pl.BlockSpec((tm, D), row), pl.BlockSpec((tm, D), row)],
        out_shape=[jax.ShapeDtypeStruct((T, D), F32), jax.ShapeDtypeStruct((T, D), BF16)],
        compiler_params=_cparams(("parallel",)),
        name="dense_ffn",
    )(hb, h, wg, wu, wd, g.reshape(1, D), beta.reshape(1, D))


def _route_top2(h, w_hi, w_lo):
    h_hi = h.astype(BF16)
    h_lo = (h - h_hi.astype(F32)).astype(BF16)
    logits = (jnp.dot(h_hi, w_hi, preferred_element_type=F32) + jnp.dot(h_lo, w_hi, preferred_element_type=F32)
              + jnp.dot(h_hi, w_lo, preferred_element_type=F32))
    lane = lax.broadcasted_iota(jnp.int32, logits.shape, 1)
    l1 = jnp.where(lane < N_EXPERTS, logits, -jnp.inf)
    m1 = jnp.max(l1, -1, keepdims=True)
    i1 = jnp.min(jnp.where(l1 == m1, lane, LANES), -1, keepdims=True)
    l2 = jnp.where(lane == i1, -jnp.inf, l1)
    m2 = jnp.max(l2, -1, keepdims=True)
    i2 = jnp.min(jnp.where(l2 == m2, lane, LANES), -1, keepdims=True)
    e = jnp.exp(m2 - m1)
    w1 = 1.0 / (1.0 + e)
    w2 = e / (1.0 + e)
    return jnp.where(lane == 0, i1.astype(F32),
                     jnp.where(lane == 1, i2.astype(F32),
                               jnp.where(lane == 2, w1, jnp.where(lane == 3, w2, 0.0))))


def _router_kernel(h_ref, w_ref, o_ref):
    o_ref[...] = _route_top2(h_ref[...], w_ref[0], w_ref[1])


def router_top2(h, w_router, tm=512):
    T, D = h.shape
    tm = min(tm, T)
    w_pad = jnp.zeros((D, LANES), F32).at[:, :N_EXPERTS].set(w_router.astype(F32))
    w_hi = w_pad.astype(BF16)
    w_lo = (w_pad - w_hi.astype(F32)).astype(BF16)
    return pl.pallas_call(
        _router_kernel,
        grid=(T // tm,),
        in_specs=[pl.BlockSpec((tm, D), lambda i: (i, 0)), pl.BlockSpec((2, D, LANES), lambda i: (0, 0, 0))],
        out_specs=pl.BlockSpec((tm, LANES), lambda i: (i, 0)),
        out_shape=jax.ShapeDtypeStruct((T, LANES), F32),
        compiler_params=_cparams(("parallel",)),
        name="router_top2",
    )(h, jnp.stack([w_hi, w_lo]))


W_PIECES = 8


def _expert_ffn_kernel(te_ref, nt_ref, src_ref, dst_ref, h_hbm, wg_hbm, wu_hbm, wd_hbm, y_hbm,
                       xbuf, xb_ref, ybuf, y_ref, wg_s, wu_s, wd_s, stage_in, stage_out, gsem, ssem, wsem,
                       *, tm):
    i = pl.program_id(0)
    n_i = pl.num_programs(0)
    active = i < nt_ref[0]
    buf = i % 2
    nxt = jnp.minimum(i + 1, n_i - 1)
    prv = jnp.maximum(i - 1, 0)
    expert = te_ref[i]
    new_expert = jnp.logical_and(active, jnp.logical_or(i == 0, expert != te_ref[prv]))

    def gather_row(tile, b, r):
        tok = src_ref[tile * tm + r]
        pltpu.make_async_copy(h_hbm.at[pl.ds(pl.multiple_of(tok * SUB, SUB), SUB), :],
                              xbuf.at[b, pl.ds(pl.multiple_of(r * SUB, SUB), SUB), :], gsem.at[b]).start()

    def scatter_row(tile, r):
        slot = dst_ref[tile * tm + r]
        pltpu.make_async_copy(ybuf.at[pl.ds(pl.multiple_of(r * SUB, SUB), SUB), :],
                              y_hbm.at[pl.ds(pl.multiple_of(slot * SUB, SUB), SUB), :], ssem).start()

    def wait_gather(b):
        pltpu.make_async_copy(h_hbm.at[pl.ds(0, tm * SUB), :], xbuf.at[b], gsem.at[b]).wait()

    def wait_scatter():
        pltpu.make_async_copy(ybuf, y_hbm.at[pl.ds(0, tm * SUB), :], ssem).wait()

    def looped(n, fn):
        def body(r, carry):
            fn(r)
            return carry
        lax.fori_loop(0, n, body, 0, unroll=8)

    def share_row(r):
        gather_row(nxt, 1 - buf, r)
        scatter_row(prv, r)

    @pl.when(i == 0)
    def _():
        looped(tm, lambda r: gather_row(0, 0, r))
        ybuf[...] = jnp.zeros_like(ybuf)

    @pl.when(new_expert)
    def _():
        pieces = []
        for w_hbm, w_s, stage in ((wg_hbm, wg_s, stage_in), (wu_hbm, wu_s, stage_in), (wd_hbm, wd_s, stage_out)):
            rows = w_s.shape[0] // W_PIECES
            for c in range(W_PIECES):
                pieces.append((w_hbm.at[expert, pl.ds(c * rows, rows), :], w_s.at[pl.ds(c * rows, rows), :], stage))
        copies = [pltpu.make_async_copy(src, stage.at[k % 2], wsem.at[k % 2])
                  for k, (src, _, stage) in enumerate(pieces)]
        copies[0].start()
        for k, (_, dst, stage) in enumerate(pieces):
            if k + 1 < len(pieces):
                copies[k + 1].start()
            copies[k].wait()
            dst[...] = stage[k % 2].astype(BF16)

    wait_gather(buf)
    for j in range(SUB):
        xb_ref[:, j * LANES:(j + 1) * LANES] = xbuf[buf, pl.ds(j, tm, stride=SUB), :].astype(BF16)

    @pl.when(active)
    def _():
        for r in range(tm):
            share_row(r)
        x = xb_ref[...]
        gate = jnp.dot(x, wg_s[...], preferred_element_type=F32)
        up = jnp.dot(x, wu_s[...], preferred_element_type=F32)
        a = (_silu(gate) * up).astype(BF16)
        y_ref[...] = jnp.dot(a, wd_s[...], preferred_element_type=F32)

    @pl.when(jnp.logical_not(active))
    def _():
        looped(tm, share_row)
        y_ref[...] = jnp.zeros_like(y_ref)

    wait_scatter()
    for j in range(SUB):
        ybuf[pl.ds(j, tm, stride=SUB), :] = y_ref[:, j * LANES:(j + 1) * LANES]

    @pl.when(i == n_i - 1)
    def _():
        looped(tm, lambda r: scatter_row(i, r))
        wait_scatter()
        wait_gather(1 - buf)


def expert_ffn(h_tiles, te, nt, src_tok, dst_slot, wg, wu, wd, tm):
    P = src_tok.shape[0]
    D = wg.shape[1]
    F = wg.shape[2]
    kern = functools.partial(_expert_ffn_kernel, tm=tm)
    anyspec = pl.BlockSpec(memory_space=pl.ANY)
    return pl.pallas_call(
        kern,
        grid_spec=pltpu.PrefetchScalarGridSpec(
            num_scalar_prefetch=4,
            grid=(P // tm,),
            in_specs=[anyspec, anyspec, anyspec, anyspec],
            out_specs=anyspec,
            scratch_shapes=[pltpu.VMEM((2, tm * SUB, LANES), F32),
                            pltpu.VMEM((tm, D), BF16),
                            pltpu.VMEM((tm * SUB, LANES), F32),
                            pltpu.VMEM((tm, D), F32),
                            pltpu.VMEM((D, F), BF16),
                            pltpu.VMEM((D, F), BF16),
                            pltpu.VMEM((F, D), BF16),
                            pltpu.VMEM((2, D // W_PIECES, F), F32),
                            pltpu.VMEM((2, F // W_PIECES, D), F32),
                            pltpu.SemaphoreType.DMA((2,)),
                            pltpu.SemaphoreType.DMA(()),
                            pltpu.SemaphoreType.DMA((2,))],
        ),
        out_shape=jax.ShapeDtypeStruct((P * SUB, LANES), F32),
        compiler_params=_cparams(("arbitrary",)),
        name="expert_ffn",
    )(te, nt, src_tok, dst_slot, h_tiles, wg, wu, wd)


def _moe_combine_kernel(y0_ref, y1_ref, r_ref, h_ref, g_ref, beta_ref, ho_ref, hob_ref, *, tm):
    def rows(y_ref):
        return jnp.concatenate([y_ref[pl.ds(j, tm, stride=SUB), :] for j in range(SUB)], axis=1)

    r = r_ref[...]
    ff = rows(y0_ref) * r[:, 2:3] + rows(y1_ref) * r[:, 3:4]
    hn = _layer_norm(DN_ALPHA * h_ref[...] + ff, g_ref[...], beta_ref[...])
    ho_ref[...] = hn
    hob_ref[...] = hn.astype(BF16)


def moe_combine(y_tiles, route, h, g, beta, tm=512):
    T, D = h.shape
    tm = min(tm, T)
    nb = T // tm
    row = lambda i: (i, 0)
    const = lambda i: (0, 0)
    kern = functools.partial(_moe_combine_kernel, tm=tm)
    return pl.pallas_call(
        kern,
        grid=(nb,),
        in_specs=[pl.BlockSpec((tm * SUB, LANES), lambda i: (i, 0)),
                  pl.BlockSpec((tm * SUB, LANES), lambda i: (nb + i, 0)),
                  pl.BlockSpec((tm, LANES), row),
                  pl.BlockSpec((tm, D), row),
                  pl.BlockSpec((1, D), const), pl.BlockSpec((1, D), const)],
        out_specs=[pl.BlockSpec((tm, D), row), pl.BlockSpec((tm, D), row)],
        out_shape=[jax.ShapeDtypeStruct((T, D), F32), jax.ShapeDtypeStruct((T, D), BF16)],
        compiler_params=_cparams(("parallel",)),
        name="moe_combine",
    )(y_tiles, y_tiles, route, h, g.reshape(1, D), beta.reshape(1, D))


def moe_plan(route, T, tm):
    i32 = jnp.int32
    n_slots = TOP_K * T
    experts = jnp.arange(N_EXPERTS, dtype=i32)
    e_flat = route[:, 0:TOP_K].astype(i32).T.reshape(-1)
    order = jnp.argsort(e_flat, stable=True).astype(i32)
    sizes = jnp.sum((e_flat[:, None] == experts[None, :]).astype(i32), axis=0)
    padded = ((sizes + tm - 1) // tm) * tm
    gend = jnp.cumsum(padded)
    gstart = gend - padded
    cstart = jnp.cumsum(sizes) - sizes
    n_rows = n_slots + N_EXPERTS * tm
    row = jnp.arange(n_rows, dtype=i32)
    group = jnp.sum((row[:, None] >= gend[None, :]).astype(i32), axis=1)
    onehot = (jnp.minimum(group, N_EXPERTS - 1)[:, None] == experts[None, :]).astype(i32)
    look = lambda table: jnp.sum(onehot * table[None, :], axis=1)
    local = row - look(gstart)
    real = jnp.logical_and(group < N_EXPERTS, local < look(sizes))
    slot = order[jnp.clip(look(cstart) + local, 0, n_slots - 1)]
    pad_rank = jnp.where(group < N_EXPERTS, look(gstart - cstart) + local - look(sizes), row - n_slots)
    dst_slot = jnp.where(real, slot, n_slots + pad_rank)
    src_tok = jnp.where(real, slot % T, 0)
    n_tiles = n_rows // tm
    nt = (gend[-1] // tm).astype(i32)
    tile_start = jnp.arange(n_tiles, dtype=i32) * tm
    te = jnp.minimum(jnp.sum((tile_start[:, None] >= gend[None, :]).astype(i32), axis=1), N_EXPERTS - 1)
    te = jnp.where(jnp.arange(n_tiles) < nt, te, te[jnp.maximum(nt - 1, 0)])
    return src_tok.astype(i32), dst_slot.astype(i32), te.astype(i32), nt.reshape(1)


def moe_ffn(h, h_tiles, w_router, wg, wu, wd, g, beta, tm=512):
    T, D = h.shape
    route = router_top2(h, w_router)
    src_tok, dst_slot, te, nt = moe_plan(route, T, tm)
    y_tiles = expert_ffn(h_tiles, te, nt, src_tok, dst_slot, wg, wu, wd, tm)
    return moe_combine(y_tiles, route, h, g, beta)


def permute_w_in(w_in, layer):
    widths = [RET_QK, RET_QK, RET_V, RET_V, HG_K, HG_K, HG_V, HG_V, SWA_Q, SWA_KV, SWA_KV,
              D_MODEL, D_MODEL, D_MODEL]
    starts = np.concatenate([[0], np.cumsum(widths)])
    names = ["rq", "rk", "rv", "rg", "hq", "hf", "hi", "hg", "sq", "sk", "sv", "ga", "gb", "gc"]
    off = dict(zip(names, starts[:-1].tolist()))
    wid = dict(zip(names, widths))

    def cols(name, lo=0, n=None):
        a = off[name] + lo
        return w_in[layer, :, a:a + (wid[name] if n is None else n)]

    pieces = [cols(n) for n in ("rq", "rk", "rv", "rg", "hq", "hi", "hg", "ga", "gb", "gc")]
    pieces += [cols("sq", head * SWA_HD, SWA_HD) * SWA_HD ** -0.5 for head in SWA_HEAD_ORDER]
    pieces += [cols("sk"), cols("sv")]
    main = jnp.concatenate(pieces, axis=1).astype(BF16)
    return main, cols("hf")


def swa_rows_paired(w_out):
    return jnp.concatenate([w_out[head * SWA_HD:(head + 1) * SWA_HD, :] for head in SWA_HEAD_ORDER], axis=0)


def kernel(x, ln_in_g, ln_in_b, w_in, ret_w_out, hgrn_lower_bounds, hgrn_norm_g, hgrn_w_out, swa_sinks,
           swa_w_out, w_o, ln_mix_g, ln_mix_b, ffn_w_gate, ffn_w_up, ffn_w_down, moe_router, moe_w_gate,
           moe_w_up, moe_w_down, ln_ffn_g, ln_ffn_b):
    B, S, D = x.shape
    T = B * S
    assert D == D_MODEL and S % RET_CHUNK == 0 and S % (2 * SWA_CHUNK) == 0 and S >= 4 * SWA_CHUNK

    lb_all = jnp.cumsum(jax.nn.softmax(hgrn_lower_bounds.astype(F32), axis=0), axis=0)
    lb_all = lb_all - lb_all[0]
    tabs = retention_tables(S)

    for layer in range(DEPTH):
        w_main, w_hf = permute_w_in(w_in, layer)
        if layer == 0:
            proj, h, hb = ln_matmul(x.reshape(T, D), ln_in_g, ln_in_b, w_main, tn=N_PROJ // 4, name="ln_in_proj")
        else:
            proj = matmul(hb, w_main, BF16, tn=N_PROJ // 4, name="in_proj")
        ya, yb, yc = token_mixers(proj, hb, w_hf, tabs, lb_all[layer], hgrn_norm_g[layer], swa_sinks[layer],
                                  B, S)
        dense = layer % 2 == 0
        h, aux = merge_project(ya, yb, yc, proj, h,
                               ret_w_out[layer].astype(BF16), hgrn_w_out[layer].astype(BF16),
                               swa_rows_paired(swa_w_out[layer]).astype(BF16), w_o[layer].astype(BF16),
                               ln_mix_g[layer], ln_mix_b[layer], token_tiles=not dense)

        j = layer // 2
        if dense:
            h, hb = dense_ffn(aux, h, ffn_w_gate[j].astype(BF16), ffn_w_up[j].astype(BF16),
                              ffn_w_down[j].astype(BF16), ln_ffn_g[layer], ln_ffn_b[layer])
        else:
            h, hb = moe_ffn(h, aux, moe_router[j], moe_w_gate[j], moe_w_up[j], moe_w_down[j],
                            ln_ffn_g[layer], ln_ffn_b[layer])
    return h.reshape(B, S, D)
```
